```python
import math
import jax, jax.numpy as jnp
from jax import lax
import numpy as np

D_MODEL = 2048
BATCH = 4
SEQ = 2048
DEPTH = 2
DEC_BATCH = 128
DEC_SEQ = 1
PAST_LEN = 8192
PAGE_SIZE = 128

N_A_LAYERS = DEPTH // 2
N_B_LAYERS = DEPTH - N_A_LAYERS
SSM_GROUP = 16
N_SSM_GROUPS = D_MODEL // SSM_GROUP
SSM_STATE = 64
DT_MIN = 0.001
DT_MAX = 0.1
N_HEADS = 16
QK_NOPE = 128
QK_ROPE = 64
V_HEAD = 128
KV_LORA = 512
ROPE_THETA = 10000.0
Q_BLOCK = 128
ATTN_SCALE = (QK_NOPE + QK_ROPE) ** -0.5
N_EXPERTS = 16
N_EXPERT_GROUPS = 4
EXPERTS_PER_GROUP = N_EXPERTS // N_EXPERT_GROUPS
TOP_K = 2
D_EXPERT = D_MODEL // 4
EPS = 1e-6

kernel_name = 'yoco_s5_mla_grouped_moe_adaln_step'


def rmsnorm(x, g):
    xf = x.astype(jnp.float32)
    xf = xf * lax.rsqrt(jnp.mean(xf * xf, axis=-1, keepdims=True) + EPS)
    return xf.astype(x.dtype) * g


def ada_mod(c, w, b):
    return (jax.nn.silu(c) @ w + b)[:, None, :]


def rope_cos_sin(pos):
    inv = ROPE_THETA ** (-jnp.arange(0, QK_ROPE, 2, dtype=jnp.float32) / QK_ROPE)
    ang = pos.astype(jnp.float32)[:, None] * inv[None, :]
    return jnp.cos(ang), jnp.sin(ang)


def apply_rope(x, cos, sin):
    x1, x2 = jnp.split(x, 2, axis=-1)
    return jnp.concatenate([x1 * cos - x2 * sin, x1 * sin + x2 * cos], axis=-1).astype(x.dtype)


def _complex_affine_combine(e1, e2):
    a1r, a1i, b1r, b1i = e1
    a2r, a2i, b2r, b2i = e2
    return (a1r * a2r - a1i * a2i,
            a1r * a2i + a1i * a2r,
            a2r * b1r - a2i * b1i + b2r,
            a2r * b1i + a2i * b1r + b2i)


def s5_mixer(h, h0_re, h0_im, lam_re, lam_im, log_dt, b_re, b_im, c_re, c_im, d_skip, w_glu):
    bsz, L, _ = h.shape
    u = h.astype(jnp.float32)
    ug = u.reshape(bsz, L, N_SSM_GROUPS, SSM_GROUP)
    lam_re = lam_re.astype(jnp.float32)
    lam_im = lam_im.astype(jnp.float32)
    dt = jnp.exp(log_dt.astype(jnp.float32))[:, None]
    mag = jnp.exp(lam_re * dt)
    ab_re = mag * jnp.cos(lam_im * dt)
    ab_im = mag * jnp.sin(lam_im * dt)
    den = lam_re * lam_re + lam_im * lam_im
    f_re = ((ab_re - 1.0) * lam_re + ab_im * lam_im) / den
    f_im = (ab_im * lam_re - (ab_re - 1.0) * lam_im) / den
    br = b_re.astype(jnp.float32)
    bi = b_im.astype(jnp.float32)
    bb_re = f_re[..., None] * br - f_im[..., None] * bi
    bb_im = f_re[..., None] * bi + f_im[..., None] * br
    bu_re = jnp.einsum('blgc,gnc->blgn', ug, bb_re)
    bu_im = jnp.einsum('blgc,gnc->blgn', ug, bb_im)
    a_re = jnp.broadcast_to(ab_re, (1, L) + ab_re.shape)
    a_im = jnp.broadcast_to(ab_im, (1, L) + ab_im.shape)
    _, _, hr, hi = lax.associative_scan(_complex_affine_combine, (a_re, a_im, bu_re, bu_im), axis=1)
    if h0_re is not None:
        k = jnp.arange(1, L + 1, dtype=jnp.float32)[:, None, None]
        pmag = jnp.exp(lam_re * dt * k)
        p_re = pmag * jnp.cos(lam_im * dt * k)
        p_im = pmag * jnp.sin(lam_im * dt * k)
        h0r = h0_re.astype(jnp.float32)[:, None]
        h0i = h0_im.astype(jnp.float32)[:, None]
        hr = hr + p_re * h0r - p_im * h0i
        hi = hi + p_re * h0i + p_im * h0r
    y = (jnp.einsum('blgn,gcn->blgc', hr, c_re.astype(jnp.float32))
         - jnp.einsum('blgn,gcn->blgc', hi, c_im.astype(jnp.float32)))
    y = y.reshape(bsz, L, D_MODEL) + d_skip.astype(jnp.float32) * u
    z = jax.nn.gelu(y).astype(h.dtype)
    ga, gb = jnp.split(z @ w_glu, 2, axis=-1)
    return ga * jax.nn.sigmoid(gb), hr[:, -1], hi[:, -1]


def moe(h, w_router, b_router, w_gate, w_up, w_down):
    s = jax.nn.sigmoid((h @ w_router).astype(jnp.float32))
    sel = s + b_router.astype(jnp.float32)
    sel_g = sel.reshape(sel.shape[:-1] + (N_EXPERT_GROUPS, EXPERTS_PER_GROUP))
    group_score = lax.top_k(sel_g, TOP_K)[0].sum(-1)
    g_best = jnp.argmax(group_score, axis=-1)
    sel_in = jnp.take_along_axis(sel_g, g_best[..., None, None], axis=-2)[..., 0, :]
    _, local = lax.top_k(sel_in, TOP_K)
    idx = g_best[..., None] * EXPERTS_PER_GROUP + local
    w_sel = jnp.take_along_axis(s, idx, axis=-1)
    w_sel = w_sel / jnp.sum(w_sel, axis=-1, keepdims=True)
    gates = jnp.einsum('blk,blke->ble', w_sel, jax.nn.one_hot(idx, N_EXPERTS, dtype=jnp.float32)).astype(h.dtype)
    hid = jax.nn.silu(jnp.einsum('bld,edf->blef', h, w_gate)) * jnp.einsum('bld,edf->blef', h, w_up)
    return jnp.einsum('blef,ble,efd->bld', hid, gates, w_down)


def shared_latent_kv(x, c, cos, sin, g_kv, w_ada_kv, b_ada_kv, w_dkv, g_ckv):
    shift, scale = jnp.split(ada_mod(c, w_ada_kv, b_ada_kv), 2, axis=-1)
    h = rmsnorm(x, g_kv) * (1 + scale) + shift
    lat = h @ w_dkv
    ckv = rmsnorm(lat[..., :KV_LORA], g_ckv)
    kpe = apply_rope(lat[..., KV_LORA:], cos, sin)
    return ckv, kpe


def prompt_attention(q_nope, q_pe, ckv, kpe, w_uk, w_uv):
    bsz, L = q_nope.shape[:2]
    n_blocks = L // Q_BLOCK
    k_nope = jnp.einsum('bkr,rhd->bkhd', ckv, w_uk)
    v = jnp.einsum('bkr,rhd->bkhd', ckv, w_uv)
    qn = q_nope.reshape(bsz, n_blocks, Q_BLOCK, N_HEADS, QK_NOPE).swapaxes(0, 1)
    qp = q_pe.reshape(bsz, n_blocks, Q_BLOCK, N_HEADS, QK_ROPE).swapaxes(0, 1)
    kpos = jnp.arange(L)

    def one_block(args):
        qn_b, qp_b, blk = args
        s = (jnp.einsum('bqhd,bkhd->bhqk', qn_b, k_nope)
             + jnp.einsum('bqhe,bke->bhqk', qp_b, kpe)).astype(jnp.float32) * ATTN_SCALE
        qpos = blk * Q_BLOCK + jnp.arange(Q_BLOCK)
        s = jnp.where(kpos[None, :] <= qpos[:, None], s, -jnp.inf)
        p = jax.nn.softmax(s, axis=-1).astype(v.dtype)
        return jnp.einsum('bhqk,bkhd->bqhd', p, v)

    o = lax.map(one_block, (qn, qp, jnp.arange(n_blocks)))
    return o.swapaxes(0, 1).reshape(bsz, L, N_HEADS * V_HEAD)


def sample_attention(q_nope, q_pe, ckv_new, kpe_new, cache_ckv, cache_kpe, page_table, w_uk, w_uv):
    db, T = q_nope.shape[:2]
    past = page_table.shape[1] * PAGE_SIZE
    ckv_past = cache_ckv[page_table].reshape(db, past, KV_LORA)
    kpe_past = cache_kpe[page_table].reshape(db, past, QK_ROPE)
    q_lat = jnp.einsum('bqhd,rhd->bqhr', q_nope, w_uk)
    s_past = jnp.einsum('bqhr,bkr->bhqk', q_lat, ckv_past) + jnp.einsum('bqhe,bke->bhqk', q_pe, kpe_past)
    s_new = jnp.einsum('bqhr,bkr->bhqk', q_lat, ckv_new) + jnp.einsum('bqhe,bke->bhqk', q_pe, kpe_new)
    s = jnp.concatenate([s_past, s_new], axis=-1).astype(jnp.float32) * ATTN_SCALE
    mask = jnp.concatenate([jnp.ones((T, past), dtype=bool), jnp.tril(jnp.ones((T, T), dtype=bool))], axis=-1)
    s = jnp.where(mask, s, -jnp.inf)
    p = jax.nn.softmax(s, axis=-1).astype(ckv_new.dtype)
    o_lat = (jnp.einsum('bhqk,bkr->bqhr', p[..., :past], ckv_past)
             + jnp.einsum('bhqk,bkr->bqhr', p[..., past:], ckv_new))
    o = jnp.einsum('bqhr,rhd->bqhd', o_lat, w_uv)
    return o.reshape(db, T, N_HEADS * V_HEAD)


def mla_mixer(h, cos, sin, ckv, kpe, attend, w_q, w_o):
    q = (h @ w_q).reshape(h.shape[0], h.shape[1], N_HEADS, QK_NOPE + QK_ROPE)
    q_nope = q[..., :QK_NOPE]
    q_pe = apply_rope(q[..., QK_NOPE:], cos[:, None, :], sin[:, None, :])
    return attend(q_nope, q_pe, ckv, kpe) @ w_o


def trunk(x, c, pos, h0_re, h0_im, attend, P):
    cos, sin = rope_cos_sin(pos)
    new_re, new_im = [], []
    ckv = kpe = None
    for l in range(DEPTH):
        sh1, sc1, gt1, sh2, sc2, gt2 = jnp.split(ada_mod(c, P['w_ada'][l], P['b_ada'][l]), 6, axis=-1)
        h = rmsnorm(x, P['g_norm'][l, 0]) * (1 + sc1) + sh1
        if l < N_A_LAYERS:
            a = l
            y, hr, hi = s5_mixer(h,
                                 None if h0_re is None else h0_re[a],
                                 None if h0_im is None else h0_im[a],
                                 P['ssm_lam_re'][a], P['ssm_lam_im'][a], P['ssm_log_dt'][a],
                                 P['ssm_b_re'][a], P['ssm_b_im'][a], P['ssm_c_re'][a], P['ssm_c_im'][a],
                                 P['ssm_d'][a], P['w_glu'][a])
            new_re.append(hr)
            new_im.append(hi)
        else:
            b = l - N_A_LAYERS
            y = mla_mixer(h, cos, sin, ckv, kpe, attend, P['w_q'][b], P['w_o'][b])
        x = x + gt1 * y
        h = rmsnorm(x, P['g_norm'][l, 1]) * (1 + sc2) + sh2
        x = x + gt2 * moe(h, P['w_router'], P['b_router'], P['w_gate'][l], P['w_up'][l], P['w_down'][l])
        if l == N_A_LAYERS - 1:
            ckv, kpe = shared_latent_kv(x, c, cos, sin, P['g_kv'], P['w_ada_kv'], P['b_ada_kv'],
                                        P['w_dkv'], P['g_ckv'])
    return rmsnorm(x, P['g_final']), jnp.stack(new_re), jnp.stack(new_im), ckv, kpe


def setup_inputs(seed: int = 0) -> dict:
    key = jax.random.key(seed)
    ks = jax.random.split(key, 40)
    f32 = jnp.float32

    def nrm(i, shape, scale):
        return scale * jax.random.normal(ks[i], shape, f32)

    n_pages = PAST_LEN // PAGE_SIZE
    n_pool = (DEC_BATCH * n_pages * 5) // 4
    perm = jax.random.permutation(ks[0], n_pool)
    page_table = perm[: DEC_BATCH * n_pages].reshape(DEC_BATCH, n_pages).astype(jnp.int32)
    log_dt = jax.random.uniform(ks[1], (N_A_LAYERS, N_SSM_GROUPS), f32, math.log(DT_MIN), math.log(DT_MAX))
    lam_im = jnp.broadcast_to(jnp.pi * jnp.arange(SSM_STATE, dtype=f32), (N_A_LAYERS, N_SSM_GROUPS, SSM_STATE))
    d = D_MODEL
    return {
        'x_prompt': nrm(2, (BATCH, SEQ, d), 1.0),
        'x_sample': nrm(3, (DEC_BATCH, DEC_SEQ, d), 1.0),
        'c_prompt': nrm(4, (BATCH, d), 1.0),
        'c_sample': nrm(5, (DEC_BATCH, d), 1.0),
        'state_ssm_re': nrm(6, (N_A_LAYERS, DEC_BATCH, N_SSM_GROUPS, SSM_STATE), 0.5),
        'state_ssm_im': nrm(7, (N_A_LAYERS, DEC_BATCH, N_SSM_GROUPS, SSM_STATE), 0.5),
        'cache_ckv': nrm(8, (n_pool, PAGE_SIZE, KV_LORA), 1.0),
        'cache_kpe': nrm(9, (n_pool, PAGE_SIZE, QK_ROPE), 1.0),
        'page_table': page_table,
        'g_norm': 1.0 + nrm(10, (DEPTH, 2, d), 0.02),
        'w_ada': nrm(11, (DEPTH, d, 6 * d), 0.5 * d ** -0.5),
        'b_ada': nrm(12, (DEPTH, 6 * d), 0.02),
        'ssm_lam_re': -0.5 + nrm(13, (N_A_LAYERS, N_SSM_GROUPS, SSM_STATE), 0.01),
        'ssm_lam_im': lam_im,
        'ssm_log_dt': log_dt,
        'ssm_b_re': nrm(14, (N_A_LAYERS, N_SSM_GROUPS, SSM_STATE, SSM_GROUP), (2 * SSM_GROUP) ** -0.5),
        'ssm_b_im': nrm(15, (N_A_LAYERS, N_SSM_GROUPS, SSM_STATE, SSM_GROUP), (2 * SSM_GROUP) ** -0.5),
        'ssm_c_re': nrm(16, (N_A_LAYERS, N_SSM_GROUPS, SSM_GROUP, SSM_STATE), (2 * SSM_STATE) ** -0.5),
        'ssm_c_im': nrm(17, (N_A_LAYERS, N_SSM_GROUPS, SSM_GROUP, SSM_STATE), (2 * SSM_STATE) ** -0.5),
        'ssm_d': nrm(18, (N_A_LAYERS, d), 1.0),
        'w_glu': nrm(19, (N_A_LAYERS, d, 2 * d), d ** -0.5),
        'g_kv': 1.0 + nrm(20, (d,), 0.02),
        'w_ada_kv': nrm(21, (d, 2 * d), 0.5 * d ** -0.5),
        'b_ada_kv': nrm(22, (2 * d,), 0.02),
        'w_dkv': nrm(23, (d, KV_LORA + QK_ROPE), d ** -0.5),
        'g_ckv': 1.0 + nrm(24, (KV_LORA,), 0.02),
        'w_uk': nrm(25, (KV_LORA, N_HEADS, QK_NOPE), KV_LORA ** -0.5),
        'w_uv': nrm(26, (KV_LORA, N_HEADS, V_HEAD), KV_LORA ** -0.5),
        'w_q': nrm(27, (N_B_LAYERS, d, N_HEADS * (QK_NOPE + QK_ROPE)), d ** -0.5),
        'w_o': nrm(28, (N_B_LAYERS, N_HEADS * V_HEAD, d), (N_HEADS * V_HEAD) ** -0.5),
        'w_router': nrm(29, (d, N_EXPERTS), d ** -0.5),
        'b_router': nrm(30, (N_EXPERTS,), 0.01),
        'w_gate': nrm(31, (DEPTH, N_EXPERTS, d, D_EXPERT), d ** -0.5),
        'w_up': nrm(32, (DEPTH, N_EXPERTS, d, D_EXPERT), d ** -0.5),
        'w_down': nrm(33, (DEPTH, N_EXPERTS, D_EXPERT, d), D_EXPERT ** -0.5),
        'g_final': 1.0 + nrm(34, (d,), 0.02),
    }


def reference(x_prompt, x_sample, c_prompt, c_sample, state_ssm_re, state_ssm_im, cache_ckv, cache_kpe,
              page_table, g_norm, w_ada, b_ada, ssm_lam_re, ssm_lam_im, ssm_log_dt, ssm_b_re, ssm_b_im,
              ssm_c_re, ssm_c_im, ssm_d, w_glu, g_kv, w_ada_kv, b_ada_kv, w_dkv, g_ckv, w_uk, w_uv,
              w_q, w_o, w_router, b_router, w_gate, w_up, w_down, g_final):
    P = dict(g_norm=g_norm, w_ada=w_ada, b_ada=b_ada, ssm_lam_re=ssm_lam_re, ssm_lam_im=ssm_lam_im,
             ssm_log_dt=ssm_log_dt, ssm_b_re=ssm_b_re, ssm_b_im=ssm_b_im, ssm_c_re=ssm_c_re,
             ssm_c_im=ssm_c_im, ssm_d=ssm_d, w_glu=w_glu, g_kv=g_kv, w_ada_kv=w_ada_kv,
             b_ada_kv=b_ada_kv, w_dkv=w_dkv, g_ckv=g_ckv, w_q=w_q, w_o=w_o, w_router=w_router,
             b_router=b_router, w_gate=w_gate, w_up=w_up, w_down=w_down, g_final=g_final)

    def attend_prompt(q_nope, q_pe, ckv, kpe):
        return prompt_attention(q_nope, q_pe, ckv, kpe, w_uk, w_uv)

    def attend_sample(q_nope, q_pe, ckv, kpe):
        return sample_attention(q_nope, q_pe, ckv, kpe, cache_ckv, cache_kpe, page_table, w_uk, w_uv)

    pos_prompt = jnp.arange(x_prompt.shape[1])
    pos_sample = page_table.shape[1] * PAGE_SIZE + jnp.arange(x_sample.shape[1])
    y_prompt, ssm_re_p, ssm_im_p, ckv_p, kpe_p = trunk(x_prompt, c_prompt, pos_prompt, None, None,
                                                        attend_prompt, P)
    y_sample, ssm_re_s, ssm_im_s, ckv_s, kpe_s = trunk(x_sample, c_sample, pos_sample, state_ssm_re,
                                                        state_ssm_im, attend_sample, P)
    return (y_prompt, y_sample, ssm_re_p, ssm_im_p, ckv_p, kpe_p, ssm_re_s, ssm_im_s, ckv_s, kpe_s)
```

```python
import functools
import math

import jax
import jax.numpy as jnp
from jax import lax
from jax.experimental import pallas as pl
from jax.experimental.pallas import tpu as pltpu

F32 = jnp.float32
BF16 = jnp.bfloat16

SSM_GROUP = 16
SSM_STATE = 64
N_HEADS = 16
QK_NOPE = 128
QK_ROPE = 64
V_HEAD = 128
KV_LORA = 512
ROPE_THETA = 10000.0
PAGE_SIZE = 128
N_EXPERTS = 16
N_EXPERT_GROUPS = 4
EXPERTS_PER_GROUP = N_EXPERTS // N_EXPERT_GROUPS
EPS = 1e-6
ATTN_SCALE = (QK_NOPE + QK_ROPE) ** -0.5
QK_HEAD = QK_NOPE + QK_ROPE

V7X_VMEM_BYTES = 64 * 1024 * 1024
VMEM_LIMIT_BYTES = V7X_VMEM_BYTES - 8 * 1024 * 1024
LANES = 128
S5_CHUNK = 16


def _params(*sem):
    return pltpu.CompilerParams(dimension_semantics=sem, vmem_limit_bytes=VMEM_LIMIT_BYTES)


def _normmod(x, g, sc, sh):
    ms = jnp.mean(x * x, axis=-1, keepdims=True)
    return (x * lax.rsqrt(ms + EPS)) * g * (1.0 + sc) + sh


def _rope(p, cos2, sin2):
    half = QK_ROPE // 2
    swapped = jnp.concatenate([p[:, half:], p[:, :half]], axis=1)
    return p * cos2 + swapped * sin2


def _mod_spec(mods, chunk, tm, rows_per_b, width, j_blocks=None):
    _, r, _ = mods.shape
    if j_blocks is None:
        return pl.BlockSpec((None, r, width), lambda i, j: ((i * tm) // rows_per_b, 0, chunk))
    return pl.BlockSpec((None, r, width), lambda i, j: ((i * tm) // rows_per_b, 0, chunk * j_blocks + j))


def _ada_kernel(c_ref, w_ref, b_ref, o_ref, cb_ref):
    @pl.when((pl.program_id(0) == 0) & (pl.program_id(1) == 0))
    def _():
        c = c_ref[...]
        cb_ref[...] = (c * jax.nn.sigmoid(c)).astype(BF16)

    acc = jnp.dot(cb_ref[...], w_ref[...].astype(BF16), preferred_element_type=F32)
    o_ref[...] = acc + b_ref[...]


def _ada(c_all, w, b, tn=1024):
    m, d = c_all.shape
    lw, _, n = w.shape
    return pl.pallas_call(
        _ada_kernel,
        grid=(lw, n // tn),
        in_specs=[
            pl.BlockSpec((m, d), lambda l, j: (0, 0)),
            pl.BlockSpec((None, d, tn), lambda l, j: (l, 0, j)),
            pl.BlockSpec((None, 1, tn), lambda l, j: (l, 0, j)),
        ],
        out_specs=pl.BlockSpec((None, m, tn), lambda l, j: (l, 0, j)),
        out_shape=jax.ShapeDtypeStruct((lw, m, n), F32),
        scratch_shapes=[pltpu.VMEM((m, d), BF16)],
        compiler_params=_params("arbitrary", "arbitrary"),
        name="ada_mod",
    )(c_all, w, b.reshape(lw, 1, n))


def _normmod_kernel(x_ref, g_ref, sc_ref, sh_ref, o_ref):
    o_ref[...] = _normmod(x_ref[...], g_ref[...], sc_ref[...], sh_ref[...]).astype(o_ref.dtype)


def _normmod_call(x, g, mods, sc_chunk, sh_chunk, rows_per_b, tm, out_dtype):
    t, d = x.shape
    return pl.pallas_call(
        _normmod_kernel,
        grid=(t // tm, 1),
        in_specs=[
            pl.BlockSpec((tm, d), lambda i, j: (i, 0)),
            pl.BlockSpec((1, d), lambda i, j: (0, 0)),
            _mod_spec(mods, sc_chunk, tm, rows_per_b, d),
            _mod_spec(mods, sh_chunk, tm, rows_per_b, d),
        ],
        out_specs=pl.BlockSpec((tm, d), lambda i, j: (i, 0)),
        out_shape=jax.ShapeDtypeStruct((t, d), out_dtype),
        compiler_params=_params("arbitrary", "arbitrary"),
        name="norm_mod",
    )(x, g.reshape(1, d), mods, mods)


def _final_norm_kernel(x_ref, g_ref, o_ref):
    x = x_ref[...]
    ms = jnp.mean(x * x, axis=-1, keepdims=True)
    o_ref[...] = (x * lax.rsqrt(ms + EPS)) * g_ref[...]


def _final_norm(x, g, tm):
    t, d = x.shape
    return pl.pallas_call(
        _final_norm_kernel,
        grid=(t // tm,),
        in_specs=[pl.BlockSpec((tm, d), lambda i: (i, 0)), pl.BlockSpec((1, d), lambda i: (0, 0))],
        out_specs=pl.BlockSpec((tm, d), lambda i: (i, 0)),
        out_shape=jax.ShapeDtypeStruct((t, d), F32),
        compiler_params=_params("arbitrary"),
        name="final_norm",
    )(x, g.reshape(1, d))


def _proj_res_kernel(x_ref, *refs, glu):
    if glu:
        wa_ref, wb_ref, res_ref, gt_ref, o_ref = refs
    else:
        wa_ref, res_ref, gt_ref, o_ref = refs
    x = x_ref[...]
    y = jnp.dot(x, wa_ref[...].astype(BF16), preferred_element_type=F32)
    if glu:
        yb = jnp.dot(x, wb_ref[...].astype(BF16), preferred_element_type=F32)
        y = y * jax.nn.sigmoid(yb)
    o_ref[...] = res_ref[...] + gt_ref[...] * y


def _proj_res(x, w, res, mods, gt_chunk, rows_per_b, tm, tn, glu):
    t, k = x.shape
    n = res.shape[1]
    nj = n // tn
    in_specs = [pl.BlockSpec((tm, k), lambda i, j: (i, 0)), pl.BlockSpec((k, tn), lambda i, j: (0, j))]
    args = [x, w]
    if glu:
        in_specs.append(pl.BlockSpec((k, tn), lambda i, j: (0, nj + j)))
        args.append(w)
    in_specs += [pl.BlockSpec((tm, tn), lambda i, j: (i, j)), _mod_spec(mods, gt_chunk, tm, rows_per_b, tn, nj)]
    args += [res, mods]
    return pl.pallas_call(
        functools.partial(_proj_res_kernel, glu=glu),
        grid=(t // tm, nj),
        in_specs=in_specs,
        out_specs=pl.BlockSpec((tm, tn), lambda i, j: (i, j)),
        out_shape=jax.ShapeDtypeStruct((t, n), F32),
        compiler_params=_params("arbitrary", "arbitrary"),
        name="glu_proj" if glu else "out_proj",
    )(*args)


def _cpow(lr_dt, li_dt, k):
    mag = jnp.exp(k * lr_dt)
    ang = k * li_dt
    return mag * jnp.cos(ang), mag * jnp.sin(ang)


def _rep_rows(e, n):
    q, w = e.shape
    return jnp.concatenate([jnp.broadcast_to(e[t:t + 1], (n, w)) for t in range(q)], axis=0)


def _tile_rows(c, n):
    return jnp.concatenate([c] * n, axis=0)


def _s5_discretise(lr, li, ldt, bt_re, bt_im):
    dt = jnp.exp(ldt)
    lr_dt = lr * dt
    li_dt = li * dt
    mag = jnp.exp(lr_dt)
    ab_re = mag * jnp.cos(li_dt)
    ab_im = mag * jnp.sin(li_dt)
    den = lr * lr + li * li
    f_re = ((ab_re - 1.0) * lr + ab_im * li) / den
    f_im = (ab_im * lr - (ab_re - 1.0) * li) / den
    bb_re = f_re * bt_re - f_im * bt_im
    bb_im = f_re * bt_im + f_im * bt_re
    return lr_dt, li_dt, ab_re, ab_im, bb_re, bb_im


def _gelu_tanh(y):
    return 0.5 * y * (1.0 + jnp.tanh(math.sqrt(2.0 / math.pi) * (y + 0.044715 * (y * y * y))))


def _s5_prompt_kernel(x_ref, lr_ref, li_ref, ldt_ref, btr_ref, bti_ref, cr_ref, ci_ref, d_ref,
                      z_ref, sre_ref, sim_ref, s_re, s_im, hp_re, hp_im, yl, *, gb, nb, nchunk):
    q = S5_CHUNK
    cg = SSM_GROUP
    half = q // 2
    ti = lax.broadcasted_iota(jnp.int32, (q, 1), 0).astype(F32)
    rows = lax.broadcasted_iota(jnp.int32, (q * cg, q * cg), 0)
    cols = lax.broadcasted_iota(jnp.int32, (q * cg, q * cg), 1)
    shift = cg.bit_length() - 1
    causal = (rows >> shift) <= (cols >> shift)
    diag = rows == cols
    a16 = []
    vcs = []
    for g in range(gb):
        lr_dt, li_dt, _, _, bb_re, bb_im = _s5_discretise(
            lr_ref[g], li_ref[g], ldt_ref[g], btr_ref[g], bti_ref[g])
        c_re = cr_ref[g]
        c_im = ci_ref[g]
        e1r, e1i = _cpow(lr_dt, li_dt, ti - half)
        e2r, e2i = _cpow(lr_dt, li_dt, half - ti)
        e1r, e1i, e2r, e2i = (_rep_rows(e, cg) for e in (e1r, e1i, e2r, e2i))
        ct_re, ct_im = _tile_rows(c_re, q), _tile_rows(c_im, q)
        bt_re, bt_im = _tile_rows(bb_re, q), _tile_rows(bb_im, q)
        pc_re = ct_re * e1r - ct_im * e1i
        pc_im = ct_re * e1i + ct_im * e1r
        pb_re = bt_re * e2r - bt_im * e2i
        pb_im = bt_re * e2i + bt_im * e2r
        dn = (((1,), (1,)), ((), ()))
        tt = (lax.dot_general(pb_re.astype(BF16), pc_re.astype(BF16), dn, preferred_element_type=F32)
              - lax.dot_general(pb_im.astype(BF16), pc_im.astype(BF16), dn, preferred_element_type=F32))
        tt = jnp.where(causal, tt, 0.0) + jnp.where(diag, d_ref[g], 0.0)
        e4r, e4i = _cpow(lr_dt, li_dt, (q - 1) - ti)
        e4r, e4i = _rep_rows(e4r, cg), _rep_rows(e4i, cg)
        w1_re = bt_re * e4r - bt_im * e4i
        w1_im = bt_re * e4i + bt_im * e4r
        e3r, e3i = _cpow(lr_dt, li_dt, ti + 1.0)
        e3r, e3i = _rep_rows(e3r, cg), _rep_rows(e3i, cg)
        vcs.append(((ct_re * e3r - ct_im * e3i).astype(BF16), (ct_re * e3i + ct_im * e3r).astype(BF16)))
        a16.append(_cpow(lr_dt, li_dt, float(q)))
        x = x_ref[g]
        yl[g] = jnp.dot(x, tt.astype(BF16), preferred_element_type=F32)
        s_re[g] = jnp.dot(x, w1_re.astype(BF16), preferred_element_type=F32)
        s_im[g] = jnp.dot(x, w1_im.astype(BF16), preferred_element_type=F32)

    def carry(ch, hs):
        new = []
        r0 = pl.multiple_of(ch * nb, nb)
        for g in range(gb):
            hr, hi = hs[2 * g], hs[2 * g + 1]
            hp_re[g, pl.ds(r0, nb), :] = hr
            hp_im[g, pl.ds(r0, nb), :] = hi
            ar, ai = a16[g]
            new.append(ar * hr - ai * hi + s_re[g, pl.ds(r0, nb), :])
            new.append(ar * hi + ai * hr + s_im[g, pl.ds(r0, nb), :])
        return tuple(new)

    zero = jnp.zeros((nb, SSM_STATE), F32)
    hs = lax.fori_loop(0, nchunk, carry, (zero,) * (2 * gb), unroll=8)
    dn = (((1,), (1,)), ((), ()))
    for g in range(gb):
        sre_ref[g] = hs[2 * g]
        sim_ref[g] = hs[2 * g + 1]
        vc_re, vc_im = vcs[g]
        y = (yl[g]
             + lax.dot_general(hp_re[g].astype(BF16), vc_re, dn, preferred_element_type=F32)
             - lax.dot_general(hp_im[g].astype(BF16), vc_im, dn, preferred_element_type=F32))
        z_ref[g] = _gelu_tanh(y).astype(BF16)


def _s5_prompt(hx, lam_re, lam_im, log_dt, bt_re, bt_im, c_re, c_im, d_rep, nb, gb=2):
    g, r, w = hx.shape
    n = SSM_STATE
    nchunk = r // nb
    vec = lambda: pl.BlockSpec((gb, 1, n), lambda i: (i, 0, 0))
    mat = lambda: pl.BlockSpec((gb, SSM_GROUP, n), lambda i: (i, 0, 0))
    return pl.pallas_call(
        functools.partial(_s5_prompt_kernel, gb=gb, nb=nb, nchunk=nchunk),
        grid=(g // gb,),
        in_specs=[pl.BlockSpec((gb, r, w), lambda i: (i, 0, 0)), vec(), vec(), vec(),
                  mat(), mat(), mat(), mat(), pl.BlockSpec((gb, 1, w), lambda i: (i, 0, 0))],
        out_specs=[pl.BlockSpec((gb, r, w), lambda i: (i, 0, 0)),
                   pl.BlockSpec((gb, nb, n), lambda i: (i, 0, 0)),
                   pl.BlockSpec((gb, nb, n), lambda i: (i, 0, 0))],
        out_shape=[jax.ShapeDtypeStruct((g, r, w), BF16),
                   jax.ShapeDtypeStruct((g, nb, n), F32),
                   jax.ShapeDtypeStruct((g, nb, n), F32)],
        scratch_shapes=[pltpu.VMEM((gb, r, n), F32), pltpu.VMEM((gb, r, n), F32),
                        pltpu.VMEM((gb, r, n), F32), pltpu.VMEM((gb, r, n), F32),
                        pltpu.VMEM((gb, r, w), F32)],
        compiler_params=_params("arbitrary"),
        name="s5_prompt",
    )(hx, lam_re, lam_im, log_dt, bt_re, bt_im, c_re, c_im, d_rep)


def _s5_sample_kernel(u_ref, h0r_ref, h0i_ref, lr_ref, li_ref, ldt_ref, btr_ref, bti_ref,
                      cr_ref, ci_ref, d_ref, z_ref, nr_ref, ni_ref, *, gb):
    hp = lax.Precision.HIGHEST
    dn = (((1,), (1,)), ((), ()))
    for g in range(gb):
        _, _, ab_re, ab_im, bb_re, bb_im = _s5_discretise(
            lr_ref[g], li_ref[g], ldt_ref[g], btr_ref[g], bti_ref[g])
        u = u_ref[g]
        bu_re = jnp.dot(u, bb_re, preferred_element_type=F32, precision=hp)
        bu_im = jnp.dot(u, bb_im, preferred_element_type=F32, precision=hp)
        h0r = h0r_ref[g]
        h0i = h0i_ref[g]
        hr = bu_re + ab_re * h0r - ab_im * h0i
        hi = bu_im + ab_re * h0i + ab_im * h0r
        nr_ref[g] = hr
        ni_ref[g] = hi
        y = (lax.dot_general(hr, cr_ref[g], dn, preferred_element_type=F32, precision=hp)
             - lax.dot_general(hi, ci_ref[g], dn, preferred_element_type=F32, precision=hp))
        y = y + d_ref[g] * u
        z_ref[g] = _gelu_tanh(y).astype(BF16)


def _s5_sample(u, h0_re, h0_im, lam_re, lam_im, log_dt, bt_re, bt_im, c_re, c_im, d_g, gb=8):
    g, b, cg = u.shape
    n = SSM_STATE
    vec = lambda: pl.BlockSpec((gb, 1, n), lambda i: (i, 0, 0))
    mat = lambda: pl.BlockSpec((gb, cg, n), lambda i: (i, 0, 0))
    st = lambda: pl.BlockSpec((gb, b, n), lambda i: (i, 0, 0))
    return pl.pallas_call(
        functools.partial(_s5_sample_kernel, gb=gb),
        grid=(g // gb,),
        in_specs=[pl.BlockSpec((gb, b, cg), lambda i: (i, 0, 0)), st(), st(), vec(), vec(), vec(),
                  mat(), mat(), mat(), mat(), pl.BlockSpec((gb, 1, cg), lambda i: (i, 0, 0))],
        out_specs=[pl.BlockSpec((gb, b, cg), lambda i: (i, 0, 0)), st(), st()],
        out_shape=[jax.ShapeDtypeStruct((g, b, cg), BF16),
                   jax.ShapeDtypeStruct((g, b, n), F32),
                   jax.ShapeDtypeStruct((g, b, n), F32)],
        compiler_params=_params("arbitrary"),
        name="s5_sample",
    )(u, h0_re, h0_im, lam_re, lam_im, log_dt, bt_re, bt_im, c_re, c_im, d_g)


def _route_kernel(x_ref, g_ref, sc_ref, sh_ref, wr_ref, br_ref, h_ref, gates_ref):
    h = _normmod(x_ref[...], g_ref[...], sc_ref[...], sh_ref[...])
    h_ref[...] = h.astype(BF16)
    logits = lax.dot_general(wr_ref[...], h, (((1,), (1,)), ((), ())),
                             preferred_element_type=F32, precision=lax.Precision.HIGHEST)
    s = jax.nn.sigmoid(logits)
    sel = s + br_ref[...]
    epg = EXPERTS_PER_GROUP
    srow = [s[e:e + 1, :] for e in range(N_EXPERTS)]
    row = [sel[e:e + 1, :] for e in range(N_EXPERTS)]

    def top2_sum(a, b, c, d):
        hi1, lo1 = jnp.maximum(a, b), jnp.minimum(a, b)
        hi2, lo2 = jnp.maximum(c, d), jnp.minimum(c, d)
        return jnp.maximum(hi1, hi2) + jnp.maximum(jnp.minimum(hi1, hi2), jnp.maximum(lo1, lo2))

    gs = [top2_sum(*row[epg * g:epg * (g + 1)]) for g in range(N_EXPERT_GROUPS)]
    best = gs[0]
    gidx = jnp.zeros_like(best, dtype=jnp.int32)
    for g in range(1, N_EXPERT_GROUPS):
        better = gs[g] > best
        gidx = jnp.where(better, g, gidx)
        best = jnp.where(better, gs[g], best)

    def pick(rows_, j):
        v = rows_[j]
        for g in range(1, N_EXPERT_GROUPS):
            v = jnp.where(gidx == g, rows_[epg * g + j], v)
        return v

    v = [pick(row, j) for j in range(epg)]
    u = [pick(srow, j) for j in range(epg)]

    def argmax_first(vals):
        bv, bi = vals[0], jnp.zeros_like(gidx)
        for j in range(1, epg):
            better = vals[j] > bv
            bi = jnp.where(better, j, bi)
            bv = jnp.where(better, vals[j], bv)
        return bi

    i1 = argmax_first(v)
    i2 = argmax_first([jnp.where(i1 == j, -jnp.inf, v[j]) for j in range(epg)])

    def take(vals, idx):
        out = vals[0]
        for j in range(1, epg):
            out = jnp.where(idx == j, vals[j], out)
        return out

    w1, w2 = take(u, i1), take(u, i2)
    tot = w1 + w2
    w1, w2 = w1 / tot, w2 / tot
    e1 = gidx * epg + i1
    e2 = gidx * epg + i2
    gates_ref[...] = jnp.concatenate(
        [jnp.where(e1 == e, w1, 0.0) + jnp.where(e2 == e, w2, 0.0) for e in range(N_EXPERTS)], axis=0)


def _route(x, g, mods, sc_chunk, sh_chunk, rows_per_b, tm, w_router, b_router):
    t, d = x.shape
    e = N_EXPERTS
    return pl.pallas_call(
        _route_kernel,
        grid=(t // tm, 1),
        in_specs=[
            pl.BlockSpec((tm, d), lambda i, j: (i, 0)),
            pl.BlockSpec((1, d), lambda i, j: (0, 0)),
            _mod_spec(mods, sc_chunk, tm, rows_per_b, d),
            _mod_spec(mods, sh_chunk, tm, rows_per_b, d),
            pl.BlockSpec((e, d), lambda i, j: (0, 0)),
            pl.BlockSpec((e, 1), lambda i, j: (0, 0)),
        ],
        out_specs=[pl.BlockSpec((tm, d), lambda i, j: (i, 0)), pl.BlockSpec((e, tm), lambda i, j: (0, i))],
        out_shape=[jax.ShapeDtypeStruct((t, d), BF16), jax.ShapeDtypeStruct((e, t), F32)],
        compiler_params=_params("arbitrary", "arbitrary"),
        name="moe_route",
    )(x, g.reshape(1, d), mods, mods, w_router.T, b_router.reshape(e, 1))


def _moe_kernel(h_ref, gates_ref, wg_ref, wu_ref, wd_ref, res_ref, gt_ref, o_ref, acc_ref):
    e = pl.program_id(1)
    first = (e == 0) & (pl.program_id(2) == 0)
    last = (e == pl.num_programs(1) - 1) & (pl.program_id(2) == pl.num_programs(2) - 1)

    @pl.when(first)
    def _():
        acc_ref[...] = jnp.zeros_like(acc_ref)

    h = h_ref[...]
    a = jnp.dot(h, wg_ref[...].astype(BF16), preferred_element_type=F32)
    b = jnp.dot(h, wu_ref[...].astype(BF16), preferred_element_type=F32)
    gates = gates_ref[...]
    lane = lax.broadcasted_iota(jnp.int32, gates.shape, 1)
    gcol = jnp.sum(jnp.where(lane == e, gates, 0.0), axis=1, keepdims=True)
    hid = (a * jax.nn.sigmoid(a)) * b * gcol
    acc_ref[...] += jnp.dot(hid.astype(BF16), wd_ref[...].astype(BF16), preferred_element_type=F32)

    @pl.when(last)
    def _():
        o_ref[...] = res_ref[...] + gt_ref[...] * acc_ref[...]


def _moe(h, gates, w_gate, w_up, w_down, layer, res, mods, gt_chunk, rows_per_b, tm, fsplit=2):
    t, d = h.shape
    e, _, f = w_gate.shape[1:]
    tf = f // fsplit
    r = mods.shape[1]
    return pl.pallas_call(
        _moe_kernel,
        grid=(t // tm, e, fsplit),
        in_specs=[
            pl.BlockSpec((tm, d), lambda i, j, k: (i, 0)),
            pl.BlockSpec((tm, e), lambda i, j, k: (i, 0)),
            pl.BlockSpec((None, None, d, tf), lambda i, j, k: (layer, j, 0, k)),
            pl.BlockSpec((None, None, d, tf), lambda i, j, k: (layer, j, 0, k)),
            pl.BlockSpec((None, None, tf, d), lambda i, j, k: (layer, j, k, 0)),
            pl.BlockSpec((tm, d), lambda i, j, k: (i, 0)),
            pl.BlockSpec((None, r, d), lambda i, j, k: ((i * tm) // rows_per_b, 0, gt_chunk)),
        ],
        out_specs=pl.BlockSpec((tm, d), lambda i, j, k: (i, 0)),
        out_shape=jax.ShapeDtypeStruct((t, d), F32),
        scratch_shapes=[pltpu.VMEM((tm, d), F32)],
        compiler_params=_params("arbitrary", "arbitrary", "arbitrary"),
        name="moe_experts",
    )(h, gates, w_gate, w_up, w_down, res, mods)


def _kv_kernel(x_ref, g_ref, sc_ref, sh_ref, w_ref, gc_ref, cos_ref, sin_ref, ckv_ref, kpe_ref, wb_ref):
    @pl.when(pl.program_id(0) == 0)
    def _():
        wb_ref[...] = w_ref[...].astype(BF16)

    h = _normmod(x_ref[...], g_ref[...], sc_ref[...], sh_ref[...]).astype(BF16)
    lat = jnp.dot(h, wb_ref[...], preferred_element_type=F32)
    c = lat[:, :KV_LORA]
    ms = jnp.mean(c * c, axis=-1, keepdims=True)
    ckv_ref[...] = (c * lax.rsqrt(ms + EPS)) * gc_ref[...]
    kpe_ref[...] = _rope(lat[:, KV_LORA:], cos_ref[...], sin_ref[...])


def _kv(x, g_kv, mods, rows_per_b, tm, w_dkv, g_ckv, cos2, sin2):
    t, d = x.shape
    n = w_dkv.shape[1]
    nt = cos2.shape[0] // tm
    return pl.pallas_call(
        _kv_kernel,
        grid=(t // tm, 1),
        in_specs=[
            pl.BlockSpec((tm, d), lambda i, j: (i, 0)),
            pl.BlockSpec((1, d), lambda i, j: (0, 0)),
            _mod_spec(mods, 1, tm, rows_per_b, d),
            _mod_spec(mods, 0, tm, rows_per_b, d),
            pl.BlockSpec((d, n), lambda i, j: (0, 0)),
            pl.BlockSpec((1, KV_LORA), lambda i, j: (0, 0)),
            pl.BlockSpec((tm, QK_ROPE), lambda i, j: (i % nt, 0)),
            pl.BlockSpec((tm, QK_ROPE), lambda i, j: (i % nt, 0)),
        ],
        out_specs=[pl.BlockSpec((tm, KV_LORA), lambda i, j: (i, 0)),
                   pl.BlockSpec((tm, QK_ROPE), lambda i, j: (i, 0))],
        out_shape=[jax.ShapeDtypeStruct((t, KV_LORA), F32), jax.ShapeDtypeStruct((t, QK_ROPE), F32)],
        scratch_shapes=[pltpu.VMEM((d, n), BF16)],
        compiler_params=_params("arbitrary", "arbitrary"),
        name="latent_kv",
    )(x, g_kv.reshape(1, d), mods, mods, w_dkv, g_ckv.reshape(1, KV_LORA), cos2, sin2)


def _qproj_kernel(x_ref, g_ref, sc_ref, sh_ref, w_ref, cos_ref, sin_ref, q_ref, hb_ref):
    @pl.when(pl.program_id(1) == 0)
    def _():
        hb_ref[...] = _normmod(x_ref[...], g_ref[...], sc_ref[...], sh_ref[...]).astype(BF16)

    acc = jnp.dot(hb_ref[...], w_ref[...].astype(BF16), preferred_element_type=F32)
    q_ref[:, :QK_NOPE] = acc[:, :QK_NOPE].astype(BF16)
    q_ref[:, QK_NOPE:] = _rope(acc[:, QK_NOPE:], cos_ref[...], sin_ref[...]).astype(BF16)


def _qproj(x, g, mods, sc_chunk, sh_chunk, rows_per_b, tm, w_qt, cos2, sin2):
    t, d = x.shape
    nh = w_qt.shape[0]
    nt = rows_per_b // tm
    return pl.pallas_call(
        _qproj_kernel,
        grid=(t // tm, nh),
        in_specs=[
            pl.BlockSpec((tm, d), lambda i, j: (i, 0)),
            pl.BlockSpec((1, d), lambda i, j: (0, 0)),
            _mod_spec(mods, sc_chunk, tm, rows_per_b, d),
            _mod_spec(mods, sh_chunk, tm, rows_per_b, d),
            pl.BlockSpec((None, d, QK_HEAD), lambda i, j: (j, 0, 0)),
            pl.BlockSpec((tm, QK_ROPE), lambda i, j: (i % nt, 0)),
            pl.BlockSpec((tm, QK_ROPE), lambda i, j: (i % nt, 0)),
        ],
        out_specs=pl.BlockSpec((None, None, tm, QK_HEAD), lambda i, j: (i // nt, j, i % nt, 0)),
        out_shape=jax.ShapeDtypeStruct((t // rows_per_b, nh, rows_per_b, QK_HEAD), BF16),
        scratch_shapes=[pltpu.VMEM((tm, d), BF16)],
        compiler_params=_params("arbitrary", "arbitrary"),
        name="q_proj",
    )(x, g.reshape(1, d), mods, mods, w_qt, cos2, sin2)


def _kvup_kernel(ckv_ref, kpe_ref, wk_ref, wv_ref, k_ref, v_ref, *, hb):
    c = ckv_ref[...].astype(BF16)
    kn = jnp.dot(c, wk_ref[...].astype(BF16), preferred_element_type=F32)
    vv = jnp.dot(c, wv_ref[...].astype(BF16), preferred_element_type=F32)
    kpe = kpe_ref[...].astype(BF16)
    for h in range(hb):
        k_ref[h, :, :QK_NOPE] = kn[:, h * QK_NOPE:(h + 1) * QK_NOPE].astype(BF16)
        k_ref[h, :, QK_NOPE:] = kpe
        v_ref[h] = vv[:, h * V_HEAD:(h + 1) * V_HEAD].astype(BF16)


def _kvup(ckv, kpe, w_uk2, w_uv2, bsz, seq, tm, hb=4):
    t, r = ckv.shape
    nt = seq // tm
    return pl.pallas_call(
        functools.partial(_kvup_kernel, hb=hb),
        grid=(t // tm, N_HEADS // hb),
        in_specs=[
            pl.BlockSpec((tm, r), lambda i, j: (i, 0)),
            pl.BlockSpec((tm, QK_ROPE), lambda i, j: (i, 0)),
            pl.BlockSpec((r, hb * QK_NOPE), lambda i, j: (0, j)),
            pl.BlockSpec((r, hb * V_HEAD), lambda i, j: (0, j)),
        ],
        out_specs=[pl.BlockSpec((None, hb, tm, QK_HEAD), lambda i, j: (i // nt, j, i % nt, 0)),
                   pl.BlockSpec((None, hb, tm, V_HEAD), lambda i, j: (i // nt, j, i % nt, 0))],
        out_shape=[jax.ShapeDtypeStruct((bsz, N_HEADS, seq, QK_HEAD), BF16),
                   jax.ShapeDtypeStruct((bsz, N_HEADS, seq, V_HEAD), BF16)],
        compiler_params=_params("arbitrary", "arbitrary"),
        name="kv_up",
    )(ckv, kpe, w_uk2, w_uv2)


def _flash_kernel(q_ref, k_ref, v_ref, o_ref, m_ref, l_ref, acc_ref, *, tq, tk):
    qi = pl.program_id(2)
    kj = pl.program_id(3)

    @pl.when(kj == 0)
    def _():
        m_ref[...] = jnp.full_like(m_ref, -jnp.inf)
        l_ref[...] = jnp.zeros_like(l_ref)
        acc_ref[...] = jnp.zeros_like(acc_ref)

    @pl.when(kj * tk <= qi * tq + (tq - 1))
    def _():
        s = lax.dot_general(q_ref[...], k_ref[...], (((1,), (1,)), ((), ())),
                            preferred_element_type=F32) * ATTN_SCALE
        qpos = qi * tq + lax.broadcasted_iota(jnp.int32, (tq, tk), 0)
        kpos = kj * tk + lax.broadcasted_iota(jnp.int32, (tq, tk), 1)
        s = jnp.where(kpos <= qpos, s, -jnp.inf)
        m_prev = m_ref[...]
        m_next = jnp.maximum(m_prev, jnp.max(s, axis=1, keepdims=True))
        alpha = jnp.exp(m_prev - m_next)
        p = jnp.exp(s - jnp.tile(m_next, (1, tk // LANES)))
        l_ref[...] = alpha * l_ref[...] + jnp.sum(p, axis=1, keepdims=True)
        acc_ref[...] = alpha * acc_ref[...] + jnp.dot(p.astype(BF16), v_ref[...], preferred_element_type=F32)
        m_ref[...] = m_next

    @pl.when(kj == pl.num_programs(3) - 1)
    def _():
        o_ref[...] = (acc_ref[...] / l_ref[...]).astype(BF16)


def _flash(q, k, v, tq=512, tk=512):
    bsz, nh, seq, _ = q.shape
    nq, nk = seq // tq, seq // tk
    last_k = lambda i: (i * tq + tq - 1) // tk
    return pl.pallas_call(
        functools.partial(_flash_kernel, tq=tq, tk=tk),
        grid=(bsz, nh, nq, nk),
        in_specs=[
            pl.BlockSpec((None, None, tq, QK_HEAD), lambda b, h, i, j: (b, h, i, 0)),
            pl.BlockSpec((None, None, tk, QK_HEAD), lambda b, h, i, j: (b, h, jnp.minimum(j, last_k(i)), 0)),
            pl.BlockSpec((None, None, tk, V_HEAD), lambda b, h, i, j: (b, h, jnp.minimum(j, last_k(i)), 0)),
        ],
        out_specs=pl.BlockSpec((tq, V_HEAD), lambda b, h, i, j: (b * nq + i, h)),
        out_shape=jax.ShapeDtypeStruct((bsz * seq, nh * V_HEAD), BF16),
        scratch_shapes=[pltpu.VMEM((tq, LANES), F32), pltpu.VMEM((tq, LANES), F32),
                        pltpu.VMEM((tq, V_HEAD), F32)],
        compiler_params=_params("arbitrary", "arbitrary", "arbitrary", "arbitrary"),
        name="flash_attn",
    )(q, k, v)


def _qlat_kernel(q_ref, wk_ref, o_ref):
    q = q_ref[...]
    qlat = lax.dot_general(q[:, :QK_NOPE], wk_ref[...].astype(BF16), (((1,), (1,)), ((), ())),
                           preferred_element_type=F32)
    o_ref[:, :KV_LORA] = qlat.astype(BF16)
    o_ref[:, KV_LORA:] = q[:, QK_NOPE:]


def _qlat(q, w_uk2):
    nh, b, _ = q.shape
    r = w_uk2.shape[0]
    return pl.pallas_call(
        _qlat_kernel,
        grid=(nh,),
        in_specs=[pl.BlockSpec((None, b, QK_HEAD), lambda h: (h, 0, 0)),
                  pl.BlockSpec((r, QK_NOPE), lambda h: (0, h))],
        out_specs=pl.BlockSpec((None, b, r + QK_ROPE), lambda h: (h, 0, 0)),
        out_shape=jax.ShapeDtypeStruct((nh, b, r + QK_ROPE), BF16),
        compiler_params=_params("arbitrary"),
        name="q_latent",
    )(q, w_uk2)


def _paged_kernel(pt_ref, q_ref, *refs, npg):
    ckv_refs = refs[:npg]
    kpe_refs = refs[npg:2 * npg]
    cnew_ref, pnew_ref, o_ref, qpad_ref, m_ref, l_ref, acc_ref = refs[2 * npg:]
    j = pl.program_id(1)
    nh = N_HEADS
    dn = (((1,), (1,)), ((), ()))

    @pl.when(j == 0)
    def _():
        qpad_ref[...] = jnp.zeros_like(qpad_ref)
        qpad_ref[:nh, :] = q_ref[...]
        m_ref[...] = jnp.full_like(m_ref, -jnp.inf)
        l_ref[...] = jnp.zeros_like(l_ref)
        acc_ref[...] = jnp.zeros_like(acc_ref)

    qlat = qpad_ref[:, :KV_LORA]
    qpe = qpad_ref[:, KV_LORA:]
    kcs = []
    sts = []
    for i in range(npg):
        kc = ckv_refs[i][...].astype(BF16)
        kp = kpe_refs[i][...].astype(BF16)
        s = (lax.dot_general(kc, qlat, dn, preferred_element_type=F32)
             + lax.dot_general(kp, qpe, dn, preferred_element_type=F32))
        sts.append(s.T[:nh, :])
        kcs.append(kc)
    s = jnp.concatenate(sts, axis=1) * ATTN_SCALE
    m_prev = m_ref[...]
    m_next = jnp.maximum(m_prev, jnp.max(s, axis=1, keepdims=True))
    alpha = jnp.exp(m_prev - m_next)
    p = jnp.exp(s - jnp.tile(m_next, (1, s.shape[1] // LANES)))
    l_ref[...] = alpha * l_ref[...] + jnp.sum(p, axis=1, keepdims=True)
    pb = p.astype(BF16)
    pv = jnp.dot(pb[:, :PAGE_SIZE], kcs[0], preferred_element_type=F32)
    for i in range(1, npg):
        pv = pv + jnp.dot(pb[:, i * PAGE_SIZE:(i + 1) * PAGE_SIZE], kcs[i], preferred_element_type=F32)
    acc_ref[...] = jnp.tile(alpha, (1, KV_LORA // LANES)) * acc_ref[...] + pv
    m_ref[...] = m_next

    @pl.when(j == pl.num_programs(1) - 1)
    def _():
        q = q_ref[...].astype(F32)
        cnew = cnew_ref[...]
        s_new = (jnp.sum(q[:, :KV_LORA] * cnew, axis=1, keepdims=True)
                 + jnp.sum(q[:, KV_LORA:] * pnew_ref[...], axis=1, keepdims=True)) * ATTN_SCALE
        m_p = m_ref[...]
        m_n = jnp.maximum(m_p, s_new)
        a = jnp.exp(m_p - m_n)
        p_new = jnp.exp(s_new - m_n)
        l_fin = a * l_ref[...] + p_new
        acc = (jnp.tile(a, (1, KV_LORA // LANES)) * acc_ref[...]
               + jnp.tile(p_new, (1, KV_LORA // LANES)) * cnew)
        o_ref[...] = (acc / jnp.tile(l_fin, (1, KV_LORA // LANES))).astype(BF16)


def _paged_attn(qcat, cache_ckv, cache_kpe, page_table, ckv_new, kpe_new, npg=8):
    b, nh, w = qcat.shape
    r = KV_LORA
    n_pages = page_table.shape[1]
    ckv_specs = [pl.BlockSpec((None, PAGE_SIZE, r), functools.partial(
        lambda bi, j, pt, i: (pt[bi, j * npg + i], 0, 0), i=i)) for i in range(npg)]
    kpe_specs = [pl.BlockSpec((None, PAGE_SIZE, QK_ROPE), functools.partial(
        lambda bi, j, pt, i: (pt[bi, j * npg + i], 0, 0), i=i)) for i in range(npg)]
    grid_spec = pltpu.PrefetchScalarGridSpec(
        num_scalar_prefetch=1,
        grid=(b, n_pages // npg),
        in_specs=[pl.BlockSpec((None, nh, w), lambda bi, j, pt: (bi, 0, 0))] + ckv_specs + kpe_specs + [
            pl.BlockSpec((None, 1, r), lambda bi, j, pt: (bi, 0, 0)),
            pl.BlockSpec((None, 1, QK_ROPE), lambda bi, j, pt: (bi, 0, 0)),
        ],
        out_specs=pl.BlockSpec((None, nh, r), lambda bi, j, pt: (bi, 0, 0)),
        scratch_shapes=[pltpu.VMEM((LANES, w), BF16), pltpu.VMEM((nh, LANES), F32),
                        pltpu.VMEM((nh, LANES), F32), pltpu.VMEM((nh, r), F32)],
    )
    return pl.pallas_call(
        functools.partial(_paged_kernel, npg=npg),
        grid_spec=grid_spec,
        out_shape=jax.ShapeDtypeStruct((b, nh, r), BF16),
        compiler_params=_params("arbitrary", "arbitrary"),
        name="paged_attn",
    )(page_table, qcat, *([cache_ckv] * npg), *([cache_kpe] * npg),
      ckv_new.reshape(b, 1, r), kpe_new.reshape(b, 1, QK_ROPE))


def _ovup_kernel(o_ref, wv_ref, a_ref):
    a_ref[...] = jnp.dot(o_ref[...], wv_ref[...].astype(BF16), preferred_element_type=F32).astype(BF16)


def _ovup(olat, w_uv2):
    nh, b, r = olat.shape
    return pl.pallas_call(
        _ovup_kernel,
        grid=(nh,),
        in_specs=[pl.BlockSpec((None, b, r), lambda h: (h, 0, 0)), pl.BlockSpec((r, V_HEAD), lambda h: (0, h))],
        out_specs=pl.BlockSpec((b, V_HEAD), lambda h: (0, h)),
        out_shape=jax.ShapeDtypeStruct((b, nh * V_HEAD), BF16),
        compiler_params=_params("arbitrary"),
        name="o_latent_up",
    )(olat, w_uv2)


def _rope_tables(pos):
    inv = ROPE_THETA ** (-jnp.arange(0, QK_ROPE, 2, dtype=F32) / QK_ROPE)
    ang = pos.astype(F32)[:, None] * inv[None, :]
    cos, sin = jnp.cos(ang), jnp.sin(ang)
    return jnp.concatenate([cos, cos], axis=1), jnp.concatenate([-sin, sin], axis=1)


def _trunk(x, mods, mods_kv, rows_per_b, tm, cos2, sin2, P, s5_fn, attn_fn):
    z, st_re, st_im = s5_fn(x)
    x = _proj_res(z, P["w_glu"][0], x, mods[0], 2, rows_per_b, tm, 256, glu=True)
    h2, gates_t = _route(x, P["g_norm"][0, 1], mods[0], 4, 3, rows_per_b, min(tm, 512),
                         P["w_router"], P["b_router"])
    x = _moe(h2, gates_t.T, P["w_gate"], P["w_up"], P["w_down"], 0, x, mods[0], 5, rows_per_b, min(tm, 512))
    ckv, kpe = _kv(x, P["g_kv"], mods_kv, rows_per_b, min(tm, 512), P["w_dkv"], P["g_ckv"], cos2, sin2)
    q = _qproj(x, P["g_norm"][1, 0], mods[1], 1, 0, rows_per_b, tm, P["w_qt"], cos2, sin2)
    attn = attn_fn(q, ckv, kpe)
    x = _proj_res(attn, P["w_o"][0], x, mods[1], 2, rows_per_b, tm, 512, glu=False)
    h2, gates_t = _route(x, P["g_norm"][1, 1], mods[1], 4, 3, rows_per_b, min(tm, 512),
                         P["w_router"], P["b_router"])
    x = _moe(h2, gates_t.T, P["w_gate"], P["w_up"], P["w_down"], 1, x, mods[1], 5, rows_per_b, min(tm, 512))
    y = _final_norm(x, P["g_final"], min(tm, 512))
    return y, st_re, st_im, ckv, kpe


def kernel(x_prompt, x_sample, c_prompt, c_sample, state_ssm_re, state_ssm_im, cache_ckv, cache_kpe, page_table, g_norm, w_ada, b_ada, ssm_lam_re, ssm_lam_im, ssm_log_dt, ssm_b_re, ssm_b_im, ssm_c_re, ssm_c_im, ssm_d, w_glu, g_kv, w_ada_kv, b_ada_kv, w_dkv, g_ckv, w_uk, w_uv, w_q, w_o, w_router, b_router, w_gate, w_up, w_down, g_final):
    bsz, seq, d = x_prompt.shape
    db = x_sample.shape[0]
    g = d // SSM_GROUP
    n = SSM_STATE
    past = page_table.shape[1] * PAGE_SIZE
    P = dict(g_norm=g_norm, w_glu=w_glu, g_kv=g_kv, w_dkv=w_dkv, g_ckv=g_ckv, w_o=w_o, w_router=w_router,
             b_router=b_router, w_gate=w_gate, w_up=w_up, w_down=w_down, g_final=g_final)
    P["w_qt"] = w_q[0].reshape(d, N_HEADS, QK_HEAD).transpose(1, 0, 2)
    w_uk2 = w_uk.reshape(KV_LORA, N_HEADS * QK_NOPE)
    w_uv2 = w_uv.reshape(KV_LORA, N_HEADS * V_HEAD)

    m_rows = db + bsz
    m_pad = -(-m_rows // 16) * 16
    c_all = jnp.concatenate([c_sample, c_prompt, jnp.zeros((m_pad - m_rows, d), F32)], axis=0)
    mod = _ada(c_all, w_ada, b_ada)
    mod_kv = _ada(c_all, w_ada_kv[None], b_ada_kv[None])
    mods_s = [mod[l, :db][None] for l in range(mod.shape[0])]
    mods_p = [mod[l, db:m_rows][:, None, :] for l in range(mod.shape[0])]
    modkv_s = mod_kv[0, :db][None]
    modkv_p = mod_kv[0, db:m_rows][:, None, :]

    lam_re = ssm_lam_re[0].reshape(g, 1, n)
    lam_im = ssm_lam_im[0].reshape(g, 1, n)
    log_dt = jnp.broadcast_to(ssm_log_dt[0].reshape(g, 1, 1), (g, 1, n))
    bt_re = ssm_b_re[0].transpose(0, 2, 1)
    bt_im = ssm_b_im[0].transpose(0, 2, 1)
    c_re, c_im = ssm_c_re[0], ssm_c_im[0]
    d_g = ssm_d[0].reshape(g, 1, SSM_GROUP)
    d_rep = jnp.tile(d_g, (1, 1, S5_CHUNK))
    nchunk = seq // S5_CHUNK
    tm_p = 1024

    def s5_prompt(x):
        h = _normmod_call(x, g_norm[0, 0], mods_p[0], 1, 0, seq, tm_p, BF16)
        hx = h.reshape(bsz, nchunk, S5_CHUNK, g, SSM_GROUP).transpose(3, 1, 0, 2, 4)
        hx = hx.reshape(g, nchunk * bsz, S5_CHUNK * SSM_GROUP)
        zg, st_re, st_im = _s5_prompt(hx, lam_re, lam_im, log_dt, bt_re, bt_im, c_re, c_im, d_rep, bsz)
        z = zg.reshape(g, nchunk, bsz, S5_CHUNK, SSM_GROUP).transpose(2, 1, 3, 0, 4).reshape(bsz * seq, d)
        return z, st_re.transpose(1, 0, 2)[None], st_im.transpose(1, 0, 2)[None]

    def s5_sample(x):
        h = _normmod_call(x, g_norm[0, 0], mods_s[0], 1, 0, db, db, F32)
        u = h.reshape(db, g, SSM_GROUP).transpose(1, 0, 2)
        h0_re = state_ssm_re[0].transpose(1, 0, 2)
        h0_im = state_ssm_im[0].transpose(1, 0, 2)
        zg, nr, ni = _s5_sample(u, h0_re, h0_im, lam_re, lam_im, log_dt, bt_re, bt_im, c_re, c_im, d_g)
        return zg.transpose(1, 0, 2).reshape(db, d), nr.transpose(1, 0, 2)[None], ni.transpose(1, 0, 2)[None]

    def attn_prompt(q, ckv, kpe):
        k, v = _kvup(ckv, kpe, w_uk2, w_uv2, bsz, seq, tm_p)
        return _flash(q, k, v)

    def attn_sample(q, ckv, kpe):
        qcat = _qlat(q[0], w_uk2).transpose(1, 0, 2)
        olat = _paged_attn(qcat, cache_ckv, cache_kpe, page_table, ckv, kpe)
        return _ovup(olat.transpose(1, 0, 2), w_uv2)

    cos_p, sin_p = _rope_tables(jnp.arange(seq))
    cos_s, sin_s = _rope_tables(jnp.full((db,), past))

    y_p, sre_p, sim_p, ckv_p, kpe_p = _trunk(
        x_prompt.reshape(bsz * seq, d), mods_p, modkv_p, seq, tm_p, cos_p, sin_p, P, s5_prompt, attn_prompt)
    y_s, sre_s, sim_s, ckv_s, kpe_s = _trunk(
        x_sample.reshape(db, d), mods_s, modkv_s, db, db, cos_s, sin_s, P, s5_sample, attn_sample)
    return (y_p.reshape(bsz, seq, d), y_s.reshape(db, 1, d), sre_p, sim_p,
            ckv_p.reshape(bsz, seq, KV_LORA), kpe_p.reshape(bsz, seq, QK_ROPE),
            sre_s, sim_s, ckv_s.reshape(db, 1, KV_LORA), kpe_s.reshape(db, 1, QK_ROPE))
```

```python
import functools
import math

import jax
import jax.numpy as jnp
from jax import lax
from jax.experimental import pallas as pl
from jax.experimental.pallas import tpu as pltpu

F32 = jnp.float32
BF16 = jnp.bfloat16
I32 = jnp.int32

SSM_GROUP = 16
SSM_STATE = 64
N_HEADS = 16
QK_NOPE = 128
QK_ROPE = 64
V_HEAD = 128
KV_LORA = 512
ROPE_THETA = 10000.0
PAGE_SIZE = 128
N_EXPERTS = 16
N_EXPERT_GROUPS = 4
EXPERTS_PER_GROUP = N_EXPERTS // N_EXPERT_GROUPS
TOP_K = 2
EPS = 1e-6
ATTN_SCALE = (QK_NOPE + QK_ROPE) ** -0.5
QK_HEAD = QK_NOPE + QK_ROPE

V7X_VMEM_BYTES = 64 * 1024 * 1024
VMEM_LIMIT_BYTES = V7X_VMEM_BYTES - 8 * 1024 * 1024
LANES = 128
SMEM_I32_BLOCK = 1024
S5_CHUNK = 16
DN_LAST = (((1,), (1,)), ((), ()))


def _params(*sem):
    return pltpu.CompilerParams(dimension_semantics=sem, vmem_limit_bytes=VMEM_LIMIT_BYTES)


def _normmod(x, g, sc, sh):
    ms = jnp.mean(x * x, axis=-1, keepdims=True)
    return (x * lax.rsqrt(ms + EPS)) * g * (1.0 + sc) + sh


def _rope(p, cos2, sin2):
    half = QK_ROPE // 2
    swapped = jnp.concatenate([p[:, half:], p[:, :half]], axis=1)
    return p * cos2 + swapped * sin2


def _mod_spec(mods, chunk, tm, rows_per_b, width, j_blocks=None):
    _, r, _ = mods.shape
    if j_blocks is None:
        return pl.BlockSpec((None, r, width), lambda i, j: ((i * tm) // rows_per_b, 0, chunk))
    return pl.BlockSpec((None, r, width), lambda i, j: ((i * tm) // rows_per_b, 0, chunk * j_blocks + j))


def _ada_kernel(c_ref, w_ref, b_ref, o_ref, cb_ref):
    @pl.when((pl.program_id(0) == 0) & (pl.program_id(1) == 0))
    def _():
        c = c_ref[...]
        cb_ref[...] = (c * jax.nn.sigmoid(c)).astype(BF16)

    acc = jnp.dot(cb_ref[...], w_ref[...].astype(BF16), preferred_element_type=F32)
    o_ref[...] = acc + b_ref[...]


def _ada(c_all, w, b, tn=1024):
    m, d = c_all.shape
    lw, _, n = w.shape
    return pl.pallas_call(
        _ada_kernel,
        grid=(lw, n // tn),
        in_specs=[
            pl.BlockSpec((m, d), lambda l, j: (0, 0)),
            pl.BlockSpec((None, d, tn), lambda l, j: (l, 0, j)),
            pl.BlockSpec((None, 1, tn), lambda l, j: (l, 0, j)),
        ],
        out_specs=pl.BlockSpec((None, m, tn), lambda l, j: (l, 0, j)),
        out_shape=jax.ShapeDtypeStruct((lw, m, n), F32),
        scratch_shapes=[pltpu.VMEM((m, d), BF16)],
        compiler_params=_params("arbitrary", "arbitrary"),
        name="ada_mod",
    )(c_all, w, b.reshape(lw, 1, n))


def _normmod_kernel(x_ref, g_ref, sc_ref, sh_ref, o_ref):
    o_ref[...] = _normmod(x_ref[...], g_ref[...], sc_ref[...], sh_ref[...]).astype(o_ref.dtype)


def _normmod_call(x, g, mods, sc_chunk, sh_chunk, rows_per_b, tm, out_dtype):
    t, d = x.shape
    return pl.pallas_call(
        _normmod_kernel,
        grid=(t // tm, 1),
        in_specs=[
            pl.BlockSpec((tm, d), lambda i, j: (i, 0)),
            pl.BlockSpec((1, d), lambda i, j: (0, 0)),
            _mod_spec(mods, sc_chunk, tm, rows_per_b, d),
            _mod_spec(mods, sh_chunk, tm, rows_per_b, d),
        ],
        out_specs=pl.BlockSpec((tm, d), lambda i, j: (i, 0)),
        out_shape=jax.ShapeDtypeStruct((t, d), out_dtype),
        compiler_params=_params("arbitrary", "arbitrary"),
        name="norm_mod",
    )(x, g.reshape(1, d), mods, mods)


def _final_norm_kernel(x_ref, g_ref, o_ref):
    x = x_ref[...]
    ms = jnp.mean(x * x, axis=-1, keepdims=True)
    o_ref[...] = (x * lax.rsqrt(ms + EPS)) * g_ref[...]


def _final_norm(x, g, tm):
    t, d = x.shape
    return pl.pallas_call(
        _final_norm_kernel,
        grid=(t // tm,),
        in_specs=[pl.BlockSpec((tm, d), lambda i: (i, 0)), pl.BlockSpec((1, d), lambda i: (0, 0))],
        out_specs=pl.BlockSpec((tm, d), lambda i: (i, 0)),
        out_shape=jax.ShapeDtypeStruct((t, d), F32),
        compiler_params=_params("arbitrary"),
        name="final_norm",
    )(x, g.reshape(1, d))


def _proj_res_kernel(x_ref, *refs, glu, grouped, tm):
    refs = list(refs)
    xb_ref = refs.pop() if grouped else None
    if glu:
        wa_ref, wb_ref, res_ref, gt_ref, o_ref = refs
    else:
        wa_ref, res_ref, gt_ref, o_ref = refs
    if grouped:
        @pl.when(pl.program_id(1) == 0)
        def _():
            cg = SSM_GROUP
            for k in range(tm // LANES):
                zt = x_ref[:, k * LANES * cg:(k + 1) * LANES * cg].T
                xb_ref[k * LANES:(k + 1) * LANES, :] = zt.reshape(LANES, cg, LANES).reshape(LANES, cg * LANES)

        x = xb_ref[...]
    else:
        x = x_ref[...]
    y = jnp.dot(x, wa_ref[...].astype(BF16), preferred_element_type=F32)
    if glu:
        yb = jnp.dot(x, wb_ref[...].astype(BF16), preferred_element_type=F32)
        y = y * jax.nn.sigmoid(yb)
    o_ref[...] = res_ref[...] + gt_ref[...] * y


def _proj_res(x, w, res, mods, gt_chunk, rows_per_b, tm, tn, glu, grouped=False):
    t, n = res.shape
    k = w.shape[0]
    nj = n // tn
    if grouped:
        x_spec = pl.BlockSpec((x.shape[0], tm * SSM_GROUP), lambda i, j: (0, i))
    else:
        x_spec = pl.BlockSpec((tm, k), lambda i, j: (i, 0))
    in_specs = [x_spec, pl.BlockSpec((k, tn), lambda i, j: (0, j))]
    args = [x, w]
    if glu:
        in_specs.append(pl.BlockSpec((k, tn), lambda i, j: (0, nj + j)))
        args.append(w)
    in_specs += [pl.BlockSpec((tm, tn), lambda i, j: (i, j)), _mod_spec(mods, gt_chunk, tm, rows_per_b, tn, nj)]
    args += [res, mods]
    return pl.pallas_call(
        functools.partial(_proj_res_kernel, glu=glu, grouped=grouped, tm=tm),
        grid=(t // tm, nj),
        in_specs=in_specs,
        out_specs=pl.BlockSpec((tm, tn), lambda i, j: (i, j)),
        out_shape=jax.ShapeDtypeStruct((t, n), F32),
        scratch_shapes=[pltpu.VMEM((tm, k), BF16)] if grouped else [],
        compiler_params=_params("arbitrary", "arbitrary"),
        name="glu_proj" if glu else "out_proj",
    )(*args)


def _s5in_kernel(x_ref, g_ref, sc_ref, sh_ref, p_ref, o_ref):
    h = _normmod(x_ref[...], g_ref[...], sc_ref[...], sh_ref[...]).astype(BF16)
    hp = jnp.dot(h, p_ref[...], preferred_element_type=F32).astype(BF16)
    tm = hp.shape[0]
    o_ref[...] = hp.reshape(tm, SSM_GROUP, LANES).reshape(tm * SSM_GROUP, LANES).T


def _s5_in(x, g, mods, sc_chunk, sh_chunk, rows_per_b, tm, perm):
    t, d = x.shape
    ng = d // SSM_GROUP
    return pl.pallas_call(
        _s5in_kernel,
        grid=(t // tm, 1),
        in_specs=[
            pl.BlockSpec((tm, d), lambda i, j: (i, 0)),
            pl.BlockSpec((1, d), lambda i, j: (0, 0)),
            _mod_spec(mods, sc_chunk, tm, rows_per_b, d),
            _mod_spec(mods, sh_chunk, tm, rows_per_b, d),
            pl.BlockSpec((d, d), lambda i, j: (0, 0)),
        ],
        out_specs=pl.BlockSpec((ng, tm * SSM_GROUP), lambda i, j: (0, i)),
        out_shape=jax.ShapeDtypeStruct((ng, t * SSM_GROUP), BF16),
        compiler_params=_params("arbitrary", "arbitrary"),
        name="s5_in",
    )(x, g.reshape(1, d), mods, mods, perm)


def _cpow(lr_dt, li_dt, k):
    mag = jnp.exp(k * lr_dt)
    ang = k * li_dt
    return mag * jnp.cos(ang), mag * jnp.sin(ang)


def _rep_rows(e, n):
    q, w = e.shape
    return jnp.concatenate([jnp.broadcast_to(e[t:t + 1], (n, w)) for t in range(q)], axis=0)


def _tile_rows(c, n):
    return jnp.concatenate([c] * n, axis=0)


def _zoh(lr, li, ldt):
    dt = jnp.exp(ldt)
    lr_dt = lr * dt
    li_dt = li * dt
    mag = jnp.exp(lr_dt)
    ab_re = mag * jnp.cos(li_dt)
    ab_im = mag * jnp.sin(li_dt)
    den = lr * lr + li * li
    f_re = ((ab_re - 1.0) * lr + ab_im * li) / den
    f_im = (ab_im * lr - (ab_re - 1.0) * li) / den
    return lr_dt, li_dt, ab_re, ab_im, f_re, f_im


def _gelu_tanh(y):
    return 0.5 * y * (1.0 + jnp.tanh(math.sqrt(2.0 / math.pi) * (y + 0.044715 * (y * y * y))))


def _s5_prompt_kernel(x_ref, lr_ref, li_ref, ldt_ref, bt_ref, bts_ref, c_ref, cs_ref, d_ref,
                      z_ref, st_ref, s_sc, ss_sc, hp_sc, yl_sc, *, gb, nb, nchunk):
    q, cg, n = S5_CHUNK, SSM_GROUP, SSM_STATE
    half = q // 2
    ti = lax.broadcasted_iota(I32, (q, 1), 0).astype(F32)
    rows = lax.broadcasted_iota(I32, (q * cg, q * cg), 0)
    cols = lax.broadcasted_iota(I32, (q * cg, q * cg), 1)
    shift = cg.bit_length() - 1
    causal = (rows >> shift) <= (cols >> shift)
    diag = rows == cols
    lane = lax.broadcasted_iota(I32, (1, 2 * n), 1)
    sgn = jnp.where(lane < n, -1.0, 1.0)

    def cmul(x2, xs, er, ei):
        return x2 * er + (xs * sgn) * ei

    a_terms = []
    vcs = []
    for g in range(gb):
        lr_dt, li_dt, _, _, f_re, f_im = _zoh(lr_ref[g], li_ref[g], ldt_ref[g])
        bt2, bts = bt_ref[g], bts_ref[g]
        bb2 = cmul(bt2, bts, f_re, f_im)
        bbs = bts * f_re - (bt2 * sgn) * f_im
        bb2t, bbst = _tile_rows(bb2, q), _tile_rows(bbs, q)
        c2t, cst = _tile_rows(c_ref[g], q), _tile_rows(cs_ref[g], q)

        def table(k, lr_dt=lr_dt, li_dt=li_dt):
            er, ei = _cpow(lr_dt, li_dt, k)
            return _rep_rows(er, cg), _rep_rows(ei, cg)

        pc = cmul(c2t, cst, *table(ti - half)) * (-sgn)
        pb = cmul(bb2t, bbst, *table(half - ti))
        tt = lax.dot_general(pb.astype(BF16), pc.astype(BF16), DN_LAST, preferred_element_type=F32)
        tt = jnp.where(causal, tt, 0.0) + jnp.where(diag, d_ref[g], 0.0)
        w1 = cmul(bb2t, bbst, *table((q - 1) - ti))
        vcs.append((cmul(c2t, cst, *table(ti + 1.0)) * (-sgn)).astype(BF16))
        ar, ai = _cpow(lr_dt, li_dt, float(q))
        a_terms.append((ar, ai * sgn))
        x = x_ref[g]
        yl_sc[g] = jnp.dot(x, tt.astype(BF16), preferred_element_type=F32)
        s = jnp.dot(x, w1.astype(BF16), preferred_element_type=F32)
        s_sc[g] = s
        ss_sc[g] = pltpu.roll(s, n, axis=1)

    def carry(ch, hs):
        new = []
        for g in range(gb):
            h, hsw = hs[2 * g], hs[2 * g + 1]
            hp_sc[g, pl.ds(ch, nb, stride=nchunk), :] = h
            a1, a2 = a_terms[g]
            new.append(a1 * h + a2 * hsw + s_sc[g, pl.ds(ch, nb, stride=nchunk), :])
            new.append(a1 * hsw - a2 * h + ss_sc[g, pl.ds(ch, nb, stride=nchunk), :])
        return tuple(new)

    zero = jnp.zeros((nb, 2 * n), F32)
    hs = lax.fori_loop(0, nchunk, carry, (zero,) * (2 * gb), unroll=8)
    for g in range(gb):
        st_ref[g] = hs[2 * g]
        y = yl_sc[g] + lax.dot_general(hp_sc[g].astype(BF16), vcs[g], DN_LAST, preferred_element_type=F32)
        z_ref[g] = _gelu_tanh(y).astype(BF16)


def _s5_prompt(hx, lam_re2, lam_im2, log_dt2, bt2, bts, c2, cs, d_rep, nb, gb=2):
    g, r, w = hx.shape
    n2 = 2 * SSM_STATE
    nchunk = r // nb
    vec = lambda: pl.BlockSpec((gb, 1, n2), lambda i: (i, 0, 0))
    mat = lambda: pl.BlockSpec((gb, SSM_GROUP, n2), lambda i: (i, 0, 0))
    return pl.pallas_call(
        functools.partial(_s5_prompt_kernel, gb=gb, nb=nb, nchunk=nchunk),
        grid=(g // gb,),
        in_specs=[pl.BlockSpec((gb, r, w), lambda i: (i, 0, 0)), vec(), vec(), vec(),
                  mat(), mat(), mat(), mat(), pl.BlockSpec((gb, 1, w), lambda i: (i, 0, 0))],
        out_specs=[pl.BlockSpec((gb, r, w), lambda i: (i, 0, 0)),
                   pl.BlockSpec((gb, nb, n2), lambda i: (i, 0, 0))],
        out_shape=[jax.ShapeDtypeStruct((g, r, w), BF16),
                   jax.ShapeDtypeStruct((g, nb, n2), F32)],
        scratch_shapes=[pltpu.VMEM((gb, r, n2), F32), pltpu.VMEM((gb, r, n2), F32),
                        pltpu.VMEM((gb, r, n2), F32), pltpu.VMEM((gb, r, w), F32)],
        compiler_params=_params("arbitrary"),
        name="s5_prompt",
    )(hx, lam_re2, lam_im2, log_dt2, bt2, bts, c2, cs, d_rep)


def _s5_sample_kernel(u_ref, h0r_ref, h0i_ref, lr_ref, li_ref, ldt_ref, btr_ref, bti_ref,
                      cr_ref, ci_ref, d_ref, z_ref, nr_ref, ni_ref, *, gb):
    hp = lax.Precision.HIGHEST
    for g in range(gb):
        _, _, ab_re, ab_im, f_re, f_im = _zoh(lr_ref[g], li_ref[g], ldt_ref[g])
        bb_re = f_re * btr_ref[g] - f_im * bti_ref[g]
        bb_im = f_re * bti_ref[g] + f_im * btr_ref[g]
        u = u_ref[g]
        bu_re = jnp.dot(u, bb_re, preferred_element_type=F32, precision=hp)
        bu_im = jnp.dot(u, bb_im, preferred_element_type=F32, precision=hp)
        h0r = h0r_ref[g]
        h0i = h0i_ref[g]
        hr = bu_re + ab_re * h0r - ab_im * h0i
        hi = bu_im + ab_re * h0i + ab_im * h0r
        nr_ref[g] = hr
        ni_ref[g] = hi
        y = (lax.dot_general(hr, cr_ref[g], DN_LAST, preferred_element_type=F32, precision=hp)
             - lax.dot_general(hi, ci_ref[g], DN_LAST, preferred_element_type=F32, precision=hp))
        y = y + d_ref[g] * u
        z_ref[g] = _gelu_tanh(y).astype(BF16)


def _s5_sample(u, h0_re, h0_im, lam_re, lam_im, log_dt, bt_re, bt_im, c_re, c_im, d_g, gb=8):
    g, b, cg = u.shape
    n = SSM_STATE
    vec = lambda: pl.BlockSpec((gb, 1, n), lambda i: (i, 0, 0))
    mat = lambda: pl.BlockSpec((gb, cg, n), lambda i: (i, 0, 0))
    st = lambda: pl.BlockSpec((gb, b, n), lambda i: (i, 0, 0))
    return pl.pallas_call(
        functools.partial(_s5_sample_kernel, gb=gb),
        grid=(g // gb,),
        in_specs=[pl.BlockSpec((gb, b, cg), lambda i: (i, 0, 0)), st(), st(), vec(), vec(), vec(),
                  mat(), mat(), mat(), mat(), pl.BlockSpec((gb, 1, cg), lambda i: (i, 0, 0))],
        out_specs=[pl.BlockSpec((gb, b, cg), lambda i: (i, 0, 0)), st(), st()],
        out_shape=[jax.ShapeDtypeStruct((g, b, cg), BF16),
                   jax.ShapeDtypeStruct((g, b, n), F32),
                   jax.ShapeDtypeStruct((g, b, n), F32)],
        compiler_params=_params("arbitrary"),
        name="s5_sample",
    )(u, h0_re, h0_im, lam_re, lam_im, log_dt, bt_re, bt_im, c_re, c_im, d_g)


def _route_kernel(x_ref, g_ref, sc_ref, sh_ref, wr_ref, br_ref, h_ref, gates_ref, e_ref, w_ref, *, split):
    h = _normmod(x_ref[...], g_ref[...], sc_ref[...], sh_ref[...])
    hb = h.astype(BF16)
    h_ref[...] = hb.reshape(h_ref.shape) if split else hb
    logits = lax.dot_general(wr_ref[...], h, DN_LAST, preferred_element_type=F32,
                             precision=lax.Precision.HIGHEST)
    s = jax.nn.sigmoid(logits)
    sel = s + br_ref[...]
    epg = EXPERTS_PER_GROUP
    srow = [s[e:e + 1, :] for e in range(N_EXPERTS)]
    row = [sel[e:e + 1, :] for e in range(N_EXPERTS)]

    def top2_sum(a, b, c, d):
        hi1, lo1 = jnp.maximum(a, b), jnp.minimum(a, b)
        hi2, lo2 = jnp.maximum(c, d), jnp.minimum(c, d)
        return jnp.maximum(hi1, hi2) + jnp.maximum(jnp.minimum(hi1, hi2), jnp.maximum(lo1, lo2))

    gs = [top2_sum(*row[epg * g:epg * (g + 1)]) for g in range(N_EXPERT_GROUPS)]
    best = gs[0]
    gidx = jnp.zeros_like(best, dtype=I32)
    for g in range(1, N_EXPERT_GROUPS):
        better = gs[g] > best
        gidx = jnp.where(better, g, gidx)
        best = jnp.where(better, gs[g], best)

    def pick(rows_, j):
        v = rows_[j]
        for g in range(1, N_EXPERT_GROUPS):
            v = jnp.where(gidx == g, rows_[epg * g + j], v)
        return v

    v = [pick(row, j) for j in range(epg)]
    u = [pick(srow, j) for j in range(epg)]

    def argmax_first(vals):
        bv, bi = vals[0], jnp.zeros_like(gidx)
        for j in range(1, epg):
            better = vals[j] > bv
            bi = jnp.where(better, j, bi)
            bv = jnp.where(better, vals[j], bv)
        return bi

    i1 = argmax_first(v)
    i2 = argmax_first([jnp.where(i1 == j, -jnp.inf, v[j]) for j in range(epg)])

    def take(vals, idx):
        out = vals[0]
        for j in range(1, epg):
            out = jnp.where(idx == j, vals[j], out)
        return out

    w1, w2 = take(u, i1), take(u, i2)
    tot = w1 + w2
    w1, w2 = w1 / tot, w2 / tot
    e1 = gidx * epg + i1
    e2 = gidx * epg + i2
    gates_ref[...] = jnp.concatenate(
        [jnp.where(e1 == e, w1, 0.0) + jnp.where(e2 == e, w2, 0.0) for e in range(N_EXPERTS)], axis=0)
    e_ref[...] = jnp.concatenate([e1, e2], axis=0)
    w_ref[...] = jnp.concatenate([w1, w2], axis=0)


def _route(x, g, mods, sc_chunk, sh_chunk, rows_per_b, tm, w_router_t, b_router, split):
    t, d = x.shape
    e = N_EXPERTS
    if split:
        h_spec = pl.BlockSpec((tm, d // LANES, LANES), lambda i, j: (i, 0, 0))
        h_shape = jax.ShapeDtypeStruct((t, d // LANES, LANES), BF16)
    else:
        h_spec = pl.BlockSpec((tm, d), lambda i, j: (i, 0))
        h_shape = jax.ShapeDtypeStruct((t, d), BF16)
    return pl.pallas_call(
        functools.partial(_route_kernel, split=split),
        grid=(t // tm, 1),
        in_specs=[
            pl.BlockSpec((tm, d), lambda i, j: (i, 0)),
            pl.BlockSpec((1, d), lambda i, j: (0, 0)),
            _mod_spec(mods, sc_chunk, tm, rows_per_b, d),
            _mod_spec(mods, sh_chunk, tm, rows_per_b, d),
            pl.BlockSpec((e, d), lambda i, j: (0, 0)),
            pl.BlockSpec((e, 1), lambda i, j: (0, 0)),
        ],
        out_specs=[h_spec, pl.BlockSpec((e, tm), lambda i, j: (0, i)),
                   pl.BlockSpec((TOP_K, tm), lambda i, j: (0, i)), pl.BlockSpec((TOP_K, tm), lambda i, j: (0, i))],
        out_shape=[h_shape, jax.ShapeDtypeStruct((e, t), F32),
                   jax.ShapeDtypeStruct((TOP_K, t), I32), jax.ShapeDtypeStruct((TOP_K, t), F32)],
        compiler_params=_params("arbitrary", "arbitrary"),
        name="moe_route",
    )(x, g.reshape(1, d), mods, mods, w_router_t, b_router.reshape(e, 1))


def _moe_kernel(h_ref, gates_ref, wg_ref, wu_ref, wd_ref, res_ref, gt_ref, o_ref, acc_ref):
    e = pl.program_id(1)
    first = (e == 0) & (pl.program_id(2) == 0)
    last = (e == pl.num_programs(1) - 1) & (pl.program_id(2) == pl.num_programs(2) - 1)

    @pl.when(first)
    def _():
        acc_ref[...] = jnp.zeros_like(acc_ref)

    h = h_ref[...]
    a = jnp.dot(h, wg_ref[...].astype(BF16), preferred_element_type=F32)
    b = jnp.dot(h, wu_ref[...].astype(BF16), preferred_element_type=F32)
    gates = gates_ref[...]
    lane = lax.broadcasted_iota(I32, gates.shape, 1)
    gcol = jnp.sum(jnp.where(lane == e, gates, 0.0), axis=1, keepdims=True)
    hid = (a * jax.nn.sigmoid(a)) * b * gcol
    acc_ref[...] += jnp.dot(hid.astype(BF16), wd_ref[...].astype(BF16), preferred_element_type=F32)

    @pl.when(last)
    def _():
        o_ref[...] = res_ref[...] + gt_ref[...] * acc_ref[...]


def _moe(h, gates, w_gate, w_up, w_down, layer, res, mods, gt_chunk, rows_per_b, tm, fsplit=2):
    t, d = h.shape
    e, _, f = w_gate.shape[1:]
    tf = f // fsplit
    r = mods.shape[1]
    return pl.pallas_call(
        _moe_kernel,
        grid=(t // tm, e, fsplit),
        in_specs=[
            pl.BlockSpec((tm, d), lambda i, j, k: (i, 0)),
            pl.BlockSpec((tm, e), lambda i, j, k: (i, 0)),
            pl.BlockSpec((None, None, d, tf), lambda i, j, k: (layer, j, 0, k)),
            pl.BlockSpec((None, None, d, tf), lambda i, j, k: (layer, j, 0, k)),
            pl.BlockSpec((None, None, tf, d), lambda i, j, k: (layer, j, k, 0)),
            pl.BlockSpec((tm, d), lambda i, j, k: (i, 0)),
            pl.BlockSpec((None, r, d), lambda i, j, k: ((i * tm) // rows_per_b, 0, gt_chunk)),
        ],
        out_specs=pl.BlockSpec((tm, d), lambda i, j, k: (i, 0)),
        out_shape=jax.ShapeDtypeStruct((t, d), F32),
        scratch_shapes=[pltpu.VMEM((tm, d), F32)],
        compiler_params=_params("arbitrary", "arbitrary", "arbitrary"),
        name="moe_experts",
    )(h, gates, w_gate, w_up, w_down, res, mods)


def _row_gather_kernel(idx_ref, src_ref, dst_ref, sem, *, rows):
    base = pl.program_id(0) * rows

    def issue(r, carry):
        pltpu.make_async_copy(src_ref.at[pl.ds(idx_ref[r], 1)], dst_ref.at[pl.ds(base + r, 1)], sem).start()
        return carry

    lax.fori_loop(0, rows, issue, 0, unroll=8)

    def drain(r, carry):
        pltpu.make_async_copy(src_ref.at[pl.ds(0, 1)], dst_ref.at[pl.ds(base + r, 1)], sem).wait()
        return carry

    lax.fori_loop(0, rows, drain, 0, unroll=8)


def _row_gather(idx, src, n_rows):
    rows = SMEM_I32_BLOCK
    return pl.pallas_call(
        functools.partial(_row_gather_kernel, rows=rows),
        grid=(n_rows // rows,),
        in_specs=[pl.BlockSpec((rows,), lambda i: (i,), memory_space=pltpu.SMEM),
                  pl.BlockSpec(memory_space=pl.ANY)],
        out_specs=pl.BlockSpec(memory_space=pl.ANY),
        out_shape=jax.ShapeDtypeStruct((n_rows,) + src.shape[1:], src.dtype),
        scratch_shapes=[pltpu.SemaphoreType.DMA(())],
        compiler_params=_params("arbitrary"),
        name="moe_row_gather",
    )(idx, src)


def _experts_kernel(te_ref, nu_ref, x_ref, w_ref, wg_ref, wu_ref, wd_ref, o_ref, wgb, wub, wdb):
    i = pl.program_id(0)
    prev = te_ref[jnp.maximum(i - 1, 0)]

    @pl.when((i == 0) | (te_ref[i] != prev))
    def _():
        wgb[...] = wg_ref[...].astype(BF16)
        wub[...] = wu_ref[...].astype(BF16)
        wdb[...] = wd_ref[...].astype(BF16)

    @pl.when(i < nu_ref[0])
    def _():
        tg = x_ref.shape[0]
        x = x_ref[...].reshape(tg, wgb.shape[0])
        a = jnp.dot(x, wgb[...], preferred_element_type=F32)
        b = jnp.dot(x, wub[...], preferred_element_type=F32)
        f = a.shape[1]
        hid = (a * jax.nn.sigmoid(a)) * b * jnp.tile(w_ref[...], (1, f // LANES))
        y = jnp.dot(hid.astype(BF16), wdb[...], preferred_element_type=F32)
        o_ref[...] = y.reshape(o_ref.shape)

    @pl.when(i >= nu_ref[0])
    def _():
        o_ref[...] = jnp.zeros_like(o_ref)


def _experts(tile_expert, n_used, xs, ws, w_gate, w_up, w_down, layer, tg):
    rp = xs.shape[0]
    _, _, d, f = w_gate.shape
    grid_spec = pltpu.PrefetchScalarGridSpec(
        num_scalar_prefetch=2,
        grid=(rp // tg,),
        in_specs=[
            pl.BlockSpec((tg,) + xs.shape[1:], lambda i, te, nu: (i, 0, 0)),
            pl.BlockSpec((tg, LANES), lambda i, te, nu: (i, 0)),
            pl.BlockSpec((None, None, d, f), lambda i, te, nu: (layer, te[i], 0, 0)),
            pl.BlockSpec((None, None, d, f), lambda i, te, nu: (layer, te[i], 0, 0)),
            pl.BlockSpec((None, None, f, d), lambda i, te, nu: (layer, te[i], 0, 0)),
        ],
        out_specs=pl.BlockSpec((tg,) + xs.shape[1:], lambda i, te, nu: (i, 0, 0)),
        scratch_shapes=[pltpu.VMEM((d, f), BF16), pltpu.VMEM((d, f), BF16), pltpu.VMEM((f, d), BF16)],
    )
    return pl.pallas_call(
        _experts_kernel,
        grid_spec=grid_spec,
        out_shape=jax.ShapeDtypeStruct(xs.shape, F32),
        compiler_params=_params("arbitrary"),
        name="moe_routed_experts",
    )(tile_expert, n_used, xs, ws, w_gate, w_up, w_down)


def _combine_kernel(pos_ref, ys_ref, res_ref, gt_ref, o_ref, buf, sem, *, tm):
    def issue(r, carry):
        for k in range(TOP_K):
            pltpu.make_async_copy(ys_ref.at[pl.ds(pos_ref[k * tm + r], 1)], buf.at[k, pl.ds(r, 1)], sem).start()
        return carry

    lax.fori_loop(0, tm, issue, 0, unroll=8)

    def drain(r, carry):
        for k in range(TOP_K):
            pltpu.make_async_copy(ys_ref.at[pl.ds(0, 1)], buf.at[k, pl.ds(r, 1)], sem).wait()
        return carry

    lax.fori_loop(0, tm, drain, 0, unroll=8)
    m = buf[0] + buf[1]
    o_ref[...] = res_ref[...] + gt_ref[...] * m.reshape(o_ref.shape)


def _combine(pos, ys, res, mods, gt_chunk, rows_per_b, tm):
    t, d = res.shape
    assert TOP_K * tm == SMEM_I32_BLOCK
    r = mods.shape[1]
    return pl.pallas_call(
        functools.partial(_combine_kernel, tm=tm),
        grid=(t // tm,),
        in_specs=[pl.BlockSpec((TOP_K * tm,), lambda i: (i,), memory_space=pltpu.SMEM),
                  pl.BlockSpec(memory_space=pl.ANY),
                  pl.BlockSpec((tm, d), lambda i: (i, 0)),
                  pl.BlockSpec((None, r, d), lambda i: ((i * tm) // rows_per_b, 0, gt_chunk))],
        out_specs=pl.BlockSpec((tm, d), lambda i: (i, 0)),
        out_shape=jax.ShapeDtypeStruct((t, d), F32),
        scratch_shapes=[pltpu.VMEM((TOP_K, tm) + ys.shape[1:], F32), pltpu.SemaphoreType.DMA(())],
        compiler_params=_params("arbitrary"),
        name="moe_combine",
    )(pos, ys, res, mods)


def _moe_routed(h3, e_rows, w_rows, w_gate, w_up, w_down, layer, res, mods, gt_chunk, rows_per_b,
                tg=256, tmc=512):
    t = h3.shape[0]
    ne = N_EXPERTS
    a = TOP_K * t
    rp = a + ne * tg
    rp = -(-rp // SMEM_I32_BLOCK) * SMEM_I32_BLOCK
    n_tiles = rp // tg
    ea = e_rows.reshape(a)
    wa = w_rows.reshape(a)
    ta = jnp.tile(jnp.arange(t, dtype=I32), TOP_K)
    onehot = (ea[:, None] == jnp.arange(ne, dtype=I32)[None, :]).astype(I32)
    csum = jnp.cumsum(onehot, axis=0)
    rank = jnp.sum((csum - onehot) * onehot, axis=1)
    counts = csum[-1]
    ptiles = (counts + tg - 1) // tg
    tile_end = jnp.cumsum(ptiles)
    tile_start = tile_end - ptiles
    n_used = tile_end[-1:].astype(I32)
    dest = jnp.sum(onehot * tile_start[None, :], axis=1) * tg + rank
    sorted_tok = jnp.zeros((rp,), I32).at[dest].set(ta)
    sorted_w = jnp.zeros((rp,), F32).at[dest].set(wa)
    tile_expert = jnp.sum(jnp.arange(n_tiles, dtype=I32)[:, None] >= tile_end[None, :], axis=1)
    tile_expert = jnp.minimum(tile_expert, ne - 1).astype(I32)
    xs = _row_gather(sorted_tok, h3, rp)
    ws = jnp.broadcast_to(sorted_w[:, None], (rp, LANES))
    ys = _experts(tile_expert, n_used, xs, ws, w_gate, w_up, w_down, layer, tg)
    pos = dest.reshape(TOP_K, t // tmc, tmc).transpose(1, 0, 2).reshape(a)
    return _combine(pos, ys, res, mods, gt_chunk, rows_per_b, tmc)


def _kv_kernel(x_ref, g_ref, sc_ref, sh_ref, w_ref, gc_ref, cos_ref, sin_ref, ckv_ref, kpe_ref, wb_ref):
    @pl.when(pl.program_id(0) == 0)
    def _():
        wb_ref[...] = w_ref[...].astype(BF16)

    h = _normmod(x_ref[...], g_ref[...], sc_ref[...], sh_ref[...]).astype(BF16)
    lat = lax.dot_general(h, wb_ref[...], DN_LAST, preferred_element_type=F32)
    c = lat[:, :KV_LORA]
    ms = jnp.mean(c * c, axis=-1, keepdims=True)
    ckv_ref[...] = (c * lax.rsqrt(ms + EPS)) * gc_ref[...]
    kpe_ref[...] = _rope(lat[:, KV_LORA:], cos_ref[...], sin_ref[...])


def _kv(x, g_kv, mods, rows_per_b, tm, w_dkv_t, g_ckv, cos2, sin2):
    t, d = x.shape
    n = w_dkv_t.shape[0]
    nt = cos2.shape[0] // tm
    return pl.pallas_call(
        _kv_kernel,
        grid=(t // tm, 1),
        in_specs=[
            pl.BlockSpec((tm, d), lambda i, j: (i, 0)),
            pl.BlockSpec((1, d), lambda i, j: (0, 0)),
            _mod_spec(mods, 1, tm, rows_per_b, d),
            _mod_spec(mods, 0, tm, rows_per_b, d),
            pl.BlockSpec((n, d), lambda i, j: (0, 0)),
            pl.BlockSpec((1, KV_LORA), lambda i, j: (0, 0)),
            pl.BlockSpec((tm, QK_ROPE), lambda i, j: (i % nt, 0)),
            pl.BlockSpec((tm, QK_ROPE), lambda i, j: (i % nt, 0)),
        ],
        out_specs=[pl.BlockSpec((tm, KV_LORA), lambda i, j: (i, 0)),
                   pl.BlockSpec((tm, QK_ROPE), lambda i, j: (i, 0))],
        out_shape=[jax.ShapeDtypeStruct((t, KV_LORA), F32), jax.ShapeDtypeStruct((t, QK_ROPE), F32)],
        scratch_shapes=[pltpu.VMEM((n, d), BF16)],
        compiler_params=_params("arbitrary", "arbitrary"),
        name="latent_kv",
    )(x, g_kv.reshape(1, d), mods, mods, w_dkv_t, g_ckv.reshape(1, KV_LORA), cos2, sin2)


def _qproj_kernel(x_ref, g_ref, sc_ref, sh_ref, w_ref, cos_ref, sin_ref, q_ref, hb_ref):
    @pl.when(pl.program_id(1) == 0)
    def _():
        hb_ref[...] = _normmod(x_ref[...], g_ref[...], sc_ref[...], sh_ref[...]).astype(BF16)

    acc = jnp.dot(hb_ref[...], w_ref[...].astype(BF16), preferred_element_type=F32)
    q_ref[:, :QK_NOPE] = acc[:, :QK_NOPE].astype(BF16)
    q_ref[:, QK_NOPE:] = _rope(acc[:, QK_NOPE:], cos_ref[...], sin_ref[...]).astype(BF16)


def _qproj(x, g, mods, sc_chunk, sh_chunk, rows_per_b, tm, w_qt, cos2, sin2):
    t, d = x.shape
    nh = w_qt.shape[0]
    nt = rows_per_b // tm
    return pl.pallas_call(
        _qproj_kernel,
        grid=(t // tm, nh),
        in_specs=[
            pl.BlockSpec((tm, d), lambda i, j: (i, 0)),
            pl.BlockSpec((1, d), lambda i, j: (0, 0)),
            _mod_spec(mods, sc_chunk, tm, rows_per_b, d),
            _mod_spec(mods, sh_chunk, tm, rows_per_b, d),
            pl.BlockSpec((None, d, QK_HEAD), lambda i, j: (j, 0, 0)),
            pl.BlockSpec((tm, QK_ROPE), lambda i, j: (i % nt, 0)),
            pl.BlockSpec((tm, QK_ROPE), lambda i, j: (i % nt, 0)),
        ],
        out_specs=pl.BlockSpec((None, None, tm, QK_HEAD), lambda i, j: (i // nt, j, i % nt, 0)),
        out_shape=jax.ShapeDtypeStruct((t // rows_per_b, nh, rows_per_b, QK_HEAD), BF16),
        scratch_shapes=[pltpu.VMEM((tm, d), BF16)],
        compiler_params=_params("arbitrary", "arbitrary"),
        name="q_proj",
    )(x, g.reshape(1, d), mods, mods, w_qt, cos2, sin2)


def _kvup_kernel(ckv_ref, kpe_ref, wk_ref, wv_ref, k_ref, v_ref, *, hb):
    c = ckv_ref[...].astype(BF16)
    kn = jnp.dot(c, wk_ref[...].astype(BF16), preferred_element_type=F32)
    vv = jnp.dot(c, wv_ref[...].astype(BF16), preferred_element_type=F32)
    kpe = kpe_ref[...].astype(BF16)
    for h in range(hb):
        k_ref[h, :, :QK_NOPE] = kn[:, h * QK_NOPE:(h + 1) * QK_NOPE].astype(BF16)
        k_ref[h, :, QK_NOPE:] = kpe
        v_ref[h] = vv[:, h * V_HEAD:(h + 1) * V_HEAD].astype(BF16)


def _kvup(ckv, kpe, w_uk2, w_uv2, bsz, seq, tm, hb=4):
    t, r = ckv.shape
    nt = seq // tm
    return pl.pallas_call(
        functools.partial(_kvup_kernel, hb=hb),
        grid=(t // tm, N_HEADS // hb),
        in_specs=[
            pl.BlockSpec((tm, r), lambda i, j: (i, 0)),
            pl.BlockSpec((tm, QK_ROPE), lambda i, j: (i, 0)),
            pl.BlockSpec((r, hb * QK_NOPE), lambda i, j: (0, j)),
            pl.BlockSpec((r, hb * V_HEAD), lambda i, j: (0, j)),
        ],
        out_specs=[pl.BlockSpec((None, hb, tm, QK_HEAD), lambda i, j: (i // nt, j, i % nt, 0)),
                   pl.BlockSpec((None, hb, tm, V_HEAD), lambda i, j: (i // nt, j, i % nt, 0))],
        out_shape=[jax.ShapeDtypeStruct((bsz, N_HEADS, seq, QK_HEAD), BF16),
                   jax.ShapeDtypeStruct((bsz, N_HEADS, seq, V_HEAD), BF16)],
        compiler_params=_params("arbitrary", "arbitrary"),
        name="kv_up",
    )(ckv, kpe, w_uk2, w_uv2)


def _flash_kernel(qt_ref, kt_ref, q_ref, k_ref, v_ref, o_ref, m_ref, l_ref, acc_ref, *, tq, tk):
    p_idx = pl.program_id(2)
    qi = qt_ref[p_idx]
    kj = kt_ref[p_idx]

    @pl.when(kj == 0)
    def _():
        m_ref[...] = jnp.full_like(m_ref, -jnp.inf)
        l_ref[...] = jnp.zeros_like(l_ref)
        acc_ref[...] = jnp.zeros_like(acc_ref)

    def tile(masked):
        s = lax.dot_general(q_ref[...], k_ref[...], DN_LAST, preferred_element_type=F32) * ATTN_SCALE
        if masked:
            qpos = lax.broadcasted_iota(I32, (tq, tk), 0)
            kpos = lax.broadcasted_iota(I32, (tq, tk), 1)
            s = jnp.where(kpos <= qpos, s, -jnp.inf)
        m_prev = m_ref[...]
        m_next = jnp.maximum(m_prev, jnp.max(s, axis=1, keepdims=True))
        alpha = jnp.exp(m_prev - m_next)
        p = jnp.exp(s - jnp.tile(m_next, (1, tk // LANES)))
        l_ref[...] = alpha * l_ref[...] + jnp.sum(p, axis=1, keepdims=True)
        acc_ref[...] = alpha * acc_ref[...] + jnp.dot(p.astype(BF16), v_ref[...], preferred_element_type=F32)
        m_ref[...] = m_next

    @pl.when(kj < qi)
    def _():
        tile(False)

    @pl.when(kj == qi)
    def _():
        tile(True)
        o_ref[...] = (acc_ref[...] / l_ref[...]).astype(BF16)


def _flash(q, k, v, tile_len=512):
    bsz, nh, seq, _ = q.shape
    tq = tk = tile_len
    nq = seq // tq
    pairs = [(i, j) for i in range(nq) for j in range(i + 1)]
    qt = jnp.asarray([p[0] for p in pairs], I32)
    kt = jnp.asarray([p[1] for p in pairs], I32)
    grid_spec = pltpu.PrefetchScalarGridSpec(
        num_scalar_prefetch=2,
        grid=(bsz, nh, len(pairs)),
        in_specs=[
            pl.BlockSpec((None, None, tq, QK_HEAD), lambda b, h, p, qt, kt: (b, h, qt[p], 0)),
            pl.BlockSpec((None, None, tk, QK_HEAD), lambda b, h, p, qt, kt: (b, h, kt[p], 0)),
            pl.BlockSpec((None, None, tk, V_HEAD), lambda b, h, p, qt, kt: (b, h, kt[p], 0)),
        ],
        out_specs=pl.BlockSpec((tq, V_HEAD), lambda b, h, p, qt, kt: (b * nq + qt[p], h)),
        scratch_shapes=[pltpu.VMEM((tq, LANES), F32), pltpu.VMEM((tq, LANES), F32),
                        pltpu.VMEM((tq, V_HEAD), F32)],
    )
    return pl.pallas_call(
        functools.partial(_flash_kernel, tq=tq, tk=tk),
        grid_spec=grid_spec,
        out_shape=jax.ShapeDtypeStruct((bsz * seq, nh * V_HEAD), BF16),
        compiler_params=_params("arbitrary", "arbitrary", "arbitrary"),
        name="flash_attn",
    )(qt, kt, q, k, v)


def _qlat_kernel(q_ref, wk_ref, o_ref):
    q = q_ref[...]
    qlat = lax.dot_general(q[:, :QK_NOPE], wk_ref[...].astype(BF16), DN_LAST, preferred_element_type=F32)
    o_ref[:, :KV_LORA] = qlat.astype(BF16)
    o_ref[:, KV_LORA:] = q[:, QK_NOPE:]


def _qlat(q, w_uk2):
    nh, b, _ = q.shape
    r = w_uk2.shape[0]
    return pl.pallas_call(
        _qlat_kernel,
        grid=(nh,),
        in_specs=[pl.BlockSpec((None, b, QK_HEAD), lambda h: (h, 0, 0)),
                  pl.BlockSpec((r, QK_NOPE), lambda h: (0, h))],
        out_specs=pl.BlockSpec((None, b, r + QK_ROPE), lambda h: (h, 0, 0)),
        out_shape=jax.ShapeDtypeStruct((nh, b, r + QK_ROPE), BF16),
        compiler_params=_params("arbitrary"),
        name="q_latent",
    )(q, w_uk2)


def _paged_kernel(pt_ref, q_ref, *refs, npg):
    ckv_refs = refs[:npg]
    kpe_refs = refs[npg:2 * npg]
    cnew_ref, pnew_ref, o_ref, qpad_ref, m_ref, l_ref, acc_ref = refs[2 * npg:]
    j = pl.program_id(1)
    nh = N_HEADS

    @pl.when(j == 0)
    def _():
        qpad_ref[...] = jnp.zeros_like(qpad_ref)
        qpad_ref[:nh, :] = q_ref[...]
        m_ref[...] = jnp.full_like(m_ref, -jnp.inf)
        l_ref[...] = jnp.zeros_like(l_ref)
        acc_ref[...] = jnp.zeros_like(acc_ref)

    qlat = qpad_ref[:, :KV_LORA]
    qpe = qpad_ref[:, KV_LORA:]
    kcs = []
    sts = []
    for i in range(npg):
        kc = ckv_refs[i][...].astype(BF16)
        s_c = lax.dot_general(kc, qlat, DN_LAST, preferred_element_type=F32)
        s_p = jnp.dot(qpe, kpe_refs[i][...].astype(BF16), preferred_element_type=F32)
        sts.append(s_c.T[:nh, :] + s_p[:nh, :])
        kcs.append(kc)
    s = jnp.concatenate(sts, axis=1) * ATTN_SCALE
    m_prev = m_ref[...]
    m_next = jnp.maximum(m_prev, jnp.max(s, axis=1, keepdims=True))
    alpha = jnp.exp(m_prev - m_next)
    p = jnp.exp(s - jnp.tile(m_next, (1, s.shape[1] // LANES)))
    l_ref[...] = alpha * l_ref[...] + jnp.sum(p, axis=1, keepdims=True)
    pb = p.astype(BF16)
    pv = jnp.dot(pb[:, :PAGE_SIZE], kcs[0], preferred_element_type=F32)
    for i in range(1, npg):
        pv = pv + jnp.dot(pb[:, i * PAGE_SIZE:(i + 1) * PAGE_SIZE], kcs[i], preferred_element_type=F32)
    acc_ref[...] = jnp.tile(alpha, (1, KV_LORA // LANES)) * acc_ref[...] + pv
    m_ref[...] = m_next

    @pl.when(j == pl.num_programs(1) - 1)
    def _():
        q = q_ref[...].astype(F32)
        cnew = cnew_ref[...]
        s_new = (jnp.sum(q[:, :KV_LORA] * cnew, axis=1, keepdims=True)
                 + jnp.sum(q[:, KV_LORA:] * pnew_ref[...], axis=1, keepdims=True)) * ATTN_SCALE
        m_p = m_ref[...]
        m_n = jnp.maximum(m_p, s_new)
        a = jnp.exp(m_p - m_n)
        p_new = jnp.exp(s_new - m_n)
        l_fin = a * l_ref[...] + p_new
        acc = (jnp.tile(a, (1, KV_LORA // LANES)) * acc_ref[...]
               + jnp.tile(p_new, (1, KV_LORA // LANES)) * cnew)
        o_ref[...] = (acc / jnp.tile(l_fin, (1, KV_LORA // LANES))).astype(BF16)


def _paged_attn(qcat, cache_ckv, cache_kpe_t, page_table, ckv_new, kpe_new, npg=8):
    b, nh, w = qcat.shape
    r = KV_LORA
    n_pages = page_table.shape[1]
    ckv_specs = [pl.BlockSpec((None, PAGE_SIZE, r), functools.partial(
        lambda bi, j, pt, i: (pt[bi, j * npg + i], 0, 0), i=i)) for i in range(npg)]
    kpe_specs = [pl.BlockSpec((None, QK_ROPE, PAGE_SIZE), functools.partial(
        lambda bi, j, pt, i: (pt[bi, j * npg + i], 0, 0), i=i)) for i in range(npg)]
    grid_spec = pltpu.PrefetchScalarGridSpec(
        num_scalar_prefetch=1,
        grid=(b, n_pages // npg),
        in_specs=[pl.BlockSpec((None, nh, w), lambda bi, j, pt: (bi, 0, 0))] + ckv_specs + kpe_specs + [
            pl.BlockSpec((None, 1, r), lambda bi, j, pt: (bi, 0, 0)),
            pl.BlockSpec((None, 1, QK_ROPE), lambda bi, j, pt: (bi, 0, 0)),
        ],
        out_specs=pl.BlockSpec((None, nh, r), lambda bi, j, pt: (bi, 0, 0)),
        scratch_shapes=[pltpu.VMEM((LANES, w), BF16), pltpu.VMEM((nh, LANES), F32),
                        pltpu.VMEM((nh, LANES), F32), pltpu.VMEM((nh, r), F32)],
    )
    return pl.pallas_call(
        functools.partial(_paged_kernel, npg=npg),
        grid_spec=grid_spec,
        out_shape=jax.ShapeDtypeStruct((b, nh, r), BF16),
        compiler_params=_params("arbitrary", "arbitrary"),
        name="paged_attn",
    )(page_table, qcat, *([cache_ckv] * npg), *([cache_kpe_t] * npg),
      ckv_new.reshape(b, 1, r), kpe_new.reshape(b, 1, QK_ROPE))


def _ovup_kernel(o_ref, wv_ref, a_ref):
    a_ref[...] = jnp.dot(o_ref[...], wv_ref[...].astype(BF16), preferred_element_type=F32).astype(BF16)


def _ovup(olat, w_uv2):
    nh, b, r = olat.shape
    return pl.pallas_call(
        _ovup_kernel,
        grid=(nh,),
        in_specs=[pl.BlockSpec((None, b, r), lambda h: (h, 0, 0)), pl.BlockSpec((r, V_HEAD), lambda h: (0, h))],
        out_specs=pl.BlockSpec((b, V_HEAD), lambda h: (0, h)),
        out_shape=jax.ShapeDtypeStruct((b, nh * V_HEAD), BF16),
        compiler_params=_params("arbitrary"),
        name="o_latent_up",
    )(olat, w_uv2)


def _rope_tables(pos):
    inv = ROPE_THETA ** (-jnp.arange(0, QK_ROPE, 2, dtype=F32) / QK_ROPE)
    ang = pos.astype(F32)[:, None] * inv[None, :]
    cos, sin = jnp.cos(ang), jnp.sin(ang)
    return jnp.concatenate([cos, cos], axis=1), jnp.concatenate([-sin, sin], axis=1)


def _trunk(x, mods, mods_kv, rows_per_b, tm, cos2, sin2, P, s5_glu_fn, attn_fn, routed):
    tms = min(tm, 512)

    def moe_block(x, layer):
        h2, gates_t, e_rows, w_rows = _route(x, P["g_norm"][layer, 1], mods[layer], 4, 3, rows_per_b, tms,
                                             P["w_router_t"], P["b_router"], split=routed)
        if routed:
            return _moe_routed(h2, e_rows, w_rows, P["w_gate"], P["w_up"], P["w_down"], layer, x,
                               mods[layer], 5, rows_per_b)
        return _moe(h2, gates_t.T, P["w_gate"], P["w_up"], P["w_down"], layer, x, mods[layer], 5,
                    rows_per_b, tms)

    x, st_re, st_im = s5_glu_fn(x)
    x = moe_block(x, 0)
    ckv, kpe = _kv(x, P["g_kv"], mods_kv, rows_per_b, tms, P["w_dkv_t"], P["g_ckv"], cos2, sin2)
    q = _qproj(x, P["g_norm"][1, 0], mods[1], 1, 0, rows_per_b, tm, P["w_qt"], cos2, sin2)
    attn = attn_fn(q, ckv, kpe)
    x = _proj_res(attn, P["w_o"][0], x, mods[1], 2, rows_per_b, tm, 512, glu=False)
    x = moe_block(x, 1)
    y = _final_norm(x, P["g_final"], tms)
    return y, st_re, st_im, ckv, kpe


def kernel(x_prompt, x_sample, c_prompt, c_sample, state_ssm_re, state_ssm_im, cache_ckv, cache_kpe, page_table, g_norm, w_ada, b_ada, ssm_lam_re, ssm_lam_im, ssm_log_dt, ssm_b_re, ssm_b_im, ssm_c_re, ssm_c_im, ssm_d, w_glu, g_kv, w_ada_kv, b_ada_kv, w_dkv, g_ckv, w_uk, w_uv, w_q, w_o, w_router, b_router, w_gate, w_up, w_down, g_final):
    bsz, seq, d = x_prompt.shape
    db = x_sample.shape[0]
    g = d // SSM_GROUP
    n = SSM_STATE
    assert g == LANES, "the group-major relayout puts one S5 group per lane"
    past = page_table.shape[1] * PAGE_SIZE
    P = dict(g_norm=g_norm, g_kv=g_kv, g_ckv=g_ckv, w_o=w_o, b_router=b_router,
             w_gate=w_gate, w_up=w_up, w_down=w_down, g_final=g_final)
    P["w_qt"] = w_q[0].reshape(d, N_HEADS, QK_HEAD).transpose(1, 0, 2)
    P["w_dkv_t"] = w_dkv.T
    P["w_router_t"] = w_router.T
    w_uk2 = w_uk.reshape(KV_LORA, N_HEADS * QK_NOPE)
    w_uv2 = w_uv.reshape(KV_LORA, N_HEADS * V_HEAD)

    m_rows = db + bsz
    m_pad = -(-m_rows // 16) * 16
    c_all = jnp.concatenate([c_sample, c_prompt, jnp.zeros((m_pad - m_rows, d), F32)], axis=0)
    mod = _ada(c_all, w_ada, b_ada)
    mod_kv = _ada(c_all, w_ada_kv[None], b_ada_kv[None])
    mods_s = [mod[l, :db][None] for l in range(mod.shape[0])]
    mods_p = [mod[l, db:m_rows][:, None, :] for l in range(mod.shape[0])]
    modkv_s = mod_kv[0, :db][None]
    modkv_p = mod_kv[0, db:m_rows][:, None, :]

    lam_re = ssm_lam_re[0].reshape(g, 1, n)
    lam_im = ssm_lam_im[0].reshape(g, 1, n)
    log_dt = jnp.broadcast_to(ssm_log_dt[0].reshape(g, 1, 1), (g, 1, n))
    bt_re = ssm_b_re[0].transpose(0, 2, 1)
    bt_im = ssm_b_im[0].transpose(0, 2, 1)
    c_re, c_im = ssm_c_re[0], ssm_c_im[0]
    d_g = ssm_d[0].reshape(g, 1, SSM_GROUP)
    d_rep = jnp.tile(d_g, (1, 1, S5_CHUNK))
    dup = lambda a: jnp.concatenate([a, a], axis=-1)
    cat = lambda a, b: jnp.concatenate([a, b], axis=-1)
    nchunk = seq // S5_CHUNK
    tm_p = 1024

    src = jnp.arange(d, dtype=I32)
    dst = (src % SSM_GROUP) * g + src // SSM_GROUP
    perm = (dst[:, None] == jnp.arange(d, dtype=I32)[None, :]).astype(BF16)
    w_glu_p = w_glu[0].reshape(g, SSM_GROUP, -1).transpose(1, 0, 2).reshape(d, -1)

    def s5_glu_prompt(x):
        hx = _s5_in(x, g_norm[0, 0], mods_p[0], 1, 0, seq, 256, perm)
        hx = hx.reshape(g, bsz * nchunk, S5_CHUNK * SSM_GROUP)
        zg, st = _s5_prompt(hx, dup(lam_re), dup(lam_im), dup(log_dt), cat(bt_re, bt_im), cat(bt_im, bt_re),
                            cat(c_re, c_im), cat(c_im, c_re), d_rep, bsz)
        zg = zg.reshape(g, bsz * seq * SSM_GROUP)
        x = _proj_res(zg, w_glu_p, x, mods_p[0], 2, seq, tm_p, 256, glu=True, grouped=True)
        st = st.transpose(1, 0, 2)[None]
        return x, st[..., :n], st[..., n:]

    def s5_glu_sample(x):
        h = _normmod_call(x, g_norm[0, 0], mods_s[0], 1, 0, db, db, F32)
        u = h.reshape(db, g, SSM_GROUP).transpose(1, 0, 2)
        h0_re = state_ssm_re[0].transpose(1, 0, 2)
        h0_im = state_ssm_im[0].transpose(1, 0, 2)
        zg, nr, ni = _s5_sample(u, h0_re, h0_im, lam_re, lam_im, log_dt, bt_re, bt_im, c_re, c_im, d_g)
        z = zg.transpose(1, 0, 2).reshape(db, d)
        x = _proj_res(z, w_glu[0], x, mods_s[0], 2, db, db, 256, glu=True)
        return x, nr.transpose(1, 0, 2)[None], ni.transpose(1, 0, 2)[None]

    def attn_prompt(q, ckv, kpe):
        k, v = _kvup(ckv, kpe, w_uk2, w_uv2, bsz, seq, tm_p)
        return _flash(q, k, v)

    def attn_sample(q, ckv, kpe):
        qcat = _qlat(q[0], w_uk2).transpose(1, 0, 2)
        olat = _paged_attn(qcat, cache_ckv, cache_kpe.transpose(0, 2, 1), page_table, ckv, kpe)
        return _ovup(olat.transpose(1, 0, 2), w_uv2)

    cos_p, sin_p = _rope_tables(jnp.arange(seq))
    cos_s, sin_s = _rope_tables(jnp.full((db,), past))

    y_p, sre_p, sim_p, ckv_p, kpe_p = _trunk(
        x_prompt.reshape(bsz * seq, d), mods_p, modkv_p, seq, tm_p, cos_p, sin_p, P,
        s5_glu_prompt, attn_prompt, routed=True)
    y_s, sre_s, sim_s, ckv_s, kpe_s = _trunk(
        x_sample.reshape(db, d), mods_s, modkv_s, db, db, cos_s, sin_s, P,
        s5_glu_sample, attn_sample, routed=False)
    return (y_p.reshape(bsz, seq, d), y_s.reshape(db, 1, d), sre_p, sim_p,
            ckv_p.reshape(bsz, seq, KV_LORA), kpe_p.reshape(bsz, seq, QK_ROPE),
            sre_s, sim_s, ckv_s.reshape(db, 1, KV_LORA), kpe_s.reshape(db, 1, QK_ROPE))
```

```python
import functools
import math

import jax
import jax.numpy as jnp
from jax import lax
from jax.experimental import pallas as pl
from jax.experimental.pallas import tpu as pltpu

F32 = jnp.float32
BF16 = jnp.bfloat16
I32 = jnp.int32

SSM_GROUP = 16
SSM_STATE = 64
N_HEADS = 16
QK_NOPE = 128
QK_ROPE = 64
V_HEAD = 128
KV_LORA = 512
ROPE_THETA = 10000.0
PAGE_SIZE = 128
N_EXPERTS = 16
N_EXPERT_GROUPS = 4
EXPERTS_PER_GROUP = N_EXPERTS // N_EXPERT_GROUPS
TOP_K = 2
EPS = 1e-6
ATTN_SCALE = (QK_NOPE + QK_ROPE) ** -0.5
QK_HEAD = QK_NOPE + QK_ROPE

V7X_VMEM_BYTES = 64 * 1024 * 1024
VMEM_LIMIT_BYTES = V7X_VMEM_BYTES - 8 * 1024 * 1024
LANES = 128
SMEM_I32_BLOCK = 1024
S5_CHUNK = 16
DN_LAST = (((1,), (1,)), ((), ()))


def _params(*sem):
    return pltpu.CompilerParams(dimension_semantics=sem, vmem_limit_bytes=VMEM_LIMIT_BYTES)


def _normmod(x, g, sc, sh):
    ms = jnp.mean(x * x, axis=-1, keepdims=True)
    return (x * lax.rsqrt(ms + EPS)) * g * (1.0 + sc) + sh


def _rope(p, cos2, sin2):
    half = QK_ROPE // 2
    swapped = jnp.concatenate([p[:, half:], p[:, :half]], axis=1)
    return p * cos2 + swapped * sin2


def _mod_spec(mods, chunk, tm, rows_per_b, width, j_blocks=None):
    _, r, _ = mods.shape
    if j_blocks is None:
        return pl.BlockSpec((None, r, width), lambda i, j: ((i * tm) // rows_per_b, 0, chunk))
    return pl.BlockSpec((None, r, width), lambda i, j: ((i * tm) // rows_per_b, 0, chunk * j_blocks + j))


def _ada_kernel(c_ref, w_ref, b_ref, o_ref, cb_ref):
    @pl.when((pl.program_id(0) == 0) & (pl.program_id(1) == 0))
    def _():
        c = c_ref[...]
        cb_ref[...] = (c * jax.nn.sigmoid(c)).astype(BF16)

    acc = jnp.dot(cb_ref[...], w_ref[...].astype(BF16), preferred_element_type=F32)
    o_ref[...] = acc + b_ref[...]


def _ada(c_all, w, b, tn=1024):
    m, d = c_all.shape
    lw, _, n = w.shape
    return pl.pallas_call(
        _ada_kernel,
        grid=(lw, n // tn),
        in_specs=[
            pl.BlockSpec((m, d), lambda l, j: (0, 0)),
            pl.BlockSpec((None, d, tn), lambda l, j: (l, 0, j)),
            pl.BlockSpec((None, 1, tn), lambda l, j: (l, 0, j)),
        ],
        out_specs=pl.BlockSpec((None, m, tn), lambda l, j: (l, 0, j)),
        out_shape=jax.ShapeDtypeStruct((lw, m, n), F32),
        scratch_shapes=[pltpu.VMEM((m, d), BF16)],
        compiler_params=_params("arbitrary", "arbitrary"),
        name="ada_mod",
    )(c_all, w, b.reshape(lw, 1, n))


def _normmod_kernel(x_ref, g_ref, sc_ref, sh_ref, o_ref):
    o_ref[...] = _normmod(x_ref[...], g_ref[...], sc_ref[...], sh_ref[...]).astype(o_ref.dtype)


def _normmod_call(x, g, mods, sc_chunk, sh_chunk, rows_per_b, tm, out_dtype):
    t, d = x.shape
    return pl.pallas_call(
        _normmod_kernel,
        grid=(t // tm, 1),
        in_specs=[
            pl.BlockSpec((tm, d), lambda i, j: (i, 0)),
            pl.BlockSpec((1, d), lambda i, j: (0, 0)),
            _mod_spec(mods, sc_chunk, tm, rows_per_b, d),
            _mod_spec(mods, sh_chunk, tm, rows_per_b, d),
        ],
        out_specs=pl.BlockSpec((tm, d), lambda i, j: (i, 0)),
        out_shape=jax.ShapeDtypeStruct((t, d), out_dtype),
        compiler_params=_params("arbitrary", "arbitrary"),
        name="norm_mod",
    )(x, g.reshape(1, d), mods, mods)


def _final_norm_kernel(x_ref, g_ref, o_ref):
    x = x_ref[...]
    ms = jnp.mean(x * x, axis=-1, keepdims=True)
    o_ref[...] = (x * lax.rsqrt(ms + EPS)) * g_ref[...]


def _final_norm(x, g, tm):
    t, d = x.shape
    return pl.pallas_call(
        _final_norm_kernel,
        grid=(t // tm,),
        in_specs=[pl.BlockSpec((tm, d), lambda i: (i, 0)), pl.BlockSpec((1, d), lambda i: (0, 0))],
        out_specs=pl.BlockSpec((tm, d), lambda i: (i, 0)),
        out_shape=jax.ShapeDtypeStruct((t, d), F32),
        compiler_params=_params("arbitrary"),
        name="final_norm",
    )(x, g.reshape(1, d))


def _proj_res_kernel(x_ref, *refs, glu, grouped, tm):
    refs = list(refs)
    xb_ref = refs.pop() if grouped else None
    if glu:
        wa_ref, wb_ref, res_ref, gt_ref, o_ref = refs
    else:
        wa_ref, res_ref, gt_ref, o_ref = refs
    if grouped:
        @pl.when(pl.program_id(1) == 0)
        def _():
            cg = SSM_GROUP
            for k in range(tm // LANES):
                zt = x_ref[:, k * LANES * cg:(k + 1) * LANES * cg].T
                xb_ref[k * LANES:(k + 1) * LANES, :] = zt.reshape(LANES, cg, LANES).reshape(LANES, cg * LANES)

        x = xb_ref[...]
    else:
        x = x_ref[...]
    y = jnp.dot(x, wa_ref[...].astype(BF16), preferred_element_type=F32)
    if glu:
        yb = jnp.dot(x, wb_ref[...].astype(BF16), preferred_element_type=F32)
        y = y * jax.nn.sigmoid(yb)
    o_ref[...] = res_ref[...] + gt_ref[...] * y


def _proj_res(x, w, res, mods, gt_chunk, rows_per_b, tm, tn, glu, grouped=False):
    t, n = res.shape
    k = w.shape[0]
    nj = n // tn
    if grouped:
        x_spec = pl.BlockSpec((x.shape[0], tm * SSM_GROUP), lambda i, j: (0, i))
    else:
        x_spec = pl.BlockSpec((tm, k), lambda i, j: (i, 0))
    in_specs = [x_spec, pl.BlockSpec((k, tn), lambda i, j: (0, j))]
    args = [x, w]
    if glu:
        in_specs.append(pl.BlockSpec((k, tn), lambda i, j: (0, nj + j)))
        args.append(w)
    in_specs += [pl.BlockSpec((tm, tn), lambda i, j: (i, j)), _mod_spec(mods, gt_chunk, tm, rows_per_b, tn, nj)]
    args += [res, mods]
    return pl.pallas_call(
        functools.partial(_proj_res_kernel, glu=glu, grouped=grouped, tm=tm),
        grid=(t // tm, nj),
        in_specs=in_specs,
        out_specs=pl.BlockSpec((tm, tn), lambda i, j: (i, j)),
        out_shape=jax.ShapeDtypeStruct((t, n), F32),
        scratch_shapes=[pltpu.VMEM((tm, k), BF16)] if grouped else [],
        compiler_params=_params("arbitrary", "arbitrary"),
        name="glu_proj" if glu else "out_proj",
    )(*args)


def _s5in_kernel(x_ref, g_ref, sc_ref, sh_ref, p_ref, o_ref):
    h = _normmod(x_ref[...], g_ref[...], sc_ref[...], sh_ref[...]).astype(BF16)
    hp = jnp.dot(h, p_ref[...], preferred_element_type=F32).astype(BF16)
    tm = hp.shape[0]
    o_ref[...] = hp.reshape(tm, SSM_GROUP, LANES).reshape(tm * SSM_GROUP, LANES).T


def _s5_in(x, g, mods, sc_chunk, sh_chunk, rows_per_b, tm, perm):
    t, d = x.shape
    ng = d // SSM_GROUP
    return pl.pallas_call(
        _s5in_kernel,
        grid=(t // tm, 1),
        in_specs=[
            pl.BlockSpec((tm, d), lambda i, j: (i, 0)),
            pl.BlockSpec((1, d), lambda i, j: (0, 0)),
            _mod_spec(mods, sc_chunk, tm, rows_per_b, d),
            _mod_spec(mods, sh_chunk, tm, rows_per_b, d),
            pl.BlockSpec((d, d), lambda i, j: (0, 0)),
        ],
        out_specs=pl.BlockSpec((ng, tm * SSM_GROUP), lambda i, j: (0, i)),
        out_shape=jax.ShapeDtypeStruct((ng, t * SSM_GROUP), BF16),
        compiler_params=_params("arbitrary", "arbitrary"),
        name="s5_in",
    )(x, g.reshape(1, d), mods, mods, perm)


def _cpow(lr_dt, li_dt, k):
    mag = jnp.exp(k * lr_dt)
    ang = k * li_dt
    return mag * jnp.cos(ang), mag * jnp.sin(ang)


def _rep_rows(e, n):
    q, w = e.shape
    return jnp.concatenate([jnp.broadcast_to(e[t:t + 1], (n, w)) for t in range(q)], axis=0)


def _tile_rows(c, n):
    return jnp.concatenate([c] * n, axis=0)


def _zoh(lr, li, ldt):
    dt = jnp.exp(ldt)
    lr_dt = lr * dt
    li_dt = li * dt
    mag = jnp.exp(lr_dt)
    ab_re = mag * jnp.cos(li_dt)
    ab_im = mag * jnp.sin(li_dt)
    den = lr * lr + li * li
    f_re = ((ab_re - 1.0) * lr + ab_im * li) / den
    f_im = (ab_im * lr - (ab_re - 1.0) * li) / den
    return lr_dt, li_dt, ab_re, ab_im, f_re, f_im


def _gelu_tanh(y):
    return 0.5 * y * (1.0 + jnp.tanh(math.sqrt(2.0 / math.pi) * (y + 0.044715 * (y * y * y))))


def _s5_prompt_kernel(x_ref, lr_ref, li_ref, ldt_ref, bt_ref, bts_ref, c_ref, cs_ref, d_ref,
                      z_ref, st_ref, s_sc, ss_sc, hp_sc, yl_sc, *, gb, nb, nchunk):
    q, cg, n = S5_CHUNK, SSM_GROUP, SSM_STATE
    half = q // 2
    ti = lax.broadcasted_iota(I32, (q, 1), 0).astype(F32)
    rows = lax.broadcasted_iota(I32, (q * cg, q * cg), 0)
    cols = lax.broadcasted_iota(I32, (q * cg, q * cg), 1)
    shift = cg.bit_length() - 1
    causal = (rows >> shift) <= (cols >> shift)
    diag = rows == cols
    lane = lax.broadcasted_iota(I32, (1, 2 * n), 1)
    sgn = jnp.where(lane < n, -1.0, 1.0)

    def cmul(x2, xs, er, ei):
        return x2 * er + (xs * sgn) * ei

    a_terms = []
    vcs = []
    for g in range(gb):
        lr_dt, li_dt, _, _, f_re, f_im = _zoh(lr_ref[g], li_ref[g], ldt_ref[g])
        bt2, bts = bt_ref[g], bts_ref[g]
        bb2 = cmul(bt2, bts, f_re, f_im)
        bbs = bts * f_re - (bt2 * sgn) * f_im
        bb2t, bbst = _tile_rows(bb2, q), _tile_rows(bbs, q)
        c2t, cst = _tile_rows(c_ref[g], q), _tile_rows(cs_ref[g], q)

        def table(k, lr_dt=lr_dt, li_dt=li_dt):
            er, ei = _cpow(lr_dt, li_dt, k)
            return _rep_rows(er, cg), _rep_rows(ei, cg)

        pc = cmul(c2t, cst, *table(ti - half)) * (-sgn)
        pb = cmul(bb2t, bbst, *table(half - ti))
        tt = lax.dot_general(pb.astype(BF16), pc.astype(BF16), DN_LAST, preferred_element_type=F32)
        tt = jnp.where(causal, tt, 0.0) + jnp.where(diag, d_ref[g], 0.0)
        w1 = cmul(bb2t, bbst, *table((q - 1) - ti))
        vcs.append((cmul(c2t, cst, *table(ti + 1.0)) * (-sgn)).astype(BF16))
        ar, ai = _cpow(lr_dt, li_dt, float(q))
        a_terms.append((ar, ai * sgn))
        x = x_ref[g]
        yl_sc[g] = jnp.dot(x, tt.astype(BF16), preferred_element_type=F32)
        s = jnp.dot(x, w1.astype(BF16), preferred_element_type=F32)
        s_sc[g] = s
        ss_sc[g] = pltpu.roll(s, n, axis=1)

    def carry(ch, hs):
        new = []
        for g in range(gb):
            h, hsw = hs[2 * g], hs[2 * g + 1]
            hp_sc[g, pl.ds(ch, nb, stride=nchunk), :] = h
            a1, a2 = a_terms[g]
            new.append(a1 * h + a2 * hsw + s_sc[g, pl.ds(ch, nb, stride=nchunk), :])
            new.append(a1 * hsw - a2 * h + ss_sc[g, pl.ds(ch, nb, stride=nchunk), :])
        return tuple(new)

    zero = jnp.zeros((nb, 2 * n), F32)
    hs = lax.fori_loop(0, nchunk, carry, (zero,) * (2 * gb), unroll=8)
    for g in range(gb):
        st_ref[g] = hs[2 * g]
        y = yl_sc[g] + lax.dot_general(hp_sc[g].astype(BF16), vcs[g], DN_LAST, preferred_element_type=F32)
        z_ref[g] = _gelu_tanh(y).astype(BF16)


def _s5_prompt(hx, lam_re2, lam_im2, log_dt2, bt2, bts, c2, cs, d_rep, nb, gb=2):
    g, r, w = hx.shape
    n2 = 2 * SSM_STATE
    nchunk = r // nb
    vec = lambda: pl.BlockSpec((gb, 1, n2), lambda i: (i, 0, 0))
    mat = lambda: pl.BlockSpec((gb, SSM_GROUP, n2), lambda i: (i, 0, 0))
    return pl.pallas_call(
        functools.partial(_s5_prompt_kernel, gb=gb, nb=nb, nchunk=nchunk),
        grid=(g // gb,),
        in_specs=[pl.BlockSpec((gb, r, w), lambda i: (i, 0, 0)), vec(), vec(), vec(),
                  mat(), mat(), mat(), mat(), pl.BlockSpec((gb, 1, w), lambda i: (i, 0, 0))],
        out_specs=[pl.BlockSpec((gb, r, w), lambda i: (i, 0, 0)),
                   pl.BlockSpec((gb, nb, n2), lambda i: (i, 0, 0))],
        out_shape=[jax.ShapeDtypeStruct((g, r, w), BF16),
                   jax.ShapeDtypeStruct((g, nb, n2), F32)],
        scratch_shapes=[pltpu.VMEM((gb, r, n2), F32), pltpu.VMEM((gb, r, n2), F32),
                        pltpu.VMEM((gb, r, n2), F32), pltpu.VMEM((gb, r, w), F32)],
        compiler_params=_params("arbitrary"),
        name="s5_prompt",
    )(hx, lam_re2, lam_im2, log_dt2, bt2, bts, c2, cs, d_rep)


def _s5_sample_kernel(u_ref, h0r_ref, h0i_ref, lr_ref, li_ref, ldt_ref, btr_ref, bti_ref,
                      cr_ref, ci_ref, d_ref, z_ref, nr_ref, ni_ref, *, gb):
    hp = lax.Precision.HIGHEST
    for g in range(gb):
        _, _, ab_re, ab_im, f_re, f_im = _zoh(lr_ref[g], li_ref[g], ldt_ref[g])
        bb_re = f_re * btr_ref[g] - f_im * bti_ref[g]
        bb_im = f_re * bti_ref[g] + f_im * btr_ref[g]
        u = u_ref[g]
        bu_re = jnp.dot(u, bb_re, preferred_element_type=F32, precision=hp)
        bu_im = jnp.dot(u, bb_im, preferred_element_type=F32, precision=hp)
        h0r = h0r_ref[g]
        h0i = h0i_ref[g]
        hr = bu_re + ab_re * h0r - ab_im * h0i
        hi = bu_im + ab_re * h0i + ab_im * h0r
        nr_ref[g] = hr
        ni_ref[g] = hi
        y = (lax.dot_general(hr, cr_ref[g], DN_LAST, preferred_element_type=F32, precision=hp)
             - lax.dot_general(hi, ci_ref[g], DN_LAST, preferred_element_type=F32, precision=hp))
        y = y + d_ref[g] * u
        z_ref[g] = _gelu_tanh(y).astype(BF16)


def _s5_sample(u, h0_re, h0_im, lam_re, lam_im, log_dt, bt_re, bt_im, c_re, c_im, d_g, gb=8):
    g, b, cg = u.shape
    n = SSM_STATE
    vec = lambda: pl.BlockSpec((gb, 1, n), lambda i: (i, 0, 0))
    mat = lambda: pl.BlockSpec((gb, cg, n), lambda i: (i, 0, 0))
    st = lambda: pl.BlockSpec((gb, b, n), lambda i: (i, 0, 0))
    return pl.pallas_call(
        functools.partial(_s5_sample_kernel, gb=gb),
        grid=(g // gb,),
        in_specs=[pl.BlockSpec((gb, b, cg), lambda i: (i, 0, 0)), st(), st(), vec(), vec(), vec(),
                  mat(), mat(), mat(), mat(), pl.BlockSpec((gb, 1, cg), lambda i: (i, 0, 0))],
        out_specs=[pl.BlockSpec((gb, b, cg), lambda i: (i, 0, 0)), st(), st()],
        out_shape=[jax.ShapeDtypeStruct((g, b, cg), BF16),
                   jax.ShapeDtypeStruct((g, b, n), F32),
                   jax.ShapeDtypeStruct((g, b, n), F32)],
        compiler_params=_params("arbitrary"),
        name="s5_sample",
    )(u, h0_re, h0_im, lam_re, lam_im, log_dt, bt_re, bt_im, c_re, c_im, d_g)


def _route_kernel(x_ref, g_ref, sc_ref, sh_ref, wr_ref, br_ref, h_ref, gates_ref, e_ref, w_ref, *, split):
    h = _normmod(x_ref[...], g_ref[...], sc_ref[...], sh_ref[...])
    hb = h.astype(BF16)
    h_ref[...] = hb.reshape(h_ref.shape) if split else hb
    logits = lax.dot_general(wr_ref[...], h, DN_LAST, preferred_element_type=F32,
                             precision=lax.Precision.HIGHEST)
    s = jax.nn.sigmoid(logits)
    sel = s + br_ref[...]
    epg = EXPERTS_PER_GROUP
    srow = [s[e:e + 1, :] for e in range(N_EXPERTS)]
    row = [sel[e:e + 1, :] for e in range(N_EXPERTS)]

    def top2_sum(a, b, c, d):
        hi1, lo1 = jnp.maximum(a, b), jnp.minimum(a, b)
        hi2, lo2 = jnp.maximum(c, d), jnp.minimum(c, d)
        return jnp.maximum(hi1, hi2) + jnp.maximum(jnp.minimum(hi1, hi2), jnp.maximum(lo1, lo2))

    gs = [top2_sum(*row[epg * g:epg * (g + 1)]) for g in range(N_EXPERT_GROUPS)]
    best = gs[0]
    gidx = jnp.zeros_like(best, dtype=I32)
    for g in range(1, N_EXPERT_GROUPS):
        better = gs[g] > best
        gidx = jnp.where(better, g, gidx)
        best = jnp.where(better, gs[g], best)

    def pick(rows_, j):
        v = rows_[j]
        for g in range(1, N_EXPERT_GROUPS):
            v = jnp.where(gidx == g, rows_[epg * g + j], v)
        return v

    v = [pick(row, j) for j in range(epg)]
    u = [pick(srow, j) for j in range(epg)]

    def argmax_first(vals):
        bv, bi = vals[0], jnp.zeros_like(gidx)
        for j in range(1, epg):
            better = vals[j] > bv
            bi = jnp.where(better, j, bi)
            bv = jnp.where(better, vals[j], bv)
        return bi

    i1 = argmax_first(v)
    i2 = argmax_first([jnp.where(i1 == j, -jnp.inf, v[j]) for j in range(epg)])

    def take(vals, idx):
        out = vals[0]
        for j in range(1, epg):
            out = jnp.where(idx == j, vals[j], out)
        return out

    w1, w2 = take(u, i1), take(u, i2)
    tot = w1 + w2
    w1, w2 = w1 / tot, w2 / tot
    e1 = gidx * epg + i1
    e2 = gidx * epg + i2
    gates_ref[...] = jnp.concatenate(
        [jnp.where(e1 == e, w1, 0.0) + jnp.where(e2 == e, w2, 0.0) for e in range(N_EXPERTS)], axis=0)
    e_ref[...] = jnp.concatenate([e1, e2], axis=0)
    w_ref[...] = jnp.concatenate([w1, w2], axis=0)


def _route(x, g, mods, sc_chunk, sh_chunk, rows_per_b, tm, w_router_t, b_router, split):
    t, d = x.shape
    e = N_EXPERTS
    if split:
        h_spec = pl.BlockSpec((tm, d // LANES, LANES), lambda i, j: (i, 0, 0))
        h_shape = jax.ShapeDtypeStruct((t, d // LANES, LANES), BF16)
    else:
        h_spec = pl.BlockSpec((tm, d), lambda i, j: (i, 0))
        h_shape = jax.ShapeDtypeStruct((t, d), BF16)
    return pl.pallas_call(
        functools.partial(_route_kernel, split=split),
        grid=(t // tm, 1),
        in_specs=[
            pl.BlockSpec((tm, d), lambda i, j: (i, 0)),
            pl.BlockSpec((1, d), lambda i, j: (0, 0)),
            _mod_spec(mods, sc_chunk, tm, rows_per_b, d),
            _mod_spec(mods, sh_chunk, tm, rows_per_b, d),
            pl.BlockSpec((e, d), lambda i, j: (0, 0)),
            pl.BlockSpec((e, 1), lambda i, j: (0, 0)),
        ],
        out_specs=[h_spec, pl.BlockSpec((e, tm), lambda i, j: (0, i)),
                   pl.BlockSpec((TOP_K, tm), lambda i, j: (0, i)), pl.BlockSpec((TOP_K, tm), lambda i, j: (0, i))],
        out_shape=[h_shape, jax.ShapeDtypeStruct((e, t), F32),
                   jax.ShapeDtypeStruct((TOP_K, t), I32), jax.ShapeDtypeStruct((TOP_K, t), F32)],
        compiler_params=_params("arbitrary", "arbitrary"),
        name="moe_route",
    )(x, g.reshape(1, d), mods, mods, w_router_t, b_router.reshape(e, 1))


def _moe_kernel(h_ref, gates_ref, wg_ref, wu_ref, wd_ref, res_ref, gt_ref, o_ref, acc_ref):
    e = pl.program_id(1)
    first = (e == 0) & (pl.program_id(2) == 0)
    last = (e == pl.num_programs(1) - 1) & (pl.program_id(2) == pl.num_programs(2) - 1)

    @pl.when(first)
    def _():
        acc_ref[...] = jnp.zeros_like(acc_ref)

    h = h_ref[...]
    a = jnp.dot(h, wg_ref[...].astype(BF16), preferred_element_type=F32)
    b = jnp.dot(h, wu_ref[...].astype(BF16), preferred_element_type=F32)
    gates = gates_ref[...]
    lane = lax.broadcasted_iota(I32, gates.shape, 1)
    gcol = jnp.sum(jnp.where(lane == e, gates, 0.0), axis=1, keepdims=True)
    hid = (a * jax.nn.sigmoid(a)) * b * gcol
    acc_ref[...] += jnp.dot(hid.astype(BF16), wd_ref[...].astype(BF16), preferred_element_type=F32)

    @pl.when(last)
    def _():
        o_ref[...] = res_ref[...] + gt_ref[...] * acc_ref[...]


def _moe(h, gates, w_gate, w_up, w_down, layer, res, mods, gt_chunk, rows_per_b, tm, fsplit=2):
    t, d = h.shape
    e, _, f = w_gate.shape[1:]
    tf = f // fsplit
    r = mods.shape[1]
    return pl.pallas_call(
        _moe_kernel,
        grid=(t // tm, e, fsplit),
        in_specs=[
            pl.BlockSpec((tm, d), lambda i, j, k: (i, 0)),
            pl.BlockSpec((tm, e), lambda i, j, k: (i, 0)),
            pl.BlockSpec((None, None, d, tf), lambda i, j, k: (layer, j, 0, k)),
            pl.BlockSpec((None, None, d, tf), lambda i, j, k: (layer, j, 0, k)),
            pl.BlockSpec((None, None, tf, d), lambda i, j, k: (layer, j, k, 0)),
            pl.BlockSpec((tm, d), lambda i, j, k: (i, 0)),
            pl.BlockSpec((None, r, d), lambda i, j, k: ((i * tm) // rows_per_b, 0, gt_chunk)),
        ],
        out_specs=pl.BlockSpec((tm, d), lambda i, j, k: (i, 0)),
        out_shape=jax.ShapeDtypeStruct((t, d), F32),
        scratch_shapes=[pltpu.VMEM((tm, d), F32)],
        compiler_params=_params("arbitrary", "arbitrary", "arbitrary"),
        name="moe_experts",
    )(h, gates, w_gate, w_up, w_down, res, mods)


def _experts_kernel(te_ref, nu_ref, idxc_ref, idxn_ref, h_ref, wg_ref, wu_ref, wd_ref, o_ref,
                    wgb, wub, wdb, xbuf, sem, *, tg):
    i = pl.program_id(0)
    nu = nu_ref[0]
    slot = i % 2
    per_blk = SMEM_I32_BLOCK // tg

    def start_gather(idx_ref, tile, dst_slot):
        off = (tile % per_blk) * tg

        def issue(r, carry):
            pltpu.make_async_copy(h_ref.at[pl.ds(idx_ref[off + r], 1)], xbuf.at[dst_slot, pl.ds(r, 1)],
                                  sem.at[dst_slot]).start()
            return carry

        lax.fori_loop(0, tg, issue, 0, unroll=8)

    @pl.when(i == 0)
    def _():
        start_gather(idxc_ref, i, slot)

    @pl.when(i + 1 < nu)
    def _():
        start_gather(idxn_ref, i + 1, 1 - slot)

    prev = te_ref[jnp.maximum(i - 1, 0)]

    @pl.when((i == 0) | (te_ref[i] != prev))
    def _():
        wgb[...] = wg_ref[...].astype(BF16)
        wub[...] = wu_ref[...].astype(BF16)
        wdb[...] = wd_ref[...].astype(BF16)

    @pl.when(i < nu)
    def _():
        def drain(r, carry):
            pltpu.make_async_copy(h_ref.at[pl.ds(0, 1)], xbuf.at[slot, pl.ds(r, 1)], sem.at[slot]).wait()
            return carry

        lax.fori_loop(0, tg, drain, 0, unroll=8)
        x = xbuf[slot].reshape(tg, wgb.shape[0])
        a = jnp.dot(x, wgb[...], preferred_element_type=F32)
        b = jnp.dot(x, wub[...], preferred_element_type=F32)
        hid = (a * jax.nn.sigmoid(a)) * b
        y = jnp.dot(hid.astype(BF16), wdb[...], preferred_element_type=F32)
        o_ref[...] = y.reshape(o_ref.shape)

    @pl.when(i >= nu)
    def _():
        o_ref[...] = jnp.zeros_like(o_ref)


def _experts(tile_expert, n_used, sorted_tok, h3, w_gate, w_up, w_down, layer, tg):
    rp = sorted_tok.shape[0]
    _, _, d, f = w_gate.shape
    blk = SMEM_I32_BLOCK
    per_blk = blk // tg
    last_blk = rp // blk - 1
    row_shape = h3.shape[1:]
    grid_spec = pltpu.PrefetchScalarGridSpec(
        num_scalar_prefetch=2,
        grid=(rp // tg,),
        in_specs=[
            pl.BlockSpec((blk,), lambda i, te, nu: (i // per_blk,), memory_space=pltpu.SMEM),
            pl.BlockSpec((blk,), lambda i, te, nu: (jnp.minimum((i + 1) // per_blk, last_blk),),
                         memory_space=pltpu.SMEM),
            pl.BlockSpec(memory_space=pl.ANY),
            pl.BlockSpec((None, None, d, f), lambda i, te, nu: (layer, te[i], 0, 0)),
            pl.BlockSpec((None, None, d, f), lambda i, te, nu: (layer, te[i], 0, 0)),
            pl.BlockSpec((None, None, f, d), lambda i, te, nu: (layer, te[i], 0, 0)),
        ],
        out_specs=pl.BlockSpec((tg,) + row_shape, lambda i, te, nu: (i, 0, 0)),
        scratch_shapes=[pltpu.VMEM((d, f), BF16), pltpu.VMEM((d, f), BF16), pltpu.VMEM((f, d), BF16),
                        pltpu.VMEM((2, tg) + row_shape, h3.dtype), pltpu.SemaphoreType.DMA((2,))],
    )
    return pl.pallas_call(
        functools.partial(_experts_kernel, tg=tg),
        grid_spec=grid_spec,
        out_shape=jax.ShapeDtypeStruct((rp,) + row_shape, F32),
        compiler_params=_params("arbitrary"),
        name="moe_routed_experts",
    )(tile_expert, n_used, sorted_tok, sorted_tok, h3, w_gate, w_up, w_down)


def _combine_kernel(pos_ref, ys_ref, w_ref, res_ref, gt_ref, o_ref, buf, sem, *, tm):
    def issue(r, carry):
        for k in range(TOP_K):
            pltpu.make_async_copy(ys_ref.at[pl.ds(pos_ref[k * tm + r], 1)], buf.at[k, pl.ds(r, 1)], sem).start()
        return carry

    lax.fori_loop(0, tm, issue, 0, unroll=8)

    def drain(r, carry):
        for k in range(TOP_K):
            pltpu.make_async_copy(ys_ref.at[pl.ds(0, 1)], buf.at[k, pl.ds(r, 1)], sem).wait()
        return carry

    lax.fori_loop(0, tm, drain, 0, unroll=8)
    d = o_ref.shape[1]
    w = w_ref[...]
    m = (buf[0].reshape(tm, d) * jnp.tile(w[:, :LANES], (1, d // LANES))
         + buf[1].reshape(tm, d) * jnp.tile(w[:, LANES:], (1, d // LANES)))
    o_ref[...] = res_ref[...] + gt_ref[...] * m


def _combine(pos, ys, wtok, res, mods, gt_chunk, rows_per_b, tm):
    t, d = res.shape
    assert TOP_K * tm == SMEM_I32_BLOCK
    r = mods.shape[1]
    return pl.pallas_call(
        functools.partial(_combine_kernel, tm=tm),
        grid=(t // tm,),
        in_specs=[pl.BlockSpec((TOP_K * tm,), lambda i: (i,), memory_space=pltpu.SMEM),
                  pl.BlockSpec(memory_space=pl.ANY),
                  pl.BlockSpec((tm, TOP_K * LANES), lambda i: (i, 0)),
                  pl.BlockSpec((tm, d), lambda i: (i, 0)),
                  pl.BlockSpec((None, r, d), lambda i: ((i * tm) // rows_per_b, 0, gt_chunk))],
        out_specs=pl.BlockSpec((tm, d), lambda i: (i, 0)),
        out_shape=jax.ShapeDtypeStruct((t, d), F32),
        scratch_shapes=[pltpu.VMEM((TOP_K, tm) + ys.shape[1:], F32), pltpu.SemaphoreType.DMA(())],
        compiler_params=_params("arbitrary"),
        name="moe_combine",
    )(pos, ys, wtok, res, mods)


def _moe_routed(h3, e_rows, w_rows, w_gate, w_up, w_down, layer, res, mods, gt_chunk, rows_per_b,
                tg=256, tmc=512):
    t = h3.shape[0]
    ne = N_EXPERTS
    a = TOP_K * t
    rp = a + ne * tg
    rp = -(-rp // SMEM_I32_BLOCK) * SMEM_I32_BLOCK
    n_tiles = rp // tg
    ea = e_rows.reshape(a)
    ta = jnp.tile(jnp.arange(t, dtype=I32), TOP_K)
    onehot = (ea[:, None] == jnp.arange(ne, dtype=I32)[None, :]).astype(I32)
    csum = jnp.cumsum(onehot, axis=0)
    rank = jnp.sum((csum - onehot) * onehot, axis=1)
    counts = csum[-1]
    ptiles = (counts + tg - 1) // tg
    tile_end = jnp.cumsum(ptiles)
    tile_start = tile_end - ptiles
    n_used = tile_end[-1:].astype(I32)
    dest = jnp.sum(onehot * tile_start[None, :], axis=1) * tg + rank
    sorted_tok = jnp.zeros((rp,), I32).at[dest].set(ta, unique_indices=True)
    tile_expert = jnp.sum(jnp.arange(n_tiles, dtype=I32)[:, None] >= tile_end[None, :], axis=1)
    tile_expert = jnp.minimum(tile_expert, ne - 1).astype(I32)
    ys = _experts(tile_expert, n_used, sorted_tok, h3, w_gate, w_up, w_down, layer, tg)
    pos = dest.reshape(TOP_K, t // tmc, tmc).transpose(1, 0, 2).reshape(a)
    wtok = jnp.concatenate([jnp.broadcast_to(w_rows[k][:, None], (t, LANES)) for k in range(TOP_K)], axis=1)
    return _combine(pos, ys, wtok, res, mods, gt_chunk, rows_per_b, tmc)


def _kv_kernel(x_ref, g_ref, sc_ref, sh_ref, w_ref, gc_ref, cos_ref, sin_ref, ckv_ref, kpe_ref, wb_ref):
    @pl.when(pl.program_id(0) == 0)
    def _():
        wb_ref[...] = w_ref[...].astype(BF16)

    h = _normmod(x_ref[...], g_ref[...], sc_ref[...], sh_ref[...]).astype(BF16)
    lat = lax.dot_general(h, wb_ref[...], DN_LAST, preferred_element_type=F32)
    c = lat[:, :KV_LORA]
    ms = jnp.mean(c * c, axis=-1, keepdims=True)
    ckv_ref[...] = (c * lax.rsqrt(ms + EPS)) * gc_ref[...]
    kpe_ref[...] = _rope(lat[:, KV_LORA:], cos_ref[...], sin_ref[...])


def _kv(x, g_kv, mods, rows_per_b, tm, w_dkv_t, g_ckv, cos2, sin2):
    t, d = x.shape
    n = w_dkv_t.shape[0]
    nt = cos2.shape[0] // tm
    return pl.pallas_call(
        _kv_kernel,
        grid=(t // tm, 1),
        in_specs=[
            pl.BlockSpec((tm, d), lambda i, j: (i, 0)),
            pl.BlockSpec((1, d), lambda i, j: (0, 0)),
            _mod_spec(mods, 1, tm, rows_per_b, d),
            _mod_spec(mods, 0, tm, rows_per_b, d),
            pl.BlockSpec((n, d), lambda i, j: (0, 0)),
            pl.BlockSpec((1, KV_LORA), lambda i, j: (0, 0)),
            pl.BlockSpec((tm, QK_ROPE), lambda i, j: (i % nt, 0)),
            pl.BlockSpec((tm, QK_ROPE), lambda i, j: (i % nt, 0)),
        ],
        out_specs=[pl.BlockSpec((tm, KV_LORA), lambda i, j: (i, 0)),
                   pl.BlockSpec((tm, QK_ROPE), lambda i, j: (i, 0))],
        out_shape=[jax.ShapeDtypeStruct((t, KV_LORA), F32), jax.ShapeDtypeStruct((t, QK_ROPE), F32)],
        scratch_shapes=[pltpu.VMEM((n, d), BF16)],
        compiler_params=_params("arbitrary", "arbitrary"),
        name="latent_kv",
    )(x, g_kv.reshape(1, d), mods, mods, w_dkv_t, g_ckv.reshape(1, KV_LORA), cos2, sin2)


def _qproj_kernel(x_ref, g_ref, sc_ref, sh_ref, w_ref, cos_ref, sin_ref, q_ref, hb_ref):
    @pl.when(pl.program_id(1) == 0)
    def _():
        hb_ref[...] = _normmod(x_ref[...], g_ref[...], sc_ref[...], sh_ref[...]).astype(BF16)

    acc = jnp.dot(hb_ref[...], w_ref[...].astype(BF16), preferred_element_type=F32)
    q_ref[:, :QK_NOPE] = acc[:, :QK_NOPE].astype(BF16)
    q_ref[:, QK_NOPE:] = _rope(acc[:, QK_NOPE:], cos_ref[...], sin_ref[...]).astype(BF16)


def _qproj(x, g, mods, sc_chunk, sh_chunk, rows_per_b, tm, w_qt, cos2, sin2):
    t, d = x.shape
    nh = w_qt.shape[0]
    nt = rows_per_b // tm
    return pl.pallas_call(
        _qproj_kernel,
        grid=(t // tm, nh),
        in_specs=[
            pl.BlockSpec((tm, d), lambda i, j: (i, 0)),
            pl.BlockSpec((1, d), lambda i, j: (0, 0)),
            _mod_spec(mods, sc_chunk, tm, rows_per_b, d),
            _mod_spec(mods, sh_chunk, tm, rows_per_b, d),
            pl.BlockSpec((None, d, QK_HEAD), lambda i, j: (j, 0, 0)),
            pl.BlockSpec((tm, QK_ROPE), lambda i, j: (i % nt, 0)),
            pl.BlockSpec((tm, QK_ROPE), lambda i, j: (i % nt, 0)),
        ],
        out_specs=pl.BlockSpec((None, None, tm, QK_HEAD), lambda i, j: (i // nt, j, i % nt, 0)),
        out_shape=jax.ShapeDtypeStruct((t // rows_per_b, nh, rows_per_b, QK_HEAD), BF16),
        scratch_shapes=[pltpu.VMEM((tm, d), BF16)],
        compiler_params=_params("arbitrary", "arbitrary"),
        name="q_proj",
    )(x, g.reshape(1, d), mods, mods, w_qt, cos2, sin2)


def _kvup_kernel(ckv_ref, kpe_ref, wk_ref, wv_ref, k_ref, v_ref, *, hb):
    c = ckv_ref[...].astype(BF16)
    kn = jnp.dot(c, wk_ref[...].astype(BF16), preferred_element_type=F32)
    vv = jnp.dot(c, wv_ref[...].astype(BF16), preferred_element_type=F32)
    kpe = kpe_ref[...].astype(BF16)
    for h in range(hb):
        k_ref[h, :, :QK_NOPE] = kn[:, h * QK_NOPE:(h + 1) * QK_NOPE].astype(BF16)
        k_ref[h, :, QK_NOPE:] = kpe
        v_ref[h] = vv[:, h * V_HEAD:(h + 1) * V_HEAD].astype(BF16)


def _kvup(ckv, kpe, w_uk2, w_uv2, bsz, seq, tm, hb=4):
    t, r = ckv.shape
    nt = seq // tm
    return pl.pallas_call(
        functools.partial(_kvup_kernel, hb=hb),
        grid=(t // tm, N_HEADS // hb),
        in_specs=[
            pl.BlockSpec((tm, r), lambda i, j: (i, 0)),
            pl.BlockSpec((tm, QK_ROPE), lambda i, j: (i, 0)),
            pl.BlockSpec((r, hb * QK_NOPE), lambda i, j: (0, j)),
            pl.BlockSpec((r, hb * V_HEAD), lambda i, j: (0, j)),
        ],
        out_specs=[pl.BlockSpec((None, hb, tm, QK_HEAD), lambda i, j: (i // nt, j, i % nt, 0)),
                   pl.BlockSpec((None, hb, tm, V_HEAD), lambda i, j: (i // nt, j, i % nt, 0))],
        out_shape=[jax.ShapeDtypeStruct((bsz, N_HEADS, seq, QK_HEAD), BF16),
                   jax.ShapeDtypeStruct((bsz, N_HEADS, seq, V_HEAD), BF16)],
        compiler_params=_params("arbitrary", "arbitrary"),
        name="kv_up",
    )(ckv, kpe, w_uk2, w_uv2)


def _flash_kernel(qt_ref, kt_ref, q_ref, k_ref, v_ref, o_ref, m_ref, l_ref, acc_ref, *, tq, tk, hb):
    p_idx = pl.program_id(2)
    qi = qt_ref[p_idx]
    kj = kt_ref[p_idx]

    @pl.when(kj == 0)
    def _():
        m_ref[...] = jnp.full_like(m_ref, -jnp.inf)
        l_ref[...] = jnp.zeros_like(l_ref)
        acc_ref[...] = jnp.zeros_like(acc_ref)

    def tile(h, masked):
        s = lax.dot_general(q_ref[h], k_ref[h], DN_LAST, preferred_element_type=F32) * ATTN_SCALE
        if masked:
            qpos = lax.broadcasted_iota(I32, (tq, tk), 0)
            kpos = lax.broadcasted_iota(I32, (tq, tk), 1)
            s = jnp.where(kpos <= qpos, s, -jnp.inf)
        m_prev = m_ref[h]
        m_next = jnp.maximum(m_prev, jnp.max(s, axis=1, keepdims=True))
        alpha = jnp.exp(m_prev - m_next)
        p = jnp.exp(s - jnp.tile(m_next, (1, tk // LANES)))
        l_ref[h] = alpha * l_ref[h] + jnp.sum(p, axis=1, keepdims=True)
        acc_ref[h] = alpha * acc_ref[h] + jnp.dot(p.astype(BF16), v_ref[h], preferred_element_type=F32)
        m_ref[h] = m_next

    @pl.when(kj < qi)
    def _():
        for h in range(hb):
            tile(h, False)

    @pl.when(kj == qi)
    def _():
        for h in range(hb):
            tile(h, True)
            o_ref[:, h * V_HEAD:(h + 1) * V_HEAD] = (acc_ref[h] / l_ref[h]).astype(BF16)


def _flash(q, k, v, tile_len=512, hb=2):
    bsz, nh, seq, _ = q.shape
    tq = tk = tile_len
    nq = seq // tq
    pairs = [(i, j) for i in range(nq) for j in range(i + 1)]
    qt = jnp.asarray([p[0] for p in pairs], I32)
    kt = jnp.asarray([p[1] for p in pairs], I32)
    grid_spec = pltpu.PrefetchScalarGridSpec(
        num_scalar_prefetch=2,
        grid=(bsz, nh // hb, len(pairs)),
        in_specs=[
            pl.BlockSpec((None, hb, tq, QK_HEAD), lambda b, h, p, qt, kt: (b, h, qt[p], 0)),
            pl.BlockSpec((None, hb, tk, QK_HEAD), lambda b, h, p, qt, kt: (b, h, kt[p], 0)),
            pl.BlockSpec((None, hb, tk, V_HEAD), lambda b, h, p, qt, kt: (b, h, kt[p], 0)),
        ],
        out_specs=pl.BlockSpec((tq, hb * V_HEAD), lambda b, h, p, qt, kt: (b * nq + qt[p], h)),
        scratch_shapes=[pltpu.VMEM((hb, tq, LANES), F32), pltpu.VMEM((hb, tq, LANES), F32),
                        pltpu.VMEM((hb, tq, V_HEAD), F32)],
    )
    return pl.pallas_call(
        functools.partial(_flash_kernel, tq=tq, tk=tk, hb=hb),
        grid_spec=grid_spec,
        out_shape=jax.ShapeDtypeStruct((bsz * seq, nh * V_HEAD), BF16),
        compiler_params=_params("arbitrary", "arbitrary", "arbitrary"),
        name="flash_attn",
    )(qt, kt, q, k, v)


def _qlat_kernel(q_ref, wk_ref, o_ref):
    q = q_ref[...]
    qlat = lax.dot_general(q[:, :QK_NOPE], wk_ref[...].astype(BF16), DN_LAST, preferred_element_type=F32)
    o_ref[:, :KV_LORA] = qlat.astype(BF16)
    o_ref[:, KV_LORA:] = q[:, QK_NOPE:]


def _qlat(q, w_uk2):
    nh, b, _ = q.shape
    r = w_uk2.shape[0]
    return pl.pallas_call(
        _qlat_kernel,
        grid=(nh,),
        in_specs=[pl.BlockSpec((None, b, QK_HEAD), lambda h: (h, 0, 0)),
                  pl.BlockSpec((r, QK_NOPE), lambda h: (0, h))],
        out_specs=pl.BlockSpec((None, b, r + QK_ROPE), lambda h: (h, 0, 0)),
        out_shape=jax.ShapeDtypeStruct((nh, b, r + QK_ROPE), BF16),
        compiler_params=_params("arbitrary"),
        name="q_latent",
    )(q, w_uk2)


def _paged_kernel(pt_ref, q_ref, *refs, npg, nsteps):
    ckv_refs = refs[:npg]
    kpe_refs = refs[npg:2 * npg]
    cnew_ref, pnew_ref, o_ref, qpad_ref, kcb_ref, pp_ref, m_ref, l_ref, acc_ref = refs[2 * npg:]
    bi = pl.program_id(0)
    j = pl.program_id(1)
    nh = N_HEADS
    wide = KV_LORA // LANES
    first = j == 0

    @pl.when(first & (bi == 0))
    def _():
        kcb_ref[...] = jnp.zeros_like(kcb_ref)
        pp_ref[...] = jnp.zeros_like(pp_ref)
        m_ref[...] = jnp.zeros_like(m_ref)
        l_ref[...] = jnp.zeros_like(l_ref)
        acc_ref[...] = jnp.zeros_like(acc_ref)

    @pl.when(first)
    def _():
        qpad_ref[...] = jnp.zeros_like(qpad_ref)
        qpad_ref[:nh, :] = q_ref[...]

    def step(cur, prv):
        pv_prev = jnp.dot(pp_ref[...], kcb_ref[prv], preferred_element_type=F32)
        acc_in = jnp.where(first, 0.0, acc_ref[...] + pv_prev)
        m_prev = jnp.where(first, -jnp.inf, m_ref[...])
        l_prev = jnp.where(first, 0.0, l_ref[...])
        for i in range(npg):
            kcb_ref[cur, i * PAGE_SIZE:(i + 1) * PAGE_SIZE, :] = ckv_refs[i][...].astype(BF16)
        s_c = lax.dot_general(kcb_ref[cur], qpad_ref[:, :KV_LORA], DN_LAST,
                              preferred_element_type=F32)
        kp = jnp.concatenate([kpe_refs[i][...].astype(BF16) for i in range(npg)], axis=1)
        s_p = jnp.dot(qpad_ref[:, KV_LORA:], kp, preferred_element_type=F32)
        s = (s_c.T[:nh, :] + s_p[:nh, :]) * ATTN_SCALE
        m_next = jnp.maximum(m_prev, jnp.max(s, axis=1, keepdims=True))
        alpha = jnp.exp(m_prev - m_next)
        p = jnp.exp(s - jnp.tile(m_next, (1, s.shape[1] // LANES)))
        l_ref[...] = alpha * l_prev + jnp.sum(p, axis=1, keepdims=True)
        acc_ref[...] = jnp.tile(alpha, (1, wide)) * acc_in
        m_ref[...] = m_next
        pp_ref[...] = p.astype(BF16)

    @pl.when(j % 2 == 0)
    def _():
        step(0, 1)

    @pl.when(j % 2 == 1)
    def _():
        step(1, 0)

    @pl.when(j == nsteps - 1)
    def _():
        last_slot = (nsteps - 1) % 2
        acc_past = acc_ref[...] + jnp.dot(pp_ref[...], kcb_ref[last_slot], preferred_element_type=F32)
        q = q_ref[...].astype(F32)
        cnew = cnew_ref[...]
        s_new = (jnp.sum(q[:, :KV_LORA] * cnew, axis=1, keepdims=True)
                 + jnp.sum(q[:, KV_LORA:] * pnew_ref[...], axis=1, keepdims=True)) * ATTN_SCALE
        m_p = m_ref[...]
        m_n = jnp.maximum(m_p, s_new)
        a = jnp.exp(m_p - m_n)
        p_new = jnp.exp(s_new - m_n)
        l_fin = a * l_ref[...] + p_new
        acc = jnp.tile(a, (1, wide)) * acc_past + jnp.tile(p_new, (1, wide)) * cnew
        o_ref[...] = (acc / jnp.tile(l_fin, (1, wide))).astype(BF16)


def _paged_attn(qcat, cache_ckv, cache_kpe_t, page_table, ckv_new, kpe_new, npg=8):
    b, nh, w = qcat.shape
    r = KV_LORA
    n_pages = page_table.shape[1]
    ckv_specs = [pl.BlockSpec((None, PAGE_SIZE, r), functools.partial(
        lambda bi, j, pt, i: (pt[bi, j * npg + i], 0, 0), i=i)) for i in range(npg)]
    kpe_specs = [pl.BlockSpec((None, QK_ROPE, PAGE_SIZE), functools.partial(
        lambda bi, j, pt, i: (pt[bi, j * npg + i], 0, 0), i=i)) for i in range(npg)]
    grid_spec = pltpu.PrefetchScalarGridSpec(
        num_scalar_prefetch=1,
        grid=(b, n_pages // npg),
        in_specs=[pl.BlockSpec((None, nh, w), lambda bi, j, pt: (bi, 0, 0))] + ckv_specs + kpe_specs + [
            pl.BlockSpec((None, 1, r), lambda bi, j, pt: (bi, 0, 0)),
            pl.BlockSpec((None, 1, QK_ROPE), lambda bi, j, pt: (bi, 0, 0)),
        ],
        out_specs=pl.BlockSpec((None, nh, r), lambda bi, j, pt: (bi, 0, 0)),
        scratch_shapes=[pltpu.VMEM((LANES, w), BF16), pltpu.VMEM((2, npg * PAGE_SIZE, r), BF16),
                        pltpu.VMEM((nh, npg * PAGE_SIZE), BF16), pltpu.VMEM((nh, LANES), F32),
                        pltpu.VMEM((nh, LANES), F32), pltpu.VMEM((nh, r), F32)],
    )
    return pl.pallas_call(
        functools.partial(_paged_kernel, npg=npg, nsteps=n_pages // npg),
        grid_spec=grid_spec,
        out_shape=jax.ShapeDtypeStruct((b, nh, r), BF16),
        compiler_params=_params("arbitrary", "arbitrary"),
        name="paged_attn",
    )(page_table, qcat, *([cache_ckv] * npg), *([cache_kpe_t] * npg),
      ckv_new.reshape(b, 1, r), kpe_new.reshape(b, 1, QK_ROPE))


def _ovup_kernel(o_ref, wv_ref, a_ref):
    a_ref[...] = jnp.dot(o_ref[...], wv_ref[...].astype(BF16), preferred_element_type=F32).astype(BF16)


def _ovup(olat, w_uv2):
    nh, b, r = olat.shape
    return pl.pallas_call(
        _ovup_kernel,
        grid=(nh,),
        in_specs=[pl.BlockSpec((None, b, r), lambda h: (h, 0, 0)), pl.BlockSpec((r, V_HEAD), lambda h: (0, h))],
        out_specs=pl.BlockSpec((b, V_HEAD), lambda h: (0, h)),
        out_shape=jax.ShapeDtypeStruct((b, nh * V_HEAD), BF16),
        compiler_params=_params("arbitrary"),
        name="o_latent_up",
    )(olat, w_uv2)


def _rope_tables(pos):
    inv = ROPE_THETA ** (-jnp.arange(0, QK_ROPE, 2, dtype=F32) / QK_ROPE)
    ang = pos.astype(F32)[:, None] * inv[None, :]
    cos, sin = jnp.cos(ang), jnp.sin(ang)
    return jnp.concatenate([cos, cos], axis=1), jnp.concatenate([-sin, sin], axis=1)


def _trunk(x, mods, mods_kv, rows_per_b, tm, cos2, sin2, P, s5_glu_fn, attn_fn, routed):
    tms = min(tm, 512)

    def moe_block(x, layer):
        h2, gates_t, e_rows, w_rows = _route(x, P["g_norm"][layer, 1], mods[layer], 4, 3, rows_per_b, tms,
                                             P["w_router_t"], P["b_router"], split=routed)
        if routed:
            return _moe_routed(h2, e_rows, w_rows, P["w_gate"], P["w_up"], P["w_down"], layer, x,
                               mods[layer], 5, rows_per_b)
        return _moe(h2, gates_t.T, P["w_gate"], P["w_up"], P["w_down"], layer, x, mods[layer], 5,
                    rows_per_b, tms)

    x, st_re, st_im = s5_glu_fn(x)
    x = moe_block(x, 0)
    ckv, kpe = _kv(x, P["g_kv"], mods_kv, rows_per_b, tms, P["w_dkv_t"], P["g_ckv"], cos2, sin2)
    q = _qproj(x, P["g_norm"][1, 0], mods[1], 1, 0, rows_per_b, tm, P["w_qt"], cos2, sin2)
    attn = attn_fn(q, ckv, kpe)
    x = _proj_res(attn, P["w_o"][0], x, mods[1], 2, rows_per_b, tm, 512, glu=False)
    x = moe_block(x, 1)
    y = _final_norm(x, P["g_final"], tms)
    return y, st_re, st_im, ckv, kpe


def kernel(x_prompt, x_sample, c_prompt, c_sample, state_ssm_re, state_ssm_im, cache_ckv, cache_kpe, page_table, g_norm, w_ada, b_ada, ssm_lam_re, ssm_lam_im, ssm_log_dt, ssm_b_re, ssm_b_im, ssm_c_re, ssm_c_im, ssm_d, w_glu, g_kv, w_ada_kv, b_ada_kv, w_dkv, g_ckv, w_uk, w_uv, w_q, w_o, w_router, b_router, w_gate, w_up, w_down, g_final):
    bsz, seq, d = x_prompt.shape
    db = x_sample.shape[0]
    g = d // SSM_GROUP
    n = SSM_STATE
    assert g == LANES, "the group-major relayout puts one S5 group per lane"
    past = page_table.shape[1] * PAGE_SIZE
    P = dict(g_norm=g_norm, g_kv=g_kv, g_ckv=g_ckv, w_o=w_o, b_router=b_router,
             w_gate=w_gate, w_up=w_up, w_down=w_down, g_final=g_final)
    P["w_qt"] = w_q[0].reshape(d, N_HEADS, QK_HEAD).transpose(1, 0, 2)
    P["w_dkv_t"] = w_dkv.T
    P["w_router_t"] = w_router.T
    w_uk2 = w_uk.reshape(KV_LORA, N_HEADS * QK_NOPE)
    w_uv2 = w_uv.reshape(KV_LORA, N_HEADS * V_HEAD)

    m_rows = db + bsz
    m_pad = -(-m_rows // 16) * 16
    c_all = jnp.concatenate([c_sample, c_prompt, jnp.zeros((m_pad - m_rows, d), F32)], axis=0)
    mod = _ada(c_all, w_ada, b_ada)
    mod_kv = _ada(c_all, w_ada_kv[None], b_ada_kv[None])
    mods_s = [mod[l, :db][None] for l in range(mod.shape[0])]
    mods_p = [mod[l, db:m_rows][:, None, :] for l in range(mod.shape[0])]
    modkv_s = mod_kv[0, :db][None]
    modkv_p = mod_kv[0, db:m_rows][:, None, :]

    lam_re = ssm_lam_re[0].reshape(g, 1, n)
    lam_im = ssm_lam_im[0].reshape(g, 1, n)
    log_dt = jnp.broadcast_to(ssm_log_dt[0].reshape(g, 1, 1), (g, 1, n))
    bt_re = ssm_b_re[0].transpose(0, 2, 1)
    bt_im = ssm_b_im[0].transpose(0, 2, 1)
    c_re, c_im = ssm_c_re[0], ssm_c_im[0]
    d_g = ssm_d[0].reshape(g, 1, SSM_GROUP)
    d_rep = jnp.tile(d_g, (1, 1, S5_CHUNK))
    dup = lambda a: jnp.concatenate([a, a], axis=-1)
    cat = lambda a, b: jnp.concatenate([a, b], axis=-1)
    nchunk = seq // S5_CHUNK
    tm_p = 1024

    src = jnp.arange(d, dtype=I32)
    dst = (src % SSM_GROUP) * g + src // SSM_GROUP
    perm = (dst[:, None] == jnp.arange(d, dtype=I32)[None, :]).astype(BF16)
    w_glu_p = w_glu[0].reshape(g, SSM_GROUP, -1).transpose(1, 0, 2).reshape(d, -1)

    def s5_glu_prompt(x):
        hx = _s5_in(x, g_norm[0, 0], mods_p[0], 1, 0, seq, 256, perm)
        hx = hx.reshape(g, bsz * nchunk, S5_CHUNK * SSM_GROUP)
        zg, st = _s5_prompt(hx, dup(lam_re), dup(lam_im), dup(log_dt), cat(bt_re, bt_im), cat(bt_im, bt_re),
                            cat(c_re, c_im), cat(c_im, c_re), d_rep, bsz)
        zg = zg.reshape(g, bsz * seq * SSM_GROUP)
        x = _proj_res(zg, w_glu_p, x, mods_p[0], 2, seq, tm_p, 256, glu=True, grouped=True)
        st = st.transpose(1, 0, 2)[None]
        return x, st[..., :n], st[..., n:]

    def s5_glu_sample(x):
        h = _normmod_call(x, g_norm[0, 0], mods_s[0], 1, 0, db, db, F32)
        u = h.reshape(db, g, SSM_GROUP).transpose(1, 0, 2)
        h0_re = state_ssm_re[0].transpose(1, 0, 2)
        h0_im = state_ssm_im[0].transpose(1, 0, 2)
        zg, nr, ni = _s5_sample(u, h0_re, h0_im, lam_re, lam_im, log_dt, bt_re, bt_im, c_re, c_im, d_g)
        z = zg.transpose(1, 0, 2).reshape(db, d)
        x = _proj_res(z, w_glu[0], x, mods_s[0], 2, db, db, 256, glu=True)
        return x, nr.transpose(1, 0, 2)[None], ni.transpose(1, 0, 2)[None]

    def attn_prompt(q, ckv, kpe):
        k, v = _kvup(ckv, kpe, w_uk2, w_uv2, bsz, seq, tm_p)
        return _flash(q, k, v)

    def attn_sample(q, ckv, kpe):
        qcat = _qlat(q[0], w_uk2).transpose(1, 0, 2)
        olat = _paged_attn(qcat, cache_ckv, cache_kpe.transpose(0, 2, 1), page_table, ckv, kpe)
        return _ovup(olat.transpose(1, 0, 2), w_uv2)

    cos_p, sin_p = _rope_tables(jnp.arange(seq))
    cos_s, sin_s = _rope_tables(jnp.full((db,), past))

    y_p, sre_p, sim_p, ckv_p, kpe_p = _trunk(
        x_prompt.reshape(bsz * seq, d), mods_p, modkv_p, seq, tm_p, cos_p, sin_p, P,
        s5_glu_prompt, attn_prompt, routed=True)
    y_s, sre_s, sim_s, ckv_s, kpe_s = _trunk(
        x_sample.reshape(db, d), mods_s, modkv_s, db, db, cos_s, sin_s, P,
        s5_glu_sample, attn_sample, routed=False)
    return (y_p.reshape(bsz, seq, d), y_s.reshape(db, 1, d), sre_p, sim_p,
            ckv_p.reshape(bsz, seq, KV_LORA), kpe_p.reshape(bsz, seq, QK_ROPE),
            sre_s, sim_s, ckv_s.reshape(db, 1, KV_LORA), kpe_s.reshape(db, 1, QK_ROPE))
```

```python
import functools
import math

import jax
import jax.numpy as jnp
from jax import lax
from jax.experimental import pallas as pl
from jax.experimental.pallas import tpu as pltpu

F32 = jnp.float32
BF16 = jnp.bfloat16
I32 = jnp.int32

SSM_GROUP = 16
SSM_STATE = 64
N_HEADS = 16
QK_NOPE = 128
QK_ROPE = 64
V_HEAD = 128
KV_LORA = 512
ROPE_THETA = 10000.0
PAGE_SIZE = 128
N_EXPERTS = 16
N_EXPERT_GROUPS = 4
EXPERTS_PER_GROUP = N_EXPERTS // N_EXPERT_GROUPS
TOP_K = 2
EPS = 1e-6
ATTN_SCALE = (QK_NOPE + QK_ROPE) ** -0.5
QK_HEAD = QK_NOPE + QK_ROPE

V7X_VMEM_BYTES = 64 * 1024 * 1024
VMEM_LIMIT_BYTES = V7X_VMEM_BYTES - 8 * 1024 * 1024
LANES = 128
SMEM_I32_BLOCK = 1024
S5_CHUNK = 16
DN_LAST = (((1,), (1,)), ((), ()))


def _params(*sem):
    return pltpu.CompilerParams(dimension_semantics=sem, vmem_limit_bytes=VMEM_LIMIT_BYTES)


def _normmod(x, g, sc, sh):
    ms = jnp.mean(x * x, axis=-1, keepdims=True)
    return (x * lax.rsqrt(ms + EPS)) * g * (1.0 + sc) + sh


def _rope(p, cos2, sin2):
    half = QK_ROPE // 2
    swapped = jnp.concatenate([p[:, half:], p[:, :half]], axis=1)
    return p * cos2 + swapped * sin2


def _mod_spec(mods, chunk, tm, rows_per_b, width, j_blocks=None):
    _, r, _ = mods.shape
    if j_blocks is None:
        return pl.BlockSpec((None, r, width), lambda i, j: ((i * tm) // rows_per_b, 0, chunk))
    return pl.BlockSpec((None, r, width), lambda i, j: ((i * tm) // rows_per_b, 0, chunk * j_blocks + j))


def _ada_kernel(c_ref, w_ref, b_ref, o_ref, cb_ref):
    @pl.when((pl.program_id(0) == 0) & (pl.program_id(1) == 0))
    def _():
        c = c_ref[...]
        cb_ref[...] = (c * jax.nn.sigmoid(c)).astype(BF16)

    acc = jnp.dot(cb_ref[...], w_ref[...].astype(BF16), preferred_element_type=F32)
    o_ref[...] = acc + b_ref[...]


def _ada(c_all, w, b, tn=1024):
    m, d = c_all.shape
    lw, _, n = w.shape
    return pl.pallas_call(
        _ada_kernel,
        grid=(lw, n // tn),
        in_specs=[
            pl.BlockSpec((m, d), lambda l, j: (0, 0)),
            pl.BlockSpec((None, d, tn), lambda l, j: (l, 0, j)),
            pl.BlockSpec((None, 1, tn), lambda l, j: (l, 0, j)),
        ],
        out_specs=pl.BlockSpec((None, m, tn), lambda l, j: (l, 0, j)),
        out_shape=jax.ShapeDtypeStruct((lw, m, n), F32),
        scratch_shapes=[pltpu.VMEM((m, d), BF16)],
        compiler_params=_params("arbitrary", "arbitrary"),
        name="ada_mod",
    )(c_all, w, b.reshape(lw, 1, n))


def _normmod_kernel(x_ref, g_ref, sc_ref, sh_ref, o_ref):
    o_ref[...] = _normmod(x_ref[...], g_ref[...], sc_ref[...], sh_ref[...]).astype(o_ref.dtype)


def _normmod_call(x, g, mods, sc_chunk, sh_chunk, rows_per_b, tm, out_dtype):
    t, d = x.shape
    return pl.pallas_call(
        _normmod_kernel,
        grid=(t // tm, 1),
        in_specs=[
            pl.BlockSpec((tm, d), lambda i, j: (i, 0)),
            pl.BlockSpec((1, d), lambda i, j: (0, 0)),
            _mod_spec(mods, sc_chunk, tm, rows_per_b, d),
            _mod_spec(mods, sh_chunk, tm, rows_per_b, d),
        ],
        out_specs=pl.BlockSpec((tm, d), lambda i, j: (i, 0)),
        out_shape=jax.ShapeDtypeStruct((t, d), out_dtype),
        compiler_params=_params("arbitrary", "arbitrary"),
        name="norm_mod",
    )(x, g.reshape(1, d), mods, mods)


def _final_norm_kernel(x_ref, g_ref, o_ref):
    x = x_ref[...]
    ms = jnp.mean(x * x, axis=-1, keepdims=True)
    o_ref[...] = (x * lax.rsqrt(ms + EPS)) * g_ref[...]


def _final_norm(x, g, tm):
    t, d = x.shape
    return pl.pallas_call(
        _final_norm_kernel,
        grid=(t // tm,),
        in_specs=[pl.BlockSpec((tm, d), lambda i: (i, 0)), pl.BlockSpec((1, d), lambda i: (0, 0))],
        out_specs=pl.BlockSpec((tm, d), lambda i: (i, 0)),
        out_shape=jax.ShapeDtypeStruct((t, d), F32),
        compiler_params=_params("arbitrary"),
        name="final_norm",
    )(x, g.reshape(1, d))


def _proj_res_kernel(x_ref, *refs, glu, grouped, tm):
    refs = list(refs)
    xb_ref = refs.pop() if grouped else None
    if glu:
        wa_ref, wb_ref, res_ref, gt_ref, o_ref = refs
    else:
        wa_ref, res_ref, gt_ref, o_ref = refs
    if grouped:
        @pl.when(pl.program_id(1) == 0)
        def _():
            cg = SSM_GROUP
            for k in range(tm // LANES):
                zt = x_ref[:, k * LANES * cg:(k + 1) * LANES * cg].T
                xb_ref[k * LANES:(k + 1) * LANES, :] = zt.reshape(LANES, cg, LANES).reshape(LANES, cg * LANES)

        x = xb_ref[...]
    else:
        x = x_ref[...]
    y = jnp.dot(x, wa_ref[...].astype(BF16), preferred_element_type=F32)
    if glu:
        yb = jnp.dot(x, wb_ref[...].astype(BF16), preferred_element_type=F32)
        y = y * jax.nn.sigmoid(yb)
    o_ref[...] = res_ref[...] + gt_ref[...] * y


def _proj_res(x, w, res, mods, gt_chunk, rows_per_b, tm, tn, glu, grouped=False):
    t, n = res.shape
    k = w.shape[0]
    nj = n // tn
    if grouped:
        x_spec = pl.BlockSpec((x.shape[0], tm * SSM_GROUP), lambda i, j: (0, i))
    else:
        x_spec = pl.BlockSpec((tm, k), lambda i, j: (i, 0))
    in_specs = [x_spec, pl.BlockSpec((k, tn), lambda i, j: (0, j))]
    args = [x, w]
    if glu:
        in_specs.append(pl.BlockSpec((k, tn), lambda i, j: (0, nj + j)))
        args.append(w)
    in_specs += [pl.BlockSpec((tm, tn), lambda i, j: (i, j)), _mod_spec(mods, gt_chunk, tm, rows_per_b, tn, nj)]
    args += [res, mods]
    return pl.pallas_call(
        functools.partial(_proj_res_kernel, glu=glu, grouped=grouped, tm=tm),
        grid=(t // tm, nj),
        in_specs=in_specs,
        out_specs=pl.BlockSpec((tm, tn), lambda i, j: (i, j)),
        out_shape=jax.ShapeDtypeStruct((t, n), F32),
        scratch_shapes=[pltpu.VMEM((tm, k), BF16)] if grouped else [],
        compiler_params=_params("arbitrary", "arbitrary"),
        name="glu_proj" if glu else "out_proj",
    )(*args)


def _s5in_kernel(x_ref, g_ref, sc_ref, sh_ref, p_ref, o_ref):
    h = _normmod(x_ref[...], g_ref[...], sc_ref[...], sh_ref[...]).astype(BF16)
    hp = jnp.dot(h, p_ref[...], preferred_element_type=F32).astype(BF16)
    tm = hp.shape[0]
    o_ref[...] = hp.reshape(tm, SSM_GROUP, LANES).reshape(tm * SSM_GROUP, LANES).T


def _s5_in(x, g, mods, sc_chunk, sh_chunk, rows_per_b, tm, perm):
    t, d = x.shape
    ng = d // SSM_GROUP
    return pl.pallas_call(
        _s5in_kernel,
        grid=(t // tm, 1),
        in_specs=[
            pl.BlockSpec((tm, d), lambda i, j: (i, 0)),
            pl.BlockSpec((1, d), lambda i, j: (0, 0)),
            _mod_spec(mods, sc_chunk, tm, rows_per_b, d),
            _mod_spec(mods, sh_chunk, tm, rows_per_b, d),
            pl.BlockSpec((d, d), lambda i, j: (0, 0)),
        ],
        out_specs=pl.BlockSpec((ng, tm * SSM_GROUP), lambda i, j: (0, i)),
        out_shape=jax.ShapeDtypeStruct((ng, t * SSM_GROUP), BF16),
        compiler_params=_params("arbitrary", "arbitrary"),
        name="s5_in",
    )(x, g.reshape(1, d), mods, mods, perm)


def _cpow(lr_dt, li_dt, k):
    mag = jnp.exp(k * lr_dt)
    ang = k * li_dt
    return mag * jnp.cos(ang), mag * jnp.sin(ang)


def _rep_rows(e, n):
    q, w = e.shape
    return jnp.concatenate([jnp.broadcast_to(e[t:t + 1], (n, w)) for t in range(q)], axis=0)


def _tile_rows(c, n):
    return jnp.concatenate([c] * n, axis=0)


def _zoh(lr, li, ldt):
    dt = jnp.exp(ldt)
    lr_dt = lr * dt
    li_dt = li * dt
    mag = jnp.exp(lr_dt)
    ab_re = mag * jnp.cos(li_dt)
    ab_im = mag * jnp.sin(li_dt)
    den = lr * lr + li * li
    f_re = ((ab_re - 1.0) * lr + ab_im * li) / den
    f_im = (ab_im * lr - (ab_re - 1.0) * li) / den
    return lr_dt, li_dt, ab_re, ab_im, f_re, f_im


def _gelu_tanh(y):
    return 0.5 * y * (1.0 + jnp.tanh(math.sqrt(2.0 / math.pi) * (y + 0.044715 * (y * y * y))))


def _s5_prompt_kernel(x_ref, lr_ref, li_ref, ldt_ref, bt_ref, bts_ref, c_ref, cs_ref, d_ref,
                      z_ref, st_ref, s_sc, ss_sc, hp_sc, yl_sc, *, gb, nb, nchunk):
    q, cg, n = S5_CHUNK, SSM_GROUP, SSM_STATE
    half = q // 2
    ti = lax.broadcasted_iota(I32, (q, 1), 0).astype(F32)
    rows = lax.broadcasted_iota(I32, (q * cg, q * cg), 0)
    cols = lax.broadcasted_iota(I32, (q * cg, q * cg), 1)
    shift = cg.bit_length() - 1
    causal = (rows >> shift) <= (cols >> shift)
    diag = rows == cols
    lane = lax.broadcasted_iota(I32, (1, 2 * n), 1)
    sgn = jnp.where(lane < n, -1.0, 1.0)

    def cmul(x2, xs, er, ei):
        return x2 * er + (xs * sgn) * ei

    a_terms = []
    vcs = []
    for g in range(gb):
        lr_dt, li_dt, _, _, f_re, f_im = _zoh(lr_ref[g], li_ref[g], ldt_ref[g])
        bt2, bts = bt_ref[g], bts_ref[g]
        bb2 = cmul(bt2, bts, f_re, f_im)
        bbs = bts * f_re - (bt2 * sgn) * f_im
        bb2t, bbst = _tile_rows(bb2, q), _tile_rows(bbs, q)
        c2t, cst = _tile_rows(c_ref[g], q), _tile_rows(cs_ref[g], q)

        def table(k, lr_dt=lr_dt, li_dt=li_dt):
            er, ei = _cpow(lr_dt, li_dt, k)
            return _rep_rows(er, cg), _rep_rows(ei, cg)

        pc = cmul(c2t, cst, *table(ti - half)) * (-sgn)
        pb = cmul(bb2t, bbst, *table(half - ti))
        tt = lax.dot_general(pb.astype(BF16), pc.astype(BF16), DN_LAST, preferred_element_type=F32)
        tt = jnp.where(causal, tt, 0.0) + jnp.where(diag, d_ref[g], 0.0)
        w1 = cmul(bb2t, bbst, *table((q - 1) - ti))
        vcs.append((cmul(c2t, cst, *table(ti + 1.0)) * (-sgn)).astype(BF16))
        ar, ai = _cpow(lr_dt, li_dt, float(q))
        a_terms.append((ar, ai * sgn))
        x = x_ref[g]
        yl_sc[g] = jnp.dot(x, tt.astype(BF16), preferred_element_type=F32)
        s = jnp.dot(x, w1.astype(BF16), preferred_element_type=F32)
        s_sc[g] = s
        ss_sc[g] = pltpu.roll(s, n, axis=1)

    def carry(ch, hs):
        new = []
        for g in range(gb):
            h, hsw = hs[2 * g], hs[2 * g + 1]
            hp_sc[g, pl.ds(ch, nb, stride=nchunk), :] = h
            a1, a2 = a_terms[g]
            new.append(a1 * h + a2 * hsw + s_sc[g, pl.ds(ch, nb, stride=nchunk), :])
            new.append(a1 * hsw - a2 * h + ss_sc[g, pl.ds(ch, nb, stride=nchunk), :])
        return tuple(new)

    zero = jnp.zeros((nb, 2 * n), F32)
    hs = lax.fori_loop(0, nchunk, carry, (zero,) * (2 * gb), unroll=8)
    for g in range(gb):
        st_ref[g] = hs[2 * g]
        y = yl_sc[g] + lax.dot_general(hp_sc[g].astype(BF16), vcs[g], DN_LAST, preferred_element_type=F32)
        z_ref[g] = _gelu_tanh(y).astype(BF16)


def _s5_prompt(hx, lam_re2, lam_im2, log_dt2, bt2, bts, c2, cs, d_rep, nb, gb=2):
    g, r, w = hx.shape
    n2 = 2 * SSM_STATE
    nchunk = r // nb
    vec = lambda: pl.BlockSpec((gb, 1, n2), lambda i: (i, 0, 0))
    mat = lambda: pl.BlockSpec((gb, SSM_GROUP, n2), lambda i: (i, 0, 0))
    return pl.pallas_call(
        functools.partial(_s5_prompt_kernel, gb=gb, nb=nb, nchunk=nchunk),
        grid=(g // gb,),
        in_specs=[pl.BlockSpec((gb, r, w), lambda i: (i, 0, 0)), vec(), vec(), vec(),
                  mat(), mat(), mat(), mat(), pl.BlockSpec((gb, 1, w), lambda i: (i, 0, 0))],
        out_specs=[pl.BlockSpec((gb, r, w), lambda i: (i, 0, 0)),
                   pl.BlockSpec((gb, nb, n2), lambda i: (i, 0, 0))],
        out_shape=[jax.ShapeDtypeStruct((g, r, w), BF16),
                   jax.ShapeDtypeStruct((g, nb, n2), F32)],
        scratch_shapes=[pltpu.VMEM((gb, r, n2), F32), pltpu.VMEM((gb, r, n2), F32),
                        pltpu.VMEM((gb, r, n2), F32), pltpu.VMEM((gb, r, w), F32)],
        compiler_params=_params("arbitrary"),
        name="s5_prompt",
    )(hx, lam_re2, lam_im2, log_dt2, bt2, bts, c2, cs, d_rep)


def _s5_sample_kernel(u_ref, h0r_ref, h0i_ref, lr_ref, li_ref, ldt_ref, btr_ref, bti_ref,
                      cr_ref, ci_ref, d_ref, z_ref, nr_ref, ni_ref, *, gb):
    hp = lax.Precision.HIGHEST
    for g in range(gb):
        _, _, ab_re, ab_im, f_re, f_im = _zoh(lr_ref[g], li_ref[g], ldt_ref[g])
        bb_re = f_re * btr_ref[g] - f_im * bti_ref[g]
        bb_im = f_re * bti_ref[g] + f_im * btr_ref[g]
        u = u_ref[g]
        bu_re = jnp.dot(u, bb_re, preferred_element_type=F32, precision=hp)
        bu_im = jnp.dot(u, bb_im, preferred_element_type=F32, precision=hp)
        h0r = h0r_ref[g]
        h0i = h0i_ref[g]
        hr = bu_re + ab_re * h0r - ab_im * h0i
        hi = bu_im + ab_re * h0i + ab_im * h0r
        nr_ref[g] = hr
        ni_ref[g] = hi
        y = (lax.dot_general(hr, cr_ref[g], DN_LAST, preferred_element_type=F32, precision=hp)
             - lax.dot_general(hi, ci_ref[g], DN_LAST, preferred_element_type=F32, precision=hp))
        y = y + d_ref[g] * u
        z_ref[g] = _gelu_tanh(y).astype(BF16)


def _s5_sample(u, h0_re, h0_im, lam_re, lam_im, log_dt, bt_re, bt_im, c_re, c_im, d_g, gb=8):
    g, b, cg = u.shape
    n = SSM_STATE
    vec = lambda: pl.BlockSpec((gb, 1, n), lambda i: (i, 0, 0))
    mat = lambda: pl.BlockSpec((gb, cg, n), lambda i: (i, 0, 0))
    st = lambda: pl.BlockSpec((gb, b, n), lambda i: (i, 0, 0))
    return pl.pallas_call(
        functools.partial(_s5_sample_kernel, gb=gb),
        grid=(g // gb,),
        in_specs=[pl.BlockSpec((gb, b, cg), lambda i: (i, 0, 0)), st(), st(), vec(), vec(), vec(),
                  mat(), mat(), mat(), mat(), pl.BlockSpec((gb, 1, cg), lambda i: (i, 0, 0))],
        out_specs=[pl.BlockSpec((gb, b, cg), lambda i: (i, 0, 0)), st(), st()],
        out_shape=[jax.ShapeDtypeStruct((g, b, cg), BF16),
                   jax.ShapeDtypeStruct((g, b, n), F32),
                   jax.ShapeDtypeStruct((g, b, n), F32)],
        compiler_params=_params("arbitrary"),
        name="s5_sample",
    )(u, h0_re, h0_im, lam_re, lam_im, log_dt, bt_re, bt_im, c_re, c_im, d_g)


def _route_kernel(x_ref, g_ref, sc_ref, sh_ref, wr_ref, br_ref, h_ref, gates_ref, e_ref, w_ref, *, split):
    h = _normmod(x_ref[...], g_ref[...], sc_ref[...], sh_ref[...])
    hb = h.astype(BF16)
    h_ref[...] = hb.reshape(h_ref.shape) if split else hb
    logits = lax.dot_general(wr_ref[...], h, DN_LAST, preferred_element_type=F32,
                             precision=lax.Precision.HIGHEST)
    s = jax.nn.sigmoid(logits)
    sel = s + br_ref[...]
    epg = EXPERTS_PER_GROUP
    srow = [s[e:e + 1, :] for e in range(N_EXPERTS)]
    row = [sel[e:e + 1, :] for e in range(N_EXPERTS)]

    def top2_sum(a, b, c, d):
        hi1, lo1 = jnp.maximum(a, b), jnp.minimum(a, b)
        hi2, lo2 = jnp.maximum(c, d), jnp.minimum(c, d)
        return jnp.maximum(hi1, hi2) + jnp.maximum(jnp.minimum(hi1, hi2), jnp.maximum(lo1, lo2))

    gs = [top2_sum(*row[epg * g:epg * (g + 1)]) for g in range(N_EXPERT_GROUPS)]
    best = gs[0]
    gidx = jnp.zeros_like(best, dtype=I32)
    for g in range(1, N_EXPERT_GROUPS):
        better = gs[g] > best
        gidx = jnp.where(better, g, gidx)
        best = jnp.where(better, gs[g], best)

    def pick(rows_, j):
        v = rows_[j]
        for g in range(1, N_EXPERT_GROUPS):
            v = jnp.where(gidx == g, rows_[epg * g + j], v)
        return v

    v = [pick(row, j) for j in range(epg)]
    u = [pick(srow, j) for j in range(epg)]

    def argmax_first(vals):
        bv, bi = vals[0], jnp.zeros_like(gidx)
        for j in range(1, epg):
            better = vals[j] > bv
            bi = jnp.where(better, j, bi)
            bv = jnp.where(better, vals[j], bv)
        return bi

    i1 = argmax_first(v)
    i2 = argmax_first([jnp.where(i1 == j, -jnp.inf, v[j]) for j in range(epg)])

    def take(vals, idx):
        out = vals[0]
        for j in range(1, epg):
            out = jnp.where(idx == j, vals[j], out)
        return out

    w1, w2 = take(u, i1), take(u, i2)
    tot = w1 + w2
    w1, w2 = w1 / tot, w2 / tot
    e1 = gidx * epg + i1
    e2 = gidx * epg + i2
    gates_ref[...] = jnp.concatenate(
        [jnp.where(e1 == e, w1, 0.0) + jnp.where(e2 == e, w2, 0.0) for e in range(N_EXPERTS)], axis=0)
    e_ref[...] = jnp.concatenate([e1, e2], axis=0)
    w_ref[...] = jnp.concatenate([w1, w2], axis=0)


def _route(x, g, mods, sc_chunk, sh_chunk, rows_per_b, tm, w_router_t, b_router, split):
    t, d = x.shape
    e = N_EXPERTS
    if split:
        h_spec = pl.BlockSpec((tm, d // LANES, LANES), lambda i, j: (i, 0, 0))
        h_shape = jax.ShapeDtypeStruct((t, d // LANES, LANES), BF16)
    else:
        h_spec = pl.BlockSpec((tm, d), lambda i, j: (i, 0))
        h_shape = jax.ShapeDtypeStruct((t, d), BF16)
    return pl.pallas_call(
        functools.partial(_route_kernel, split=split),
        grid=(t // tm, 1),
        in_specs=[
            pl.BlockSpec((tm, d), lambda i, j: (i, 0)),
            pl.BlockSpec((1, d), lambda i, j: (0, 0)),
            _mod_spec(mods, sc_chunk, tm, rows_per_b, d),
            _mod_spec(mods, sh_chunk, tm, rows_per_b, d),
            pl.BlockSpec((e, d), lambda i, j: (0, 0)),
            pl.BlockSpec((e, 1), lambda i, j: (0, 0)),
        ],
        out_specs=[h_spec, pl.BlockSpec((e, tm), lambda i, j: (0, i)),
                   pl.BlockSpec((TOP_K, tm), lambda i, j: (0, i)), pl.BlockSpec((TOP_K, tm), lambda i, j: (0, i))],
        out_shape=[h_shape, jax.ShapeDtypeStruct((e, t), F32),
                   jax.ShapeDtypeStruct((TOP_K, t), I32), jax.ShapeDtypeStruct((TOP_K, t), F32)],
        compiler_params=_params("arbitrary", "arbitrary"),
        name="moe_route",
    )(x, g.reshape(1, d), mods, mods, w_router_t, b_router.reshape(e, 1))


def _moe_kernel(h_ref, gates_ref, wg_ref, wu_ref, wd_ref, res_ref, gt_ref, o_ref, acc_ref):
    e = pl.program_id(1)
    first = (e == 0) & (pl.program_id(2) == 0)
    last = (e == pl.num_programs(1) - 1) & (pl.program_id(2) == pl.num_programs(2) - 1)

    @pl.when(first)
    def _():
        acc_ref[...] = jnp.zeros_like(acc_ref)

    h = h_ref[...]
    a = jnp.dot(h, wg_ref[...].astype(BF16), preferred_element_type=F32)
    b = jnp.dot(h, wu_ref[...].astype(BF16), preferred_element_type=F32)
    gates = gates_ref[...]
    lane = lax.broadcasted_iota(I32, gates.shape, 1)
    gcol = jnp.sum(jnp.where(lane == e, gates, 0.0), axis=1, keepdims=True)
    hid = (a * jax.nn.sigmoid(a)) * b * gcol
    acc_ref[...] += jnp.dot(hid.astype(BF16), wd_ref[...].astype(BF16), preferred_element_type=F32)

    @pl.when(last)
    def _():
        o_ref[...] = res_ref[...] + gt_ref[...] * acc_ref[...]


def _moe(h, gates, w_gate, w_up, w_down, layer, res, mods, gt_chunk, rows_per_b, tm, fsplit=2):
    t, d = h.shape
    e, _, f = w_gate.shape[1:]
    tf = f // fsplit
    r = mods.shape[1]
    return pl.pallas_call(
        _moe_kernel,
        grid=(t // tm, e, fsplit),
        in_specs=[
            pl.BlockSpec((tm, d), lambda i, j, k: (i, 0)),
            pl.BlockSpec((tm, e), lambda i, j, k: (i, 0)),
            pl.BlockSpec((None, None, d, tf), lambda i, j, k: (layer, j, 0, k)),
            pl.BlockSpec((None, None, d, tf), lambda i, j, k: (layer, j, 0, k)),
            pl.BlockSpec((None, None, tf, d), lambda i, j, k: (layer, j, k, 0)),
            pl.BlockSpec((tm, d), lambda i, j, k: (i, 0)),
            pl.BlockSpec((None, r, d), lambda i, j, k: ((i * tm) // rows_per_b, 0, gt_chunk)),
        ],
        out_specs=pl.BlockSpec((tm, d), lambda i, j, k: (i, 0)),
        out_shape=jax.ShapeDtypeStruct((t, d), F32),
        scratch_shapes=[pltpu.VMEM((tm, d), F32)],
        compiler_params=_params("arbitrary", "arbitrary", "arbitrary"),
        name="moe_experts",
    )(h, gates, w_gate, w_up, w_down, res, mods)


def _experts_kernel(te_ref, nu_ref, idxc_ref, idxn_ref, h_ref, wg_ref, wu_ref, wd_ref, o_ref,
                    wgb, wub, wdb, xbuf, sem, *, tg):
    i = pl.program_id(0)
    nu = nu_ref[0]
    slot = i % 2
    per_blk = SMEM_I32_BLOCK // tg

    def start_gather(idx_ref, tile, dst_slot):
        off = (tile % per_blk) * tg

        def issue(r, carry):
            pltpu.make_async_copy(h_ref.at[pl.ds(idx_ref[off + r], 1)], xbuf.at[dst_slot, pl.ds(r, 1)],
                                  sem.at[dst_slot]).start()
            return carry

        lax.fori_loop(0, tg, issue, 0, unroll=8)

    @pl.when(i == 0)
    def _():
        start_gather(idxc_ref, i, slot)

    @pl.when(i + 1 < nu)
    def _():
        start_gather(idxn_ref, i + 1, 1 - slot)

    prev = te_ref[jnp.maximum(i - 1, 0)]

    @pl.when((i == 0) | (te_ref[i] != prev))
    def _():
        wgb[...] = wg_ref[...].astype(BF16)
        wub[...] = wu_ref[...].astype(BF16)
        wdb[...] = wd_ref[...].astype(BF16)

    @pl.when(i < nu)
    def _():
        def drain(r, carry):
            pltpu.make_async_copy(h_ref.at[pl.ds(0, 1)], xbuf.at[slot, pl.ds(r, 1)], sem.at[slot]).wait()
            return carry

        lax.fori_loop(0, tg, drain, 0, unroll=8)
        x = xbuf[slot].reshape(tg, wgb.shape[0])
        a = jnp.dot(x, wgb[...], preferred_element_type=F32)
        b = jnp.dot(x, wub[...], preferred_element_type=F32)
        hid = (a * jax.nn.sigmoid(a)) * b
        y = jnp.dot(hid.astype(BF16), wdb[...], preferred_element_type=F32)
        o_ref[...] = y.reshape(o_ref.shape)

    @pl.when(i >= nu)
    def _():
        o_ref[...] = jnp.zeros_like(o_ref)


def _experts(tile_expert, n_used, sorted_tok, h3, w_gate, w_up, w_down, layer, tg):
    rp = sorted_tok.shape[0]
    _, _, d, f = w_gate.shape
    blk = SMEM_I32_BLOCK
    per_blk = blk // tg
    last_blk = rp // blk - 1
    row_shape = h3.shape[1:]
    grid_spec = pltpu.PrefetchScalarGridSpec(
        num_scalar_prefetch=2,
        grid=(rp // tg,),
        in_specs=[
            pl.BlockSpec((blk,), lambda i, te, nu: (i // per_blk,), memory_space=pltpu.SMEM),
            pl.BlockSpec((blk,), lambda i, te, nu: (jnp.minimum((i + 1) // per_blk, last_blk),),
                         memory_space=pltpu.SMEM),
            pl.BlockSpec(memory_space=pl.ANY),
            pl.BlockSpec((None, None, d, f), lambda i, te, nu: (layer, te[i], 0, 0)),
            pl.BlockSpec((None, None, d, f), lambda i, te, nu: (layer, te[i], 0, 0)),
            pl.BlockSpec((None, None, f, d), lambda i, te, nu: (layer, te[i], 0, 0)),
        ],
        out_specs=pl.BlockSpec((tg,) + row_shape, lambda i, te, nu: (i, 0, 0)),
        scratch_shapes=[pltpu.VMEM((d, f), BF16), pltpu.VMEM((d, f), BF16), pltpu.VMEM((f, d), BF16),
                        pltpu.VMEM((2, tg) + row_shape, h3.dtype), pltpu.SemaphoreType.DMA((2,))],
    )
    return pl.pallas_call(
        functools.partial(_experts_kernel, tg=tg),
        grid_spec=grid_spec,
        out_shape=jax.ShapeDtypeStruct((rp,) + row_shape, F32),
        compiler_params=_params("arbitrary"),
        name="moe_routed_experts",
    )(tile_expert, n_used, sorted_tok, sorted_tok, h3, w_gate, w_up, w_down)


def _combine_kernel(pos_ref, ys_ref, w_ref, res_ref, gt_ref, *refs, tm, final):
    if final:
        gf_ref, o_ref, buf, sem = refs
    else:
        o_ref, buf, sem = refs

    def issue(r, carry):
        for k in range(TOP_K):
            pltpu.make_async_copy(ys_ref.at[pl.ds(pos_ref[k * tm + r], 1)], buf.at[k, pl.ds(r, 1)], sem).start()
        return carry

    lax.fori_loop(0, tm, issue, 0, unroll=8)

    def drain(r, carry):
        for k in range(TOP_K):
            pltpu.make_async_copy(ys_ref.at[pl.ds(0, 1)], buf.at[k, pl.ds(r, 1)], sem).wait()
        return carry

    lax.fori_loop(0, tm, drain, 0, unroll=8)
    d = o_ref.shape[1]
    w = w_ref[...]
    m = (buf[0].reshape(tm, d) * jnp.tile(w[:, :LANES], (1, d // LANES))
         + buf[1].reshape(tm, d) * jnp.tile(w[:, LANES:], (1, d // LANES)))
    x = res_ref[...] + gt_ref[...] * m
    if final:
        ms = jnp.mean(x * x, axis=-1, keepdims=True)
        x = (x * lax.rsqrt(ms + EPS)) * gf_ref[...]
    o_ref[...] = x


def _combine(pos, ys, wtok, res, mods, gt_chunk, rows_per_b, tm, g_final=None):
    t, d = res.shape
    assert TOP_K * tm == SMEM_I32_BLOCK
    r = mods.shape[1]
    final = g_final is not None
    in_specs = [pl.BlockSpec((TOP_K * tm,), lambda i: (i,), memory_space=pltpu.SMEM),
                pl.BlockSpec(memory_space=pl.ANY),
                pl.BlockSpec((tm, TOP_K * LANES), lambda i: (i, 0)),
                pl.BlockSpec((tm, d), lambda i: (i, 0)),
                pl.BlockSpec((None, r, d), lambda i: ((i * tm) // rows_per_b, 0, gt_chunk))]
    args = [pos, ys, wtok, res, mods]
    if final:
        in_specs.append(pl.BlockSpec((1, d), lambda i: (0, 0)))
        args.append(g_final.reshape(1, d))
    return pl.pallas_call(
        functools.partial(_combine_kernel, tm=tm, final=final),
        grid=(t // tm,),
        in_specs=in_specs,
        out_specs=pl.BlockSpec((tm, d), lambda i: (i, 0)),
        out_shape=jax.ShapeDtypeStruct((t, d), F32),
        scratch_shapes=[pltpu.VMEM((TOP_K, tm) + ys.shape[1:], F32), pltpu.SemaphoreType.DMA(())],
        compiler_params=_params("arbitrary"),
        name="moe_combine",
    )(*args)


def _moe_routed(h3, e_rows, w_rows, w_gate, w_up, w_down, layer, res, mods, gt_chunk, rows_per_b,
                g_final=None, tg=256, tmc=512):
    t = h3.shape[0]
    ne = N_EXPERTS
    a = TOP_K * t
    rp = a + ne * tg
    rp = -(-rp // SMEM_I32_BLOCK) * SMEM_I32_BLOCK
    n_tiles = rp // tg
    ea = e_rows.reshape(a)
    ta = jnp.tile(jnp.arange(t, dtype=I32), TOP_K)
    onehot = (ea[:, None] == jnp.arange(ne, dtype=I32)[None, :]).astype(I32)
    csum = jnp.cumsum(onehot, axis=0)
    rank = jnp.sum((csum - onehot) * onehot, axis=1)
    counts = csum[-1]
    ptiles = (counts + tg - 1) // tg
    tile_end = jnp.cumsum(ptiles)
    tile_start = tile_end - ptiles
    n_used = tile_end[-1:].astype(I32)
    dest = jnp.sum(onehot * tile_start[None, :], axis=1) * tg + rank
    sorted_tok = jnp.zeros((rp,), I32).at[dest].set(ta, unique_indices=True)
    tile_expert = jnp.sum(jnp.arange(n_tiles, dtype=I32)[:, None] >= tile_end[None, :], axis=1)
    tile_expert = jnp.minimum(tile_expert, ne - 1).astype(I32)
    ys = _experts(tile_expert, n_used, sorted_tok, h3, w_gate, w_up, w_down, layer, tg)
    pos = dest.reshape(TOP_K, t // tmc, tmc).transpose(1, 0, 2).reshape(a)
    wtok = jnp.concatenate([jnp.broadcast_to(w_rows[k][:, None], (t, LANES)) for k in range(TOP_K)], axis=1)
    return _combine(pos, ys, wtok, res, mods, gt_chunk, rows_per_b, tmc, g_final)


def _kv_kernel(x_ref, g_ref, sc_ref, sh_ref, w_ref, gc_ref, cos_ref, sin_ref, ckv_ref, kpe_ref, wb_ref):
    @pl.when(pl.program_id(0) == 0)
    def _():
        wb_ref[...] = w_ref[...].astype(BF16)

    h = _normmod(x_ref[...], g_ref[...], sc_ref[...], sh_ref[...]).astype(BF16)
    lat = lax.dot_general(h, wb_ref[...], DN_LAST, preferred_element_type=F32)
    c = lat[:, :KV_LORA]
    ms = jnp.mean(c * c, axis=-1, keepdims=True)
    ckv_ref[...] = (c * lax.rsqrt(ms + EPS)) * gc_ref[...]
    kpe_ref[...] = _rope(lat[:, KV_LORA:], cos_ref[...], sin_ref[...])


def _kv(x, g_kv, mods, rows_per_b, tm, w_dkv_t, g_ckv, cos2, sin2):
    t, d = x.shape
    n = w_dkv_t.shape[0]
    nt = cos2.shape[0] // tm
    return pl.pallas_call(
        _kv_kernel,
        grid=(t // tm, 1),
        in_specs=[
            pl.BlockSpec((tm, d), lambda i, j: (i, 0)),
            pl.BlockSpec((1, d), lambda i, j: (0, 0)),
            _mod_spec(mods, 1, tm, rows_per_b, d),
            _mod_spec(mods, 0, tm, rows_per_b, d),
            pl.BlockSpec((n, d), lambda i, j: (0, 0)),
            pl.BlockSpec((1, KV_LORA), lambda i, j: (0, 0)),
            pl.BlockSpec((tm, QK_ROPE), lambda i, j: (i % nt, 0)),
            pl.BlockSpec((tm, QK_ROPE), lambda i, j: (i % nt, 0)),
        ],
        out_specs=[pl.BlockSpec((tm, KV_LORA), lambda i, j: (i, 0)),
                   pl.BlockSpec((tm, QK_ROPE), lambda i, j: (i, 0))],
        out_shape=[jax.ShapeDtypeStruct((t, KV_LORA), F32), jax.ShapeDtypeStruct((t, QK_ROPE), F32)],
        scratch_shapes=[pltpu.VMEM((n, d), BF16)],
        compiler_params=_params("arbitrary", "arbitrary"),
        name="latent_kv",
    )(x, g_kv.reshape(1, d), mods, mods, w_dkv_t, g_ckv.reshape(1, KV_LORA), cos2, sin2)


def _qproj_kernel(x_ref, g_ref, sc_ref, sh_ref, w_ref, cos_ref, sin_ref, q_ref, hb_ref, *, hb):
    @pl.when(pl.program_id(1) == 0)
    def _():
        hb_ref[...] = _normmod(x_ref[...], g_ref[...], sc_ref[...], sh_ref[...]).astype(BF16)

    acc = jnp.dot(hb_ref[...], w_ref[...].astype(BF16), preferred_element_type=F32)
    cos2, sin2 = cos_ref[...], sin_ref[...]
    for h in range(hb):
        c0 = h * QK_HEAD
        q_ref[h, :, :QK_NOPE] = acc[:, c0:c0 + QK_NOPE].astype(BF16)
        q_ref[h, :, QK_NOPE:] = _rope(acc[:, c0 + QK_NOPE:c0 + QK_HEAD], cos2, sin2).astype(BF16)


def _qproj(x, g, mods, sc_chunk, sh_chunk, rows_per_b, tm, w_q2, cos2, sin2, hb=4):
    t, d = x.shape
    nh = w_q2.shape[1] // QK_HEAD
    nt = rows_per_b // tm
    return pl.pallas_call(
        functools.partial(_qproj_kernel, hb=hb),
        grid=(t // tm, nh // hb),
        in_specs=[
            pl.BlockSpec((tm, d), lambda i, j: (i, 0)),
            pl.BlockSpec((1, d), lambda i, j: (0, 0)),
            _mod_spec(mods, sc_chunk, tm, rows_per_b, d),
            _mod_spec(mods, sh_chunk, tm, rows_per_b, d),
            pl.BlockSpec((d, hb * QK_HEAD), lambda i, j: (0, j)),
            pl.BlockSpec((tm, QK_ROPE), lambda i, j: (i % nt, 0)),
            pl.BlockSpec((tm, QK_ROPE), lambda i, j: (i % nt, 0)),
        ],
        out_specs=pl.BlockSpec((None, hb, tm, QK_HEAD), lambda i, j: (i // nt, j, i % nt, 0)),
        out_shape=jax.ShapeDtypeStruct((t // rows_per_b, nh, rows_per_b, QK_HEAD), BF16),
        scratch_shapes=[pltpu.VMEM((tm, d), BF16)],
        compiler_params=_params("arbitrary", "arbitrary"),
        name="q_proj",
    )(x, g.reshape(1, d), mods, mods, w_q2, cos2, sin2)


def _kvup_kernel(ckv_ref, kpe_ref, wk_ref, wv_ref, k_ref, v_ref, *, hb):
    c = ckv_ref[...].astype(BF16)
    kn = jnp.dot(c, wk_ref[...].astype(BF16), preferred_element_type=F32)
    vv = jnp.dot(c, wv_ref[...].astype(BF16), preferred_element_type=F32)
    kpe = kpe_ref[...].astype(BF16)
    for h in range(hb):
        k_ref[h, :, :QK_NOPE] = kn[:, h * QK_NOPE:(h + 1) * QK_NOPE].astype(BF16)
        k_ref[h, :, QK_NOPE:] = kpe
        v_ref[h] = vv[:, h * V_HEAD:(h + 1) * V_HEAD].astype(BF16)


def _kvup(ckv, kpe, w_uk2, w_uv2, bsz, seq, tm, hb=4):
    t, r = ckv.shape
    nt = seq // tm
    return pl.pallas_call(
        functools.partial(_kvup_kernel, hb=hb),
        grid=(t // tm, N_HEADS // hb),
        in_specs=[
            pl.BlockSpec((tm, r), lambda i, j: (i, 0)),
            pl.BlockSpec((tm, QK_ROPE), lambda i, j: (i, 0)),
            pl.BlockSpec((r, hb * QK_NOPE), lambda i, j: (0, j)),
            pl.BlockSpec((r, hb * V_HEAD), lambda i, j: (0, j)),
        ],
        out_specs=[pl.BlockSpec((None, hb, tm, QK_HEAD), lambda i, j: (i // nt, j, i % nt, 0)),
                   pl.BlockSpec((None, hb, tm, V_HEAD), lambda i, j: (i // nt, j, i % nt, 0))],
        out_shape=[jax.ShapeDtypeStruct((bsz, N_HEADS, seq, QK_HEAD), BF16),
                   jax.ShapeDtypeStruct((bsz, N_HEADS, seq, V_HEAD), BF16)],
        compiler_params=_params("arbitrary", "arbitrary"),
        name="kv_up",
    )(ckv, kpe, w_uk2, w_uv2)


def _flash_kernel(qt_ref, kt_ref, q_ref, k_ref, v_ref, o_ref, m_ref, l_ref, acc_ref, *, tq, tk, hb):
    p_idx = pl.program_id(2)
    qi = qt_ref[p_idx]
    kj = kt_ref[p_idx]

    @pl.when(kj == 0)
    def _():
        m_ref[...] = jnp.full_like(m_ref, -jnp.inf)
        l_ref[...] = jnp.zeros_like(l_ref)
        acc_ref[...] = jnp.zeros_like(acc_ref)

    def tile(h, masked):
        s = lax.dot_general(q_ref[h], k_ref[h], DN_LAST, preferred_element_type=F32) * ATTN_SCALE
        if masked:
            qpos = lax.broadcasted_iota(I32, (tq, tk), 0)
            kpos = lax.broadcasted_iota(I32, (tq, tk), 1)
            s = jnp.where(kpos <= qpos, s, -jnp.inf)
        m_prev = m_ref[h]
        m_next = jnp.maximum(m_prev, jnp.max(s, axis=1, keepdims=True))
        alpha = jnp.exp(m_prev - m_next)
        p = jnp.exp(s - jnp.tile(m_next, (1, tk // LANES)))
        l_ref[h] = alpha * l_ref[h] + jnp.sum(p, axis=1, keepdims=True)
        acc_ref[h] = alpha * acc_ref[h] + jnp.dot(p.astype(BF16), v_ref[h], preferred_element_type=F32)
        m_ref[h] = m_next

    @pl.when(kj < qi)
    def _():
        for h in range(hb):
            tile(h, False)

    @pl.when(kj == qi)
    def _():
        for h in range(hb):
            tile(h, True)
            o_ref[:, h * V_HEAD:(h + 1) * V_HEAD] = (acc_ref[h] / l_ref[h]).astype(BF16)


def _flash(q, k, v, tile_len=512, hb=2):
    bsz, nh, seq, _ = q.shape
    tq = tk = tile_len
    nq = seq // tq
    pairs = [(i, j) for i in range(nq) for j in range(i + 1)]
    qt = jnp.asarray([p[0] for p in pairs], I32)
    kt = jnp.asarray([p[1] for p in pairs], I32)
    grid_spec = pltpu.PrefetchScalarGridSpec(
        num_scalar_prefetch=2,
        grid=(bsz, nh // hb, len(pairs)),
        in_specs=[
            pl.BlockSpec((None, hb, tq, QK_HEAD), lambda b, h, p, qt, kt: (b, h, qt[p], 0)),
            pl.BlockSpec((None, hb, tk, QK_HEAD), lambda b, h, p, qt, kt: (b, h, kt[p], 0)),
            pl.BlockSpec((None, hb, tk, V_HEAD), lambda b, h, p, qt, kt: (b, h, kt[p], 0)),
        ],
        out_specs=pl.BlockSpec((tq, hb * V_HEAD), lambda b, h, p, qt, kt: (b * nq + qt[p], h)),
        scratch_shapes=[pltpu.VMEM((hb, tq, LANES), F32), pltpu.VMEM((hb, tq, LANES), F32),
                        pltpu.VMEM((hb, tq, V_HEAD), F32)],
    )
    return pl.pallas_call(
        functools.partial(_flash_kernel, tq=tq, tk=tk, hb=hb),
        grid_spec=grid_spec,
        out_shape=jax.ShapeDtypeStruct((bsz * seq, nh * V_HEAD), BF16),
        compiler_params=_params("arbitrary", "arbitrary", "arbitrary"),
        name="flash_attn",
    )(qt, kt, q, k, v)


def _qlat_kernel(q_ref, wk_ref, o_ref):
    q = q_ref[...]
    qlat = lax.dot_general(q[:, :QK_NOPE], wk_ref[...].astype(BF16), DN_LAST, preferred_element_type=F32)
    o_ref[:, :KV_LORA] = qlat.astype(BF16)
    o_ref[:, KV_LORA:] = q[:, QK_NOPE:]


def _qlat(q, w_uk2):
    nh, b, _ = q.shape
    r = w_uk2.shape[0]
    return pl.pallas_call(
        _qlat_kernel,
        grid=(nh,),
        in_specs=[pl.BlockSpec((None, b, QK_HEAD), lambda h: (h, 0, 0)),
                  pl.BlockSpec((r, QK_NOPE), lambda h: (0, h))],
        out_specs=pl.BlockSpec((None, b, r + QK_ROPE), lambda h: (h, 0, 0)),
        out_shape=jax.ShapeDtypeStruct((nh, b, r + QK_ROPE), BF16),
        compiler_params=_params("arbitrary"),
        name="q_latent",
    )(q, w_uk2)


def _paged_kernel(pt_ref, q_ref, *refs, npg, nsteps, rb):
    n_in = rb * npg
    ckv_refs = refs[:n_in]
    kpe_refs = refs[n_in:2 * n_in]
    cnew_ref, pnew_ref, o_ref, qpad_ref, kcb_ref, pp_ref, m_ref, l_ref, acc_ref = refs[2 * n_in:]
    bi = pl.program_id(0)
    j = pl.program_id(1)
    nh = N_HEADS
    wide = KV_LORA // LANES
    first = j == 0

    @pl.when(first & (bi == 0))
    def _():
        kcb_ref[...] = jnp.zeros_like(kcb_ref)
        pp_ref[...] = jnp.zeros_like(pp_ref)
        m_ref[...] = jnp.zeros_like(m_ref)
        l_ref[...] = jnp.zeros_like(l_ref)
        acc_ref[...] = jnp.zeros_like(acc_ref)

    @pl.when(first)
    def _():
        qpad_ref[...] = jnp.zeros_like(qpad_ref)
        for r in range(rb):
            qpad_ref[r, :nh, :] = q_ref[r]

    def step(cur, prv):
        for r in range(rb):
            pv_prev = jnp.dot(pp_ref[r], kcb_ref[r, prv], preferred_element_type=F32)
            acc_in = jnp.where(first, 0.0, acc_ref[r] + pv_prev)
            m_prev = jnp.where(first, -jnp.inf, m_ref[r])
            l_prev = jnp.where(first, 0.0, l_ref[r])
            for i in range(npg):
                kcb_ref[r, cur, i * PAGE_SIZE:(i + 1) * PAGE_SIZE, :] = ckv_refs[r * npg + i][...].astype(BF16)
            s_c = lax.dot_general(kcb_ref[r, cur], qpad_ref[r, :, :KV_LORA], DN_LAST,
                                  preferred_element_type=F32)
            kp = jnp.concatenate([kpe_refs[r * npg + i][...].astype(BF16) for i in range(npg)],
                                 axis=1)
            s_p = jnp.dot(qpad_ref[r, :, KV_LORA:], kp, preferred_element_type=F32)
            s = (s_c.T[:nh, :] + s_p[:nh, :]) * ATTN_SCALE
            m_next = jnp.maximum(m_prev, jnp.max(s, axis=1, keepdims=True))
            alpha = jnp.exp(m_prev - m_next)
            p = jnp.exp(s - jnp.tile(m_next, (1, s.shape[1] // LANES)))
            l_ref[r] = alpha * l_prev + jnp.sum(p, axis=1, keepdims=True)
            acc_ref[r] = jnp.tile(alpha, (1, wide)) * acc_in
            m_ref[r] = m_next
            pp_ref[r] = p.astype(BF16)

    @pl.when(j % 2 == 0)
    def _():
        step(0, 1)

    @pl.when(j % 2 == 1)
    def _():
        step(1, 0)

    @pl.when(j == nsteps - 1)
    def _():
        last_slot = (nsteps - 1) % 2
        for r in range(rb):
            acc_past = acc_ref[r] + jnp.dot(pp_ref[r], kcb_ref[r, last_slot], preferred_element_type=F32)
            q = q_ref[r].astype(F32)
            cnew = cnew_ref[r]
            s_new = (jnp.sum(q[:, :KV_LORA] * cnew, axis=1, keepdims=True)
                     + jnp.sum(q[:, KV_LORA:] * pnew_ref[r], axis=1, keepdims=True)) * ATTN_SCALE
            m_p = m_ref[r]
            m_n = jnp.maximum(m_p, s_new)
            a = jnp.exp(m_p - m_n)
            p_new = jnp.exp(s_new - m_n)
            l_fin = a * l_ref[r] + p_new
            acc = jnp.tile(a, (1, wide)) * acc_past + jnp.tile(p_new, (1, wide)) * cnew
            o_ref[r] = (acc / jnp.tile(l_fin, (1, wide))).astype(BF16)


def _paged_attn(qcat, cache_ckv, cache_kpe_t, page_table, ckv_new, kpe_new, npg=8, rb=2):
    b, nh, w = qcat.shape
    r = KV_LORA
    n_pages = page_table.shape[1]
    keys = npg * PAGE_SIZE

    def page_map(bi, j, pt, row, i):
        return (pt[bi * rb + row, j * npg + i], 0, 0)

    ckv_specs = [pl.BlockSpec((None, PAGE_SIZE, r), functools.partial(page_map, row=row, i=i))
                 for row in range(rb) for i in range(npg)]
    kpe_specs = [pl.BlockSpec((None, QK_ROPE, PAGE_SIZE), functools.partial(page_map, row=row, i=i))
                 for row in range(rb) for i in range(npg)]
    grid_spec = pltpu.PrefetchScalarGridSpec(
        num_scalar_prefetch=1,
        grid=(b // rb, n_pages // npg),
        in_specs=[pl.BlockSpec((rb, nh, w), lambda bi, j, pt: (bi, 0, 0))] + ckv_specs + kpe_specs + [
            pl.BlockSpec((rb, 1, r), lambda bi, j, pt: (bi, 0, 0)),
            pl.BlockSpec((rb, 1, QK_ROPE), lambda bi, j, pt: (bi, 0, 0)),
        ],
        out_specs=pl.BlockSpec((rb, nh, r), lambda bi, j, pt: (bi, 0, 0)),
        scratch_shapes=[pltpu.VMEM((rb, LANES, w), BF16), pltpu.VMEM((rb, 2, keys, r), BF16),
                        pltpu.VMEM((rb, nh, keys), BF16), pltpu.VMEM((rb, nh, LANES), F32),
                        pltpu.VMEM((rb, nh, LANES), F32), pltpu.VMEM((rb, nh, r), F32)],
    )
    return pl.pallas_call(
        functools.partial(_paged_kernel, npg=npg, nsteps=n_pages // npg, rb=rb),
        grid_spec=grid_spec,
        out_shape=jax.ShapeDtypeStruct((b, nh, r), BF16),
        compiler_params=_params("arbitrary", "arbitrary"),
        name="paged_attn",
    )(page_table, qcat, *([cache_ckv] * (rb * npg)), *([cache_kpe_t] * (rb * npg)),
      ckv_new.reshape(b, 1, r), kpe_new.reshape(b, 1, QK_ROPE))


def _ovup_kernel(o_ref, wv_ref, a_ref):
    a_ref[...] = jnp.dot(o_ref[...], wv_ref[...].astype(BF16), preferred_element_type=F32).astype(BF16)


def _ovup(olat, w_uv2):
    nh, b, r = olat.shape
    return pl.pallas_call(
        _ovup_kernel,
        grid=(nh,),
        in_specs=[pl.BlockSpec((None, b, r), lambda h: (h, 0, 0)), pl.BlockSpec((r, V_HEAD), lambda h: (0, h))],
        out_specs=pl.BlockSpec((b, V_HEAD), lambda h: (0, h)),
        out_shape=jax.ShapeDtypeStruct((b, nh * V_HEAD), BF16),
        compiler_params=_params("arbitrary"),
        name="o_latent_up",
    )(olat, w_uv2)


def _rope_tables(pos):
    inv = ROPE_THETA ** (-jnp.arange(0, QK_ROPE, 2, dtype=F32) / QK_ROPE)
    ang = pos.astype(F32)[:, None] * inv[None, :]
    cos, sin = jnp.cos(ang), jnp.sin(ang)
    return jnp.concatenate([cos, cos], axis=1), jnp.concatenate([-sin, sin], axis=1)


def _trunk(x, mods, mods_kv, rows_per_b, tm, cos2, sin2, P, s5_glu_fn, attn_fn, routed):
    tms = min(tm, 512)

    def moe_block(x, layer, g_final=None):
        h2, gates_t, e_rows, w_rows = _route(x, P["g_norm"][layer, 1], mods[layer], 4, 3, rows_per_b, tms,
                                             P["w_router_t"], P["b_router"], split=routed)
        if routed:
            return _moe_routed(h2, e_rows, w_rows, P["w_gate"], P["w_up"], P["w_down"], layer, x,
                               mods[layer], 5, rows_per_b, g_final)
        x = _moe(h2, gates_t.T, P["w_gate"], P["w_up"], P["w_down"], layer, x, mods[layer], 5,
                 rows_per_b, tms)
        return x if g_final is None else _final_norm(x, g_final, tms)

    x, st_re, st_im = s5_glu_fn(x)
    x = moe_block(x, 0)
    ckv, kpe = _kv(x, P["g_kv"], mods_kv, rows_per_b, tms, P["w_dkv_t"], P["g_ckv"], cos2, sin2)
    q = _qproj(x, P["g_norm"][1, 0], mods[1], 1, 0, rows_per_b, tm, P["w_q2"], cos2, sin2)
    attn = attn_fn(q, ckv, kpe)
    x = _proj_res(attn, P["w_o"][0], x, mods[1], 2, rows_per_b, tm, 512, glu=False)
    y = moe_block(x, 1, P["g_final"])
    return y, st_re, st_im, ckv, kpe


def kernel(x_prompt, x_sample, c_prompt, c_sample, state_ssm_re, state_ssm_im, cache_ckv, cache_kpe, page_table, g_norm, w_ada, b_ada, ssm_lam_re, ssm_lam_im, ssm_log_dt, ssm_b_re, ssm_b_im, ssm_c_re, ssm_c_im, ssm_d, w_glu, g_kv, w_ada_kv, b_ada_kv, w_dkv, g_ckv, w_uk, w_uv, w_q, w_o, w_router, b_router, w_gate, w_up, w_down, g_final):
    bsz, seq, d = x_prompt.shape
    db = x_sample.shape[0]
    g = d // SSM_GROUP
    n = SSM_STATE
    assert g == LANES, "the group-major relayout puts one S5 group per lane"
    past = page_table.shape[1] * PAGE_SIZE
    P = dict(g_norm=g_norm, g_kv=g_kv, g_ckv=g_ckv, w_o=w_o, b_router=b_router,
             w_gate=w_gate, w_up=w_up, w_down=w_down, g_final=g_final)
    P["w_q2"] = w_q[0]
    P["w_dkv_t"] = w_dkv.T
    P["w_router_t"] = w_router.T
    w_uk2 = w_uk.reshape(KV_LORA, N_HEADS * QK_NOPE)
    w_uv2 = w_uv.reshape(KV_LORA, N_HEADS * V_HEAD)

    m_rows = db + bsz
    m_pad = -(-m_rows // 16) * 16
    c_all = jnp.concatenate([c_sample, c_prompt, jnp.zeros((m_pad - m_rows, d), F32)], axis=0)
    mod = _ada(c_all, w_ada, b_ada)
    mod_kv = _ada(c_all, w_ada_kv[None], b_ada_kv[None])
    mods_s = [mod[l, :db][None] for l in range(mod.shape[0])]
    mods_p = [mod[l, db:m_rows][:, None, :] for l in range(mod.shape[0])]
    modkv_s = mod_kv[0, :db][None]
    modkv_p = mod_kv[0, db:m_rows][:, None, :]

    lam_re = ssm_lam_re[0].reshape(g, 1, n)
    lam_im = ssm_lam_im[0].reshape(g, 1, n)
    log_dt = jnp.broadcast_to(ssm_log_dt[0].reshape(g, 1, 1), (g, 1, n))
    bt_re = ssm_b_re[0].transpose(0, 2, 1)
    bt_im = ssm_b_im[0].transpose(0, 2, 1)
    c_re, c_im = ssm_c_re[0], ssm_c_im[0]
    d_g = ssm_d[0].reshape(g, 1, SSM_GROUP)
    d_rep = jnp.tile(d_g, (1, 1, S5_CHUNK))
    dup = lambda a: jnp.concatenate([a, a], axis=-1)
    cat = lambda a, b: jnp.concatenate([a, b], axis=-1)
    nchunk = seq // S5_CHUNK
    tm_p = 1024

    src = jnp.arange(d, dtype=I32)
    dst = (src % SSM_GROUP) * g + src // SSM_GROUP
    perm = (dst[:, None] == jnp.arange(d, dtype=I32)[None, :]).astype(BF16)
    w_glu_p = w_glu[0].reshape(g, SSM_GROUP, -1).transpose(1, 0, 2).reshape(d, -1)

    def s5_glu_prompt(x):
        hx = _s5_in(x, g_norm[0, 0], mods_p[0], 1, 0, seq, 256, perm)
        hx = hx.reshape(g, bsz * nchunk, S5_CHUNK * SSM_GROUP)
        zg, st = _s5_prompt(hx, dup(lam_re), dup(lam_im), dup(log_dt), cat(bt_re, bt_im), cat(bt_im, bt_re),
                            cat(c_re, c_im), cat(c_im, c_re), d_rep, bsz)
        zg = zg.reshape(g, bsz * seq * SSM_GROUP)
        x = _proj_res(zg, w_glu_p, x, mods_p[0], 2, seq, tm_p, 256, glu=True, grouped=True)
        st = st.transpose(1, 0, 2)[None]
        return x, st[..., :n], st[..., n:]

    def s5_glu_sample(x):
        h = _normmod_call(x, g_norm[0, 0], mods_s[0], 1, 0, db, db, F32)
        u = h.reshape(db, g, SSM_GROUP).transpose(1, 0, 2)
        h0_re = state_ssm_re[0].transpose(1, 0, 2)
        h0_im = state_ssm_im[0].transpose(1, 0, 2)
        zg, nr, ni = _s5_sample(u, h0_re, h0_im, lam_re, lam_im, log_dt, bt_re, bt_im, c_re, c_im, d_g)
        z = zg.transpose(1, 0, 2).reshape(db, d)
        x = _proj_res(z, w_glu[0], x, mods_s[0], 2, db, db, 256, glu=True)
        return x, nr.transpose(1, 0, 2)[None], ni.transpose(1, 0, 2)[None]

    def attn_prompt(q, ckv, kpe):
        k, v = _kvup(ckv, kpe, w_uk2, w_uv2, bsz, seq, tm_p)
        return _flash(q, k, v)

    def attn_sample(q, ckv, kpe):
        qcat = _qlat(q[0], w_uk2).transpose(1, 0, 2)
        olat = _paged_attn(qcat, cache_ckv, cache_kpe.transpose(0, 2, 1), page_table, ckv, kpe)
        return _ovup(olat.transpose(1, 0, 2), w_uv2)

    cos_p, sin_p = _rope_tables(jnp.arange(seq))
    cos_s, sin_s = _rope_tables(jnp.full((db,), past))

    y_p, sre_p, sim_p, ckv_p, kpe_p = _trunk(
        x_prompt.reshape(bsz * seq, d), mods_p, modkv_p, seq, tm_p, cos_p, sin_p, P,
        s5_glu_prompt, attn_prompt, routed=True)
    y_s, sre_s, sim_s, ckv_s, kpe_s = _trunk(
        x_sample.reshape(db, d), mods_s, modkv_s, db, db, cos_s, sin_s, P,
        s5_glu_sample, attn_sample, routed=False)
    return (y_p.reshape(bsz, seq, d), y_s.reshape(db, 1, d), sre_p, sim_p,
            ckv_p.reshape(bsz, seq, KV_LORA), kpe_p.reshape(bsz, seq, QK_ROPE),
            sre_s, sim_s, ckv_s.reshape(db, 1, KV_LORA), kpe_s.reshape(db, 1, QK_ROPE))
```

```python
import functools
import math

import jax
import jax.numpy as jnp
from jax import lax
from jax.experimental import pallas as pl
from jax.experimental.pallas import tpu as pltpu

F32 = jnp.float32
BF16 = jnp.bfloat16
I32 = jnp.int32

SSM_GROUP = 16
SSM_STATE = 64
N_HEADS = 16
QK_NOPE = 128
QK_ROPE = 64
V_HEAD = 128
KV_LORA = 512
ROPE_THETA = 10000.0
PAGE_SIZE = 128
N_EXPERTS = 16
N_EXPERT_GROUPS = 4
EXPERTS_PER_GROUP = N_EXPERTS // N_EXPERT_GROUPS
TOP_K = 2
EPS = 1e-6
ATTN_SCALE = (QK_NOPE + QK_ROPE) ** -0.5
QK_HEAD = QK_NOPE + QK_ROPE
LOG2_E = math.log2(math.e)

V7X_VMEM_BYTES = 64 * 1024 * 1024
VMEM_LIMIT_BYTES = V7X_VMEM_BYTES - 8 * 1024 * 1024
LANES = 128
SMEM_I32_BLOCK = 1024
DMA_GROUP = 8
S5_CHUNK = 16
DN_LAST = (((1,), (1,)), ((), ()))


def _params(*sem):
    return pltpu.CompilerParams(dimension_semantics=sem, vmem_limit_bytes=VMEM_LIMIT_BYTES)


def _normmod(x, g, sc, sh):
    ms = jnp.mean(x * x, axis=-1, keepdims=True)
    return (x * lax.rsqrt(ms + EPS)) * g * (1.0 + sc) + sh


def _rope(p, cos2, sin2):
    half = QK_ROPE // 2
    swapped = jnp.concatenate([p[:, half:], p[:, :half]], axis=1)
    return p * cos2 + swapped * sin2


def _mod_spec(mods, chunk, tm, rows_per_b, width, j_blocks=None):
    _, r, _ = mods.shape
    if j_blocks is None:
        return pl.BlockSpec((None, r, width), lambda i, j: ((i * tm) // rows_per_b, 0, chunk))
    return pl.BlockSpec((None, r, width), lambda i, j: ((i * tm) // rows_per_b, 0, chunk * j_blocks + j))


def _ada_kernel(c_ref, w_ref, b_ref, o_ref, cb_ref):
    @pl.when((pl.program_id(0) == 0) & (pl.program_id(1) == 0))
    def _():
        c = c_ref[...]
        cb_ref[...] = (c * jax.nn.sigmoid(c)).astype(BF16)

    acc = jnp.dot(cb_ref[...], w_ref[...].astype(BF16), preferred_element_type=F32)
    o_ref[...] = acc + b_ref[...]


def _ada(c_all, w, b, tn=1024):
    m, d = c_all.shape
    lw, _, n = w.shape
    return pl.pallas_call(
        _ada_kernel,
        grid=(lw, n // tn),
        in_specs=[
            pl.BlockSpec((m, d), lambda l, j: (0, 0)),
            pl.BlockSpec((None, d, tn), lambda l, j: (l, 0, j)),
            pl.BlockSpec((None, 1, tn), lambda l, j: (l, 0, j)),
        ],
        out_specs=pl.BlockSpec((None, m, tn), lambda l, j: (l, 0, j)),
        out_shape=jax.ShapeDtypeStruct((lw, m, n), F32),
        scratch_shapes=[pltpu.VMEM((m, d), BF16)],
        compiler_params=_params("arbitrary", "arbitrary"),
        name="ada_mod",
    )(c_all, w, b.reshape(lw, 1, n))


def _normmod_kernel(x_ref, g_ref, sc_ref, sh_ref, o_ref):
    o_ref[...] = _normmod(x_ref[...], g_ref[...], sc_ref[...], sh_ref[...]).astype(o_ref.dtype)


def _normmod_call(x, g, mods, sc_chunk, sh_chunk, rows_per_b, tm, out_dtype):
    t, d = x.shape
    return pl.pallas_call(
        _normmod_kernel,
        grid=(t // tm, 1),
        in_specs=[
            pl.BlockSpec((tm, d), lambda i, j: (i, 0)),
            pl.BlockSpec((1, d), lambda i, j: (0, 0)),
            _mod_spec(mods, sc_chunk, tm, rows_per_b, d),
            _mod_spec(mods, sh_chunk, tm, rows_per_b, d),
        ],
        out_specs=pl.BlockSpec((tm, d), lambda i, j: (i, 0)),
        out_shape=jax.ShapeDtypeStruct((t, d), out_dtype),
        compiler_params=_params("arbitrary", "arbitrary"),
        name="norm_mod",
    )(x, g.reshape(1, d), mods, mods)


def _final_norm_kernel(x_ref, g_ref, o_ref):
    x = x_ref[...]
    ms = jnp.mean(x * x, axis=-1, keepdims=True)
    o_ref[...] = (x * lax.rsqrt(ms + EPS)) * g_ref[...]


def _final_norm(x, g, tm):
    t, d = x.shape
    return pl.pallas_call(
        _final_norm_kernel,
        grid=(t // tm,),
        in_specs=[pl.BlockSpec((tm, d), lambda i: (i, 0)), pl.BlockSpec((1, d), lambda i: (0, 0))],
        out_specs=pl.BlockSpec((tm, d), lambda i: (i, 0)),
        out_shape=jax.ShapeDtypeStruct((t, d), F32),
        compiler_params=_params("arbitrary"),
        name="final_norm",
    )(x, g.reshape(1, d))


def _proj_res_kernel(x_ref, *refs, glu, grouped, tm):
    refs = list(refs)
    xb_ref = refs.pop() if grouped else None
    if glu:
        wa_ref, wb_ref, res_ref, gt_ref, o_ref = refs
    else:
        wa_ref, res_ref, gt_ref, o_ref = refs
    if grouped:
        @pl.when(pl.program_id(1) == 0)
        def _():
            cg = SSM_GROUP
            for k in range(tm // LANES):
                zt = x_ref[:, k * LANES * cg:(k + 1) * LANES * cg].T
                xb_ref[k * LANES:(k + 1) * LANES, :] = zt.reshape(LANES, cg, LANES).reshape(LANES, cg * LANES)

        x = xb_ref[...]
    else:
        x = x_ref[...]
    y = jnp.dot(x, wa_ref[...].astype(BF16), preferred_element_type=F32)
    if glu:
        yb = jnp.dot(x, wb_ref[...].astype(BF16), preferred_element_type=F32)
        y = y * jax.nn.sigmoid(yb)
    o_ref[...] = res_ref[...] + gt_ref[...] * y


def _proj_res(x, w, res, mods, gt_chunk, rows_per_b, tm, tn, glu, grouped=False):
    t, n = res.shape
    k = w.shape[0]
    nj = n // tn
    if grouped:
        x_spec = pl.BlockSpec((x.shape[0], tm * SSM_GROUP), lambda i, j: (0, i))
    else:
        x_spec = pl.BlockSpec((tm, k), lambda i, j: (i, 0))
    in_specs = [x_spec, pl.BlockSpec((k, tn), lambda i, j: (0, j))]
    args = [x, w]
    if glu:
        in_specs.append(pl.BlockSpec((k, tn), lambda i, j: (0, nj + j)))
        args.append(w)
    in_specs += [pl.BlockSpec((tm, tn), lambda i, j: (i, j)), _mod_spec(mods, gt_chunk, tm, rows_per_b, tn, nj)]
    args += [res, mods]
    return pl.pallas_call(
        functools.partial(_proj_res_kernel, glu=glu, grouped=grouped, tm=tm),
        grid=(t // tm, nj),
        in_specs=in_specs,
        out_specs=pl.BlockSpec((tm, tn), lambda i, j: (i, j)),
        out_shape=jax.ShapeDtypeStruct((t, n), F32),
        scratch_shapes=[pltpu.VMEM((tm, k), BF16)] if grouped else [],
        compiler_params=_params("arbitrary", "arbitrary"),
        name="glu_proj" if glu else "out_proj",
    )(*args)


def _s5in_kernel(x_ref, g_ref, sc_ref, sh_ref, p_ref, o_ref):
    h = _normmod(x_ref[...], g_ref[...], sc_ref[...], sh_ref[...]).astype(BF16)
    hp = jnp.dot(h, p_ref[...], preferred_element_type=F32).astype(BF16)
    tm = hp.shape[0]
    o_ref[...] = hp.reshape(tm, SSM_GROUP, LANES).reshape(tm * SSM_GROUP, LANES).T


def _s5_in(x, g, mods, sc_chunk, sh_chunk, rows_per_b, tm, perm):
    t, d = x.shape
    ng = d // SSM_GROUP
    return pl.pallas_call(
        _s5in_kernel,
        grid=(t // tm, 1),
        in_specs=[
            pl.BlockSpec((tm, d), lambda i, j: (i, 0)),
            pl.BlockSpec((1, d), lambda i, j: (0, 0)),
            _mod_spec(mods, sc_chunk, tm, rows_per_b, d),
            _mod_spec(mods, sh_chunk, tm, rows_per_b, d),
            pl.BlockSpec((d, d), lambda i, j: (0, 0)),
        ],
        out_specs=pl.BlockSpec((ng, tm * SSM_GROUP), lambda i, j: (0, i)),
        out_shape=jax.ShapeDtypeStruct((ng, t * SSM_GROUP), BF16),
        compiler_params=_params("arbitrary", "arbitrary"),
        name="s5_in",
    )(x, g.reshape(1, d), mods, mods, perm)


def _cpow(lr_dt, li_dt, k):
    mag = jnp.exp(k * lr_dt)
    ang = k * li_dt
    return mag * jnp.cos(ang), mag * jnp.sin(ang)


def _rep_rows(e, n):
    q, w = e.shape
    return jnp.concatenate([jnp.broadcast_to(e[t:t + 1], (n, w)) for t in range(q)], axis=0)


def _tile_rows(c, n):
    return jnp.concatenate([c] * n, axis=0)


def _zoh(lr, li, ldt):
    dt = jnp.exp(ldt)
    lr_dt = lr * dt
    li_dt = li * dt
    mag = jnp.exp(lr_dt)
    ab_re = mag * jnp.cos(li_dt)
    ab_im = mag * jnp.sin(li_dt)
    den = lr * lr + li * li
    f_re = ((ab_re - 1.0) * lr + ab_im * li) / den
    f_im = (ab_im * lr - (ab_re - 1.0) * li) / den
    return lr_dt, li_dt, ab_re, ab_im, f_re, f_im


def _gelu_tanh(y):
    return 0.5 * y * (1.0 + jnp.tanh(math.sqrt(2.0 / math.pi) * (y + 0.044715 * (y * y * y))))


def _s5_prompt_kernel(x_ref, lr_ref, li_ref, ldt_ref, bt_ref, bts_ref, c_ref, cs_ref, d_ref,
                      z_ref, st_ref, s_sc, ss_sc, hp_sc, yl_sc, *, gb, nb, nchunk):
    q, cg, n = S5_CHUNK, SSM_GROUP, SSM_STATE
    half = q // 2
    ti = lax.broadcasted_iota(I32, (q, 1), 0).astype(F32)
    rows = lax.broadcasted_iota(I32, (q * cg, q * cg), 0)
    cols = lax.broadcasted_iota(I32, (q * cg, q * cg), 1)
    shift = cg.bit_length() - 1
    causal = (rows >> shift) <= (cols >> shift)
    diag = rows == cols
    lane = lax.broadcasted_iota(I32, (1, 2 * n), 1)
    sgn = jnp.where(lane < n, -1.0, 1.0)

    def cmul(x2, xs, er, ei):
        return x2 * er + (xs * sgn) * ei

    a_terms = []
    vcs = []
    for g in range(gb):
        lr_dt, li_dt, _, _, f_re, f_im = _zoh(lr_ref[g], li_ref[g], ldt_ref[g])
        bt2, bts = bt_ref[g], bts_ref[g]
        bb2 = cmul(bt2, bts, f_re, f_im)
        bbs = bts * f_re - (bt2 * sgn) * f_im
        bb2t, bbst = _tile_rows(bb2, q), _tile_rows(bbs, q)
        c2t, cst = _tile_rows(c_ref[g], q), _tile_rows(cs_ref[g], q)

        def table(k, lr_dt=lr_dt, li_dt=li_dt):
            er, ei = _cpow(lr_dt, li_dt, k)
            return _rep_rows(er, cg), _rep_rows(ei, cg)

        pc = cmul(c2t, cst, *table(ti - half)) * (-sgn)
        pb = cmul(bb2t, bbst, *table(half - ti))
        tt = lax.dot_general(pb.astype(BF16), pc.astype(BF16), DN_LAST, preferred_element_type=F32)
        tt = jnp.where(causal, tt, 0.0) + jnp.where(diag, d_ref[g], 0.0)
        w1 = cmul(bb2t, bbst, *table((q - 1) - ti))
        vcs.append((cmul(c2t, cst, *table(ti + 1.0)) * (-sgn)).astype(BF16))
        ar, ai = _cpow(lr_dt, li_dt, float(q))
        a_terms.append((ar, ai * sgn))
        x = x_ref[g]
        yl_sc[g] = jnp.dot(x, tt.astype(BF16), preferred_element_type=F32)
        s = jnp.dot(x, w1.astype(BF16), preferred_element_type=F32)
        s_sc[g] = s
        ss_sc[g] = pltpu.roll(s, n, axis=1)

    def carry(ch, hs):
        new = []
        for g in range(gb):
            h, hsw = hs[2 * g], hs[2 * g + 1]
            hp_sc[g, pl.ds(ch, nb, stride=nchunk), :] = h
            a1, a2 = a_terms[g]
            new.append(a1 * h + a2 * hsw + s_sc[g, pl.ds(ch, nb, stride=nchunk), :])
            new.append(a1 * hsw - a2 * h + ss_sc[g, pl.ds(ch, nb, stride=nchunk), :])
        return tuple(new)

    zero = jnp.zeros((nb, 2 * n), F32)
    hs = lax.fori_loop(0, nchunk, carry, (zero,) * (2 * gb), unroll=8)
    for g in range(gb):
        st_ref[g] = hs[2 * g]
        y = yl_sc[g] + lax.dot_general(hp_sc[g].astype(BF16), vcs[g], DN_LAST, preferred_element_type=F32)
        z_ref[g] = _gelu_tanh(y).astype(BF16)


def _s5_prompt(hx, lam_re2, lam_im2, log_dt2, bt2, bts, c2, cs, d_rep, nb, gb=2):
    g, r, w = hx.shape
    n2 = 2 * SSM_STATE
    nchunk = r // nb
    vec = lambda: pl.BlockSpec((gb, 1, n2), lambda i: (i, 0, 0))
    mat = lambda: pl.BlockSpec((gb, SSM_GROUP, n2), lambda i: (i, 0, 0))
    return pl.pallas_call(
        functools.partial(_s5_prompt_kernel, gb=gb, nb=nb, nchunk=nchunk),
        grid=(g // gb,),
        in_specs=[pl.BlockSpec((gb, r, w), lambda i: (i, 0, 0)), vec(), vec(), vec(),
                  mat(), mat(), mat(), mat(), pl.BlockSpec((gb, 1, w), lambda i: (i, 0, 0))],
        out_specs=[pl.BlockSpec((gb, r, w), lambda i: (i, 0, 0)),
                   pl.BlockSpec((gb, nb, n2), lambda i: (i, 0, 0))],
        out_shape=[jax.ShapeDtypeStruct((g, r, w), BF16),
                   jax.ShapeDtypeStruct((g, nb, n2), F32)],
        scratch_shapes=[pltpu.VMEM((gb, r, n2), F32), pltpu.VMEM((gb, r, n2), F32),
                        pltpu.VMEM((gb, r, n2), F32), pltpu.VMEM((gb, r, w), F32)],
        compiler_params=_params("arbitrary"),
        name="s5_prompt",
    )(hx, lam_re2, lam_im2, log_dt2, bt2, bts, c2, cs, d_rep)


def _s5_sample_kernel(u_ref, h0r_ref, h0i_ref, lr_ref, li_ref, ldt_ref, btr_ref, bti_ref,
                      cr_ref, ci_ref, d_ref, z_ref, nr_ref, ni_ref, *, gb):
    hp = lax.Precision.HIGHEST
    for g in range(gb):
        _, _, ab_re, ab_im, f_re, f_im = _zoh(lr_ref[g], li_ref[g], ldt_ref[g])
        bb_re = f_re * btr_ref[g] - f_im * bti_ref[g]
        bb_im = f_re * bti_ref[g] + f_im * btr_ref[g]
        u = u_ref[g]
        bu_re = jnp.dot(u, bb_re, preferred_element_type=F32, precision=hp)
        bu_im = jnp.dot(u, bb_im, preferred_element_type=F32, precision=hp)
        h0r = h0r_ref[g]
        h0i = h0i_ref[g]
        hr = bu_re + ab_re * h0r - ab_im * h0i
        hi = bu_im + ab_re * h0i + ab_im * h0r
        nr_ref[g] = hr
        ni_ref[g] = hi
        y = (lax.dot_general(hr, cr_ref[g], DN_LAST, preferred_element_type=F32, precision=hp)
             - lax.dot_general(hi, ci_ref[g], DN_LAST, preferred_element_type=F32, precision=hp))
        y = y + d_ref[g] * u
        z_ref[g] = _gelu_tanh(y).astype(BF16)


def _s5_sample(u, h0_re, h0_im, lam_re, lam_im, log_dt, bt_re, bt_im, c_re, c_im, d_g, gb=8):
    g, b, cg = u.shape
    n = SSM_STATE
    vec = lambda: pl.BlockSpec((gb, 1, n), lambda i: (i, 0, 0))
    mat = lambda: pl.BlockSpec((gb, cg, n), lambda i: (i, 0, 0))
    st = lambda: pl.BlockSpec((gb, b, n), lambda i: (i, 0, 0))
    return pl.pallas_call(
        functools.partial(_s5_sample_kernel, gb=gb),
        grid=(g // gb,),
        in_specs=[pl.BlockSpec((gb, b, cg), lambda i: (i, 0, 0)), st(), st(), vec(), vec(), vec(),
                  mat(), mat(), mat(), mat(), pl.BlockSpec((gb, 1, cg), lambda i: (i, 0, 0))],
        out_specs=[pl.BlockSpec((gb, b, cg), lambda i: (i, 0, 0)), st(), st()],
        out_shape=[jax.ShapeDtypeStruct((g, b, cg), BF16),
                   jax.ShapeDtypeStruct((g, b, n), F32),
                   jax.ShapeDtypeStruct((g, b, n), F32)],
        compiler_params=_params("arbitrary"),
        name="s5_sample",
    )(u, h0_re, h0_im, lam_re, lam_im, log_dt, bt_re, bt_im, c_re, c_im, d_g)


def _route_kernel(x_ref, g_ref, sc_ref, sh_ref, wr_ref, br_ref, h_ref, gates_ref, e_ref, w_ref, *, split):
    h = _normmod(x_ref[...], g_ref[...], sc_ref[...], sh_ref[...])
    hb = h.astype(BF16)
    h_ref[...] = hb.reshape(h_ref.shape) if split else hb
    logits = lax.dot_general(wr_ref[...], h, DN_LAST, preferred_element_type=F32,
                             precision=lax.Precision.HIGHEST)
    s = jax.nn.sigmoid(logits)
    sel = s + br_ref[...]
    epg = EXPERTS_PER_GROUP
    srow = [s[e:e + 1, :] for e in range(N_EXPERTS)]
    row = [sel[e:e + 1, :] for e in range(N_EXPERTS)]

    def top2_sum(a, b, c, d):
        hi1, lo1 = jnp.maximum(a, b), jnp.minimum(a, b)
        hi2, lo2 = jnp.maximum(c, d), jnp.minimum(c, d)
        return jnp.maximum(hi1, hi2) + jnp.maximum(jnp.minimum(hi1, hi2), jnp.maximum(lo1, lo2))

    gs = [top2_sum(*row[epg * g:epg * (g + 1)]) for g in range(N_EXPERT_GROUPS)]
    best = gs[0]
    gidx = jnp.zeros_like(best, dtype=I32)
    for g in range(1, N_EXPERT_GROUPS):
        better = gs[g] > best
        gidx = jnp.where(better, g, gidx)
        best = jnp.where(better, gs[g], best)

    def pick(rows_, j):
        v = rows_[j]
        for g in range(1, N_EXPERT_GROUPS):
            v = jnp.where(gidx == g, rows_[epg * g + j], v)
        return v

    v = [pick(row, j) for j in range(epg)]
    u = [pick(srow, j) for j in range(epg)]

    def argmax_first(vals):
        bv, bi = vals[0], jnp.zeros_like(gidx)
        for j in range(1, epg):
            better = vals[j] > bv
            bi = jnp.where(better, j, bi)
            bv = jnp.where(better, vals[j], bv)
        return bi

    i1 = argmax_first(v)
    i2 = argmax_first([jnp.where(i1 == j, -jnp.inf, v[j]) for j in range(epg)])

    def take(vals, idx):
        out = vals[0]
        for j in range(1, epg):
            out = jnp.where(idx == j, vals[j], out)
        return out

    w1, w2 = take(u, i1), take(u, i2)
    tot = w1 + w2
    w1, w2 = w1 / tot, w2 / tot
    e1 = gidx * epg + i1
    e2 = gidx * epg + i2
    gates_ref[...] = jnp.concatenate(
        [jnp.where(e1 == e, w1, 0.0) + jnp.where(e2 == e, w2, 0.0) for e in range(N_EXPERTS)], axis=0)
    e_ref[...] = jnp.concatenate([e1, e2], axis=0)
    w_ref[...] = jnp.concatenate([w1, w2], axis=0)


def _route(x, g, mods, sc_chunk, sh_chunk, rows_per_b, tm, w_router_t, b_router, split):
    t, d = x.shape
    e = N_EXPERTS
    if split:
        h_spec = pl.BlockSpec((tm, d // LANES, LANES), lambda i, j: (i, 0, 0))
        h_shape = jax.ShapeDtypeStruct((t, d // LANES, LANES), BF16)
    else:
        h_spec = pl.BlockSpec((tm, d), lambda i, j: (i, 0))
        h_shape = jax.ShapeDtypeStruct((t, d), BF16)
    return pl.pallas_call(
        functools.partial(_route_kernel, split=split),
        grid=(t // tm, 1),
        in_specs=[
            pl.BlockSpec((tm, d), lambda i, j: (i, 0)),
            pl.BlockSpec((1, d), lambda i, j: (0, 0)),
            _mod_spec(mods, sc_chunk, tm, rows_per_b, d),
            _mod_spec(mods, sh_chunk, tm, rows_per_b, d),
            pl.BlockSpec((e, d), lambda i, j: (0, 0)),
            pl.BlockSpec((e, 1), lambda i, j: (0, 0)),
        ],
        out_specs=[h_spec, pl.BlockSpec((e, tm), lambda i, j: (0, i)),
                   pl.BlockSpec((TOP_K, tm), lambda i, j: (0, i)), pl.BlockSpec((TOP_K, tm), lambda i, j: (0, i))],
        out_shape=[h_shape, jax.ShapeDtypeStruct((e, t), F32),
                   jax.ShapeDtypeStruct((TOP_K, t), I32), jax.ShapeDtypeStruct((TOP_K, t), F32)],
        compiler_params=_params("arbitrary", "arbitrary"),
        name="moe_route",
    )(x, g.reshape(1, d), mods, mods, w_router_t, b_router.reshape(e, 1))


def _moe_kernel(h_ref, gates_ref, wg_ref, wu_ref, wd_ref, res_ref, gt_ref, o_ref, acc_ref):
    e = pl.program_id(1)
    first = (e == 0) & (pl.program_id(2) == 0)
    last = (e == pl.num_programs(1) - 1) & (pl.program_id(2) == pl.num_programs(2) - 1)

    @pl.when(first)
    def _():
        acc_ref[...] = jnp.zeros_like(acc_ref)

    h = h_ref[...]
    a = jnp.dot(h, wg_ref[...].astype(BF16), preferred_element_type=F32)
    b = jnp.dot(h, wu_ref[...].astype(BF16), preferred_element_type=F32)
    gates = gates_ref[...]
    lane = lax.broadcasted_iota(I32, gates.shape, 1)
    gcol = jnp.sum(jnp.where(lane == e, gates, 0.0), axis=1, keepdims=True)
    hid = (a * jax.nn.sigmoid(a)) * b * gcol
    acc_ref[...] += jnp.dot(hid.astype(BF16), wd_ref[...].astype(BF16), preferred_element_type=F32)

    @pl.when(last)
    def _():
        o_ref[...] = res_ref[...] + gt_ref[...] * acc_ref[...]


def _moe(h, gates, w_gate, w_up, w_down, layer, res, mods, gt_chunk, rows_per_b, tm, fsplit=2):
    t, d = h.shape
    e, _, f = w_gate.shape[1:]
    tf = f // fsplit
    r = mods.shape[1]
    return pl.pallas_call(
        _moe_kernel,
        grid=(t // tm, e, fsplit),
        in_specs=[
            pl.BlockSpec((tm, d), lambda i, j, k: (i, 0)),
            pl.BlockSpec((tm, e), lambda i, j, k: (i, 0)),
            pl.BlockSpec((None, None, d, tf), lambda i, j, k: (layer, j, 0, k)),
            pl.BlockSpec((None, None, d, tf), lambda i, j, k: (layer, j, 0, k)),
            pl.BlockSpec((None, None, tf, d), lambda i, j, k: (layer, j, k, 0)),
            pl.BlockSpec((tm, d), lambda i, j, k: (i, 0)),
            pl.BlockSpec((None, r, d), lambda i, j, k: ((i * tm) // rows_per_b, 0, gt_chunk)),
        ],
        out_specs=pl.BlockSpec((tm, d), lambda i, j, k: (i, 0)),
        out_shape=jax.ShapeDtypeStruct((t, d), F32),
        scratch_shapes=[pltpu.VMEM((tm, d), F32)],
        compiler_params=_params("arbitrary", "arbitrary", "arbitrary"),
        name="moe_experts",
    )(h, gates, w_gate, w_up, w_down, res, mods)


def _experts_kernel(te_ref, nu_ref, idxc_ref, idxn_ref, h_ref, wg_ref, wu_ref, wd_ref, o_ref,
                    wgb, wub, wdb, xbuf, sem, *, tg):
    i = pl.program_id(0)
    nu = nu_ref[0]
    slot = i % 2
    per_blk = SMEM_I32_BLOCK // tg

    n_parts = 4
    part = tg // n_parts

    def start_gather(idx_ref, tile, dst_slot, lo, hi):
        off = (tile % per_blk) * tg

        def issue(grp, carry):
            for u in range(DMA_GROUP):
                r = grp * DMA_GROUP + u
                pltpu.make_async_copy(h_ref.at[pl.ds(idx_ref[off + r], 1)], xbuf.at[dst_slot, pl.ds(r, 1)],
                                      sem.at[dst_slot]).start(priority=u % 2)
            return carry

        lax.fori_loop(lo // DMA_GROUP, hi // DMA_GROUP, issue, 0)

    def prefetch_part(k):
        @pl.when(i + 1 < nu)
        def _():
            start_gather(idxn_ref, i + 1, 1 - slot, k * part, (k + 1) * part)

    @pl.when(i == 0)
    def _():
        start_gather(idxc_ref, i, slot, 0, tg)

    prev = te_ref[jnp.maximum(i - 1, 0)]

    @pl.when((i == 0) | (te_ref[i] != prev))
    def _():
        wgb[...] = wg_ref[...].astype(BF16)
        wub[...] = wu_ref[...].astype(BF16)
        wdb[...] = wd_ref[...].astype(BF16)

    @pl.when(i < nu)
    def _():
        def drain(r, carry):
            pltpu.make_async_copy(h_ref.at[pl.ds(0, 1)], xbuf.at[slot, pl.ds(r, 1)], sem.at[slot]).wait()
            return carry

        lax.fori_loop(0, tg, drain, 0, unroll=8)
        x = xbuf[slot].reshape(tg, wgb.shape[0])
        prefetch_part(0)
        a = jnp.dot(x, wgb[...], preferred_element_type=F32)
        prefetch_part(1)
        b = jnp.dot(x, wub[...], preferred_element_type=F32)
        prefetch_part(2)
        hid = ((a * jax.nn.sigmoid(a)) * b).astype(BF16)
        prefetch_part(3)
        y = jnp.dot(hid, wdb[...], preferred_element_type=F32)
        o_ref[...] = y.reshape(o_ref.shape)

    @pl.when(i >= nu)
    def _():
        o_ref[...] = jnp.zeros_like(o_ref)


def _experts(tile_expert, n_used, sorted_tok, h3, w_gate, w_up, w_down, layer, tg):
    rp = sorted_tok.shape[0]
    _, _, d, f = w_gate.shape
    blk = SMEM_I32_BLOCK
    per_blk = blk // tg
    last_blk = rp // blk - 1
    row_shape = h3.shape[1:]
    grid_spec = pltpu.PrefetchScalarGridSpec(
        num_scalar_prefetch=2,
        grid=(rp // tg,),
        in_specs=[
            pl.BlockSpec((blk,), lambda i, te, nu: (i // per_blk,), memory_space=pltpu.SMEM),
            pl.BlockSpec((blk,), lambda i, te, nu: (jnp.minimum((i + 1) // per_blk, last_blk),),
                         memory_space=pltpu.SMEM),
            pl.BlockSpec(memory_space=pl.ANY),
            pl.BlockSpec((None, None, d, f), lambda i, te, nu: (layer, te[i], 0, 0)),
            pl.BlockSpec((None, None, d, f), lambda i, te, nu: (layer, te[i], 0, 0)),
            pl.BlockSpec((None, None, f, d), lambda i, te, nu: (layer, te[i], 0, 0)),
        ],
        out_specs=pl.BlockSpec((tg,) + row_shape, lambda i, te, nu: (i, 0, 0)),
        scratch_shapes=[pltpu.VMEM((d, f), BF16), pltpu.VMEM((d, f), BF16), pltpu.VMEM((f, d), BF16),
                        pltpu.VMEM((2, tg) + row_shape, h3.dtype), pltpu.SemaphoreType.DMA((2,))],
    )
    return pl.pallas_call(
        functools.partial(_experts_kernel, tg=tg),
        grid_spec=grid_spec,
        out_shape=jax.ShapeDtypeStruct((rp,) + row_shape, F32),
        compiler_params=_params("arbitrary"),
        name="moe_routed_experts",
    )(tile_expert, n_used, sorted_tok, sorted_tok, h3, w_gate, w_up, w_down)


def _combine_kernel(pos_ref, ys_ref, w_ref, res_ref, gt_ref, *refs, tm, final):
    if final:
        gf_ref, o_ref, buf, sem = refs
    else:
        o_ref, buf, sem = refs

    def issue(grp, carry):
        for u in range(DMA_GROUP):
            r = grp * DMA_GROUP + u
            for k in range(TOP_K):
                pltpu.make_async_copy(ys_ref.at[pl.ds(pos_ref[k * tm + r], 1)], buf.at[k, pl.ds(r, 1)],
                                      sem).start(priority=(u + k) % 2)
        return carry

    lax.fori_loop(0, tm // DMA_GROUP, issue, 0)

    def drain(r, carry):
        for k in range(TOP_K):
            pltpu.make_async_copy(ys_ref.at[pl.ds(0, 1)], buf.at[k, pl.ds(r, 1)], sem).wait()
        return carry

    lax.fori_loop(0, tm, drain, 0, unroll=8)
    d = o_ref.shape[1]
    w = w_ref[...]
    m = (buf[0].reshape(tm, d) * jnp.tile(w[:, :LANES], (1, d // LANES))
         + buf[1].reshape(tm, d) * jnp.tile(w[:, LANES:], (1, d // LANES)))
    x = res_ref[...] + gt_ref[...] * m
    if final:
        ms = jnp.mean(x * x, axis=-1, keepdims=True)
        x = (x * lax.rsqrt(ms + EPS)) * gf_ref[...]
    o_ref[...] = x


def _combine(pos, ys, wtok, res, mods, gt_chunk, rows_per_b, tm, g_final=None):
    t, d = res.shape
    assert TOP_K * tm == SMEM_I32_BLOCK
    r = mods.shape[1]
    final = g_final is not None
    in_specs = [pl.BlockSpec((TOP_K * tm,), lambda i: (i,), memory_space=pltpu.SMEM),
                pl.BlockSpec(memory_space=pl.ANY),
                pl.BlockSpec((tm, TOP_K * LANES), lambda i: (i, 0)),
                pl.BlockSpec((tm, d), lambda i: (i, 0)),
                pl.BlockSpec((None, r, d), lambda i: ((i * tm) // rows_per_b, 0, gt_chunk))]
    args = [pos, ys, wtok, res, mods]
    if final:
        in_specs.append(pl.BlockSpec((1, d), lambda i: (0, 0)))
        args.append(g_final.reshape(1, d))
    return pl.pallas_call(
        functools.partial(_combine_kernel, tm=tm, final=final),
        grid=(t // tm,),
        in_specs=in_specs,
        out_specs=pl.BlockSpec((tm, d), lambda i: (i, 0)),
        out_shape=jax.ShapeDtypeStruct((t, d), F32),
        scratch_shapes=[pltpu.VMEM((TOP_K, tm) + ys.shape[1:], F32), pltpu.SemaphoreType.DMA(())],
        compiler_params=_params("arbitrary"),
        name="moe_combine",
    )(*args)


def _moe_routed(h3, e_rows, w_rows, w_gate, w_up, w_down, layer, res, mods, gt_chunk, rows_per_b,
                g_final=None, tg=256, tmc=512):
    t = h3.shape[0]
    ne = N_EXPERTS
    a = TOP_K * t
    rp = a + ne * tg
    rp = -(-rp // SMEM_I32_BLOCK) * SMEM_I32_BLOCK
    n_tiles = rp // tg
    ea = e_rows.reshape(a)
    ta = jnp.tile(jnp.arange(t, dtype=I32), TOP_K)
    onehot = (ea[:, None] == jnp.arange(ne, dtype=I32)[None, :]).astype(I32)
    csum = jnp.cumsum(onehot, axis=0)
    rank = jnp.sum((csum - onehot) * onehot, axis=1)
    counts = csum[-1]
    ptiles = (counts + tg - 1) // tg
    tile_end = jnp.cumsum(ptiles)
    tile_start = tile_end - ptiles
    n_used = tile_end[-1:].astype(I32)
    dest = jnp.sum(onehot * tile_start[None, :], axis=1) * tg + rank
    sorted_tok = jnp.zeros((rp,), I32).at[dest].set(ta, unique_indices=True)
    tile_expert = jnp.sum(jnp.arange(n_tiles, dtype=I32)[:, None] >= tile_end[None, :], axis=1)
    tile_expert = jnp.minimum(tile_expert, ne - 1).astype(I32)
    ys = _experts(tile_expert, n_used, sorted_tok, h3, w_gate, w_up, w_down, layer, tg)
    pos = dest.reshape(TOP_K, t // tmc, tmc).transpose(1, 0, 2).reshape(a)
    wtok = jnp.concatenate([jnp.broadcast_to(w_rows[k][:, None], (t, LANES)) for k in range(TOP_K)], axis=1)
    return _combine(pos, ys, wtok, res, mods, gt_chunk, rows_per_b, tmc, g_final)


def _kv_kernel(x_ref, g_ref, sc_ref, sh_ref, w_ref, gc_ref, cos_ref, sin_ref, ckv_ref, kpe_ref, wb_ref):
    @pl.when(pl.program_id(0) == 0)
    def _():
        wb_ref[...] = w_ref[...].astype(BF16)

    h = _normmod(x_ref[...], g_ref[...], sc_ref[...], sh_ref[...]).astype(BF16)
    lat = lax.dot_general(h, wb_ref[...], DN_LAST, preferred_element_type=F32)
    c = lat[:, :KV_LORA]
    ms = jnp.mean(c * c, axis=-1, keepdims=True)
    ckv_ref[...] = (c * lax.rsqrt(ms + EPS)) * gc_ref[...]
    kpe_ref[...] = _rope(lat[:, KV_LORA:], cos_ref[...], sin_ref[...])


def _kv(x, g_kv, mods, rows_per_b, tm, w_dkv_t, g_ckv, cos2, sin2):
    t, d = x.shape
    n = w_dkv_t.shape[0]
    nt = cos2.shape[0] // tm
    return pl.pallas_call(
        _kv_kernel,
        grid=(t // tm, 1),
        in_specs=[
            pl.BlockSpec((tm, d), lambda i, j: (i, 0)),
            pl.BlockSpec((1, d), lambda i, j: (0, 0)),
            _mod_spec(mods, 1, tm, rows_per_b, d),
            _mod_spec(mods, 0, tm, rows_per_b, d),
            pl.BlockSpec((n, d), lambda i, j: (0, 0)),
            pl.BlockSpec((1, KV_LORA), lambda i, j: (0, 0)),
            pl.BlockSpec((tm, QK_ROPE), lambda i, j: (i % nt, 0)),
            pl.BlockSpec((tm, QK_ROPE), lambda i, j: (i % nt, 0)),
        ],
        out_specs=[pl.BlockSpec((tm, KV_LORA), lambda i, j: (i, 0)),
                   pl.BlockSpec((tm, QK_ROPE), lambda i, j: (i, 0))],
        out_shape=[jax.ShapeDtypeStruct((t, KV_LORA), F32), jax.ShapeDtypeStruct((t, QK_ROPE), F32)],
        scratch_shapes=[pltpu.VMEM((n, d), BF16)],
        compiler_params=_params("arbitrary", "arbitrary"),
        name="latent_kv",
    )(x, g_kv.reshape(1, d), mods, mods, w_dkv_t, g_ckv.reshape(1, KV_LORA), cos2, sin2)


def _qproj_kernel(x_ref, g_ref, sc_ref, sh_ref, w_ref, cos_ref, sin_ref, q_ref, hb_ref, *, hb):
    @pl.when(pl.program_id(1) == 0)
    def _():
        hb_ref[...] = _normmod(x_ref[...], g_ref[...], sc_ref[...], sh_ref[...]).astype(BF16)

    acc = jnp.dot(hb_ref[...], w_ref[...].astype(BF16), preferred_element_type=F32)
    cos2, sin2 = cos_ref[...], sin_ref[...]
    for h in range(hb):
        c0 = h * QK_HEAD
        q_ref[h, :, :QK_NOPE] = acc[:, c0:c0 + QK_NOPE].astype(BF16)
        q_ref[h, :, QK_NOPE:] = _rope(acc[:, c0 + QK_NOPE:c0 + QK_HEAD], cos2, sin2).astype(BF16)


def _qproj(x, g, mods, sc_chunk, sh_chunk, rows_per_b, tm, w_q2, cos2, sin2, hb=4):
    t, d = x.shape
    nh = w_q2.shape[1] // QK_HEAD
    nt = rows_per_b // tm
    return pl.pallas_call(
        functools.partial(_qproj_kernel, hb=hb),
        grid=(t // tm, nh // hb),
        in_specs=[
            pl.BlockSpec((tm, d), lambda i, j: (i, 0)),
            pl.BlockSpec((1, d), lambda i, j: (0, 0)),
            _mod_spec(mods, sc_chunk, tm, rows_per_b, d),
            _mod_spec(mods, sh_chunk, tm, rows_per_b, d),
            pl.BlockSpec((d, hb * QK_HEAD), lambda i, j: (0, j)),
            pl.BlockSpec((tm, QK_ROPE), lambda i, j: (i % nt, 0)),
            pl.BlockSpec((tm, QK_ROPE), lambda i, j: (i % nt, 0)),
        ],
        out_specs=pl.BlockSpec((None, hb, tm, QK_HEAD), lambda i, j: (i // nt, j, i % nt, 0)),
        out_shape=jax.ShapeDtypeStruct((t // rows_per_b, nh, rows_per_b, QK_HEAD), BF16),
        scratch_shapes=[pltpu.VMEM((tm, d), BF16)],
        compiler_params=_params("arbitrary", "arbitrary"),
        name="q_proj",
    )(x, g.reshape(1, d), mods, mods, w_q2, cos2, sin2)


def _kvup_kernel(ckv_ref, kpe_ref, wk_ref, wv_ref, k_ref, v_ref, *, hb):
    c = ckv_ref[...].astype(BF16)
    kn = jnp.dot(c, wk_ref[...].astype(BF16), preferred_element_type=F32)
    vv = jnp.dot(c, wv_ref[...].astype(BF16), preferred_element_type=F32)
    kpe = kpe_ref[...].astype(BF16)
    for h in range(hb):
        k_ref[h, :, :QK_NOPE] = kn[:, h * QK_NOPE:(h + 1) * QK_NOPE].astype(BF16)
        k_ref[h, :, QK_NOPE:] = kpe
        v_ref[h] = vv[:, h * V_HEAD:(h + 1) * V_HEAD].astype(BF16)


def _kvup(ckv, kpe, w_uk2, w_uv2, bsz, seq, tm, hb=4):
    t, r = ckv.shape
    nt = seq // tm
    return pl.pallas_call(
        functools.partial(_kvup_kernel, hb=hb),
        grid=(t // tm, N_HEADS // hb),
        in_specs=[
            pl.BlockSpec((tm, r), lambda i, j: (i, 0)),
            pl.BlockSpec((tm, QK_ROPE), lambda i, j: (i, 0)),
            pl.BlockSpec((r, hb * QK_NOPE), lambda i, j: (0, j)),
            pl.BlockSpec((r, hb * V_HEAD), lambda i, j: (0, j)),
        ],
        out_specs=[pl.BlockSpec((None, hb, tm, QK_HEAD), lambda i, j: (i // nt, j, i % nt, 0)),
                   pl.BlockSpec((None, hb, tm, V_HEAD), lambda i, j: (i // nt, j, i % nt, 0))],
        out_shape=[jax.ShapeDtypeStruct((bsz, N_HEADS, seq, QK_HEAD), BF16),
                   jax.ShapeDtypeStruct((bsz, N_HEADS, seq, V_HEAD), BF16)],
        compiler_params=_params("arbitrary", "arbitrary"),
        name="kv_up",
    )(ckv, kpe, w_uk2, w_uv2)


def _flash_kernel(qt_ref, kt_ref, q_ref, k_ref, v_ref, o_ref, m_ref, l_ref, acc_ref, *, tq, tk, hb):
    p_idx = pl.program_id(2)
    qi = qt_ref[p_idx]
    kj = kt_ref[p_idx]

    @pl.when(kj == 0)
    def _():
        m_ref[...] = jnp.full_like(m_ref, -jnp.inf)
        l_ref[...] = jnp.zeros_like(l_ref)
        acc_ref[...] = jnp.zeros_like(acc_ref)

    def tile(h, masked):
        s = lax.dot_general(q_ref[h], k_ref[h], DN_LAST, preferred_element_type=F32) * (ATTN_SCALE * LOG2_E)
        if masked:
            qpos = lax.broadcasted_iota(I32, (tq, tk), 0)
            kpos = lax.broadcasted_iota(I32, (tq, tk), 1)
            s = jnp.where(kpos <= qpos, s, -jnp.inf)
        m_prev = m_ref[h]
        m_next = jnp.maximum(m_prev, jnp.max(s, axis=1, keepdims=True))
        alpha = jnp.exp2(m_prev - m_next)
        p = jnp.exp2(s - jnp.tile(m_next, (1, tk // LANES)))
        l_ref[h] = alpha * l_ref[h] + jnp.sum(p, axis=1, keepdims=True)
        acc_ref[h] = alpha * acc_ref[h] + jnp.dot(p.astype(BF16), v_ref[h], preferred_element_type=F32)
        m_ref[h] = m_next

    @pl.when(kj < qi)
    def _():
        for h in range(hb):
            tile(h, False)

    @pl.when(kj == qi)
    def _():
        for h in range(hb):
            tile(h, True)
            o_ref[:, h * V_HEAD:(h + 1) * V_HEAD] = (acc_ref[h] / l_ref[h]).astype(BF16)


def _flash(q, k, v, tile_len=512, hb=2):
    bsz, nh, seq, _ = q.shape
    tq = tk = tile_len
    nq = seq // tq
    pairs = [(i, j) for i in range(nq) for j in range(i + 1)]
    qt = jnp.asarray([p[0] for p in pairs], I32)
    kt = jnp.asarray([p[1] for p in pairs], I32)
    grid_spec = pltpu.PrefetchScalarGridSpec(
        num_scalar_prefetch=2,
        grid=(bsz, nh // hb, len(pairs)),
        in_specs=[
            pl.BlockSpec((None, hb, tq, QK_HEAD), lambda b, h, p, qt, kt: (b, h, qt[p], 0)),
            pl.BlockSpec((None, hb, tk, QK_HEAD), lambda b, h, p, qt, kt: (b, h, kt[p], 0)),
            pl.BlockSpec((None, hb, tk, V_HEAD), lambda b, h, p, qt, kt: (b, h, kt[p], 0)),
        ],
        out_specs=pl.BlockSpec((tq, hb * V_HEAD), lambda b, h, p, qt, kt: (b * nq + qt[p], h)),
        scratch_shapes=[pltpu.VMEM((hb, tq, LANES), F32), pltpu.VMEM((hb, tq, LANES), F32),
                        pltpu.VMEM((hb, tq, V_HEAD), F32)],
    )
    return pl.pallas_call(
        functools.partial(_flash_kernel, tq=tq, tk=tk, hb=hb),
        grid_spec=grid_spec,
        out_shape=jax.ShapeDtypeStruct((bsz * seq, nh * V_HEAD), BF16),
        compiler_params=_params("arbitrary", "arbitrary", "arbitrary"),
        name="flash_attn",
    )(qt, kt, q, k, v)


def _qlat_kernel(q_ref, wk_ref, o_ref):
    q = q_ref[...]
    qlat = lax.dot_general(q[:, :QK_NOPE], wk_ref[...].astype(BF16), DN_LAST, preferred_element_type=F32)
    o_ref[:, :KV_LORA] = qlat.astype(BF16)
    o_ref[:, KV_LORA:] = q[:, QK_NOPE:]


def _qlat(q, w_uk2):
    nh, b, _ = q.shape
    r = w_uk2.shape[0]
    return pl.pallas_call(
        _qlat_kernel,
        grid=(nh,),
        in_specs=[pl.BlockSpec((None, b, QK_HEAD), lambda h: (h, 0, 0)),
                  pl.BlockSpec((r, QK_NOPE), lambda h: (0, h))],
        out_specs=pl.BlockSpec((None, b, r + QK_ROPE), lambda h: (h, 0, 0)),
        out_shape=jax.ShapeDtypeStruct((nh, b, r + QK_ROPE), BF16),
        compiler_params=_params("arbitrary"),
        name="q_latent",
    )(q, w_uk2)


def _paged_kernel(pt_ref, q_ref, ckv_hbm, kpe_hbm, cnew_ref, pnew_ref, o_ref, qpad_ref, kcb_ref, pp_ref,
                  m_ref, l_ref, acc_ref, pgc_ref, pgk_ref, semc, semk, *, npg, nsteps, nrows, rb):
    bi = pl.program_id(0)
    j = pl.program_id(1)
    nh = N_HEADS
    wide = KV_LORA // LANES
    first = j == 0

    def page_copies(row_blk, step, slot):
        copies = []
        for r in range(rb):
            for i in range(npg):
                pg = pt_ref[row_blk * rb + r, step * npg + i]
                copies.append(pltpu.make_async_copy(ckv_hbm.at[pg], pgc_ref.at[slot, r * npg + i], semc.at[slot]))
                copies.append(pltpu.make_async_copy(kpe_hbm.at[pg], pgk_ref.at[slot, r * npg + i], semk.at[slot]))
        return copies

    @pl.when(first & (bi == 0))
    def _():
        for c in page_copies(bi, j, 0):
            c.start()

    @pl.when(first & (bi == 0))
    def _():
        kcb_ref[...] = jnp.zeros_like(kcb_ref)
        pp_ref[...] = jnp.zeros_like(pp_ref)
        m_ref[...] = jnp.zeros_like(m_ref)
        l_ref[...] = jnp.zeros_like(l_ref)
        acc_ref[...] = jnp.zeros_like(acc_ref)

    @pl.when(first)
    def _():
        qpad_ref[...] = jnp.zeros_like(qpad_ref)
        for r in range(rb):
            qpad_ref[r, :nh, :] = q_ref[r]

    def step(cur, prv):
        more_steps = j + 1 < nsteps
        nxt_step = jnp.where(more_steps, j + 1, 0)
        nxt_blk = jnp.where(more_steps, bi, bi + 1)

        @pl.when(more_steps | (bi + 1 < nrows // rb))
        def _():
            for c in page_copies(nxt_blk, nxt_step, prv):
                c.start()

        for k in range(rb * npg):
            pltpu.make_async_copy(ckv_hbm.at[0], pgc_ref.at[cur, k], semc.at[cur]).wait()
            pltpu.make_async_copy(kpe_hbm.at[0], pgk_ref.at[cur, k], semk.at[cur]).wait()
        for r in range(rb):
            pv_prev = jnp.dot(pp_ref[r], kcb_ref[r, prv], preferred_element_type=F32)
            acc_in = jnp.where(first, 0.0, acc_ref[r] + pv_prev)
            m_prev = jnp.where(first, -jnp.inf, m_ref[r])
            l_prev = jnp.where(first, 0.0, l_ref[r])
            for i in range(npg):
                kcb_ref[r, cur, i * PAGE_SIZE:(i + 1) * PAGE_SIZE, :] = pgc_ref[cur, r * npg + i].astype(BF16)
            s_c = lax.dot_general(kcb_ref[r, cur], qpad_ref[r, :, :KV_LORA], DN_LAST,
                                  preferred_element_type=F32)
            kp = jnp.concatenate([pgk_ref[cur, r * npg + i].astype(BF16) for i in range(npg)],
                                 axis=1)
            s_p = jnp.dot(qpad_ref[r, :, KV_LORA:], kp, preferred_element_type=F32)
            s = (s_c.T[:nh, :] + s_p[:nh, :]) * ATTN_SCALE
            m_next = jnp.maximum(m_prev, jnp.max(s, axis=1, keepdims=True))
            alpha = jnp.exp(m_prev - m_next)
            p = jnp.exp(s - jnp.tile(m_next, (1, s.shape[1] // LANES)))
            l_ref[r] = alpha * l_prev + jnp.sum(p, axis=1, keepdims=True)
            acc_ref[r] = jnp.tile(alpha, (1, wide)) * acc_in
            m_ref[r] = m_next
            pp_ref[r] = p.astype(BF16)

    @pl.when(j % 2 == 0)
    def _():
        step(0, 1)

    @pl.when(j % 2 == 1)
    def _():
        step(1, 0)

    @pl.when(j == nsteps - 1)
    def _():
        last_slot = (nsteps - 1) % 2
        for r in range(rb):
            acc_past = acc_ref[r] + jnp.dot(pp_ref[r], kcb_ref[r, last_slot], preferred_element_type=F32)
            q = q_ref[r].astype(F32)
            cnew = cnew_ref[r]
            s_new = (jnp.sum(q[:, :KV_LORA] * cnew, axis=1, keepdims=True)
                     + jnp.sum(q[:, KV_LORA:] * pnew_ref[r], axis=1, keepdims=True)) * ATTN_SCALE
            m_p = m_ref[r]
            m_n = jnp.maximum(m_p, s_new)
            a = jnp.exp(m_p - m_n)
            p_new = jnp.exp(s_new - m_n)
            l_fin = a * l_ref[r] + p_new
            acc = jnp.tile(a, (1, wide)) * acc_past + jnp.tile(p_new, (1, wide)) * cnew
            o_ref[r] = (acc / jnp.tile(l_fin, (1, wide))).astype(BF16)


def _paged_attn(qcat, cache_ckv, cache_kpe_t, page_table, ckv_new, kpe_new, npg=8, rb=2):
    b, nh, w = qcat.shape
    r = KV_LORA
    n_pages = page_table.shape[1]
    keys = npg * PAGE_SIZE
    nsteps = n_pages // npg
    assert nsteps % 2 == 0, "page-buffer slot = step parity must carry across batch-row blocks"
    grid_spec = pltpu.PrefetchScalarGridSpec(
        num_scalar_prefetch=1,
        grid=(b // rb, nsteps),
        in_specs=[pl.BlockSpec((rb, nh, w), lambda bi, j, pt: (bi, 0, 0)),
                  pl.BlockSpec(memory_space=pl.ANY),
                  pl.BlockSpec(memory_space=pl.ANY),
                  pl.BlockSpec((rb, 1, r), lambda bi, j, pt: (bi, 0, 0)),
                  pl.BlockSpec((rb, 1, QK_ROPE), lambda bi, j, pt: (bi, 0, 0))],
        out_specs=pl.BlockSpec((rb, nh, r), lambda bi, j, pt: (bi, 0, 0)),
        scratch_shapes=[pltpu.VMEM((rb, LANES, w), BF16), pltpu.VMEM((rb, 2, keys, r), BF16),
                        pltpu.VMEM((rb, nh, keys), BF16), pltpu.VMEM((rb, nh, LANES), F32),
                        pltpu.VMEM((rb, nh, LANES), F32), pltpu.VMEM((rb, nh, r), F32),
                        pltpu.VMEM((2, rb * npg, PAGE_SIZE, r), F32),
                        pltpu.VMEM((2, rb * npg, QK_ROPE, PAGE_SIZE), F32),
                        pltpu.SemaphoreType.DMA((2,)), pltpu.SemaphoreType.DMA((2,))],
    )
    return pl.pallas_call(
        functools.partial(_paged_kernel, npg=npg, nsteps=nsteps, nrows=b, rb=rb),
        grid_spec=grid_spec,
        out_shape=jax.ShapeDtypeStruct((b, nh, r), BF16),
        compiler_params=_params("arbitrary", "arbitrary"),
        name="paged_attn",
    )(page_table, qcat, cache_ckv, cache_kpe_t, ckv_new.reshape(b, 1, r), kpe_new.reshape(b, 1, QK_ROPE))


def _ovup_kernel(o_ref, wv_ref, a_ref):
    a_ref[...] = jnp.dot(o_ref[...], wv_ref[...].astype(BF16), preferred_element_type=F32).astype(BF16)


def _ovup(olat, w_uv2):
    nh, b, r = olat.shape
    return pl.pallas_call(
        _ovup_kernel,
        grid=(nh,),
        in_specs=[pl.BlockSpec((None, b, r), lambda h: (h, 0, 0)), pl.BlockSpec((r, V_HEAD), lambda h: (0, h))],
        out_specs=pl.BlockSpec((b, V_HEAD), lambda h: (0, h)),
        out_shape=jax.ShapeDtypeStruct((b, nh * V_HEAD), BF16),
        compiler_params=_params("arbitrary"),
        name="o_latent_up",
    )(olat, w_uv2)


def _rope_tables(pos):
    inv = ROPE_THETA ** (-jnp.arange(0, QK_ROPE, 2, dtype=F32) / QK_ROPE)
    ang = pos.astype(F32)[:, None] * inv[None, :]
    cos, sin = jnp.cos(ang), jnp.sin(ang)
    return jnp.concatenate([cos, cos], axis=1), jnp.concatenate([-sin, sin], axis=1)


def _trunk(x, mods, mods_kv, rows_per_b, tm, cos2, sin2, P, s5_glu_fn, attn_fn, routed):
    tms = min(tm, 512)

    def moe_block(x, layer, g_final=None):
        h2, gates_t, e_rows, w_rows = _route(x, P["g_norm"][layer, 1], mods[layer], 4, 3, rows_per_b, tms,
                                             P["w_router_t"], P["b_router"], split=routed)
        if routed:
            return _moe_routed(h2, e_rows, w_rows, P["w_gate"], P["w_up"], P["w_down"], layer, x,
                               mods[layer], 5, rows_per_b, g_final)
        x = _moe(h2, gates_t.T, P["w_gate"], P["w_up"], P["w_down"], layer, x, mods[layer], 5,
                 rows_per_b, tms)
        return x if g_final is None else _final_norm(x, g_final, tms)

    x, st_re, st_im = s5_glu_fn(x)
    x = moe_block(x, 0)
    ckv, kpe = _kv(x, P["g_kv"], mods_kv, rows_per_b, tms, P["w_dkv_t"], P["g_ckv"], cos2, sin2)
    q = _qproj(x, P["g_norm"][1, 0], mods[1], 1, 0, rows_per_b, tm, P["w_q2"], cos2, sin2)
    attn = attn_fn(q, ckv, kpe)
    x = _proj_res(attn, P["w_o"][0], x, mods[1], 2, rows_per_b, tm, 512, glu=False)
    y = moe_block(x, 1, P["g_final"])
    return y, st_re, st_im, ckv, kpe


def kernel(x_prompt, x_sample, c_prompt, c_sample, state_ssm_re, state_ssm_im, cache_ckv, cache_kpe, page_table, g_norm, w_ada, b_ada, ssm_lam_re, ssm_lam_im, ssm_log_dt, ssm_b_re, ssm_b_im, ssm_c_re, ssm_c_im, ssm_d, w_glu, g_kv, w_ada_kv, b_ada_kv, w_dkv, g_ckv, w_uk, w_uv, w_q, w_o, w_router, b_router, w_gate, w_up, w_down, g_final):
    bsz, seq, d = x_prompt.shape
    db = x_sample.shape[0]
    g = d // SSM_GROUP
    n = SSM_STATE
    assert g == LANES, "the group-major relayout puts one S5 group per lane"
    past = page_table.shape[1] * PAGE_SIZE
    P = dict(g_norm=g_norm, g_kv=g_kv, g_ckv=g_ckv, w_o=w_o, b_router=b_router,
             w_gate=w_gate, w_up=w_up, w_down=w_down, g_final=g_final)
    P["w_q2"] = w_q[0]
    P["w_dkv_t"] = w_dkv.T
    P["w_router_t"] = w_router.T
    w_uk2 = w_uk.reshape(KV_LORA, N_HEADS * QK_NOPE)
    w_uv2 = w_uv.reshape(KV_LORA, N_HEADS * V_HEAD)

    m_rows = db + bsz
    m_pad = -(-m_rows // 16) * 16
    c_all = jnp.concatenate([c_sample, c_prompt, jnp.zeros((m_pad - m_rows, d), F32)], axis=0)
    mod = _ada(c_all, w_ada, b_ada)
    mod_kv = _ada(c_all, w_ada_kv[None], b_ada_kv[None])
    mods_s = [mod[l, :db][None] for l in range(mod.shape[0])]
    mods_p = [mod[l, db:m_rows][:, None, :] for l in range(mod.shape[0])]
    modkv_s = mod_kv[0, :db][None]
    modkv_p = mod_kv[0, db:m_rows][:, None, :]

    lam_re = ssm_lam_re[0].reshape(g, 1, n)
    lam_im = ssm_lam_im[0].reshape(g, 1, n)
    log_dt = jnp.broadcast_to(ssm_log_dt[0].reshape(g, 1, 1), (g, 1, n))
    bt_re = ssm_b_re[0].transpose(0, 2, 1)
    bt_im = ssm_b_im[0].transpose(0, 2, 1)
    c_re, c_im = ssm_c_re[0], ssm_c_im[0]
    d_g = ssm_d[0].reshape(g, 1, SSM_GROUP)
    d_rep = jnp.tile(d_g, (1, 1, S5_CHUNK))
    dup = lambda a: jnp.concatenate([a, a], axis=-1)
    cat = lambda a, b: jnp.concatenate([a, b], axis=-1)
    nchunk = seq // S5_CHUNK
    tm_p = 1024

    src = jnp.arange(d, dtype=I32)
    dst = (src % SSM_GROUP) * g + src // SSM_GROUP
    perm = (dst[:, None] == jnp.arange(d, dtype=I32)[None, :]).astype(BF16)
    w_glu_p = w_glu[0].reshape(g, SSM_GROUP, -1).transpose(1, 0, 2).reshape(d, -1)

    def s5_glu_prompt(x):
        hx = _s5_in(x, g_norm[0, 0], mods_p[0], 1, 0, seq, 256, perm)
        hx = hx.reshape(g, bsz * nchunk, S5_CHUNK * SSM_GROUP)
        zg, st = _s5_prompt(hx, dup(lam_re), dup(lam_im), dup(log_dt), cat(bt_re, bt_im), cat(bt_im, bt_re),
                            cat(c_re, c_im), cat(c_im, c_re), d_rep, bsz)
        zg = zg.reshape(g, bsz * seq * SSM_GROUP)
        x = _proj_res(zg, w_glu_p, x, mods_p[0], 2, seq, tm_p, 256, glu=True, grouped=True)
        st = st.transpose(1, 0, 2)[None]
        return x, st[..., :n], st[..., n:]

    def s5_glu_sample(x):
        h = _normmod_call(x, g_norm[0, 0], mods_s[0], 1, 0, db, db, F32)
        u = h.reshape(db, g, SSM_GROUP).transpose(1, 0, 2)
        h0_re = state_ssm_re[0].transpose(1, 0, 2)
        h0_im = state_ssm_im[0].transpose(1, 0, 2)
        zg, nr, ni = _s5_sample(u, h0_re, h0_im, lam_re, lam_im, log_dt, bt_re, bt_im, c_re, c_im, d_g)
        z = zg.transpose(1, 0, 2).reshape(db, d)
        x = _proj_res(z, w_glu[0], x, mods_s[0], 2, db, db, 256, glu=True)
        return x, nr.transpose(1, 0, 2)[None], ni.transpose(1, 0, 2)[None]

    def attn_prompt(q, ckv, kpe):
        k, v = _kvup(ckv, kpe, w_uk2, w_uv2, bsz, seq, tm_p)
        return _flash(q, k, v)

    def attn_sample(q, ckv, kpe):
        qcat = _qlat(q[0], w_uk2).transpose(1, 0, 2)
        olat = _paged_attn(qcat, cache_ckv, cache_kpe.transpose(0, 2, 1), page_table, ckv, kpe)
        return _ovup(olat.transpose(1, 0, 2), w_uv2)

    cos_p, sin_p = _rope_tables(jnp.arange(seq))
    cos_s, sin_s = _rope_tables(jnp.full((db,), past))

    y_p, sre_p, sim_p, ckv_p, kpe_p = _trunk(
        x_prompt.reshape(bsz * seq, d), mods_p, modkv_p, seq, tm_p, cos_p, sin_p, P,
        s5_glu_prompt, attn_prompt, routed=True)
    y_s, sre_s, sim_s, ckv_s, kpe_s = _trunk(
        x_sample.reshape(db, d), mods_s, modkv_s, db, db, cos_s, sin_s, P,
        s5_glu_sample, attn_sample, routed=False)
    return (y_p.reshape(bsz, seq, d), y_s.reshape(db, 1, d), sre_p, sim_p,
            ckv_p.reshape(bsz, seq, KV_LORA), kpe_p.reshape(bsz, seq, QK_ROPE),
            sre_s, sim_s, ckv_s.reshape(db, 1, KV_LORA), kpe_s.reshape(db, 1, QK_ROPE))
```

```python
import functools
import math

import jax
import jax.numpy as jnp
from jax import lax
from jax.experimental import pallas as pl
from jax.experimental.pallas import tpu as pltpu

F32 = jnp.float32
BF16 = jnp.bfloat16
I32 = jnp.int32

SSM_GROUP = 16
SSM_STATE = 64
N_HEADS = 16
QK_NOPE = 128
QK_ROPE = 64
V_HEAD = 128
KV_LORA = 512
ROPE_THETA = 10000.0
PAGE_SIZE = 128
N_EXPERTS = 16
N_EXPERT_GROUPS = 4
EXPERTS_PER_GROUP = N_EXPERTS // N_EXPERT_GROUPS
TOP_K = 2
EPS = 1e-6
ATTN_SCALE = (QK_NOPE + QK_ROPE) ** -0.5
QK_HEAD = QK_NOPE + QK_ROPE
LOG2_E = math.log2(math.e)

V7X_VMEM_BYTES = 64 * 1024 * 1024
VMEM_LIMIT_BYTES = V7X_VMEM_BYTES - 8 * 1024 * 1024
LANES = 128
SMEM_I32_BLOCK = 1024
DMA_GROUP = 8
PAGE_SLOTS = 3
S5_CHUNK = 16
DN_LAST = (((1,), (1,)), ((), ()))


def _params(*sem):
    return pltpu.CompilerParams(dimension_semantics=sem, vmem_limit_bytes=VMEM_LIMIT_BYTES)


def _normmod(x, g, sc, sh):
    ms = jnp.mean(x * x, axis=-1, keepdims=True)
    return (x * lax.rsqrt(ms + EPS)) * g * (1.0 + sc) + sh


def _rope(p, cos2, sin2):
    half = QK_ROPE // 2
    swapped = jnp.concatenate([p[:, half:], p[:, :half]], axis=1)
    return p * cos2 + swapped * sin2


def _mod_spec(mods, chunk, tm, rows_per_b, width, j_blocks=None):
    _, r, _ = mods.shape
    if j_blocks is None:
        return pl.BlockSpec((None, r, width), lambda i, j: ((i * tm) // rows_per_b, 0, chunk))
    return pl.BlockSpec((None, r, width), lambda i, j: ((i * tm) // rows_per_b, 0, chunk * j_blocks + j))


def _ada_kernel(c_ref, w_ref, b_ref, o_ref, cb_ref):
    @pl.when((pl.program_id(0) == 0) & (pl.program_id(1) == 0))
    def _():
        c = c_ref[...]
        cb_ref[...] = (c * jax.nn.sigmoid(c)).astype(BF16)

    acc = jnp.dot(cb_ref[...], w_ref[...].astype(BF16), preferred_element_type=F32)
    o_ref[...] = acc + b_ref[...]


def _ada(c_all, w, b, tn=1024):
    m, d = c_all.shape
    lw, _, n = w.shape
    return pl.pallas_call(
        _ada_kernel,
        grid=(lw, n // tn),
        in_specs=[
            pl.BlockSpec((m, d), lambda l, j: (0, 0)),
            pl.BlockSpec((None, d, tn), lambda l, j: (l, 0, j)),
            pl.BlockSpec((None, 1, tn), lambda l, j: (l, 0, j)),
        ],
        out_specs=pl.BlockSpec((None, m, tn), lambda l, j: (l, 0, j)),
        out_shape=jax.ShapeDtypeStruct((lw, m, n), F32),
        scratch_shapes=[pltpu.VMEM((m, d), BF16)],
        compiler_params=_params("arbitrary", "arbitrary"),
        name="ada_mod",
    )(c_all, w, b.reshape(lw, 1, n))


def _normmod_kernel(x_ref, g_ref, sc_ref, sh_ref, o_ref):
    o_ref[...] = _normmod(x_ref[...], g_ref[...], sc_ref[...], sh_ref[...]).astype(o_ref.dtype)


def _normmod_call(x, g, mods, sc_chunk, sh_chunk, rows_per_b, tm, out_dtype):
    t, d = x.shape
    return pl.pallas_call(
        _normmod_kernel,
        grid=(t // tm, 1),
        in_specs=[
            pl.BlockSpec((tm, d), lambda i, j: (i, 0)),
            pl.BlockSpec((1, d), lambda i, j: (0, 0)),
            _mod_spec(mods, sc_chunk, tm, rows_per_b, d),
            _mod_spec(mods, sh_chunk, tm, rows_per_b, d),
        ],
        out_specs=pl.BlockSpec((tm, d), lambda i, j: (i, 0)),
        out_shape=jax.ShapeDtypeStruct((t, d), out_dtype),
        compiler_params=_params("arbitrary", "arbitrary"),
        name="norm_mod",
    )(x, g.reshape(1, d), mods, mods)


def _final_norm_kernel(x_ref, g_ref, o_ref):
    x = x_ref[...]
    ms = jnp.mean(x * x, axis=-1, keepdims=True)
    o_ref[...] = (x * lax.rsqrt(ms + EPS)) * g_ref[...]


def _final_norm(x, g, tm):
    t, d = x.shape
    return pl.pallas_call(
        _final_norm_kernel,
        grid=(t // tm,),
        in_specs=[pl.BlockSpec((tm, d), lambda i: (i, 0)), pl.BlockSpec((1, d), lambda i: (0, 0))],
        out_specs=pl.BlockSpec((tm, d), lambda i: (i, 0)),
        out_shape=jax.ShapeDtypeStruct((t, d), F32),
        compiler_params=_params("arbitrary"),
        name="final_norm",
    )(x, g.reshape(1, d))


def _proj_res_kernel(x_ref, *refs, glu, grouped, tm):
    refs = list(refs)
    xb_ref = refs.pop() if grouped else None
    if glu:
        wa_ref, wb_ref, res_ref, gt_ref, o_ref = refs
    else:
        wa_ref, res_ref, gt_ref, o_ref = refs
    if grouped:
        @pl.when(pl.program_id(1) == 0)
        def _():
            cg = SSM_GROUP
            for k in range(tm // LANES):
                zt = x_ref[:, k * LANES * cg:(k + 1) * LANES * cg].T
                xb_ref[k * LANES:(k + 1) * LANES, :] = zt.reshape(LANES, cg, LANES).reshape(LANES, cg * LANES)

        x = xb_ref[...]
    else:
        x = x_ref[...]
    y = jnp.dot(x, wa_ref[...].astype(BF16), preferred_element_type=F32)
    if glu:
        yb = jnp.dot(x, wb_ref[...].astype(BF16), preferred_element_type=F32)
        y = y * jax.nn.sigmoid(yb)
    o_ref[...] = res_ref[...] + gt_ref[...] * y


def _proj_res(x, w, res, mods, gt_chunk, rows_per_b, tm, tn, glu, grouped=False):
    t, n = res.shape
    k = w.shape[0]
    nj = n // tn
    if grouped:
        x_spec = pl.BlockSpec((x.shape[0], tm * SSM_GROUP), lambda i, j: (0, i))
    else:
        x_spec = pl.BlockSpec((tm, k), lambda i, j: (i, 0))
    in_specs = [x_spec, pl.BlockSpec((k, tn), lambda i, j: (0, j))]
    args = [x, w]
    if glu:
        in_specs.append(pl.BlockSpec((k, tn), lambda i, j: (0, nj + j)))
        args.append(w)
    in_specs += [pl.BlockSpec((tm, tn), lambda i, j: (i, j)), _mod_spec(mods, gt_chunk, tm, rows_per_b, tn, nj)]
    args += [res, mods]
    return pl.pallas_call(
        functools.partial(_proj_res_kernel, glu=glu, grouped=grouped, tm=tm),
        grid=(t // tm, nj),
        in_specs=in_specs,
        out_specs=pl.BlockSpec((tm, tn), lambda i, j: (i, j)),
        out_shape=jax.ShapeDtypeStruct((t, n), F32),
        scratch_shapes=[pltpu.VMEM((tm, k), BF16)] if grouped else [],
        compiler_params=_params("arbitrary", "arbitrary"),
        name="glu_proj" if glu else "out_proj",
    )(*args)


def _s5in_kernel(x_ref, g_ref, sc_ref, sh_ref, p_ref, o_ref):
    h = _normmod(x_ref[...], g_ref[...], sc_ref[...], sh_ref[...]).astype(BF16)
    hp = jnp.dot(h, p_ref[...], preferred_element_type=F32).astype(BF16)
    tm = hp.shape[0]
    o_ref[...] = hp.reshape(tm, SSM_GROUP, LANES).reshape(tm * SSM_GROUP, LANES).T


def _s5_in(x, g, mods, sc_chunk, sh_chunk, rows_per_b, tm, perm):
    t, d = x.shape
    ng = d // SSM_GROUP
    return pl.pallas_call(
        _s5in_kernel,
        grid=(t // tm, 1),
        in_specs=[
            pl.BlockSpec((tm, d), lambda i, j: (i, 0)),
            pl.BlockSpec((1, d), lambda i, j: (0, 0)),
            _mod_spec(mods, sc_chunk, tm, rows_per_b, d),
            _mod_spec(mods, sh_chunk, tm, rows_per_b, d),
            pl.BlockSpec((d, d), lambda i, j: (0, 0)),
        ],
        out_specs=pl.BlockSpec((ng, tm * SSM_GROUP), lambda i, j: (0, i)),
        out_shape=jax.ShapeDtypeStruct((ng, t * SSM_GROUP), BF16),
        compiler_params=_params("arbitrary", "arbitrary"),
        name="s5_in",
    )(x, g.reshape(1, d), mods, mods, perm)


def _cpow(lr_dt, li_dt, k):
    mag = jnp.exp(k * lr_dt)
    ang = k * li_dt
    return mag * jnp.cos(ang), mag * jnp.sin(ang)


def _rep_rows(e, n):
    q, w = e.shape
    return jnp.concatenate([jnp.broadcast_to(e[t:t + 1], (n, w)) for t in range(q)], axis=0)


def _tile_rows(c, n):
    return jnp.concatenate([c] * n, axis=0)


def _zoh(lr, li, ldt):
    dt = jnp.exp(ldt)
    lr_dt = lr * dt
    li_dt = li * dt
    mag = jnp.exp(lr_dt)
    ab_re = mag * jnp.cos(li_dt)
    ab_im = mag * jnp.sin(li_dt)
    den = lr * lr + li * li
    f_re = ((ab_re - 1.0) * lr + ab_im * li) / den
    f_im = (ab_im * lr - (ab_re - 1.0) * li) / den
    return lr_dt, li_dt, ab_re, ab_im, f_re, f_im


def _gelu_tanh(y):
    return 0.5 * y * (1.0 + jnp.tanh(math.sqrt(2.0 / math.pi) * (y + 0.044715 * (y * y * y))))


def _s5_prompt_kernel(x_ref, lr_ref, li_ref, ldt_ref, bt_ref, bts_ref, c_ref, cs_ref, d_ref,
                      z_ref, st_ref, s_sc, ss_sc, hp_sc, yl_sc, *, gb, nb, nchunk):
    q, cg, n = S5_CHUNK, SSM_GROUP, SSM_STATE
    half = q // 2
    ti = lax.broadcasted_iota(I32, (q, 1), 0).astype(F32)
    rows = lax.broadcasted_iota(I32, (q * cg, q * cg), 0)
    cols = lax.broadcasted_iota(I32, (q * cg, q * cg), 1)
    shift = cg.bit_length() - 1
    causal = (rows >> shift) <= (cols >> shift)
    diag = rows == cols
    lane = lax.broadcasted_iota(I32, (1, 2 * n), 1)
    sgn = jnp.where(lane < n, -1.0, 1.0)

    def cmul(x2, xs, er, ei):
        return x2 * er + (xs * sgn) * ei

    a_terms = []
    vcs = []
    for g in range(gb):
        lr_dt, li_dt, _, _, f_re, f_im = _zoh(lr_ref[g], li_ref[g], ldt_ref[g])
        bt2, bts = bt_ref[g], bts_ref[g]
        bb2 = cmul(bt2, bts, f_re, f_im)
        bbs = bts * f_re - (bt2 * sgn) * f_im
        bb2t, bbst = _tile_rows(bb2, q), _tile_rows(bbs, q)
        c2t, cst = _tile_rows(c_ref[g], q), _tile_rows(cs_ref[g], q)

        def table(k, lr_dt=lr_dt, li_dt=li_dt):
            er, ei = _cpow(lr_dt, li_dt, k)
            return _rep_rows(er, cg), _rep_rows(ei, cg)

        pc = cmul(c2t, cst, *table(ti - half)) * (-sgn)
        pb = cmul(bb2t, bbst, *table(half - ti))
        tt = lax.dot_general(pb.astype(BF16), pc.astype(BF16), DN_LAST, preferred_element_type=F32)
        tt = jnp.where(causal, tt, 0.0) + jnp.where(diag, d_ref[g], 0.0)
        w1 = cmul(bb2t, bbst, *table((q - 1) - ti))
        vcs.append((cmul(c2t, cst, *table(ti + 1.0)) * (-sgn)).astype(BF16))
        ar, ai = _cpow(lr_dt, li_dt, float(q))
        a_terms.append((ar, ai * sgn))
        x = x_ref[g]
        yl_sc[g] = jnp.dot(x, tt.astype(BF16), preferred_element_type=F32)
        s = jnp.dot(x, w1.astype(BF16), preferred_element_type=F32)
        s_sc[g] = s
        ss_sc[g] = pltpu.roll(s, n, axis=1)

    def carry(ch, hs):
        new = []
        for g in range(gb):
            h, hsw = hs[2 * g], hs[2 * g + 1]
            hp_sc[g, pl.ds(ch, nb, stride=nchunk), :] = h
            a1, a2 = a_terms[g]
            new.append(a1 * h + a2 * hsw + s_sc[g, pl.ds(ch, nb, stride=nchunk), :])
            new.append(a1 * hsw - a2 * h + ss_sc[g, pl.ds(ch, nb, stride=nchunk), :])
        return tuple(new)

    zero = jnp.zeros((nb, 2 * n), F32)
    hs = lax.fori_loop(0, nchunk, carry, (zero,) * (2 * gb), unroll=8)
    for g in range(gb):
        st_ref[g] = hs[2 * g]
        y = yl_sc[g] + lax.dot_general(hp_sc[g].astype(BF16), vcs[g], DN_LAST, preferred_element_type=F32)
        z_ref[g] = _gelu_tanh(y).astype(BF16)


def _s5_prompt(hx, lam_re2, lam_im2, log_dt2, bt2, bts, c2, cs, d_rep, nb, gb=2):
    g, r, w = hx.shape
    n2 = 2 * SSM_STATE
    nchunk = r // nb
    vec = lambda: pl.BlockSpec((gb, 1, n2), lambda i: (i, 0, 0))
    mat = lambda: pl.BlockSpec((gb, SSM_GROUP, n2), lambda i: (i, 0, 0))
    return pl.pallas_call(
        functools.partial(_s5_prompt_kernel, gb=gb, nb=nb, nchunk=nchunk),
        grid=(g // gb,),
        in_specs=[pl.BlockSpec((gb, r, w), lambda i: (i, 0, 0)), vec(), vec(), vec(),
                  mat(), mat(), mat(), mat(), pl.BlockSpec((gb, 1, w), lambda i: (i, 0, 0))],
        out_specs=[pl.BlockSpec((gb, r, w), lambda i: (i, 0, 0)),
                   pl.BlockSpec((gb, nb, n2), lambda i: (i, 0, 0))],
        out_shape=[jax.ShapeDtypeStruct((g, r, w), BF16),
                   jax.ShapeDtypeStruct((g, nb, n2), F32)],
        scratch_shapes=[pltpu.VMEM((gb, r, n2), F32), pltpu.VMEM((gb, r, n2), F32),
                        pltpu.VMEM((gb, r, n2), F32), pltpu.VMEM((gb, r, w), F32)],
        compiler_params=_params("arbitrary"),
        name="s5_prompt",
    )(hx, lam_re2, lam_im2, log_dt2, bt2, bts, c2, cs, d_rep)


def _s5_sample_kernel(u_ref, h0r_ref, h0i_ref, lr_ref, li_ref, ldt_ref, btr_ref, bti_ref,
                      cr_ref, ci_ref, d_ref, z_ref, nr_ref, ni_ref, *, gb):
    hp = lax.Precision.HIGHEST
    for g in range(gb):
        _, _, ab_re, ab_im, f_re, f_im = _zoh(lr_ref[g], li_ref[g], ldt_ref[g])
        bb_re = f_re * btr_ref[g] - f_im * bti_ref[g]
        bb_im = f_re * bti_ref[g] + f_im * btr_ref[g]
        u = u_ref[g]
        bu_re = jnp.dot(u, bb_re, preferred_element_type=F32, precision=hp)
        bu_im = jnp.dot(u, bb_im, preferred_element_type=F32, precision=hp)
        h0r = h0r_ref[g]
        h0i = h0i_ref[g]
        hr = bu_re + ab_re * h0r - ab_im * h0i
        hi = bu_im + ab_re * h0i + ab_im * h0r
        nr_ref[g] = hr
        ni_ref[g] = hi
        y = (lax.dot_general(hr, cr_ref[g], DN_LAST, preferred_element_type=F32, precision=hp)
             - lax.dot_general(hi, ci_ref[g], DN_LAST, preferred_element_type=F32, precision=hp))
        y = y + d_ref[g] * u
        z_ref[g] = _gelu_tanh(y).astype(BF16)


def _s5_sample(u, h0_re, h0_im, lam_re, lam_im, log_dt, bt_re, bt_im, c_re, c_im, d_g, gb=8):
    g, b, cg = u.shape
    n = SSM_STATE
    vec = lambda: pl.BlockSpec((gb, 1, n), lambda i: (i, 0, 0))
    mat = lambda: pl.BlockSpec((gb, cg, n), lambda i: (i, 0, 0))
    st = lambda: pl.BlockSpec((gb, b, n), lambda i: (i, 0, 0))
    return pl.pallas_call(
        functools.partial(_s5_sample_kernel, gb=gb),
        grid=(g // gb,),
        in_specs=[pl.BlockSpec((gb, b, cg), lambda i: (i, 0, 0)), st(), st(), vec(), vec(), vec(),
                  mat(), mat(), mat(), mat(), pl.BlockSpec((gb, 1, cg), lambda i: (i, 0, 0))],
        out_specs=[pl.BlockSpec((gb, b, cg), lambda i: (i, 0, 0)), st(), st()],
        out_shape=[jax.ShapeDtypeStruct((g, b, cg), BF16),
                   jax.ShapeDtypeStruct((g, b, n), F32),
                   jax.ShapeDtypeStruct((g, b, n), F32)],
        compiler_params=_params("arbitrary"),
        name="s5_sample",
    )(u, h0_re, h0_im, lam_re, lam_im, log_dt, bt_re, bt_im, c_re, c_im, d_g)


def _route_kernel(x_ref, g_ref, sc_ref, sh_ref, wr_ref, br_ref, h_ref, gates_ref, e_ref, w_ref, *, split):
    h = _normmod(x_ref[...], g_ref[...], sc_ref[...], sh_ref[...])
    hb = h.astype(BF16)
    h_ref[...] = hb.reshape(h_ref.shape) if split else hb
    logits = lax.dot_general(wr_ref[...], h, DN_LAST, preferred_element_type=F32,
                             precision=lax.Precision.HIGHEST)
    s = jax.nn.sigmoid(logits)
    sel = s + br_ref[...]
    epg = EXPERTS_PER_GROUP
    srow = [s[e:e + 1, :] for e in range(N_EXPERTS)]
    row = [sel[e:e + 1, :] for e in range(N_EXPERTS)]

    def top2_sum(a, b, c, d):
        hi1, lo1 = jnp.maximum(a, b), jnp.minimum(a, b)
        hi2, lo2 = jnp.maximum(c, d), jnp.minimum(c, d)
        return jnp.maximum(hi1, hi2) + jnp.maximum(jnp.minimum(hi1, hi2), jnp.maximum(lo1, lo2))

    gs = [top2_sum(*row[epg * g:epg * (g + 1)]) for g in range(N_EXPERT_GROUPS)]
    best = gs[0]
    gidx = jnp.zeros_like(best, dtype=I32)
    for g in range(1, N_EXPERT_GROUPS):
        better = gs[g] > best
        gidx = jnp.where(better, g, gidx)
        best = jnp.where(better, gs[g], best)

    def pick(rows_, j):
        v = rows_[j]
        for g in range(1, N_EXPERT_GROUPS):
            v = jnp.where(gidx == g, rows_[epg * g + j], v)
        return v

    v = [pick(row, j) for j in range(epg)]
    u = [pick(srow, j) for j in range(epg)]

    def argmax_first(vals):
        bv, bi = vals[0], jnp.zeros_like(gidx)
        for j in range(1, epg):
            better = vals[j] > bv
            bi = jnp.where(better, j, bi)
            bv = jnp.where(better, vals[j], bv)
        return bi

    i1 = argmax_first(v)
    i2 = argmax_first([jnp.where(i1 == j, -jnp.inf, v[j]) for j in range(epg)])

    def take(vals, idx):
        out = vals[0]
        for j in range(1, epg):
            out = jnp.where(idx == j, vals[j], out)
        return out

    w1, w2 = take(u, i1), take(u, i2)
    tot = w1 + w2
    w1, w2 = w1 / tot, w2 / tot
    e1 = gidx * epg + i1
    e2 = gidx * epg + i2
    gates_ref[...] = jnp.concatenate(
        [jnp.where(e1 == e, w1, 0.0) + jnp.where(e2 == e, w2, 0.0) for e in range(N_EXPERTS)], axis=0)
    e_ref[...] = jnp.concatenate([e1, e2], axis=0)
    w_ref[...] = jnp.concatenate([w1, w2], axis=0)


def _route(x, g, mods, sc_chunk, sh_chunk, rows_per_b, tm, w_router_t, b_router, split):
    t, d = x.shape
    e = N_EXPERTS
    if split:
        h_spec = pl.BlockSpec((tm, d // LANES, LANES), lambda i, j: (i, 0, 0))
        h_shape = jax.ShapeDtypeStruct((t, d // LANES, LANES), BF16)
    else:
        h_spec = pl.BlockSpec((tm, d), lambda i, j: (i, 0))
        h_shape = jax.ShapeDtypeStruct((t, d), BF16)
    return pl.pallas_call(
        functools.partial(_route_kernel, split=split),
        grid=(t // tm, 1),
        in_specs=[
            pl.BlockSpec((tm, d), lambda i, j: (i, 0)),
            pl.BlockSpec((1, d), lambda i, j: (0, 0)),
            _mod_spec(mods, sc_chunk, tm, rows_per_b, d),
            _mod_spec(mods, sh_chunk, tm, rows_per_b, d),
            pl.BlockSpec((e, d), lambda i, j: (0, 0)),
            pl.BlockSpec((e, 1), lambda i, j: (0, 0)),
        ],
        out_specs=[h_spec, pl.BlockSpec((e, tm), lambda i, j: (0, i)),
                   pl.BlockSpec((TOP_K, tm), lambda i, j: (0, i)), pl.BlockSpec((TOP_K, tm), lambda i, j: (0, i))],
        out_shape=[h_shape, jax.ShapeDtypeStruct((e, t), F32),
                   jax.ShapeDtypeStruct((TOP_K, t), I32), jax.ShapeDtypeStruct((TOP_K, t), F32)],
        compiler_params=_params("arbitrary", "arbitrary"),
        name="moe_route",
    )(x, g.reshape(1, d), mods, mods, w_router_t, b_router.reshape(e, 1))


def _moe_kernel(h_ref, gates_ref, wg_ref, wu_ref, wd_ref, res_ref, gt_ref, o_ref, acc_ref):
    e = pl.program_id(1)
    first = (e == 0) & (pl.program_id(2) == 0)
    last = (e == pl.num_programs(1) - 1) & (pl.program_id(2) == pl.num_programs(2) - 1)

    @pl.when(first)
    def _():
        acc_ref[...] = jnp.zeros_like(acc_ref)

    h = h_ref[...]
    a = jnp.dot(h, wg_ref[...].astype(BF16), preferred_element_type=F32)
    b = jnp.dot(h, wu_ref[...].astype(BF16), preferred_element_type=F32)
    gates = gates_ref[...]
    lane = lax.broadcasted_iota(I32, gates.shape, 1)
    gcol = jnp.sum(jnp.where(lane == e, gates, 0.0), axis=1, keepdims=True)
    hid = (a * jax.nn.sigmoid(a)) * b * gcol
    acc_ref[...] += jnp.dot(hid.astype(BF16), wd_ref[...].astype(BF16), preferred_element_type=F32)

    @pl.when(last)
    def _():
        o_ref[...] = res_ref[...] + gt_ref[...] * acc_ref[...]


def _moe(h, gates, w_gate, w_up, w_down, layer, res, mods, gt_chunk, rows_per_b, tm, fsplit=2):
    t, d = h.shape
    e, _, f = w_gate.shape[1:]
    tf = f // fsplit
    r = mods.shape[1]
    return pl.pallas_call(
        _moe_kernel,
        grid=(t // tm, e, fsplit),
        in_specs=[
            pl.BlockSpec((tm, d), lambda i, j, k: (i, 0)),
            pl.BlockSpec((tm, e), lambda i, j, k: (i, 0)),
            pl.BlockSpec((None, None, d, tf), lambda i, j, k: (layer, j, 0, k)),
            pl.BlockSpec((None, None, d, tf), lambda i, j, k: (layer, j, 0, k)),
            pl.BlockSpec((None, None, tf, d), lambda i, j, k: (layer, j, k, 0)),
            pl.BlockSpec((tm, d), lambda i, j, k: (i, 0)),
            pl.BlockSpec((None, r, d), lambda i, j, k: ((i * tm) // rows_per_b, 0, gt_chunk)),
        ],
        out_specs=pl.BlockSpec((tm, d), lambda i, j, k: (i, 0)),
        out_shape=jax.ShapeDtypeStruct((t, d), F32),
        scratch_shapes=[pltpu.VMEM((tm, d), F32)],
        compiler_params=_params("arbitrary", "arbitrary", "arbitrary"),
        name="moe_experts",
    )(h, gates, w_gate, w_up, w_down, res, mods)


def _experts_kernel(te_ref, nu_ref, idxc_ref, idxn_ref, h_ref, wg_ref, wu_ref, wd_ref, o_ref,
                    wgb, wub, wdb, xbuf, sem, *, tg):
    i = pl.program_id(0)
    nu = nu_ref[0]
    slot = i % 2
    per_blk = SMEM_I32_BLOCK // tg

    n_parts = 4
    part = tg // n_parts

    def start_gather(idx_ref, tile, dst_slot, lo, hi):
        off = (tile % per_blk) * tg

        def issue(grp, carry):
            for u in range(DMA_GROUP):
                r = grp * DMA_GROUP + u
                pltpu.make_async_copy(h_ref.at[pl.ds(idx_ref[off + r], 1)], xbuf.at[dst_slot, pl.ds(r, 1)],
                                      sem.at[dst_slot]).start()
            return carry

        lax.fori_loop(lo // DMA_GROUP, hi // DMA_GROUP, issue, 0)

    def prefetch_part(k):
        @pl.when(i + 1 < nu)
        def _():
            start_gather(idxn_ref, i + 1, 1 - slot, k * part, (k + 1) * part)

    @pl.when(i == 0)
    def _():
        start_gather(idxc_ref, i, slot, 0, tg)

    prev = te_ref[jnp.maximum(i - 1, 0)]

    @pl.when((i == 0) | (te_ref[i] != prev))
    def _():
        wgb[...] = wg_ref[...].astype(BF16)
        wub[...] = wu_ref[...].astype(BF16)
        wdb[...] = wd_ref[...].astype(BF16)

    @pl.when(i < nu)
    def _():
        def drain(r, carry):
            pltpu.make_async_copy(h_ref.at[pl.ds(0, 1)], xbuf.at[slot, pl.ds(r, 1)], sem.at[slot]).wait()
            return carry

        lax.fori_loop(0, tg, drain, 0, unroll=8)
        x = xbuf[slot].reshape(tg, wgb.shape[0])
        prefetch_part(0)
        a = jnp.dot(x, wgb[...], preferred_element_type=F32)
        prefetch_part(1)
        b = jnp.dot(x, wub[...], preferred_element_type=F32)
        prefetch_part(2)
        hid = ((a * jax.nn.sigmoid(a)) * b).astype(BF16)
        prefetch_part(3)
        y = jnp.dot(hid, wdb[...], preferred_element_type=F32)
        o_ref[...] = y.reshape(o_ref.shape)

    @pl.when(i >= nu)
    def _():
        o_ref[...] = jnp.zeros_like(o_ref)


def _experts(tile_expert, n_used, sorted_tok, h3, w_gate, w_up, w_down, layer, tg):
    rp = sorted_tok.shape[0]
    _, _, d, f = w_gate.shape
    blk = SMEM_I32_BLOCK
    per_blk = blk // tg
    last_blk = rp // blk - 1
    row_shape = h3.shape[1:]
    grid_spec = pltpu.PrefetchScalarGridSpec(
        num_scalar_prefetch=2,
        grid=(rp // tg,),
        in_specs=[
            pl.BlockSpec((blk,), lambda i, te, nu: (i // per_blk,), memory_space=pltpu.SMEM),
            pl.BlockSpec((blk,), lambda i, te, nu: (jnp.minimum((i + 1) // per_blk, last_blk),),
                         memory_space=pltpu.SMEM),
            pl.BlockSpec(memory_space=pl.ANY),
            pl.BlockSpec((None, None, d, f), lambda i, te, nu: (layer, te[i], 0, 0)),
            pl.BlockSpec((None, None, d, f), lambda i, te, nu: (layer, te[i], 0, 0)),
            pl.BlockSpec((None, None, f, d), lambda i, te, nu: (layer, te[i], 0, 0)),
        ],
        out_specs=pl.BlockSpec((tg,) + row_shape, lambda i, te, nu: (i, 0, 0)),
        scratch_shapes=[pltpu.VMEM((d, f), BF16), pltpu.VMEM((d, f), BF16), pltpu.VMEM((f, d), BF16),
                        pltpu.VMEM((2, tg) + row_shape, h3.dtype), pltpu.SemaphoreType.DMA((2,))],
    )
    return pl.pallas_call(
        functools.partial(_experts_kernel, tg=tg),
        grid_spec=grid_spec,
        out_shape=jax.ShapeDtypeStruct((rp,) + row_shape, F32),
        compiler_params=_params("arbitrary"),
        name="moe_routed_experts",
    )(tile_expert, n_used, sorted_tok, sorted_tok, h3, w_gate, w_up, w_down)


def _combine_kernel(pos_ref, ys_ref, w_ref, res_ref, gt_ref, *refs, tm, final):
    if final:
        gf_ref, o_ref, buf, sem = refs
    else:
        o_ref, buf, sem = refs

    def issue(grp, carry):
        for u in range(DMA_GROUP):
            r = grp * DMA_GROUP + u
            for k in range(TOP_K):
                pltpu.make_async_copy(ys_ref.at[pl.ds(pos_ref[k * tm + r], 1)], buf.at[k, pl.ds(r, 1)],
                                      sem).start(priority=(u + k) % 2)
        return carry

    lax.fori_loop(0, tm // DMA_GROUP, issue, 0)

    def drain(r, carry):
        for k in range(TOP_K):
            pltpu.make_async_copy(ys_ref.at[pl.ds(0, 1)], buf.at[k, pl.ds(r, 1)], sem).wait()
        return carry

    lax.fori_loop(0, tm, drain, 0, unroll=8)
    d = o_ref.shape[1]
    w = w_ref[...]
    m = (buf[0].reshape(tm, d) * jnp.tile(w[:, :LANES], (1, d // LANES))
         + buf[1].reshape(tm, d) * jnp.tile(w[:, LANES:], (1, d // LANES)))
    x = res_ref[...] + gt_ref[...] * m
    if final:
        ms = jnp.mean(x * x, axis=-1, keepdims=True)
        x = (x * lax.rsqrt(ms + EPS)) * gf_ref[...]
    o_ref[...] = x


def _combine(pos, ys, wtok, res, mods, gt_chunk, rows_per_b, tm, g_final=None):
    t, d = res.shape
    assert TOP_K * tm == SMEM_I32_BLOCK
    r = mods.shape[1]
    final = g_final is not None
    in_specs = [pl.BlockSpec((TOP_K * tm,), lambda i: (i,), memory_space=pltpu.SMEM),
                pl.BlockSpec(memory_space=pl.ANY),
                pl.BlockSpec((tm, TOP_K * LANES), lambda i: (i, 0)),
                pl.BlockSpec((tm, d), lambda i: (i, 0)),
                pl.BlockSpec((None, r, d), lambda i: ((i * tm) // rows_per_b, 0, gt_chunk))]
    args = [pos, ys, wtok, res, mods]
    if final:
        in_specs.append(pl.BlockSpec((1, d), lambda i: (0, 0)))
        args.append(g_final.reshape(1, d))
    return pl.pallas_call(
        functools.partial(_combine_kernel, tm=tm, final=final),
        grid=(t // tm,),
        in_specs=in_specs,
        out_specs=pl.BlockSpec((tm, d), lambda i: (i, 0)),
        out_shape=jax.ShapeDtypeStruct((t, d), F32),
        scratch_shapes=[pltpu.VMEM((TOP_K, tm) + ys.shape[1:], F32), pltpu.SemaphoreType.DMA(())],
        compiler_params=_params("arbitrary"),
        name="moe_combine",
    )(*args)


def _moe_routed(h3, e_rows, w_rows, w_gate, w_up, w_down, layer, res, mods, gt_chunk, rows_per_b,
                g_final=None, tg=256, tmc=512):
    t = h3.shape[0]
    ne = N_EXPERTS
    a = TOP_K * t
    rp = a + ne * tg
    rp = -(-rp // SMEM_I32_BLOCK) * SMEM_I32_BLOCK
    n_tiles = rp // tg
    ea = e_rows.reshape(a)
    ta = jnp.tile(jnp.arange(t, dtype=I32), TOP_K)
    onehot = (ea[:, None] == jnp.arange(ne, dtype=I32)[None, :]).astype(I32)
    csum = jnp.cumsum(onehot, axis=0)
    rank = jnp.sum((csum - onehot) * onehot, axis=1)
    counts = csum[-1]
    ptiles = (counts + tg - 1) // tg
    tile_end = jnp.cumsum(ptiles)
    tile_start = tile_end - ptiles
    n_used = tile_end[-1:].astype(I32)
    dest = jnp.sum(onehot * tile_start[None, :], axis=1) * tg + rank
    sorted_tok = jnp.zeros((rp,), I32).at[dest].set(ta, unique_indices=True)
    tile_expert = jnp.sum(jnp.arange(n_tiles, dtype=I32)[:, None] >= tile_end[None, :], axis=1)
    tile_expert = jnp.minimum(tile_expert, ne - 1).astype(I32)
    ys = _experts(tile_expert, n_used, sorted_tok, h3, w_gate, w_up, w_down, layer, tg)
    pos = dest.reshape(TOP_K, t // tmc, tmc).transpose(1, 0, 2).reshape(a)
    wtok = jnp.concatenate([jnp.broadcast_to(w_rows[k][:, None], (t, LANES)) for k in range(TOP_K)], axis=1)
    return _combine(pos, ys, wtok, res, mods, gt_chunk, rows_per_b, tmc, g_final)


def _kv_kernel(x_ref, g_ref, sc_ref, sh_ref, w_ref, gc_ref, cos_ref, sin_ref, ckv_ref, kpe_ref, wb_ref):
    @pl.when(pl.program_id(0) == 0)
    def _():
        wb_ref[...] = w_ref[...].astype(BF16)

    h = _normmod(x_ref[...], g_ref[...], sc_ref[...], sh_ref[...]).astype(BF16)
    lat = lax.dot_general(h, wb_ref[...], DN_LAST, preferred_element_type=F32)
    c = lat[:, :KV_LORA]
    ms = jnp.mean(c * c, axis=-1, keepdims=True)
    ckv_ref[...] = (c * lax.rsqrt(ms + EPS)) * gc_ref[...]
    kpe_ref[...] = _rope(lat[:, KV_LORA:], cos_ref[...], sin_ref[...])


def _kv(x, g_kv, mods, rows_per_b, tm, w_dkv_t, g_ckv, cos2, sin2):
    t, d = x.shape
    n = w_dkv_t.shape[0]
    nt = cos2.shape[0] // tm
    return pl.pallas_call(
        _kv_kernel,
        grid=(t // tm, 1),
        in_specs=[
            pl.BlockSpec((tm, d), lambda i, j: (i, 0)),
            pl.BlockSpec((1, d), lambda i, j: (0, 0)),
            _mod_spec(mods, 1, tm, rows_per_b, d),
            _mod_spec(mods, 0, tm, rows_per_b, d),
            pl.BlockSpec((n, d), lambda i, j: (0, 0)),
            pl.BlockSpec((1, KV_LORA), lambda i, j: (0, 0)),
            pl.BlockSpec((tm, QK_ROPE), lambda i, j: (i % nt, 0)),
            pl.BlockSpec((tm, QK_ROPE), lambda i, j: (i % nt, 0)),
        ],
        out_specs=[pl.BlockSpec((tm, KV_LORA), lambda i, j: (i, 0)),
                   pl.BlockSpec((tm, QK_ROPE), lambda i, j: (i, 0))],
        out_shape=[jax.ShapeDtypeStruct((t, KV_LORA), F32), jax.ShapeDtypeStruct((t, QK_ROPE), F32)],
        scratch_shapes=[pltpu.VMEM((n, d), BF16)],
        compiler_params=_params("arbitrary", "arbitrary"),
        name="latent_kv",
    )(x, g_kv.reshape(1, d), mods, mods, w_dkv_t, g_ckv.reshape(1, KV_LORA), cos2, sin2)


def _qproj_kernel(x_ref, g_ref, sc_ref, sh_ref, w_ref, cos_ref, sin_ref, q_ref, hb_ref, *, hb):
    @pl.when(pl.program_id(1) == 0)
    def _():
        hb_ref[...] = _normmod(x_ref[...], g_ref[...], sc_ref[...], sh_ref[...]).astype(BF16)

    acc = jnp.dot(hb_ref[...], w_ref[...].astype(BF16), preferred_element_type=F32)
    cos2, sin2 = cos_ref[...], sin_ref[...]
    for h in range(hb):
        c0 = h * QK_HEAD
        q_ref[h, :, :QK_NOPE] = acc[:, c0:c0 + QK_NOPE].astype(BF16)
        q_ref[h, :, QK_NOPE:] = _rope(acc[:, c0 + QK_NOPE:c0 + QK_HEAD], cos2, sin2).astype(BF16)


def _qproj(x, g, mods, sc_chunk, sh_chunk, rows_per_b, tm, w_q2, cos2, sin2, hb=4):
    t, d = x.shape
    nh = w_q2.shape[1] // QK_HEAD
    nt = rows_per_b // tm
    return pl.pallas_call(
        functools.partial(_qproj_kernel, hb=hb),
        grid=(t // tm, nh // hb),
        in_specs=[
            pl.BlockSpec((tm, d), lambda i, j: (i, 0)),
            pl.BlockSpec((1, d), lambda i, j: (0, 0)),
            _mod_spec(mods, sc_chunk, tm, rows_per_b, d),
            _mod_spec(mods, sh_chunk, tm, rows_per_b, d),
            pl.BlockSpec((d, hb * QK_HEAD), lambda i, j: (0, j)),
            pl.BlockSpec((tm, QK_ROPE), lambda i, j: (i % nt, 0)),
            pl.BlockSpec((tm, QK_ROPE), lambda i, j: (i % nt, 0)),
        ],
        out_specs=pl.BlockSpec((None, hb, tm, QK_HEAD), lambda i, j: (i // nt, j, i % nt, 0)),
        out_shape=jax.ShapeDtypeStruct((t // rows_per_b, nh, rows_per_b, QK_HEAD), BF16),
        scratch_shapes=[pltpu.VMEM((tm, d), BF16)],
        compiler_params=_params("arbitrary", "arbitrary"),
        name="q_proj",
    )(x, g.reshape(1, d), mods, mods, w_q2, cos2, sin2)


def _kvup_kernel(ckv_ref, kpe_ref, wk_ref, wv_ref, k_ref, v_ref, *, hb):
    c = ckv_ref[...].astype(BF16)
    kn = jnp.dot(c, wk_ref[...].astype(BF16), preferred_element_type=F32)
    vv = jnp.dot(c, wv_ref[...].astype(BF16), preferred_element_type=F32)
    kpe = kpe_ref[...].astype(BF16)
    for h in range(hb):
        k_ref[h, :, :QK_NOPE] = kn[:, h * QK_NOPE:(h + 1) * QK_NOPE].astype(BF16)
        k_ref[h, :, QK_NOPE:] = kpe
        v_ref[h] = vv[:, h * V_HEAD:(h + 1) * V_HEAD].astype(BF16)


def _kvup(ckv, kpe, w_uk2, w_uv2, bsz, seq, tm, hb=4):
    t, r = ckv.shape
    nt = seq // tm
    return pl.pallas_call(
        functools.partial(_kvup_kernel, hb=hb),
        grid=(t // tm, N_HEADS // hb),
        in_specs=[
            pl.BlockSpec((tm, r), lambda i, j: (i, 0)),
            pl.BlockSpec((tm, QK_ROPE), lambda i, j: (i, 0)),
            pl.BlockSpec((r, hb * QK_NOPE), lambda i, j: (0, j)),
            pl.BlockSpec((r, hb * V_HEAD), lambda i, j: (0, j)),
        ],
        out_specs=[pl.BlockSpec((None, hb, tm, QK_HEAD), lambda i, j: (i // nt, j, i % nt, 0)),
                   pl.BlockSpec((None, hb, tm, V_HEAD), lambda i, j: (i // nt, j, i % nt, 0))],
        out_shape=[jax.ShapeDtypeStruct((bsz, N_HEADS, seq, QK_HEAD), BF16),
                   jax.ShapeDtypeStruct((bsz, N_HEADS, seq, V_HEAD), BF16)],
        compiler_params=_params("arbitrary", "arbitrary"),
        name="kv_up",
    )(ckv, kpe, w_uk2, w_uv2)


def _flash_kernel(qt_ref, kt_ref, q_ref, k_ref, v_ref, o_ref, m_ref, l_ref, acc_ref, *, tq, tk, hb):
    p_idx = pl.program_id(2)
    qi = qt_ref[p_idx]
    kj = kt_ref[p_idx]

    @pl.when(kj == 0)
    def _():
        m_ref[...] = jnp.full_like(m_ref, -jnp.inf)
        l_ref[...] = jnp.zeros_like(l_ref)
        acc_ref[...] = jnp.zeros_like(acc_ref)

    def tile(h, masked):
        s = lax.dot_general(q_ref[h], k_ref[h], DN_LAST, preferred_element_type=F32) * (ATTN_SCALE * LOG2_E)
        if masked:
            qpos = lax.broadcasted_iota(I32, (tq, tk), 0)
            kpos = lax.broadcasted_iota(I32, (tq, tk), 1)
            s = jnp.where(kpos <= qpos, s, -jnp.inf)
        m_prev = m_ref[h]
        m_next = jnp.maximum(m_prev, jnp.max(s, axis=1, keepdims=True))
        alpha = jnp.exp2(m_prev - m_next)
        p = jnp.exp2(s - jnp.tile(m_next, (1, tk // LANES)))
        l_ref[h] = alpha * l_ref[h] + jnp.sum(p, axis=1, keepdims=True)
        acc_ref[h] = alpha * acc_ref[h] + jnp.dot(p.astype(BF16), v_ref[h], preferred_element_type=F32)
        m_ref[h] = m_next

    @pl.when(kj < qi)
    def _():
        for h in range(hb):
            tile(h, False)

    @pl.when(kj == qi)
    def _():
        for h in range(hb):
            tile(h, True)
            o_ref[:, h * V_HEAD:(h + 1) * V_HEAD] = (acc_ref[h] / l_ref[h]).astype(BF16)


def _flash(q, k, v, tile_len=512, hb=2):
    bsz, nh, seq, _ = q.shape
    tq = tk = tile_len
    nq = seq // tq
    pairs = [(i, j) for i in range(nq) for j in range(i + 1)]
    qt = jnp.asarray([p[0] for p in pairs], I32)
    kt = jnp.asarray([p[1] for p in pairs], I32)
    grid_spec = pltpu.PrefetchScalarGridSpec(
        num_scalar_prefetch=2,
        grid=(bsz, nh // hb, len(pairs)),
        in_specs=[
            pl.BlockSpec((None, hb, tq, QK_HEAD), lambda b, h, p, qt, kt: (b, h, qt[p], 0)),
            pl.BlockSpec((None, hb, tk, QK_HEAD), lambda b, h, p, qt, kt: (b, h, kt[p], 0)),
            pl.BlockSpec((None, hb, tk, V_HEAD), lambda b, h, p, qt, kt: (b, h, kt[p], 0)),
        ],
        out_specs=pl.BlockSpec((tq, hb * V_HEAD), lambda b, h, p, qt, kt: (b * nq + qt[p], h)),
        scratch_shapes=[pltpu.VMEM((hb, tq, LANES), F32), pltpu.VMEM((hb, tq, LANES), F32),
                        pltpu.VMEM((hb, tq, V_HEAD), F32)],
    )
    return pl.pallas_call(
        functools.partial(_flash_kernel, tq=tq, tk=tk, hb=hb),
        grid_spec=grid_spec,
        out_shape=jax.ShapeDtypeStruct((bsz * seq, nh * V_HEAD), BF16),
        compiler_params=_params("arbitrary", "arbitrary", "arbitrary"),
        name="flash_attn",
    )(qt, kt, q, k, v)


def _qlat_kernel(q_ref, wk_ref, o_ref):
    q = q_ref[...]
    qlat = lax.dot_general(q[:, :QK_NOPE], wk_ref[...].astype(BF16), DN_LAST, preferred_element_type=F32)
    o_ref[:, :KV_LORA] = qlat.astype(BF16)
    o_ref[:, KV_LORA:] = q[:, QK_NOPE:]


def _qlat(q, w_uk2):
    nh, b, _ = q.shape
    r = w_uk2.shape[0]
    return pl.pallas_call(
        _qlat_kernel,
        grid=(nh,),
        in_specs=[pl.BlockSpec((None, b, QK_HEAD), lambda h: (h, 0, 0)),
                  pl.BlockSpec((r, QK_NOPE), lambda h: (0, h))],
        out_specs=pl.BlockSpec((None, b, r + QK_ROPE), lambda h: (h, 0, 0)),
        out_shape=jax.ShapeDtypeStruct((nh, b, r + QK_ROPE), BF16),
        compiler_params=_params("arbitrary"),
        name="q_latent",
    )(q, w_uk2)


def _paged_kernel(pt_ref, q_ref, ckv_hbm, kpe_hbm, cnew_ref, pnew_ref, o_ref, qpad_ref, kcb_ref, pp_ref,
                  m_ref, l_ref, acc_ref, pgc_ref, pgk_ref, semc, semk, *, npg, nsteps, nrows, rb):
    bi = pl.program_id(0)
    j = pl.program_id(1)
    nh = N_HEADS
    wide = KV_LORA // LANES
    first = j == 0

    g_step = bi * nsteps + j
    n_total = (nrows // rb) * nsteps

    def start_pages(gs):
        slot = gs % PAGE_SLOTS
        row_blk = gs // nsteps
        step = gs % nsteps
        for r in range(rb):
            for i in range(npg):
                pg = pt_ref[row_blk * rb + r, step * npg + i]
                pltpu.make_async_copy(ckv_hbm.at[pg], pgc_ref.at[slot, r * npg + i], semc.at[slot]).start()
                pltpu.make_async_copy(kpe_hbm.at[pg], pgk_ref.at[slot, r * npg + i], semk.at[slot]).start()

    @pl.when(g_step == 0)
    def _():
        for ahead in range(PAGE_SLOTS - 1):
            start_pages(g_step + ahead)

    @pl.when(first & (bi == 0))
    def _():
        kcb_ref[...] = jnp.zeros_like(kcb_ref)
        pp_ref[...] = jnp.zeros_like(pp_ref)
        m_ref[...] = jnp.zeros_like(m_ref)
        l_ref[...] = jnp.zeros_like(l_ref)
        acc_ref[...] = jnp.zeros_like(acc_ref)

    @pl.when(first)
    def _():
        qpad_ref[...] = jnp.zeros_like(qpad_ref)
        for r in range(rb):
            qpad_ref[r, :nh, :] = q_ref[r]

    def step(cur, prv):
        @pl.when(g_step + (PAGE_SLOTS - 1) < n_total)
        def _():
            start_pages(g_step + (PAGE_SLOTS - 1))

        pslot = g_step % PAGE_SLOTS
        for k in range(rb * npg):
            pltpu.make_async_copy(ckv_hbm.at[0], pgc_ref.at[pslot, k], semc.at[pslot]).wait()
            pltpu.make_async_copy(kpe_hbm.at[0], pgk_ref.at[pslot, k], semk.at[pslot]).wait()
        for r in range(rb):
            pv_prev = jnp.dot(pp_ref[r], kcb_ref[r, prv], preferred_element_type=F32)
            acc_in = jnp.where(first, 0.0, acc_ref[r] + pv_prev)
            m_prev = jnp.where(first, -jnp.inf, m_ref[r])
            l_prev = jnp.where(first, 0.0, l_ref[r])
            for i in range(npg):
                kcb_ref[r, cur, i * PAGE_SIZE:(i + 1) * PAGE_SIZE, :] = pgc_ref[pslot, r * npg + i].astype(BF16)
            s_c = lax.dot_general(kcb_ref[r, cur], qpad_ref[r, :, :KV_LORA], DN_LAST,
                                  preferred_element_type=F32)
            kp = jnp.concatenate([pgk_ref[pslot, r * npg + i].astype(BF16) for i in range(npg)],
                                 axis=1)
            s_p = jnp.dot(qpad_ref[r, :, KV_LORA:], kp, preferred_element_type=F32)
            s = (s_c.T[:nh, :] + s_p[:nh, :]) * ATTN_SCALE
            m_next = jnp.maximum(m_prev, jnp.max(s, axis=1, keepdims=True))
            alpha = jnp.exp(m_prev - m_next)
            p = jnp.exp(s - jnp.tile(m_next, (1, s.shape[1] // LANES)))
            l_ref[r] = alpha * l_prev + jnp.sum(p, axis=1, keepdims=True)
            acc_ref[r] = jnp.tile(alpha, (1, wide)) * acc_in
            m_ref[r] = m_next
            pp_ref[r] = p.astype(BF16)

    @pl.when(j % 2 == 0)
    def _():
        step(0, 1)

    @pl.when(j % 2 == 1)
    def _():
        step(1, 0)

    @pl.when(j == nsteps - 1)
    def _():
        last_slot = (nsteps - 1) % 2
        for r in range(rb):
            acc_past = acc_ref[r] + jnp.dot(pp_ref[r], kcb_ref[r, last_slot], preferred_element_type=F32)
            q = q_ref[r].astype(F32)
            cnew = cnew_ref[r]
            s_new = (jnp.sum(q[:, :KV_LORA] * cnew, axis=1, keepdims=True)
                     + jnp.sum(q[:, KV_LORA:] * pnew_ref[r], axis=1, keepdims=True)) * ATTN_SCALE
            m_p = m_ref[r]
            m_n = jnp.maximum(m_p, s_new)
            a = jnp.exp(m_p - m_n)
            p_new = jnp.exp(s_new - m_n)
            l_fin = a * l_ref[r] + p_new
            acc = jnp.tile(a, (1, wide)) * acc_past + jnp.tile(p_new, (1, wide)) * cnew
            o_ref[r] = (acc / jnp.tile(l_fin, (1, wide))).astype(BF16)


def _paged_attn(qcat, cache_ckv, cache_kpe_t, page_table, ckv_new, kpe_new, npg=8, rb=2):
    b, nh, w = qcat.shape
    r = KV_LORA
    n_pages = page_table.shape[1]
    keys = npg * PAGE_SIZE
    nsteps = n_pages // npg
    assert (b // rb) * nsteps >= PAGE_SLOTS - 1
    grid_spec = pltpu.PrefetchScalarGridSpec(
        num_scalar_prefetch=1,
        grid=(b // rb, nsteps),
        in_specs=[pl.BlockSpec((rb, nh, w), lambda bi, j, pt: (bi, 0, 0)),
                  pl.BlockSpec(memory_space=pl.ANY),
                  pl.BlockSpec(memory_space=pl.ANY),
                  pl.BlockSpec((rb, 1, r), lambda bi, j, pt: (bi, 0, 0)),
                  pl.BlockSpec((rb, 1, QK_ROPE), lambda bi, j, pt: (bi, 0, 0))],
        out_specs=pl.BlockSpec((rb, nh, r), lambda bi, j, pt: (bi, 0, 0)),
        scratch_shapes=[pltpu.VMEM((rb, LANES, w), BF16), pltpu.VMEM((rb, 2, keys, r), BF16),
                        pltpu.VMEM((rb, nh, keys), BF16), pltpu.VMEM((rb, nh, LANES), F32),
                        pltpu.VMEM((rb, nh, LANES), F32), pltpu.VMEM((rb, nh, r), F32),
                        pltpu.VMEM((PAGE_SLOTS, rb * npg, PAGE_SIZE, r), F32),
                        pltpu.VMEM((PAGE_SLOTS, rb * npg, QK_ROPE, PAGE_SIZE), F32),
                        pltpu.SemaphoreType.DMA((PAGE_SLOTS,)), pltpu.SemaphoreType.DMA((PAGE_SLOTS,))],
    )
    return pl.pallas_call(
        functools.partial(_paged_kernel, npg=npg, nsteps=nsteps, nrows=b, rb=rb),
        grid_spec=grid_spec,
        out_shape=jax.ShapeDtypeStruct((b, nh, r), BF16),
        compiler_params=_params("arbitrary", "arbitrary"),
        name="paged_attn",
    )(page_table, qcat, cache_ckv, cache_kpe_t, ckv_new.reshape(b, 1, r), kpe_new.reshape(b, 1, QK_ROPE))


def _ovup_kernel(o_ref, wv_ref, a_ref):
    a_ref[...] = jnp.dot(o_ref[...], wv_ref[...].astype(BF16), preferred_element_type=F32).astype(BF16)


def _ovup(olat, w_uv2):
    nh, b, r = olat.shape
    return pl.pallas_call(
        _ovup_kernel,
        grid=(nh,),
        in_specs=[pl.BlockSpec((None, b, r), lambda h: (h, 0, 0)), pl.BlockSpec((r, V_HEAD), lambda h: (0, h))],
        out_specs=pl.BlockSpec((b, V_HEAD), lambda h: (0, h)),
        out_shape=jax.ShapeDtypeStruct((b, nh * V_HEAD), BF16),
        compiler_params=_params("arbitrary"),
        name="o_latent_up",
    )(olat, w_uv2)


def _rope_tables(pos):
    inv = ROPE_THETA ** (-jnp.arange(0, QK_ROPE, 2, dtype=F32) / QK_ROPE)
    ang = pos.astype(F32)[:, None] * inv[None, :]
    cos, sin = jnp.cos(ang), jnp.sin(ang)
    return jnp.concatenate([cos, cos], axis=1), jnp.concatenate([-sin, sin], axis=1)


def _trunk(x, mods, mods_kv, rows_per_b, tm, cos2, sin2, P, s5_glu_fn, attn_fn, routed):
    tms = min(tm, 512)

    def moe_block(x, layer, g_final=None):
        h2, gates_t, e_rows, w_rows = _route(x, P["g_norm"][layer, 1], mods[layer], 4, 3, rows_per_b, tms,
                                             P["w_router_t"], P["b_router"], split=routed)
        if routed:
            return _moe_routed(h2, e_rows, w_rows, P["w_gate"], P["w_up"], P["w_down"], layer, x,
                               mods[layer], 5, rows_per_b, g_final)
        x = _moe(h2, gates_t.T, P["w_gate"], P["w_up"], P["w_down"], layer, x, mods[layer], 5,
                 rows_per_b, tms)
        return x if g_final is None else _final_norm(x, g_final, tms)

    x, st_re, st_im = s5_glu_fn(x)
    x = moe_block(x, 0)
    ckv, kpe = _kv(x, P["g_kv"], mods_kv, rows_per_b, tms, P["w_dkv_t"], P["g_ckv"], cos2, sin2)
    q = _qproj(x, P["g_norm"][1, 0], mods[1], 1, 0, rows_per_b, tm, P["w_q2"], cos2, sin2)
    attn = attn_fn(q, ckv, kpe)
    x = _proj_res(attn, P["w_o"][0], x, mods[1], 2, rows_per_b, tm, 512, glu=False)
    y = moe_block(x, 1, P["g_final"])
    return y, st_re, st_im, ckv, kpe


def kernel(x_prompt, x_sample, c_prompt, c_sample, state_ssm_re, state_ssm_im, cache_ckv, cache_kpe, page_table, g_norm, w_ada, b_ada, ssm_lam_re, ssm_lam_im, ssm_log_dt, ssm_b_re, ssm_b_im, ssm_c_re, ssm_c_im, ssm_d, w_glu, g_kv, w_ada_kv, b_ada_kv, w_dkv, g_ckv, w_uk, w_uv, w_q, w_o, w_router, b_router, w_gate, w_up, w_down, g_final):
    bsz, seq, d = x_prompt.shape
    db = x_sample.shape[0]
    g = d // SSM_GROUP
    n = SSM_STATE
    assert g == LANES, "the group-major relayout puts one S5 group per lane"
    past = page_table.shape[1] * PAGE_SIZE
    P = dict(g_norm=g_norm, g_kv=g_kv, g_ckv=g_ckv, w_o=w_o, b_router=b_router,
             w_gate=w_gate, w_up=w_up, w_down=w_down, g_final=g_final)
    P["w_q2"] = w_q[0]
    P["w_dkv_t"] = w_dkv.T
    P["w_router_t"] = w_router.T
    w_uk2 = w_uk.reshape(KV_LORA, N_HEADS * QK_NOPE)
    w_uv2 = w_uv.reshape(KV_LORA, N_HEADS * V_HEAD)

    m_rows = db + bsz
    m_pad = -(-m_rows // 16) * 16
    c_all = jnp.concatenate([c_sample, c_prompt, jnp.zeros((m_pad - m_rows, d), F32)], axis=0)
    mod = _ada(c_all, w_ada, b_ada)
    mod_kv = _ada(c_all, w_ada_kv[None], b_ada_kv[None])
    mods_s = [mod[l, :db][None] for l in range(mod.shape[0])]
    mods_p = [mod[l, db:m_rows][:, None, :] for l in range(mod.shape[0])]
    modkv_s = mod_kv[0, :db][None]
    modkv_p = mod_kv[0, db:m_rows][:, None, :]

    lam_re = ssm_lam_re[0].reshape(g, 1, n)
    lam_im = ssm_lam_im[0].reshape(g, 1, n)
    log_dt = jnp.broadcast_to(ssm_log_dt[0].reshape(g, 1, 1), (g, 1, n))
    bt_re = ssm_b_re[0].transpose(0, 2, 1)
    bt_im = ssm_b_im[0].transpose(0, 2, 1)
    c_re, c_im = ssm_c_re[0], ssm_c_im[0]
    d_g = ssm_d[0].reshape(g, 1, SSM_GROUP)
    d_rep = jnp.tile(d_g, (1, 1, S5_CHUNK))
    dup = lambda a: jnp.concatenate([a, a], axis=-1)
    cat = lambda a, b: jnp.concatenate([a, b], axis=-1)
    nchunk = seq // S5_CHUNK
    tm_p = 1024

    src = jnp.arange(d, dtype=I32)
    dst = (src % SSM_GROUP) * g + src // SSM_GROUP
    perm = (dst[:, None] == jnp.arange(d, dtype=I32)[None, :]).astype(BF16)
    w_glu_p = w_glu[0].reshape(g, SSM_GROUP, -1).transpose(1, 0, 2).reshape(d, -1)

    def s5_glu_prompt(x):
        hx = _s5_in(x, g_norm[0, 0], mods_p[0], 1, 0, seq, 256, perm)
        hx = hx.reshape(g, bsz * nchunk, S5_CHUNK * SSM_GROUP)
        zg, st = _s5_prompt(hx, dup(lam_re), dup(lam_im), dup(log_dt), cat(bt_re, bt_im), cat(bt_im, bt_re),
                            cat(c_re, c_im), cat(c_im, c_re), d_rep, bsz)
        zg = zg.reshape(g, bsz * seq * SSM_GROUP)
        x = _proj_res(zg, w_glu_p, x, mods_p[0], 2, seq, tm_p, 256, glu=True, grouped=True)
        st = st.transpose(1, 0, 2)[None]
        return x, st[..., :n], st[..., n:]

    def s5_glu_sample(x):
        h = _normmod_call(x, g_norm[0, 0], mods_s[0], 1, 0, db, db, F32)
        u = h.reshape(db, g, SSM_GROUP).transpose(1, 0, 2)
        h0_re = state_ssm_re[0].transpose(1, 0, 2)
        h0_im = state_ssm_im[0].transpose(1, 0, 2)
        zg, nr, ni = _s5_sample(u, h0_re, h0_im, lam_re, lam_im, log_dt, bt_re, bt_im, c_re, c_im, d_g)
        z = zg.transpose(1, 0, 2).reshape(db, d)
        x = _proj_res(z, w_glu[0], x, mods_s[0], 2, db, db, 256, glu=True)
        return x, nr.transpose(1, 0, 2)[None], ni.transpose(1, 0, 2)[None]

    def attn_prompt(q, ckv, kpe):
        k, v = _kvup(ckv, kpe, w_uk2, w_uv2, bsz, seq, tm_p)
        return _flash(q, k, v)

    def attn_sample(q, ckv, kpe):
        qcat = _qlat(q[0], w_uk2).transpose(1, 0, 2)
        olat = _paged_attn(qcat, cache_ckv, cache_kpe.transpose(0, 2, 1), page_table, ckv, kpe)
        return _ovup(olat.transpose(1, 0, 2), w_uv2)

    cos_p, sin_p = _rope_tables(jnp.arange(seq))
    cos_s, sin_s = _rope_tables(jnp.full((db,), past))

    y_p, sre_p, sim_p, ckv_p, kpe_p = _trunk(
        x_prompt.reshape(bsz * seq, d), mods_p, modkv_p, seq, tm_p, cos_p, sin_p, P,
        s5_glu_prompt, attn_prompt, routed=True)
    y_s, sre_s, sim_s, ckv_s, kpe_s = _trunk(
        x_sample.reshape(db, d), mods_s, modkv_s, db, db, cos_s, sin_s, P,
        s5_glu_sample, attn_sample, routed=False)
    return (y_p.reshape(bsz, seq, d), y_s.reshape(db, 1, d), sre_p, sim_p,
            ckv_p.reshape(bsz, seq, KV_LORA), kpe_p.reshape(bsz, seq, QK_ROPE),
            sre_s, sim_s, ckv_s.reshape(db, 1, KV_LORA), kpe_s.reshape(db, 1, QK_ROPE))
```

```python
import functools
import math

import jax
import jax.numpy as jnp
from jax import lax
from jax.experimental import pallas as pl
from jax.experimental.pallas import tpu as pltpu

F32 = jnp.float32
BF16 = jnp.bfloat16
I32 = jnp.int32

SSM_GROUP = 16
SSM_STATE = 64
N_HEADS = 16
QK_NOPE = 128
QK_ROPE = 64
V_HEAD = 128
KV_LORA = 512
ROPE_THETA = 10000.0
PAGE_SIZE = 128
N_EXPERTS = 16
N_EXPERT_GROUPS = 4
EXPERTS_PER_GROUP = N_EXPERTS // N_EXPERT_GROUPS
TOP_K = 2
EPS = 1e-6
ATTN_SCALE = (QK_NOPE + QK_ROPE) ** -0.5
QK_HEAD = QK_NOPE + QK_ROPE
LOG2_E = math.log2(math.e)

V7X_VMEM_BYTES = 64 * 1024 * 1024
VMEM_LIMIT_BYTES = V7X_VMEM_BYTES - 8 * 1024 * 1024
LANES = 128
SMEM_I32_BLOCK = 1024
DMA_GROUP = 8
PAGE_SLOTS = 4
S5_CHUNK = 16
DN_LAST = (((1,), (1,)), ((), ()))


def _params(*sem):
    return pltpu.CompilerParams(dimension_semantics=sem, vmem_limit_bytes=VMEM_LIMIT_BYTES)


def _normmod(x, g, sc, sh):
    ms = jnp.mean(x * x, axis=-1, keepdims=True)
    return (x * lax.rsqrt(ms + EPS)) * g * (1.0 + sc) + sh


def _rope(p, cos2, sin2):
    half = QK_ROPE // 2
    swapped = jnp.concatenate([p[:, half:], p[:, :half]], axis=1)
    return p * cos2 + swapped * sin2


def _mod_spec(mods, chunk, tm, rows_per_b, width, j_blocks=None):
    _, r, _ = mods.shape
    if j_blocks is None:
        return pl.BlockSpec((None, r, width), lambda i, j: ((i * tm) // rows_per_b, 0, chunk))
    return pl.BlockSpec((None, r, width), lambda i, j: ((i * tm) // rows_per_b, 0, chunk * j_blocks + j))


def _ada_kernel(c_ref, w_ref, b_ref, o_ref, cb_ref):
    @pl.when((pl.program_id(0) == 0) & (pl.program_id(1) == 0))
    def _():
        c = c_ref[...]
        cb_ref[...] = (c * jax.nn.sigmoid(c)).astype(BF16)

    acc = jnp.dot(cb_ref[...], w_ref[...].astype(BF16), preferred_element_type=F32)
    o_ref[...] = acc + b_ref[...]


def _ada(c_all, w, b, tn=1024):
    m, d = c_all.shape
    lw, _, n = w.shape
    return pl.pallas_call(
        _ada_kernel,
        grid=(lw, n // tn),
        in_specs=[
            pl.BlockSpec((m, d), lambda l, j: (0, 0)),
            pl.BlockSpec((None, d, tn), lambda l, j: (l, 0, j)),
            pl.BlockSpec((None, 1, tn), lambda l, j: (l, 0, j)),
        ],
        out_specs=pl.BlockSpec((None, m, tn), lambda l, j: (l, 0, j)),
        out_shape=jax.ShapeDtypeStruct((lw, m, n), F32),
        scratch_shapes=[pltpu.VMEM((m, d), BF16)],
        compiler_params=_params("arbitrary", "arbitrary"),
        name="ada_mod",
    )(c_all, w, b.reshape(lw, 1, n))


def _normmod_kernel(x_ref, g_ref, sc_ref, sh_ref, o_ref):
    o_ref[...] = _normmod(x_ref[...], g_ref[...], sc_ref[...], sh_ref[...]).astype(o_ref.dtype)


def _normmod_call(x, g, mods, sc_chunk, sh_chunk, rows_per_b, tm, out_dtype):
    t, d = x.shape
    return pl.pallas_call(
        _normmod_kernel,
        grid=(t // tm, 1),
        in_specs=[
            pl.BlockSpec((tm, d), lambda i, j: (i, 0)),
            pl.BlockSpec((1, d), lambda i, j: (0, 0)),
            _mod_spec(mods, sc_chunk, tm, rows_per_b, d),
            _mod_spec(mods, sh_chunk, tm, rows_per_b, d),
        ],
        out_specs=pl.BlockSpec((tm, d), lambda i, j: (i, 0)),
        out_shape=jax.ShapeDtypeStruct((t, d), out_dtype),
        compiler_params=_params("arbitrary", "arbitrary"),
        name="norm_mod",
    )(x, g.reshape(1, d), mods, mods)


def _final_norm_kernel(x_ref, g_ref, o_ref):
    x = x_ref[...]
    ms = jnp.mean(x * x, axis=-1, keepdims=True)
    o_ref[...] = (x * lax.rsqrt(ms + EPS)) * g_ref[...]


def _final_norm(x, g, tm):
    t, d = x.shape
    return pl.pallas_call(
        _final_norm_kernel,
        grid=(t // tm,),
        in_specs=[pl.BlockSpec((tm, d), lambda i: (i, 0)), pl.BlockSpec((1, d), lambda i: (0, 0))],
        out_specs=pl.BlockSpec((tm, d), lambda i: (i, 0)),
        out_shape=jax.ShapeDtypeStruct((t, d), F32),
        compiler_params=_params("arbitrary"),
        name="final_norm",
    )(x, g.reshape(1, d))


def _proj_res_kernel(x_ref, *refs, glu, grouped, tm):
    refs = list(refs)
    xb_ref = refs.pop() if grouped else None
    if glu:
        wa_ref, wb_ref, res_ref, gt_ref, o_ref = refs
    else:
        wa_ref, res_ref, gt_ref, o_ref = refs
    if grouped:
        @pl.when(pl.program_id(1) == 0)
        def _():
            cg = SSM_GROUP
            for k in range(tm // LANES):
                zt = x_ref[:, k * LANES * cg:(k + 1) * LANES * cg].T
                xb_ref[k * LANES:(k + 1) * LANES, :] = zt.reshape(LANES, cg, LANES).reshape(LANES, cg * LANES)

        x = xb_ref[...]
    else:
        x = x_ref[...]
    y = jnp.dot(x, wa_ref[...].astype(BF16), preferred_element_type=F32)
    if glu:
        yb = jnp.dot(x, wb_ref[...].astype(BF16), preferred_element_type=F32)
        y = y * jax.nn.sigmoid(yb)
    o_ref[...] = res_ref[...] + gt_ref[...] * y


def _proj_res(x, w, res, mods, gt_chunk, rows_per_b, tm, tn, glu, grouped=False):
    t, n = res.shape
    k = w.shape[0]
    nj = n // tn
    if grouped:
        x_spec = pl.BlockSpec((x.shape[0], tm * SSM_GROUP), lambda i, j: (0, i))
    else:
        x_spec = pl.BlockSpec((tm, k), lambda i, j: (i, 0))
    in_specs = [x_spec, pl.BlockSpec((k, tn), lambda i, j: (0, j))]
    args = [x, w]
    if glu:
        in_specs.append(pl.BlockSpec((k, tn), lambda i, j: (0, nj + j)))
        args.append(w)
    in_specs += [pl.BlockSpec((tm, tn), lambda i, j: (i, j)), _mod_spec(mods, gt_chunk, tm, rows_per_b, tn, nj)]
    args += [res, mods]
    return pl.pallas_call(
        functools.partial(_proj_res_kernel, glu=glu, grouped=grouped, tm=tm),
        grid=(t // tm, nj),
        in_specs=in_specs,
        out_specs=pl.BlockSpec((tm, tn), lambda i, j: (i, j)),
        out_shape=jax.ShapeDtypeStruct((t, n), F32),
        scratch_shapes=[pltpu.VMEM((tm, k), BF16)] if grouped else [],
        compiler_params=_params("arbitrary", "arbitrary"),
        name="glu_proj" if glu else "out_proj",
    )(*args)


def _s5in_kernel(x_ref, g_ref, sc_ref, sh_ref, p_ref, o_ref):
    h = _normmod(x_ref[...], g_ref[...], sc_ref[...], sh_ref[...]).astype(BF16)
    hp = jnp.dot(h, p_ref[...], preferred_element_type=F32).astype(BF16)
    tm = hp.shape[0]
    o_ref[...] = hp.reshape(tm, SSM_GROUP, LANES).reshape(tm * SSM_GROUP, LANES).T


def _s5_in(x, g, mods, sc_chunk, sh_chunk, rows_per_b, tm, perm):
    t, d = x.shape
    ng = d // SSM_GROUP
    return pl.pallas_call(
        _s5in_kernel,
        grid=(t // tm, 1),
        in_specs=[
            pl.BlockSpec((tm, d), lambda i, j: (i, 0)),
            pl.BlockSpec((1, d), lambda i, j: (0, 0)),
            _mod_spec(mods, sc_chunk, tm, rows_per_b, d),
            _mod_spec(mods, sh_chunk, tm, rows_per_b, d),
            pl.BlockSpec((d, d), lambda i, j: (0, 0)),
        ],
        out_specs=pl.BlockSpec((ng, tm * SSM_GROUP), lambda i, j: (0, i)),
        out_shape=jax.ShapeDtypeStruct((ng, t * SSM_GROUP), BF16),
        compiler_params=_params("arbitrary", "arbitrary"),
        name="s5_in",
    )(x, g.reshape(1, d), mods, mods, perm)


def _cpow(lr_dt, li_dt, k):
    mag = jnp.exp(k * lr_dt)
    ang = k * li_dt
    return mag * jnp.cos(ang), mag * jnp.sin(ang)


def _rep_rows(e, n):
    q, w = e.shape
    return jnp.concatenate([jnp.broadcast_to(e[t:t + 1], (n, w)) for t in range(q)], axis=0)


def _tile_rows(c, n):
    return jnp.concatenate([c] * n, axis=0)


def _zoh(lr, li, ldt):
    dt = jnp.exp(ldt)
    lr_dt = lr * dt
    li_dt = li * dt
    mag = jnp.exp(lr_dt)
    ab_re = mag * jnp.cos(li_dt)
    ab_im = mag * jnp.sin(li_dt)
    den = lr * lr + li * li
    f_re = ((ab_re - 1.0) * lr + ab_im * li) / den
    f_im = (ab_im * lr - (ab_re - 1.0) * li) / den
    return lr_dt, li_dt, ab_re, ab_im, f_re, f_im


def _gelu_tanh(y):
    return 0.5 * y * (1.0 + jnp.tanh(math.sqrt(2.0 / math.pi) * (y + 0.044715 * (y * y * y))))


def _s5_prompt_kernel(x_ref, lr_ref, li_ref, ldt_ref, bt_ref, bts_ref, c_ref, cs_ref, d_ref,
                      z_ref, st_ref, s_sc, ss_sc, hp_sc, yl_sc, *, gb, nb, nchunk):
    q, cg, n = S5_CHUNK, SSM_GROUP, SSM_STATE
    half = q // 2
    ti = lax.broadcasted_iota(I32, (q, 1), 0).astype(F32)
    rows = lax.broadcasted_iota(I32, (q * cg, q * cg), 0)
    cols = lax.broadcasted_iota(I32, (q * cg, q * cg), 1)
    shift = cg.bit_length() - 1
    causal = (rows >> shift) <= (cols >> shift)
    diag = rows == cols
    lane = lax.broadcasted_iota(I32, (1, 2 * n), 1)
    sgn = jnp.where(lane < n, -1.0, 1.0)

    def cmul(x2, xs, er, ei):
        return x2 * er + (xs * sgn) * ei

    a_terms = []
    vcs = []
    for g in range(gb):
        lr_dt, li_dt, _, _, f_re, f_im = _zoh(lr_ref[g], li_ref[g], ldt_ref[g])
        bt2, bts = bt_ref[g], bts_ref[g]
        bb2 = cmul(bt2, bts, f_re, f_im)
        bbs = bts * f_re - (bt2 * sgn) * f_im
        bb2t, bbst = _tile_rows(bb2, q), _tile_rows(bbs, q)
        c2t, cst = _tile_rows(c_ref[g], q), _tile_rows(cs_ref[g], q)

        def table(k, lr_dt=lr_dt, li_dt=li_dt):
            er, ei = _cpow(lr_dt, li_dt, k)
            return _rep_rows(er, cg), _rep_rows(ei, cg)

        pc = cmul(c2t, cst, *table(ti - half)) * (-sgn)
        pb = cmul(bb2t, bbst, *table(half - ti))
        tt = lax.dot_general(pb.astype(BF16), pc.astype(BF16), DN_LAST, preferred_element_type=F32)
        tt = jnp.where(causal, tt, 0.0) + jnp.where(diag, d_ref[g], 0.0)
        w1 = cmul(bb2t, bbst, *table((q - 1) - ti))
        vcs.append((cmul(c2t, cst, *table(ti + 1.0)) * (-sgn)).astype(BF16))
        ar, ai = _cpow(lr_dt, li_dt, float(q))
        a_terms.append((ar, ai * sgn))
        x = x_ref[g]
        yl_sc[g] = jnp.dot(x, tt.astype(BF16), preferred_element_type=F32)
        s = jnp.dot(x, w1.astype(BF16), preferred_element_type=F32)
        s_sc[g] = s
        ss_sc[g] = pltpu.roll(s, n, axis=1)

    def carry(ch, hs):
        new = []
        for g in range(gb):
            h, hsw = hs[2 * g], hs[2 * g + 1]
            hp_sc[g, pl.ds(ch, nb, stride=nchunk), :] = h
            a1, a2 = a_terms[g]
            new.append(a1 * h + a2 * hsw + s_sc[g, pl.ds(ch, nb, stride=nchunk), :])
            new.append(a1 * hsw - a2 * h + ss_sc[g, pl.ds(ch, nb, stride=nchunk), :])
        return tuple(new)

    zero = jnp.zeros((nb, 2 * n), F32)
    hs = lax.fori_loop(0, nchunk, carry, (zero,) * (2 * gb), unroll=8)
    for g in range(gb):
        st_ref[g] = hs[2 * g]
        y = yl_sc[g] + lax.dot_general(hp_sc[g].astype(BF16), vcs[g], DN_LAST, preferred_element_type=F32)
        z_ref[g] = _gelu_tanh(y).astype(BF16)


def _s5_prompt(hx, lam_re2, lam_im2, log_dt2, bt2, bts, c2, cs, d_rep, nb, gb=2):
    g, r, w = hx.shape
    n2 = 2 * SSM_STATE
    nchunk = r // nb
    vec = lambda: pl.BlockSpec((gb, 1, n2), lambda i: (i, 0, 0))
    mat = lambda: pl.BlockSpec((gb, SSM_GROUP, n2), lambda i: (i, 0, 0))
    return pl.pallas_call(
        functools.partial(_s5_prompt_kernel, gb=gb, nb=nb, nchunk=nchunk),
        grid=(g // gb,),
        in_specs=[pl.BlockSpec((gb, r, w), lambda i: (i, 0, 0)), vec(), vec(), vec(),
                  mat(), mat(), mat(), mat(), pl.BlockSpec((gb, 1, w), lambda i: (i, 0, 0))],
        out_specs=[pl.BlockSpec((gb, r, w), lambda i: (i, 0, 0)),
                   pl.BlockSpec((gb, nb, n2), lambda i: (i, 0, 0))],
        out_shape=[jax.ShapeDtypeStruct((g, r, w), BF16),
                   jax.ShapeDtypeStruct((g, nb, n2), F32)],
        scratch_shapes=[pltpu.VMEM((gb, r, n2), F32), pltpu.VMEM((gb, r, n2), F32),
                        pltpu.VMEM((gb, r, n2), F32), pltpu.VMEM((gb, r, w), F32)],
        compiler_params=_params("arbitrary"),
        name="s5_prompt",
    )(hx, lam_re2, lam_im2, log_dt2, bt2, bts, c2, cs, d_rep)


def _s5_sample_kernel(u_ref, h0r_ref, h0i_ref, lr_ref, li_ref, ldt_ref, btr_ref, bti_ref,
                      cr_ref, ci_ref, d_ref, z_ref, nr_ref, ni_ref, *, gb):
    hp = lax.Precision.HIGHEST
    for g in range(gb):
        _, _, ab_re, ab_im, f_re, f_im = _zoh(lr_ref[g], li_ref[g], ldt_ref[g])
        bb_re = f_re * btr_ref[g] - f_im * bti_ref[g]
        bb_im = f_re * bti_ref[g] + f_im * btr_ref[g]
        u = u_ref[g]
        bu_re = jnp.dot(u, bb_re, preferred_element_type=F32, precision=hp)
        bu_im = jnp.dot(u, bb_im, preferred_element_type=F32, precision=hp)
        h0r = h0r_ref[g]
        h0i = h0i_ref[g]
        hr = bu_re + ab_re * h0r - ab_im * h0i
        hi = bu_im + ab_re * h0i + ab_im * h0r
        nr_ref[g] = hr
        ni_ref[g] = hi
        y = (lax.dot_general(hr, cr_ref[g], DN_LAST, preferred_element_type=F32, precision=hp)
             - lax.dot_general(hi, ci_ref[g], DN_LAST, preferred_element_type=F32, precision=hp))
        y = y + d_ref[g] * u
        z_ref[g] = _gelu_tanh(y).astype(BF16)


def _s5_sample(u, h0_re, h0_im, lam_re, lam_im, log_dt, bt_re, bt_im, c_re, c_im, d_g, gb=8):
    g, b, cg = u.shape
    n = SSM_STATE
    vec = lambda: pl.BlockSpec((gb, 1, n), lambda i: (i, 0, 0))
    mat = lambda: pl.BlockSpec((gb, cg, n), lambda i: (i, 0, 0))
    st = lambda: pl.BlockSpec((gb, b, n), lambda i: (i, 0, 0))
    return pl.pallas_call(
        functools.partial(_s5_sample_kernel, gb=gb),
        grid=(g // gb,),
        in_specs=[pl.BlockSpec((gb, b, cg), lambda i: (i, 0, 0)), st(), st(), vec(), vec(), vec(),
                  mat(), mat(), mat(), mat(), pl.BlockSpec((gb, 1, cg), lambda i: (i, 0, 0))],
        out_specs=[pl.BlockSpec((gb, b, cg), lambda i: (i, 0, 0)), st(), st()],
        out_shape=[jax.ShapeDtypeStruct((g, b, cg), BF16),
                   jax.ShapeDtypeStruct((g, b, n), F32),
                   jax.ShapeDtypeStruct((g, b, n), F32)],
        compiler_params=_params("arbitrary"),
        name="s5_sample",
    )(u, h0_re, h0_im, lam_re, lam_im, log_dt, bt_re, bt_im, c_re, c_im, d_g)


def _route_kernel(x_ref, g_ref, sc_ref, sh_ref, wr_ref, br_ref, h_ref, gates_ref, e_ref, w_ref, *, split):
    h = _normmod(x_ref[...], g_ref[...], sc_ref[...], sh_ref[...])
    hb = h.astype(BF16)
    h_ref[...] = hb.reshape(h_ref.shape) if split else hb
    logits = lax.dot_general(wr_ref[...], h, DN_LAST, preferred_element_type=F32,
                             precision=lax.Precision.HIGHEST)
    s = jax.nn.sigmoid(logits)
    sel = s + br_ref[...]
    epg = EXPERTS_PER_GROUP
    srow = [s[e:e + 1, :] for e in range(N_EXPERTS)]
    row = [sel[e:e + 1, :] for e in range(N_EXPERTS)]

    def top2_sum(a, b, c, d):
        hi1, lo1 = jnp.maximum(a, b), jnp.minimum(a, b)
        hi2, lo2 = jnp.maximum(c, d), jnp.minimum(c, d)
        return jnp.maximum(hi1, hi2) + jnp.maximum(jnp.minimum(hi1, hi2), jnp.maximum(lo1, lo2))

    gs = [top2_sum(*row[epg * g:epg * (g + 1)]) for g in range(N_EXPERT_GROUPS)]
    best = gs[0]
    gidx = jnp.zeros_like(best, dtype=I32)
    for g in range(1, N_EXPERT_GROUPS):
        better = gs[g] > best
        gidx = jnp.where(better, g, gidx)
        best = jnp.where(better, gs[g], best)

    def pick(rows_, j):
        v = rows_[j]
        for g in range(1, N_EXPERT_GROUPS):
            v = jnp.where(gidx == g, rows_[epg * g + j], v)
        return v

    v = [pick(row, j) for j in range(epg)]
    u = [pick(srow, j) for j in range(epg)]

    def argmax_first(vals):
        bv, bi = vals[0], jnp.zeros_like(gidx)
        for j in range(1, epg):
            better = vals[j] > bv
            bi = jnp.where(better, j, bi)
            bv = jnp.where(better, vals[j], bv)
        return bi

    i1 = argmax_first(v)
    i2 = argmax_first([jnp.where(i1 == j, -jnp.inf, v[j]) for j in range(epg)])

    def take(vals, idx):
        out = vals[0]
        for j in range(1, epg):
            out = jnp.where(idx == j, vals[j], out)
        return out

    w1, w2 = take(u, i1), take(u, i2)
    tot = w1 + w2
    w1, w2 = w1 / tot, w2 / tot
    e1 = gidx * epg + i1
    e2 = gidx * epg + i2
    gates_ref[...] = jnp.concatenate(
        [jnp.where(e1 == e, w1, 0.0) + jnp.where(e2 == e, w2, 0.0) for e in range(N_EXPERTS)], axis=0)
    e_ref[...] = jnp.concatenate([e1, e2], axis=0)
    w_ref[...] = jnp.concatenate([w1, w2], axis=0)


def _route(x, g, mods, sc_chunk, sh_chunk, rows_per_b, tm, w_router_t, b_router, split):
    t, d = x.shape
    e = N_EXPERTS
    if split:
        h_spec = pl.BlockSpec((tm, d // LANES, LANES), lambda i, j: (i, 0, 0))
        h_shape = jax.ShapeDtypeStruct((t, d // LANES, LANES), BF16)
    else:
        h_spec = pl.BlockSpec((tm, d), lambda i, j: (i, 0))
        h_shape = jax.ShapeDtypeStruct((t, d), BF16)
    return pl.pallas_call(
        functools.partial(_route_kernel, split=split),
        grid=(t // tm, 1),
        in_specs=[
            pl.BlockSpec((tm, d), lambda i, j: (i, 0)),
            pl.BlockSpec((1, d), lambda i, j: (0, 0)),
            _mod_spec(mods, sc_chunk, tm, rows_per_b, d),
            _mod_spec(mods, sh_chunk, tm, rows_per_b, d),
            pl.BlockSpec((e, d), lambda i, j: (0, 0)),
            pl.BlockSpec((e, 1), lambda i, j: (0, 0)),
        ],
        out_specs=[h_spec, pl.BlockSpec((e, tm), lambda i, j: (0, i)),
                   pl.BlockSpec((TOP_K, tm), lambda i, j: (0, i)), pl.BlockSpec((TOP_K, tm), lambda i, j: (0, i))],
        out_shape=[h_shape, jax.ShapeDtypeStruct((e, t), F32),
                   jax.ShapeDtypeStruct((TOP_K, t), I32), jax.ShapeDtypeStruct((TOP_K, t), F32)],
        compiler_params=_params("arbitrary", "arbitrary"),
        name="moe_route",
    )(x, g.reshape(1, d), mods, mods, w_router_t, b_router.reshape(e, 1))


def _moe_kernel(h_ref, gates_ref, wg_ref, wu_ref, wd_ref, res_ref, gt_ref, o_ref, acc_ref):
    e = pl.program_id(1)
    first = (e == 0) & (pl.program_id(2) == 0)
    last = (e == pl.num_programs(1) - 1) & (pl.program_id(2) == pl.num_programs(2) - 1)

    @pl.when(first)
    def _():
        acc_ref[...] = jnp.zeros_like(acc_ref)

    h = h_ref[...]
    a = jnp.dot(h, wg_ref[...].astype(BF16), preferred_element_type=F32)
    b = jnp.dot(h, wu_ref[...].astype(BF16), preferred_element_type=F32)
    gates = gates_ref[...]
    lane = lax.broadcasted_iota(I32, gates.shape, 1)
    gcol = jnp.sum(jnp.where(lane == e, gates, 0.0), axis=1, keepdims=True)
    hid = (a * jax.nn.sigmoid(a)) * b * gcol
    acc_ref[...] += jnp.dot(hid.astype(BF16), wd_ref[...].astype(BF16), preferred_element_type=F32)

    @pl.when(last)
    def _():
        o_ref[...] = res_ref[...] + gt_ref[...] * acc_ref[...]


def _moe(h, gates, w_gate, w_up, w_down, layer, res, mods, gt_chunk, rows_per_b, tm, fsplit=2):
    t, d = h.shape
    e, _, f = w_gate.shape[1:]
    tf = f // fsplit
    r = mods.shape[1]
    return pl.pallas_call(
        _moe_kernel,
        grid=(t // tm, e, fsplit),
        in_specs=[
            pl.BlockSpec((tm, d), lambda i, j, k: (i, 0)),
            pl.BlockSpec((tm, e), lambda i, j, k: (i, 0)),
            pl.BlockSpec((None, None, d, tf), lambda i, j, k: (layer, j, 0, k)),
            pl.BlockSpec((None, None, d, tf), lambda i, j, k: (layer, j, 0, k)),
            pl.BlockSpec((None, None, tf, d), lambda i, j, k: (layer, j, k, 0)),
            pl.BlockSpec((tm, d), lambda i, j, k: (i, 0)),
            pl.BlockSpec((None, r, d), lambda i, j, k: ((i * tm) // rows_per_b, 0, gt_chunk)),
        ],
        out_specs=pl.BlockSpec((tm, d), lambda i, j, k: (i, 0)),
        out_shape=jax.ShapeDtypeStruct((t, d), F32),
        scratch_shapes=[pltpu.VMEM((tm, d), F32)],
        compiler_params=_params("arbitrary", "arbitrary", "arbitrary"),
        name="moe_experts",
    )(h, gates, w_gate, w_up, w_down, res, mods)


def _experts_kernel(te_ref, nu_ref, idxc_ref, idxn_ref, h_ref, wg_ref, wu_ref, wd_ref, o_ref,
                    wgb, wub, wdb, xbuf, sem, *, tg):
    i = pl.program_id(0)
    nu = nu_ref[0]
    slot = i % 2
    per_blk = SMEM_I32_BLOCK // tg

    def start_gather(idx_ref, tile, dst_slot):
        off = (tile % per_blk) * tg

        def issue(grp, carry):
            for u in range(DMA_GROUP):
                r = grp * DMA_GROUP + u
                pltpu.make_async_copy(h_ref.at[pl.ds(idx_ref[off + r], 1)], xbuf.at[dst_slot, pl.ds(r, 1)],
                                      sem.at[dst_slot]).start()
            return carry

        lax.fori_loop(0, tg // DMA_GROUP, issue, 0)

    @pl.when(i == 0)
    def _():
        start_gather(idxc_ref, i, slot)

    @pl.when(i + 1 < nu)
    def _():
        start_gather(idxn_ref, i + 1, 1 - slot)

    prev = te_ref[jnp.maximum(i - 1, 0)]

    @pl.when((i == 0) | (te_ref[i] != prev))
    def _():
        wgb[...] = wg_ref[...].astype(BF16)
        wub[...] = wu_ref[...].astype(BF16)
        wdb[...] = wd_ref[...].astype(BF16)

    @pl.when(i < nu)
    def _():
        def drain(r, carry):
            pltpu.make_async_copy(h_ref.at[pl.ds(0, 1)], xbuf.at[slot, pl.ds(r, 1)], sem.at[slot]).wait()
            return carry

        lax.fori_loop(0, tg, drain, 0, unroll=8)
        x = xbuf[slot].reshape(tg, wgb.shape[0])
        a = jnp.dot(x, wgb[...], preferred_element_type=F32)
        b = jnp.dot(x, wub[...], preferred_element_type=F32)
        hid = ((a * jax.nn.sigmoid(a)) * b).astype(BF16)
        y = jnp.dot(hid, wdb[...], preferred_element_type=F32)
        o_ref[...] = y.reshape(o_ref.shape)

    @pl.when(i >= nu)
    def _():
        o_ref[...] = jnp.zeros_like(o_ref)


def _experts(tile_expert, n_used, sorted_tok, h3, w_gate, w_up, w_down, layer, tg):
    rp = sorted_tok.shape[0]
    _, _, d, f = w_gate.shape
    blk = SMEM_I32_BLOCK
    per_blk = blk // tg
    last_blk = rp // blk - 1
    row_shape = h3.shape[1:]
    grid_spec = pltpu.PrefetchScalarGridSpec(
        num_scalar_prefetch=2,
        grid=(rp // tg,),
        in_specs=[
            pl.BlockSpec((blk,), lambda i, te, nu: (i // per_blk,), memory_space=pltpu.SMEM),
            pl.BlockSpec((blk,), lambda i, te, nu: (jnp.minimum((i + 1) // per_blk, last_blk),),
                         memory_space=pltpu.SMEM),
            pl.BlockSpec(memory_space=pl.ANY),
            pl.BlockSpec((None, None, d, f), lambda i, te, nu: (layer, te[i], 0, 0)),
            pl.BlockSpec((None, None, d, f), lambda i, te, nu: (layer, te[i], 0, 0)),
            pl.BlockSpec((None, None, f, d), lambda i, te, nu: (layer, te[i], 0, 0)),
        ],
        out_specs=pl.BlockSpec((tg,) + row_shape, lambda i, te, nu: (i, 0, 0)),
        scratch_shapes=[pltpu.VMEM((d, f), BF16), pltpu.VMEM((d, f), BF16), pltpu.VMEM((f, d), BF16),
                        pltpu.VMEM((2, tg) + row_shape, h3.dtype), pltpu.SemaphoreType.DMA((2,))],
    )
    return pl.pallas_call(
        functools.partial(_experts_kernel, tg=tg),
        grid_spec=grid_spec,
        out_shape=jax.ShapeDtypeStruct((rp,) + row_shape, F32),
        compiler_params=_params("arbitrary"),
        name="moe_routed_experts",
    )(tile_expert, n_used, sorted_tok, sorted_tok, h3, w_gate, w_up, w_down)


def _combine_kernel(pos_ref, ys_ref, w_ref, res_ref, gt_ref, *refs, tm, final):
    if final:
        gf_ref, o_ref, buf, sem = refs
    else:
        o_ref, buf, sem = refs

    def issue(grp, carry):
        for u in range(DMA_GROUP):
            r = grp * DMA_GROUP + u
            for k in range(TOP_K):
                pltpu.make_async_copy(ys_ref.at[pl.ds(pos_ref[k * tm + r], 1)], buf.at[k, pl.ds(r, 1)],
                                      sem).start(priority=(u + k) % 2)
        return carry

    lax.fori_loop(0, tm // DMA_GROUP, issue, 0)

    def drain(r, carry):
        for k in range(TOP_K):
            pltpu.make_async_copy(ys_ref.at[pl.ds(0, 1)], buf.at[k, pl.ds(r, 1)], sem).wait()
        return carry

    lax.fori_loop(0, tm, drain, 0, unroll=8)
    d = o_ref.shape[1]
    w = w_ref[...]
    m = (buf[0].reshape(tm, d) * jnp.tile(w[:, :LANES], (1, d // LANES))
         + buf[1].reshape(tm, d) * jnp.tile(w[:, LANES:], (1, d // LANES)))
    x = res_ref[...] + gt_ref[...] * m
    if final:
        ms = jnp.mean(x * x, axis=-1, keepdims=True)
        x = (x * lax.rsqrt(ms + EPS)) * gf_ref[...]
    o_ref[...] = x


def _combine(pos, ys, wtok, res, mods, gt_chunk, rows_per_b, tm, g_final=None):
    t, d = res.shape
    assert TOP_K * tm == SMEM_I32_BLOCK
    r = mods.shape[1]
    final = g_final is not None
    in_specs = [pl.BlockSpec((TOP_K * tm,), lambda i: (i,), memory_space=pltpu.SMEM),
                pl.BlockSpec(memory_space=pl.ANY),
                pl.BlockSpec((tm, TOP_K * LANES), lambda i: (i, 0)),
                pl.BlockSpec((tm, d), lambda i: (i, 0)),
                pl.BlockSpec((None, r, d), lambda i: ((i * tm) // rows_per_b, 0, gt_chunk))]
    args = [pos, ys, wtok, res, mods]
    if final:
        in_specs.append(pl.BlockSpec((1, d), lambda i: (0, 0)))
        args.append(g_final.reshape(1, d))
    return pl.pallas_call(
        functools.partial(_combine_kernel, tm=tm, final=final),
        grid=(t // tm,),
        in_specs=in_specs,
        out_specs=pl.BlockSpec((tm, d), lambda i: (i, 0)),
        out_shape=jax.ShapeDtypeStruct((t, d), F32),
        scratch_shapes=[pltpu.VMEM((TOP_K, tm) + ys.shape[1:], F32), pltpu.SemaphoreType.DMA(())],
        compiler_params=_params("arbitrary"),
        name="moe_combine",
    )(*args)


def _moe_routed(h3, e_rows, w_rows, w_gate, w_up, w_down, layer, res, mods, gt_chunk, rows_per_b,
                g_final=None, tg=512, tmc=512):
    t = h3.shape[0]
    ne = N_EXPERTS
    a = TOP_K * t
    rp = a + ne * tg
    rp = -(-rp // SMEM_I32_BLOCK) * SMEM_I32_BLOCK
    n_tiles = rp // tg
    ea = e_rows.reshape(a)
    ta = jnp.tile(jnp.arange(t, dtype=I32), TOP_K)
    onehot = (ea[:, None] == jnp.arange(ne, dtype=I32)[None, :]).astype(I32)
    csum = jnp.cumsum(onehot, axis=0)
    rank = jnp.sum((csum - onehot) * onehot, axis=1)
    counts = csum[-1]
    ptiles = (counts + tg - 1) // tg
    tile_end = jnp.cumsum(ptiles)
    tile_start = tile_end - ptiles
    n_used = tile_end[-1:].astype(I32)
    dest = jnp.sum(onehot * tile_start[None, :], axis=1) * tg + rank
    sorted_tok = jnp.zeros((rp,), I32).at[dest].set(ta, unique_indices=True)
    tile_expert = jnp.sum(jnp.arange(n_tiles, dtype=I32)[:, None] >= tile_end[None, :], axis=1)
    tile_expert = jnp.minimum(tile_expert, ne - 1).astype(I32)
    ys = _experts(tile_expert, n_used, sorted_tok, h3, w_gate, w_up, w_down, layer, tg)
    pos = dest.reshape(TOP_K, t // tmc, tmc).transpose(1, 0, 2).reshape(a)
    wtok = jnp.concatenate([jnp.broadcast_to(w_rows[k][:, None], (t, LANES)) for k in range(TOP_K)], axis=1)
    return _combine(pos, ys, wtok, res, mods, gt_chunk, rows_per_b, tmc, g_final)


def _kv_kernel(x_ref, g_ref, sc_ref, sh_ref, w_ref, gc_ref, cos_ref, sin_ref, ckv_ref, kpe_ref, wb_ref):
    @pl.when(pl.program_id(0) == 0)
    def _():
        wb_ref[...] = w_ref[...].astype(BF16)

    h = _normmod(x_ref[...], g_ref[...], sc_ref[...], sh_ref[...]).astype(BF16)
    lat = lax.dot_general(h, wb_ref[...], DN_LAST, preferred_element_type=F32)
    c = lat[:, :KV_LORA]
    ms = jnp.mean(c * c, axis=-1, keepdims=True)
    ckv_ref[...] = (c * lax.rsqrt(ms + EPS)) * gc_ref[...]
    kpe_ref[...] = _rope(lat[:, KV_LORA:], cos_ref[...], sin_ref[...])


def _kv(x, g_kv, mods, rows_per_b, tm, w_dkv_t, g_ckv, cos2, sin2):
    t, d = x.shape
    n = w_dkv_t.shape[0]
    nt = cos2.shape[0] // tm
    return pl.pallas_call(
        _kv_kernel,
        grid=(t // tm, 1),
        in_specs=[
            pl.BlockSpec((tm, d), lambda i, j: (i, 0)),
            pl.BlockSpec((1, d), lambda i, j: (0, 0)),
            _mod_spec(mods, 1, tm, rows_per_b, d),
            _mod_spec(mods, 0, tm, rows_per_b, d),
            pl.BlockSpec((n, d), lambda i, j: (0, 0)),
            pl.BlockSpec((1, KV_LORA), lambda i, j: (0, 0)),
            pl.BlockSpec((tm, QK_ROPE), lambda i, j: (i % nt, 0)),
            pl.BlockSpec((tm, QK_ROPE), lambda i, j: (i % nt, 0)),
        ],
        out_specs=[pl.BlockSpec((tm, KV_LORA), lambda i, j: (i, 0)),
                   pl.BlockSpec((tm, QK_ROPE), lambda i, j: (i, 0))],
        out_shape=[jax.ShapeDtypeStruct((t, KV_LORA), F32), jax.ShapeDtypeStruct((t, QK_ROPE), F32)],
        scratch_shapes=[pltpu.VMEM((n, d), BF16)],
        compiler_params=_params("arbitrary", "arbitrary"),
        name="latent_kv",
    )(x, g_kv.reshape(1, d), mods, mods, w_dkv_t, g_ckv.reshape(1, KV_LORA), cos2, sin2)


def _qproj_kernel(x_ref, g_ref, sc_ref, sh_ref, w_ref, cos_ref, sin_ref, q_ref, hb_ref, *, hb):
    @pl.when(pl.program_id(1) == 0)
    def _():
        hb_ref[...] = _normmod(x_ref[...], g_ref[...], sc_ref[...], sh_ref[...]).astype(BF16)

    acc = jnp.dot(hb_ref[...], w_ref[...].astype(BF16), preferred_element_type=F32)
    cos2, sin2 = cos_ref[...], sin_ref[...]
    for h in range(hb):
        c0 = h * QK_HEAD
        q_ref[h, :, :QK_NOPE] = acc[:, c0:c0 + QK_NOPE].astype(BF16)
        q_ref[h, :, QK_NOPE:] = _rope(acc[:, c0 + QK_NOPE:c0 + QK_HEAD], cos2, sin2).astype(BF16)


def _qproj(x, g, mods, sc_chunk, sh_chunk, rows_per_b, tm, w_q2, cos2, sin2, hb=4):
    t, d = x.shape
    nh = w_q2.shape[1] // QK_HEAD
    nt = rows_per_b // tm
    return pl.pallas_call(
        functools.partial(_qproj_kernel, hb=hb),
        grid=(t // tm, nh // hb),
        in_specs=[
            pl.BlockSpec((tm, d), lambda i, j: (i, 0)),
            pl.BlockSpec((1, d), lambda i, j: (0, 0)),
            _mod_spec(mods, sc_chunk, tm, rows_per_b, d),
            _mod_spec(mods, sh_chunk, tm, rows_per_b, d),
            pl.BlockSpec((d, hb * QK_HEAD), lambda i, j: (0, j)),
            pl.BlockSpec((tm, QK_ROPE), lambda i, j: (i % nt, 0)),
            pl.BlockSpec((tm, QK_ROPE), lambda i, j: (i % nt, 0)),
        ],
        out_specs=pl.BlockSpec((None, hb, tm, QK_HEAD), lambda i, j: (i // nt, j, i % nt, 0)),
        out_shape=jax.ShapeDtypeStruct((t // rows_per_b, nh, rows_per_b, QK_HEAD), BF16),
        scratch_shapes=[pltpu.VMEM((tm, d), BF16)],
        compiler_params=_params("arbitrary", "arbitrary"),
        name="q_proj",
    )(x, g.reshape(1, d), mods, mods, w_q2, cos2, sin2)


def _kvup_kernel(ckv_ref, kpe_ref, wk_ref, wv_ref, k_ref, v_ref, *, hb):
    c = ckv_ref[...].astype(BF16)
    kn = jnp.dot(c, wk_ref[...].astype(BF16), preferred_element_type=F32)
    vv = jnp.dot(c, wv_ref[...].astype(BF16), preferred_element_type=F32)
    kpe = kpe_ref[...].astype(BF16)
    for h in range(hb):
        k_ref[h, :, :QK_NOPE] = kn[:, h * QK_NOPE:(h + 1) * QK_NOPE].astype(BF16)
        k_ref[h, :, QK_NOPE:] = kpe
        v_ref[h] = vv[:, h * V_HEAD:(h + 1) * V_HEAD].astype(BF16)


def _kvup(ckv, kpe, w_uk2, w_uv2, bsz, seq, tm, hb=4):
    t, r = ckv.shape
    nt = seq // tm
    return pl.pallas_call(
        functools.partial(_kvup_kernel, hb=hb),
        grid=(t // tm, N_HEADS // hb),
        in_specs=[
            pl.BlockSpec((tm, r), lambda i, j: (i, 0)),
            pl.BlockSpec((tm, QK_ROPE), lambda i, j: (i, 0)),
            pl.BlockSpec((r, hb * QK_NOPE), lambda i, j: (0, j)),
            pl.BlockSpec((r, hb * V_HEAD), lambda i, j: (0, j)),
        ],
        out_specs=[pl.BlockSpec((None, hb, tm, QK_HEAD), lambda i, j: (i // nt, j, i % nt, 0)),
                   pl.BlockSpec((None, hb, tm, V_HEAD), lambda i, j: (i // nt, j, i % nt, 0))],
        out_shape=[jax.ShapeDtypeStruct((bsz, N_HEADS, seq, QK_HEAD), BF16),
                   jax.ShapeDtypeStruct((bsz, N_HEADS, seq, V_HEAD), BF16)],
        compiler_params=_params("arbitrary", "arbitrary"),
        name="kv_up",
    )(ckv, kpe, w_uk2, w_uv2)


def _flash_kernel(qt_ref, kt_ref, q_ref, k_ref, v_ref, o_ref, m_ref, l_ref, acc_ref, *, tq, tk, hb):
    p_idx = pl.program_id(2)
    qi = qt_ref[p_idx]
    kj = kt_ref[p_idx]

    @pl.when(kj == 0)
    def _():
        m_ref[...] = jnp.full_like(m_ref, -jnp.inf)
        l_ref[...] = jnp.zeros_like(l_ref)
        acc_ref[...] = jnp.zeros_like(acc_ref)

    def tile(h, masked):
        s = lax.dot_general(q_ref[h], k_ref[h], DN_LAST, preferred_element_type=F32) * (ATTN_SCALE * LOG2_E)
        if masked:
            qpos = lax.broadcasted_iota(I32, (tq, tk), 0)
            kpos = lax.broadcasted_iota(I32, (tq, tk), 1)
            s = jnp.where(kpos <= qpos, s, -jnp.inf)
        m_prev = m_ref[h]
        m_next = jnp.maximum(m_prev, jnp.max(s, axis=1, keepdims=True))
        alpha = jnp.exp2(m_prev - m_next)
        p = jnp.exp2(s - jnp.tile(m_next, (1, tk // LANES)))
        l_ref[h] = alpha * l_ref[h] + jnp.sum(p, axis=1, keepdims=True)
        acc_ref[h] = alpha * acc_ref[h] + jnp.dot(p.astype(BF16), v_ref[h], preferred_element_type=F32)
        m_ref[h] = m_next

    @pl.when(kj < qi)
    def _():
        for h in range(hb):
            tile(h, False)

    @pl.when(kj == qi)
    def _():
        for h in range(hb):
            tile(h, True)
            o_ref[:, h * V_HEAD:(h + 1) * V_HEAD] = (acc_ref[h] / l_ref[h]).astype(BF16)


def _flash(q, k, v, tile_len=512, hb=2):
    bsz, nh, seq, _ = q.shape
    tq = tk = tile_len
    nq = seq // tq
    pairs = [(i, j) for i in range(nq) for j in range(i + 1)]
    qt = jnp.asarray([p[0] for p in pairs], I32)
    kt = jnp.asarray([p[1] for p in pairs], I32)
    grid_spec = pltpu.PrefetchScalarGridSpec(
        num_scalar_prefetch=2,
        grid=(bsz, nh // hb, len(pairs)),
        in_specs=[
            pl.BlockSpec((None, hb, tq, QK_HEAD), lambda b, h, p, qt, kt: (b, h, qt[p], 0)),
            pl.BlockSpec((None, hb, tk, QK_HEAD), lambda b, h, p, qt, kt: (b, h, kt[p], 0)),
            pl.BlockSpec((None, hb, tk, V_HEAD), lambda b, h, p, qt, kt: (b, h, kt[p], 0)),
        ],
        out_specs=pl.BlockSpec((tq, hb * V_HEAD), lambda b, h, p, qt, kt: (b * nq + qt[p], h)),
        scratch_shapes=[pltpu.VMEM((hb, tq, LANES), F32), pltpu.VMEM((hb, tq, LANES), F32),
                        pltpu.VMEM((hb, tq, V_HEAD), F32)],
    )
    return pl.pallas_call(
        functools.partial(_flash_kernel, tq=tq, tk=tk, hb=hb),
        grid_spec=grid_spec,
        out_shape=jax.ShapeDtypeStruct((bsz * seq, nh * V_HEAD), BF16),
        compiler_params=_params("arbitrary", "arbitrary", "arbitrary"),
        name="flash_attn",
    )(qt, kt, q, k, v)


def _qlat_kernel(q_ref, wk_ref, o_ref):
    q = q_ref[...]
    qlat = lax.dot_general(q[:, :QK_NOPE], wk_ref[...].astype(BF16), DN_LAST, preferred_element_type=F32)
    o_ref[:, :KV_LORA] = qlat.astype(BF16)
    o_ref[:, KV_LORA:] = q[:, QK_NOPE:]


def _qlat(q, w_uk2):
    nh, b, _ = q.shape
    r = w_uk2.shape[0]
    return pl.pallas_call(
        _qlat_kernel,
        grid=(nh,),
        in_specs=[pl.BlockSpec((None, b, QK_HEAD), lambda h: (h, 0, 0)),
                  pl.BlockSpec((r, QK_NOPE), lambda h: (0, h))],
        out_specs=pl.BlockSpec((None, b, r + QK_ROPE), lambda h: (h, 0, 0)),
        out_shape=jax.ShapeDtypeStruct((nh, b, r + QK_ROPE), BF16),
        compiler_params=_params("arbitrary"),
        name="q_latent",
    )(q, w_uk2)


def _paged_kernel(pt_ref, q_ref, ckv_hbm, kpe_hbm, cnew_ref, pnew_ref, o_ref, qpad_ref, kcb_ref, pp_ref,
                  m_ref, l_ref, acc_ref, pgc_ref, pgk_ref, semc, semk, *, npg, nsteps, nrows, rb):
    bi = pl.program_id(0)
    j = pl.program_id(1)
    nh = N_HEADS
    wide = KV_LORA // LANES
    first = j == 0

    g_step = bi * nsteps + j
    n_total = (nrows // rb) * nsteps

    def start_pages(gs):
        slot = gs % PAGE_SLOTS
        row_blk = gs // nsteps
        step = gs % nsteps
        for r in range(rb):
            for i in range(npg):
                pg = pt_ref[row_blk * rb + r, step * npg + i]
                pltpu.make_async_copy(ckv_hbm.at[pg], pgc_ref.at[slot, r * npg + i], semc.at[slot]).start()
                pltpu.make_async_copy(kpe_hbm.at[pg], pgk_ref.at[slot, r * npg + i], semk.at[slot]).start()

    @pl.when(g_step == 0)
    def _():
        for ahead in range(PAGE_SLOTS - 1):
            start_pages(g_step + ahead)

    @pl.when(first & (bi == 0))
    def _():
        kcb_ref[...] = jnp.zeros_like(kcb_ref)
        pp_ref[...] = jnp.zeros_like(pp_ref)
        m_ref[...] = jnp.zeros_like(m_ref)
        l_ref[...] = jnp.zeros_like(l_ref)
        acc_ref[...] = jnp.zeros_like(acc_ref)

    @pl.when(first)
    def _():
        qpad_ref[...] = jnp.zeros_like(qpad_ref)
        for r in range(rb):
            qpad_ref[r, :nh, :] = q_ref[r]

    def step(cur, prv):
        @pl.when(g_step + (PAGE_SLOTS - 1) < n_total)
        def _():
            start_pages(g_step + (PAGE_SLOTS - 1))

        pslot = g_step % PAGE_SLOTS
        for k in range(rb * npg):
            pltpu.make_async_copy(ckv_hbm.at[0], pgc_ref.at[pslot, k], semc.at[pslot]).wait()
            pltpu.make_async_copy(kpe_hbm.at[0], pgk_ref.at[pslot, k], semk.at[pslot]).wait()
        for r in range(rb):
            pv_prev = jnp.dot(pp_ref[r], kcb_ref[r, prv], preferred_element_type=F32)
            acc_in = jnp.where(first, 0.0, acc_ref[r] + pv_prev)
            m_prev = jnp.where(first, -jnp.inf, m_ref[r])
            l_prev = jnp.where(first, 0.0, l_ref[r])
            for i in range(npg):
                kcb_ref[r, cur, i * PAGE_SIZE:(i + 1) * PAGE_SIZE, :] = pgc_ref[pslot, r * npg + i].astype(BF16)
            s_c = lax.dot_general(kcb_ref[r, cur], qpad_ref[r, :, :KV_LORA], DN_LAST,
                                  preferred_element_type=F32)
            kp = jnp.concatenate([pgk_ref[pslot, r * npg + i].astype(BF16) for i in range(npg)],
                                 axis=1)
            s_p = jnp.dot(qpad_ref[r, :, KV_LORA:], kp, preferred_element_type=F32)
            s = (s_c.T[:nh, :] + s_p[:nh, :]) * ATTN_SCALE
            m_next = jnp.maximum(m_prev, jnp.max(s, axis=1, keepdims=True))
            alpha = jnp.exp(m_prev - m_next)
            p = jnp.exp(s - jnp.tile(m_next, (1, s.shape[1] // LANES)))
            l_ref[r] = alpha * l_prev + jnp.sum(p, axis=1, keepdims=True)
            acc_ref[r] = jnp.tile(alpha, (1, wide)) * acc_in
            m_ref[r] = m_next
            pp_ref[r] = p.astype(BF16)

    @pl.when(j % 2 == 0)
    def _():
        step(0, 1)

    @pl.when(j % 2 == 1)
    def _():
        step(1, 0)

    @pl.when(j == nsteps - 1)
    def _():
        last_slot = (nsteps - 1) % 2
        for r in range(rb):
            acc_past = acc_ref[r] + jnp.dot(pp_ref[r], kcb_ref[r, last_slot], preferred_element_type=F32)
            q = q_ref[r].astype(F32)
            cnew = cnew_ref[r]
            s_new = (jnp.sum(q[:, :KV_LORA] * cnew, axis=1, keepdims=True)
                     + jnp.sum(q[:, KV_LORA:] * pnew_ref[r], axis=1, keepdims=True)) * ATTN_SCALE
            m_p = m_ref[r]
            m_n = jnp.maximum(m_p, s_new)
            a = jnp.exp(m_p - m_n)
            p_new = jnp.exp(s_new - m_n)
            l_fin = a * l_ref[r] + p_new
            acc = jnp.tile(a, (1, wide)) * acc_past + jnp.tile(p_new, (1, wide)) * cnew
            o_ref[r] = (acc / jnp.tile(l_fin, (1, wide))).astype(BF16)


def _paged_attn(qcat, cache_ckv, cache_kpe_t, page_table, ckv_new, kpe_new, npg=8, rb=2):
    b, nh, w = qcat.shape
    r = KV_LORA
    n_pages = page_table.shape[1]
    keys = npg * PAGE_SIZE
    nsteps = n_pages // npg
    assert (b // rb) * nsteps >= PAGE_SLOTS - 1
    grid_spec = pltpu.PrefetchScalarGridSpec(
        num_scalar_prefetch=1,
        grid=(b // rb, nsteps),
        in_specs=[pl.BlockSpec((rb, nh, w), lambda bi, j, pt: (bi, 0, 0)),
                  pl.BlockSpec(memory_space=pl.ANY),
                  pl.BlockSpec(memory_space=pl.ANY),
                  pl.BlockSpec((rb, 1, r), lambda bi, j, pt: (bi, 0, 0)),
                  pl.BlockSpec((rb, 1, QK_ROPE), lambda bi, j, pt: (bi, 0, 0))],
        out_specs=pl.BlockSpec((rb, nh, r), lambda bi, j, pt: (bi, 0, 0)),
        scratch_shapes=[pltpu.VMEM((rb, LANES, w), BF16), pltpu.VMEM((rb, 2, keys, r), BF16),
                        pltpu.VMEM((rb, nh, keys), BF16), pltpu.VMEM((rb, nh, LANES), F32),
                        pltpu.VMEM((rb, nh, LANES), F32), pltpu.VMEM((rb, nh, r), F32),
                        pltpu.VMEM((PAGE_SLOTS, rb * npg, PAGE_SIZE, r), F32),
                        pltpu.VMEM((PAGE_SLOTS, rb * npg, QK_ROPE, PAGE_SIZE), F32),
                        pltpu.SemaphoreType.DMA((PAGE_SLOTS,)), pltpu.SemaphoreType.DMA((PAGE_SLOTS,))],
    )
    return pl.pallas_call(
        functools.partial(_paged_kernel, npg=npg, nsteps=nsteps, nrows=b, rb=rb),
        grid_spec=grid_spec,
        out_shape=jax.ShapeDtypeStruct((b, nh, r), BF16),
        compiler_params=_params("arbitrary", "arbitrary"),
        name="paged_attn",
    )(page_table, qcat, cache_ckv, cache_kpe_t, ckv_new.reshape(b, 1, r), kpe_new.reshape(b, 1, QK_ROPE))


def _ovup_kernel(o_ref, wv_ref, a_ref):
    a_ref[...] = jnp.dot(o_ref[...], wv_ref[...].astype(BF16), preferred_element_type=F32).astype(BF16)


def _ovup(olat, w_uv2):
    nh, b, r = olat.shape
    return pl.pallas_call(
        _ovup_kernel,
        grid=(nh,),
        in_specs=[pl.BlockSpec((None, b, r), lambda h: (h, 0, 0)), pl.BlockSpec((r, V_HEAD), lambda h: (0, h))],
        out_specs=pl.BlockSpec((b, V_HEAD), lambda h: (0, h)),
        out_shape=jax.ShapeDtypeStruct((b, nh * V_HEAD), BF16),
        compiler_params=_params("arbitrary"),
        name="o_latent_up",
    )(olat, w_uv2)


def _rope_tables(pos):
    inv = ROPE_THETA ** (-jnp.arange(0, QK_ROPE, 2, dtype=F32) / QK_ROPE)
    ang = pos.astype(F32)[:, None] * inv[None, :]
    cos, sin = jnp.cos(ang), jnp.sin(ang)
    return jnp.concatenate([cos, cos], axis=1), jnp.concatenate([-sin, sin], axis=1)


def _trunk(x, mods, mods_kv, rows_per_b, tm, cos2, sin2, P, s5_glu_fn, attn_fn, routed):
    tms = min(tm, 512)

    def moe_block(x, layer, g_final=None):
        h2, gates_t, e_rows, w_rows = _route(x, P["g_norm"][layer, 1], mods[layer], 4, 3, rows_per_b, tms,
                                             P["w_router_t"], P["b_router"], split=routed)
        if routed:
            return _moe_routed(h2, e_rows, w_rows, P["w_gate"], P["w_up"], P["w_down"], layer, x,
                               mods[layer], 5, rows_per_b, g_final)
        x = _moe(h2, gates_t.T, P["w_gate"], P["w_up"], P["w_down"], layer, x, mods[layer], 5,
                 rows_per_b, tms)
        return x if g_final is None else _final_norm(x, g_final, tms)

    x, st_re, st_im = s5_glu_fn(x)
    x = moe_block(x, 0)
    ckv, kpe = _kv(x, P["g_kv"], mods_kv, rows_per_b, tms, P["w_dkv_t"], P["g_ckv"], cos2, sin2)
    q = _qproj(x, P["g_norm"][1, 0], mods[1], 1, 0, rows_per_b, tm, P["w_q2"], cos2, sin2)
    attn = attn_fn(q, ckv, kpe)
    x = _proj_res(attn, P["w_o"][0], x, mods[1], 2, rows_per_b, tm, 512, glu=False)
    y = moe_block(x, 1, P["g_final"])
    return y, st_re, st_im, ckv, kpe


def kernel(x_prompt, x_sample, c_prompt, c_sample, state_ssm_re, state_ssm_im, cache_ckv, cache_kpe, page_table, g_norm, w_ada, b_ada, ssm_lam_re, ssm_lam_im, ssm_log_dt, ssm_b_re, ssm_b_im, ssm_c_re, ssm_c_im, ssm_d, w_glu, g_kv, w_ada_kv, b_ada_kv, w_dkv, g_ckv, w_uk, w_uv, w_q, w_o, w_router, b_router, w_gate, w_up, w_down, g_final):
    bsz, seq, d = x_prompt.shape
    db = x_sample.shape[0]
    g = d // SSM_GROUP
    n = SSM_STATE
    assert g == LANES, "the group-major relayout puts one S5 group per lane"
    past = page_table.shape[1] * PAGE_SIZE
    P = dict(g_norm=g_norm, g_kv=g_kv, g_ckv=g_ckv, w_o=w_o, b_router=b_router,
             w_gate=w_gate, w_up=w_up, w_down=w_down, g_final=g_final)
    P["w_q2"] = w_q[0]
    P["w_dkv_t"] = w_dkv.T
    P["w_router_t"] = w_router.T
    w_uk2 = w_uk.reshape(KV_LORA, N_HEADS * QK_NOPE)
    w_uv2 = w_uv.reshape(KV_LORA, N_HEADS * V_HEAD)

    m_rows = db + bsz
    m_pad = -(-m_rows // 16) * 16
    c_all = jnp.concatenate([c_sample, c_prompt, jnp.zeros((m_pad - m_rows, d), F32)], axis=0)
    mod = _ada(c_all, w_ada, b_ada)
    mod_kv = _ada(c_all, w_ada_kv[None], b_ada_kv[None])
    mods_s = [mod[l, :db][None] for l in range(mod.shape[0])]
    mods_p = [mod[l, db:m_rows][:, None, :] for l in range(mod.shape[0])]
    modkv_s = mod_kv[0, :db][None]
    modkv_p = mod_kv[0, db:m_rows][:, None, :]

    lam_re = ssm_lam_re[0].reshape(g, 1, n)
    lam_im = ssm_lam_im[0].reshape(g, 1, n)
    log_dt = jnp.broadcast_to(ssm_log_dt[0].reshape(g, 1, 1), (g, 1, n))
    bt_re = ssm_b_re[0].transpose(0, 2, 1)
    bt_im = ssm_b_im[0].transpose(0, 2, 1)
    c_re, c_im = ssm_c_re[0], ssm_c_im[0]
    d_g = ssm_d[0].reshape(g, 1, SSM_GROUP)
    d_rep = jnp.tile(d_g, (1, 1, S5_CHUNK))
    dup = lambda a: jnp.concatenate([a, a], axis=-1)
    cat = lambda a, b: jnp.concatenate([a, b], axis=-1)
    nchunk = seq // S5_CHUNK
    tm_p = 1024

    src = jnp.arange(d, dtype=I32)
    dst = (src % SSM_GROUP) * g + src // SSM_GROUP
    perm = (dst[:, None] == jnp.arange(d, dtype=I32)[None, :]).astype(BF16)
    w_glu_p = w_glu[0].reshape(g, SSM_GROUP, -1).transpose(1, 0, 2).reshape(d, -1)

    def s5_glu_prompt(x):
        hx = _s5_in(x, g_norm[0, 0], mods_p[0], 1, 0, seq, 256, perm)
        hx = hx.reshape(g, bsz * nchunk, S5_CHUNK * SSM_GROUP)
        zg, st = _s5_prompt(hx, dup(lam_re), dup(lam_im), dup(log_dt), cat(bt_re, bt_im), cat(bt_im, bt_re),
                            cat(c_re, c_im), cat(c_im, c_re), d_rep, bsz)
        zg = zg.reshape(g, bsz * seq * SSM_GROUP)
        x = _proj_res(zg, w_glu_p, x, mods_p[0], 2, seq, tm_p, 256, glu=True, grouped=True)
        st = st.transpose(1, 0, 2)[None]
        return x, st[..., :n], st[..., n:]

    def s5_glu_sample(x):
        h = _normmod_call(x, g_norm[0, 0], mods_s[0], 1, 0, db, db, F32)
        u = h.reshape(db, g, SSM_GROUP).transpose(1, 0, 2)
        h0_re = state_ssm_re[0].transpose(1, 0, 2)
        h0_im = state_ssm_im[0].transpose(1, 0, 2)
        zg, nr, ni = _s5_sample(u, h0_re, h0_im, lam_re, lam_im, log_dt, bt_re, bt_im, c_re, c_im, d_g)
        z = zg.transpose(1, 0, 2).reshape(db, d)
        x = _proj_res(z, w_glu[0], x, mods_s[0], 2, db, db, 256, glu=True)
        return x, nr.transpose(1, 0, 2)[None], ni.transpose(1, 0, 2)[None]

    def attn_prompt(q, ckv, kpe):
        k, v = _kvup(ckv, kpe, w_uk2, w_uv2, bsz, seq, tm_p)
        return _flash(q, k, v)

    def attn_sample(q, ckv, kpe):
        qcat = _qlat(q[0], w_uk2).transpose(1, 0, 2)
        olat = _paged_attn(qcat, cache_ckv, cache_kpe.transpose(0, 2, 1), page_table, ckv, kpe)
        return _ovup(olat.transpose(1, 0, 2), w_uv2)

    cos_p, sin_p = _rope_tables(jnp.arange(seq))
    cos_s, sin_s = _rope_tables(jnp.full((db,), past))

    y_p, sre_p, sim_p, ckv_p, kpe_p = _trunk(
        x_prompt.reshape(bsz * seq, d), mods_p, modkv_p, seq, tm_p, cos_p, sin_p, P,
        s5_glu_prompt, attn_prompt, routed=True)
    y_s, sre_s, sim_s, ckv_s, kpe_s = _trunk(
        x_sample.reshape(db, d), mods_s, modkv_s, db, db, cos_s, sin_s, P,
        s5_glu_sample, attn_sample, routed=False)
    return (y_p.reshape(bsz, seq, d), y_s.reshape(db, 1, d), sre_p, sim_p,
            ckv_p.reshape(bsz, seq, KV_LORA), kpe_p.reshape(bsz, seq, QK_ROPE),
            sre_s, sim_s, ckv_s.reshape(db, 1, KV_LORA), kpe_s.reshape(db, 1, QK_ROPE))
```

```python
import functools
import math

import jax
import jax.numpy as jnp
from jax import lax
from jax.experimental import pallas as pl
from jax.experimental.pallas import tpu as pltpu

F32 = jnp.float32
BF16 = jnp.bfloat16
I32 = jnp.int32

SSM_GROUP = 16
SSM_STATE = 64
N_HEADS = 16
QK_NOPE = 128
QK_ROPE = 64
V_HEAD = 128
KV_LORA = 512
ROPE_THETA = 10000.0
PAGE_SIZE = 128
N_EXPERTS = 16
N_EXPERT_GROUPS = 4
EXPERTS_PER_GROUP = N_EXPERTS // N_EXPERT_GROUPS
TOP_K = 2
EPS = 1e-6
ATTN_SCALE = (QK_NOPE + QK_ROPE) ** -0.5
QK_HEAD = QK_NOPE + QK_ROPE
LOG2_E = math.log2(math.e)

V7X_VMEM_BYTES = 64 * 1024 * 1024
VMEM_LIMIT_BYTES = V7X_VMEM_BYTES - 8 * 1024 * 1024
LANES = 128
SMEM_I32_BLOCK = 1024
DMA_GROUP = 8
PAGE_SLOTS = 3
S5_CHUNK = 16
DN_LAST = (((1,), (1,)), ((), ()))


def _params(*sem):
    return pltpu.CompilerParams(dimension_semantics=sem, vmem_limit_bytes=VMEM_LIMIT_BYTES)


def _normmod(x, g, sc, sh):
    ms = jnp.mean(x * x, axis=-1, keepdims=True)
    return (x * lax.rsqrt(ms + EPS)) * g * (1.0 + sc) + sh


def _rope(p, cos2, sin2):
    half = QK_ROPE // 2
    swapped = jnp.concatenate([p[:, half:], p[:, :half]], axis=1)
    return p * cos2 + swapped * sin2


def _mod_spec(mods, chunk, tm, rows_per_b, width, j_blocks=None):
    _, r, _ = mods.shape
    if j_blocks is None:
        return pl.BlockSpec((None, r, width), lambda i, j: ((i * tm) // rows_per_b, 0, chunk))
    return pl.BlockSpec((None, r, width), lambda i, j: ((i * tm) // rows_per_b, 0, chunk * j_blocks + j))


def _ada_kernel(c_ref, w_ref, b_ref, o_ref, cb_ref):
    @pl.when((pl.program_id(0) == 0) & (pl.program_id(1) == 0))
    def _():
        c = c_ref[...]
        cb_ref[...] = (c * jax.nn.sigmoid(c)).astype(BF16)

    acc = jnp.dot(cb_ref[...], w_ref[...].astype(BF16), preferred_element_type=F32)
    o_ref[...] = acc + b_ref[...]


def _ada(c_all, w, b, tn=1024):
    m, d = c_all.shape
    lw, _, n = w.shape
    return pl.pallas_call(
        _ada_kernel,
        grid=(lw, n // tn),
        in_specs=[
            pl.BlockSpec((m, d), lambda l, j: (0, 0)),
            pl.BlockSpec((None, d, tn), lambda l, j: (l, 0, j)),
            pl.BlockSpec((None, 1, tn), lambda l, j: (l, 0, j)),
        ],
        out_specs=pl.BlockSpec((None, m, tn), lambda l, j: (l, 0, j)),
        out_shape=jax.ShapeDtypeStruct((lw, m, n), F32),
        scratch_shapes=[pltpu.VMEM((m, d), BF16)],
        compiler_params=_params("arbitrary", "arbitrary"),
        name="ada_mod",
    )(c_all, w, b.reshape(lw, 1, n))


def _normmod_kernel(x_ref, g_ref, sc_ref, sh_ref, o_ref):
    o_ref[...] = _normmod(x_ref[...], g_ref[...], sc_ref[...], sh_ref[...]).astype(o_ref.dtype)


def _normmod_call(x, g, mods, sc_chunk, sh_chunk, rows_per_b, tm, out_dtype):
    t, d = x.shape
    return pl.pallas_call(
        _normmod_kernel,
        grid=(t // tm, 1),
        in_specs=[
            pl.BlockSpec((tm, d), lambda i, j: (i, 0)),
            pl.BlockSpec((1, d), lambda i, j: (0, 0)),
            _mod_spec(mods, sc_chunk, tm, rows_per_b, d),
            _mod_spec(mods, sh_chunk, tm, rows_per_b, d),
        ],
        out_specs=pl.BlockSpec((tm, d), lambda i, j: (i, 0)),
        out_shape=jax.ShapeDtypeStruct((t, d), out_dtype),
        compiler_params=_params("arbitrary", "arbitrary"),
        name="norm_mod",
    )(x, g.reshape(1, d), mods, mods)


def _final_norm_kernel(x_ref, g_ref, o_ref):
    x = x_ref[...]
    ms = jnp.mean(x * x, axis=-1, keepdims=True)
    o_ref[...] = (x * lax.rsqrt(ms + EPS)) * g_ref[...]


def _final_norm(x, g, tm):
    t, d = x.shape
    return pl.pallas_call(
        _final_norm_kernel,
        grid=(t // tm,),
        in_specs=[pl.BlockSpec((tm, d), lambda i: (i, 0)), pl.BlockSpec((1, d), lambda i: (0, 0))],
        out_specs=pl.BlockSpec((tm, d), lambda i: (i, 0)),
        out_shape=jax.ShapeDtypeStruct((t, d), F32),
        compiler_params=_params("arbitrary"),
        name="final_norm",
    )(x, g.reshape(1, d))


def _proj_res_kernel(x_ref, *refs, glu, grouped, tm):
    refs = list(refs)
    xb_ref = refs.pop() if grouped else None
    if glu:
        wa_ref, wb_ref, res_ref, gt_ref, o_ref = refs
    else:
        wa_ref, res_ref, gt_ref, o_ref = refs
    if grouped:
        @pl.when(pl.program_id(1) == 0)
        def _():
            cg = SSM_GROUP
            for k in range(tm // LANES):
                zt = x_ref[:, k * LANES * cg:(k + 1) * LANES * cg].T
                xb_ref[k * LANES:(k + 1) * LANES, :] = zt.reshape(LANES, cg, LANES).reshape(LANES, cg * LANES)

        x = xb_ref[...]
    else:
        x = x_ref[...]
    y = jnp.dot(x, wa_ref[...].astype(BF16), preferred_element_type=F32)
    if glu:
        yb = jnp.dot(x, wb_ref[...].astype(BF16), preferred_element_type=F32)
        y = y * jax.nn.sigmoid(yb)
    o_ref[...] = res_ref[...] + gt_ref[...] * y


def _proj_res(x, w, res, mods, gt_chunk, rows_per_b, tm, tn, glu, grouped=False):
    t, n = res.shape
    k = w.shape[0]
    nj = n // tn
    if grouped:
        x_spec = pl.BlockSpec((x.shape[0], tm * SSM_GROUP), lambda i, j: (0, i))
    else:
        x_spec = pl.BlockSpec((tm, k), lambda i, j: (i, 0))
    in_specs = [x_spec, pl.BlockSpec((k, tn), lambda i, j: (0, j))]
    args = [x, w]
    if glu:
        in_specs.append(pl.BlockSpec((k, tn), lambda i, j: (0, nj + j)))
        args.append(w)
    in_specs += [pl.BlockSpec((tm, tn), lambda i, j: (i, j)), _mod_spec(mods, gt_chunk, tm, rows_per_b, tn, nj)]
    args += [res, mods]
    return pl.pallas_call(
        functools.partial(_proj_res_kernel, glu=glu, grouped=grouped, tm=tm),
        grid=(t // tm, nj),
        in_specs=in_specs,
        out_specs=pl.BlockSpec((tm, tn), lambda i, j: (i, j)),
        out_shape=jax.ShapeDtypeStruct((t, n), F32),
        scratch_shapes=[pltpu.VMEM((tm, k), BF16)] if grouped else [],
        compiler_params=_params("arbitrary", "arbitrary"),
        name="glu_proj" if glu else "out_proj",
    )(*args)


def _s5in_kernel(x_ref, g_ref, sc_ref, sh_ref, p_ref, o_ref):
    h = _normmod(x_ref[...], g_ref[...], sc_ref[...], sh_ref[...]).astype(BF16)
    hp = jnp.dot(h, p_ref[...], preferred_element_type=F32).astype(BF16)
    tm = hp.shape[0]
    o_ref[...] = hp.reshape(tm, SSM_GROUP, LANES).reshape(tm * SSM_GROUP, LANES).T


def _s5_in(x, g, mods, sc_chunk, sh_chunk, rows_per_b, tm, perm):
    t, d = x.shape
    ng = d // SSM_GROUP
    return pl.pallas_call(
        _s5in_kernel,
        grid=(t // tm, 1),
        in_specs=[
            pl.BlockSpec((tm, d), lambda i, j: (i, 0)),
            pl.BlockSpec((1, d), lambda i, j: (0, 0)),
            _mod_spec(mods, sc_chunk, tm, rows_per_b, d),
            _mod_spec(mods, sh_chunk, tm, rows_per_b, d),
            pl.BlockSpec((d, d), lambda i, j: (0, 0)),
        ],
        out_specs=pl.BlockSpec((ng, tm * SSM_GROUP), lambda i, j: (0, i)),
        out_shape=jax.ShapeDtypeStruct((ng, t * SSM_GROUP), BF16),
        compiler_params=_params("arbitrary", "arbitrary"),
        name="s5_in",
    )(x, g.reshape(1, d), mods, mods, perm)


def _cpow(lr_dt, li_dt, k):
    mag = jnp.exp(k * lr_dt)
    ang = k * li_dt
    return mag * jnp.cos(ang), mag * jnp.sin(ang)


def _rep_rows(e, n):
    q, w = e.shape
    return jnp.concatenate([jnp.broadcast_to(e[t:t + 1], (n, w)) for t in range(q)], axis=0)


def _tile_rows(c, n):
    return jnp.concatenate([c] * n, axis=0)


def _zoh(lr, li, ldt):
    dt = jnp.exp(ldt)
    lr_dt = lr * dt
    li_dt = li * dt
    mag = jnp.exp(lr_dt)
    ab_re = mag * jnp.cos(li_dt)
    ab_im = mag * jnp.sin(li_dt)
    den = lr * lr + li * li
    f_re = ((ab_re - 1.0) * lr + ab_im * li) / den
    f_im = (ab_im * lr - (ab_re - 1.0) * li) / den
    return lr_dt, li_dt, ab_re, ab_im, f_re, f_im


def _gelu_tanh(y):
    return 0.5 * y * (1.0 + jnp.tanh(math.sqrt(2.0 / math.pi) * (y + 0.044715 * (y * y * y))))


def _s5_prompt_kernel(x_ref, lr_ref, li_ref, ldt_ref, bt_ref, bts_ref, c_ref, cs_ref, d_ref,
                      z_ref, st_ref, s_sc, ss_sc, hp_sc, yl_sc, *, gb, nb, nchunk):
    q, cg, n = S5_CHUNK, SSM_GROUP, SSM_STATE
    half = q // 2
    ti = lax.broadcasted_iota(I32, (q, 1), 0).astype(F32)
    rows = lax.broadcasted_iota(I32, (q * cg, q * cg), 0)
    cols = lax.broadcasted_iota(I32, (q * cg, q * cg), 1)
    shift = cg.bit_length() - 1
    causal = (rows >> shift) <= (cols >> shift)
    diag = rows == cols
    lane = lax.broadcasted_iota(I32, (1, 2 * n), 1)
    sgn = jnp.where(lane < n, -1.0, 1.0)

    def cmul(x2, xs, er, ei):
        return x2 * er + (xs * sgn) * ei

    a_terms = []
    vcs = []
    for g in range(gb):
        lr_dt, li_dt, _, _, f_re, f_im = _zoh(lr_ref[g], li_ref[g], ldt_ref[g])
        bt2, bts = bt_ref[g], bts_ref[g]
        bb2 = cmul(bt2, bts, f_re, f_im)
        bbs = bts * f_re - (bt2 * sgn) * f_im
        bb2t, bbst = _tile_rows(bb2, q), _tile_rows(bbs, q)
        c2t, cst = _tile_rows(c_ref[g], q), _tile_rows(cs_ref[g], q)

        def table(k, lr_dt=lr_dt, li_dt=li_dt):
            er, ei = _cpow(lr_dt, li_dt, k)
            return _rep_rows(er, cg), _rep_rows(ei, cg)

        pc = cmul(c2t, cst, *table(ti - half)) * (-sgn)
        pb = cmul(bb2t, bbst, *table(half - ti))
        tt = lax.dot_general(pb.astype(BF16), pc.astype(BF16), DN_LAST, preferred_element_type=F32)
        tt = jnp.where(causal, tt, 0.0) + jnp.where(diag, d_ref[g], 0.0)
        w1 = cmul(bb2t, bbst, *table((q - 1) - ti))
        vcs.append((cmul(c2t, cst, *table(ti + 1.0)) * (-sgn)).astype(BF16))
        ar, ai = _cpow(lr_dt, li_dt, float(q))
        a_terms.append((ar, ai * sgn))
        x = x_ref[g]
        yl_sc[g] = jnp.dot(x, tt.astype(BF16), preferred_element_type=F32)
        s = jnp.dot(x, w1.astype(BF16), preferred_element_type=F32)
        s_sc[g] = s
        ss_sc[g] = pltpu.roll(s, n, axis=1)

    def carry(ch, hs):
        new = []
        for g in range(gb):
            h, hsw = hs[2 * g], hs[2 * g + 1]
            hp_sc[g, pl.ds(ch, nb, stride=nchunk), :] = h
            a1, a2 = a_terms[g]
            new.append(a1 * h + a2 * hsw + s_sc[g, pl.ds(ch, nb, stride=nchunk), :])
            new.append(a1 * hsw - a2 * h + ss_sc[g, pl.ds(ch, nb, stride=nchunk), :])
        return tuple(new)

    zero = jnp.zeros((nb, 2 * n), F32)
    hs = lax.fori_loop(0, nchunk, carry, (zero,) * (2 * gb), unroll=8)
    for g in range(gb):
        st_ref[g] = hs[2 * g]
        y = yl_sc[g] + lax.dot_general(hp_sc[g].astype(BF16), vcs[g], DN_LAST, preferred_element_type=F32)
        z_ref[g] = _gelu_tanh(y).astype(BF16)


def _s5_prompt(hx, lam_re2, lam_im2, log_dt2, bt2, bts, c2, cs, d_rep, nb, gb=2):
    g, r, w = hx.shape
    n2 = 2 * SSM_STATE
    nchunk = r // nb
    vec = lambda: pl.BlockSpec((gb, 1, n2), lambda i: (i, 0, 0))
    mat = lambda: pl.BlockSpec((gb, SSM_GROUP, n2), lambda i: (i, 0, 0))
    return pl.pallas_call(
        functools.partial(_s5_prompt_kernel, gb=gb, nb=nb, nchunk=nchunk),
        grid=(g // gb,),
        in_specs=[pl.BlockSpec((gb, r, w), lambda i: (i, 0, 0)), vec(), vec(), vec(),
                  mat(), mat(), mat(), mat(), pl.BlockSpec((gb, 1, w), lambda i: (i, 0, 0))],
        out_specs=[pl.BlockSpec((gb, r, w), lambda i: (i, 0, 0)),
                   pl.BlockSpec((gb, nb, n2), lambda i: (i, 0, 0))],
        out_shape=[jax.ShapeDtypeStruct((g, r, w), BF16),
                   jax.ShapeDtypeStruct((g, nb, n2), F32)],
        scratch_shapes=[pltpu.VMEM((gb, r, n2), F32), pltpu.VMEM((gb, r, n2), F32),
                        pltpu.VMEM((gb, r, n2), F32), pltpu.VMEM((gb, r, w), F32)],
        compiler_params=_params("arbitrary"),
        name="s5_prompt",
    )(hx, lam_re2, lam_im2, log_dt2, bt2, bts, c2, cs, d_rep)


def _s5_sample_kernel(u_ref, h0r_ref, h0i_ref, lr_ref, li_ref, ldt_ref, btr_ref, bti_ref,
                      cr_ref, ci_ref, d_ref, z_ref, nr_ref, ni_ref, *, gb):
    hp = lax.Precision.HIGHEST
    for g in range(gb):
        _, _, ab_re, ab_im, f_re, f_im = _zoh(lr_ref[g], li_ref[g], ldt_ref[g])
        bb_re = f_re * btr_ref[g] - f_im * bti_ref[g]
        bb_im = f_re * bti_ref[g] + f_im * btr_ref[g]
        u = u_ref[g]
        bu_re = jnp.dot(u, bb_re, preferred_element_type=F32, precision=hp)
        bu_im = jnp.dot(u, bb_im, preferred_element_type=F32, precision=hp)
        h0r = h0r_ref[g]
        h0i = h0i_ref[g]
        hr = bu_re + ab_re * h0r - ab_im * h0i
        hi = bu_im + ab_re * h0i + ab_im * h0r
        nr_ref[g] = hr
        ni_ref[g] = hi
        y = (lax.dot_general(hr, cr_ref[g], DN_LAST, preferred_element_type=F32, precision=hp)
             - lax.dot_general(hi, ci_ref[g], DN_LAST, preferred_element_type=F32, precision=hp))
        y = y + d_ref[g] * u
        z_ref[g] = _gelu_tanh(y).astype(BF16)


def _s5_sample(u, h0_re, h0_im, lam_re, lam_im, log_dt, bt_re, bt_im, c_re, c_im, d_g, gb=8):
    g, b, cg = u.shape
    n = SSM_STATE
    vec = lambda: pl.BlockSpec((gb, 1, n), lambda i: (i, 0, 0))
    mat = lambda: pl.BlockSpec((gb, cg, n), lambda i: (i, 0, 0))
    st = lambda: pl.BlockSpec((gb, b, n), lambda i: (i, 0, 0))
    return pl.pallas_call(
        functools.partial(_s5_sample_kernel, gb=gb),
        grid=(g // gb,),
        in_specs=[pl.BlockSpec((gb, b, cg), lambda i: (i, 0, 0)), st(), st(), vec(), vec(), vec(),
                  mat(), mat(), mat(), mat(), pl.BlockSpec((gb, 1, cg), lambda i: (i, 0, 0))],
        out_specs=[pl.BlockSpec((gb, b, cg), lambda i: (i, 0, 0)), st(), st()],
        out_shape=[jax.ShapeDtypeStruct((g, b, cg), BF16),
                   jax.ShapeDtypeStruct((g, b, n), F32),
                   jax.ShapeDtypeStruct((g, b, n), F32)],
        compiler_params=_params("arbitrary"),
        name="s5_sample",
    )(u, h0_re, h0_im, lam_re, lam_im, log_dt, bt_re, bt_im, c_re, c_im, d_g)


def _route_kernel(x_ref, g_ref, sc_ref, sh_ref, wr_ref, br_ref, h_ref, gates_ref, e_ref, w_ref, *, split):
    h = _normmod(x_ref[...], g_ref[...], sc_ref[...], sh_ref[...])
    hb = h.astype(BF16)
    h_ref[...] = hb.reshape(h_ref.shape) if split else hb
    w = wr_ref[...]
    w_hi = w.astype(BF16)
    w_lo = (w - w_hi.astype(F32)).astype(BF16)
    h_lo = (h - hb.astype(F32)).astype(BF16)
    logits = (lax.dot_general(w_hi, hb, DN_LAST, preferred_element_type=F32)
              + lax.dot_general(w_lo, hb, DN_LAST, preferred_element_type=F32)
              + lax.dot_general(w_hi, h_lo, DN_LAST, preferred_element_type=F32))
    s = jax.nn.sigmoid(logits)
    sel = s + br_ref[...]
    epg = EXPERTS_PER_GROUP
    srow = [s[e:e + 1, :] for e in range(N_EXPERTS)]
    row = [sel[e:e + 1, :] for e in range(N_EXPERTS)]

    def top2_sum(a, b, c, d):
        hi1, lo1 = jnp.maximum(a, b), jnp.minimum(a, b)
        hi2, lo2 = jnp.maximum(c, d), jnp.minimum(c, d)
        return jnp.maximum(hi1, hi2) + jnp.maximum(jnp.minimum(hi1, hi2), jnp.maximum(lo1, lo2))

    gs = [top2_sum(*row[epg * g:epg * (g + 1)]) for g in range(N_EXPERT_GROUPS)]
    best = gs[0]
    gidx = jnp.zeros_like(best, dtype=I32)
    for g in range(1, N_EXPERT_GROUPS):
        better = gs[g] > best
        gidx = jnp.where(better, g, gidx)
        best = jnp.where(better, gs[g], best)

    def pick(rows_, j):
        v = rows_[j]
        for g in range(1, N_EXPERT_GROUPS):
            v = jnp.where(gidx == g, rows_[epg * g + j], v)
        return v

    v = [pick(row, j) for j in range(epg)]
    u = [pick(srow, j) for j in range(epg)]

    def argmax_first(vals):
        bv, bi = vals[0], jnp.zeros_like(gidx)
        for j in range(1, epg):
            better = vals[j] > bv
            bi = jnp.where(better, j, bi)
            bv = jnp.where(better, vals[j], bv)
        return bi

    i1 = argmax_first(v)
    i2 = argmax_first([jnp.where(i1 == j, -jnp.inf, v[j]) for j in range(epg)])

    def take(vals, idx):
        out = vals[0]
        for j in range(1, epg):
            out = jnp.where(idx == j, vals[j], out)
        return out

    w1, w2 = take(u, i1), take(u, i2)
    tot = w1 + w2
    w1, w2 = w1 / tot, w2 / tot
    e1 = gidx * epg + i1
    e2 = gidx * epg + i2
    gates_ref[...] = jnp.concatenate(
        [jnp.where(e1 == e, w1, 0.0) + jnp.where(e2 == e, w2, 0.0) for e in range(N_EXPERTS)], axis=0)
    e_ref[...] = jnp.concatenate([e1, e2], axis=0)
    w_ref[...] = jnp.concatenate([w1, w2], axis=0)


def _route(x, g, mods, sc_chunk, sh_chunk, rows_per_b, tm, w_router_t, b_router, split):
    t, d = x.shape
    e = N_EXPERTS
    if split:
        h_spec = pl.BlockSpec((tm, d // LANES, LANES), lambda i, j: (i, 0, 0))
        h_shape = jax.ShapeDtypeStruct((t, d // LANES, LANES), BF16)
    else:
        h_spec = pl.BlockSpec((tm, d), lambda i, j: (i, 0))
        h_shape = jax.ShapeDtypeStruct((t, d), BF16)
    return pl.pallas_call(
        functools.partial(_route_kernel, split=split),
        grid=(t // tm, 1),
        in_specs=[
            pl.BlockSpec((tm, d), lambda i, j: (i, 0)),
            pl.BlockSpec((1, d), lambda i, j: (0, 0)),
            _mod_spec(mods, sc_chunk, tm, rows_per_b, d),
            _mod_spec(mods, sh_chunk, tm, rows_per_b, d),
            pl.BlockSpec((e, d), lambda i, j: (0, 0)),
            pl.BlockSpec((e, 1), lambda i, j: (0, 0)),
        ],
        out_specs=[h_spec, pl.BlockSpec((e, tm), lambda i, j: (0, i)),
                   pl.BlockSpec((TOP_K, tm), lambda i, j: (0, i)), pl.BlockSpec((TOP_K, tm), lambda i, j: (0, i))],
        out_shape=[h_shape, jax.ShapeDtypeStruct((e, t), F32),
                   jax.ShapeDtypeStruct((TOP_K, t), I32), jax.ShapeDtypeStruct((TOP_K, t), F32)],
        compiler_params=_params("arbitrary", "arbitrary"),
        name="moe_route",
    )(x, g.reshape(1, d), mods, mods, w_router_t, b_router.reshape(e, 1))


def _moe_kernel(h_ref, gates_ref, wg_ref, wu_ref, wd_ref, res_ref, gt_ref, o_ref, acc_ref):
    e = pl.program_id(1)
    first = (e == 0) & (pl.program_id(2) == 0)
    last = (e == pl.num_programs(1) - 1) & (pl.program_id(2) == pl.num_programs(2) - 1)

    @pl.when(first)
    def _():
        acc_ref[...] = jnp.zeros_like(acc_ref)

    h = h_ref[...]
    a = jnp.dot(h, wg_ref[...].astype(BF16), preferred_element_type=F32)
    b = jnp.dot(h, wu_ref[...].astype(BF16), preferred_element_type=F32)
    gates = gates_ref[...]
    lane = lax.broadcasted_iota(I32, gates.shape, 1)
    gcol = jnp.sum(jnp.where(lane == e, gates, 0.0), axis=1, keepdims=True)
    hid = (a * jax.nn.sigmoid(a)) * b * gcol
    acc_ref[...] += jnp.dot(hid.astype(BF16), wd_ref[...].astype(BF16), preferred_element_type=F32)

    @pl.when(last)
    def _():
        o_ref[...] = res_ref[...] + gt_ref[...] * acc_ref[...]


def _moe(h, gates, w_gate, w_up, w_down, layer, res, mods, gt_chunk, rows_per_b, tm, fsplit=2):
    t, d = h.shape
    e, _, f = w_gate.shape[1:]
    tf = f // fsplit
    r = mods.shape[1]
    return pl.pallas_call(
        _moe_kernel,
        grid=(t // tm, e, fsplit),
        in_specs=[
            pl.BlockSpec((tm, d), lambda i, j, k: (i, 0)),
            pl.BlockSpec((tm, e), lambda i, j, k: (i, 0)),
            pl.BlockSpec((None, None, d, tf), lambda i, j, k: (layer, j, 0, k)),
            pl.BlockSpec((None, None, d, tf), lambda i, j, k: (layer, j, 0, k)),
            pl.BlockSpec((None, None, tf, d), lambda i, j, k: (layer, j, k, 0)),
            pl.BlockSpec((tm, d), lambda i, j, k: (i, 0)),
            pl.BlockSpec((None, r, d), lambda i, j, k: ((i * tm) // rows_per_b, 0, gt_chunk)),
        ],
        out_specs=pl.BlockSpec((tm, d), lambda i, j, k: (i, 0)),
        out_shape=jax.ShapeDtypeStruct((t, d), F32),
        scratch_shapes=[pltpu.VMEM((tm, d), F32)],
        compiler_params=_params("arbitrary", "arbitrary", "arbitrary"),
        name="moe_experts",
    )(h, gates, w_gate, w_up, w_down, res, mods)


def _experts_kernel(te_ref, nu_ref, idxc_ref, idxn_ref, h_ref, wg_ref, wu_ref, wd_ref, o_ref,
                    wgb, wub, wdb, xbuf, sem, *, tg):
    i = pl.program_id(0)
    nu = nu_ref[0]
    slot = i % 2
    per_blk = SMEM_I32_BLOCK // tg

    def start_gather(idx_ref, tile, dst_slot):
        off = (tile % per_blk) * tg

        def issue(grp, carry):
            for u in range(DMA_GROUP):
                r = grp * DMA_GROUP + u
                pltpu.make_async_copy(h_ref.at[pl.ds(idx_ref[off + r], 1)], xbuf.at[dst_slot, pl.ds(r, 1)],
                                      sem.at[dst_slot]).start()
            return carry

        lax.fori_loop(0, tg // DMA_GROUP, issue, 0)

    @pl.when(i == 0)
    def _():
        start_gather(idxc_ref, i, slot)

    @pl.when(i + 1 < nu)
    def _():
        start_gather(idxn_ref, i + 1, 1 - slot)

    prev = te_ref[jnp.maximum(i - 1, 0)]

    @pl.when((i == 0) | (te_ref[i] != prev))
    def _():
        wgb[...] = wg_ref[...].astype(BF16)
        wub[...] = wu_ref[...].astype(BF16)
        wdb[...] = wd_ref[...].astype(BF16)

    @pl.when(i < nu)
    def _():
        def drain(r, carry):
            pltpu.make_async_copy(h_ref.at[pl.ds(0, 1)], xbuf.at[slot, pl.ds(r, 1)], sem.at[slot]).wait()
            return carry

        lax.fori_loop(0, tg, drain, 0, unroll=8)
        x = xbuf[slot].reshape(tg, wgb.shape[0])
        a = jnp.dot(x, wgb[...], preferred_element_type=F32)
        b = jnp.dot(x, wub[...], preferred_element_type=F32)
        hid = ((a * jax.nn.sigmoid(a)) * b).astype(BF16)
        y = jnp.dot(hid, wdb[...], preferred_element_type=F32)
        o_ref[...] = y.reshape(o_ref.shape)

    @pl.when(i >= nu)
    def _():
        o_ref[...] = jnp.zeros_like(o_ref)


def _experts(tile_expert, n_used, sorted_tok, h3, w_gate, w_up, w_down, layer, tg):
    rp = sorted_tok.shape[0]
    _, _, d, f = w_gate.shape
    blk = SMEM_I32_BLOCK
    per_blk = blk // tg
    last_blk = rp // blk - 1
    row_shape = h3.shape[1:]
    grid_spec = pltpu.PrefetchScalarGridSpec(
        num_scalar_prefetch=2,
        grid=(rp // tg,),
        in_specs=[
            pl.BlockSpec((blk,), lambda i, te, nu: (i // per_blk,), memory_space=pltpu.SMEM),
            pl.BlockSpec((blk,), lambda i, te, nu: (jnp.minimum((i + 1) // per_blk, last_blk),),
                         memory_space=pltpu.SMEM),
            pl.BlockSpec(memory_space=pl.ANY),
            pl.BlockSpec((None, None, d, f), lambda i, te, nu: (layer, te[i], 0, 0)),
            pl.BlockSpec((None, None, d, f), lambda i, te, nu: (layer, te[i], 0, 0)),
            pl.BlockSpec((None, None, f, d), lambda i, te, nu: (layer, te[i], 0, 0)),
        ],
        out_specs=pl.BlockSpec((tg,) + row_shape, lambda i, te, nu: (i, 0, 0)),
        scratch_shapes=[pltpu.VMEM((d, f), BF16), pltpu.VMEM((d, f), BF16), pltpu.VMEM((f, d), BF16),
                        pltpu.VMEM((2, tg) + row_shape, h3.dtype), pltpu.SemaphoreType.DMA((2,))],
    )
    return pl.pallas_call(
        functools.partial(_experts_kernel, tg=tg),
        grid_spec=grid_spec,
        out_shape=jax.ShapeDtypeStruct((rp,) + row_shape, F32),
        compiler_params=_params("arbitrary"),
        name="moe_routed_experts",
    )(tile_expert, n_used, sorted_tok, sorted_tok, h3, w_gate, w_up, w_down)


def _combine_kernel(pos_ref, ys_ref, w_ref, res_ref, gt_ref, *refs, tm, final):
    if final:
        gf_ref, o_ref, buf, sem = refs
    else:
        o_ref, buf, sem = refs

    def issue(grp, carry):
        for u in range(DMA_GROUP):
            r = grp * DMA_GROUP + u
            for k in range(TOP_K):
                pltpu.make_async_copy(ys_ref.at[pl.ds(pos_ref[k * tm + r], 1)], buf.at[k, pl.ds(r, 1)],
                                      sem).start(priority=(u + k) % 2)
        return carry

    lax.fori_loop(0, tm // DMA_GROUP, issue, 0)

    def drain(r, carry):
        for k in range(TOP_K):
            pltpu.make_async_copy(ys_ref.at[pl.ds(0, 1)], buf.at[k, pl.ds(r, 1)], sem).wait()
        return carry

    lax.fori_loop(0, tm, drain, 0, unroll=8)
    d = o_ref.shape[1]
    w = w_ref[...]
    m = (buf[0].reshape(tm, d) * jnp.tile(w[:, :LANES], (1, d // LANES))
         + buf[1].reshape(tm, d) * jnp.tile(w[:, LANES:], (1, d // LANES)))
    x = res_ref[...] + gt_ref[...] * m
    if final:
        ms = jnp.mean(x * x, axis=-1, keepdims=True)
        x = (x * lax.rsqrt(ms + EPS)) * gf_ref[...]
    o_ref[...] = x


def _combine(pos, ys, wtok, res, mods, gt_chunk, rows_per_b, tm, g_final=None):
    t, d = res.shape
    assert TOP_K * tm == SMEM_I32_BLOCK
    r = mods.shape[1]
    final = g_final is not None
    in_specs = [pl.BlockSpec((TOP_K * tm,), lambda i: (i,), memory_space=pltpu.SMEM),
                pl.BlockSpec(memory_space=pl.ANY),
                pl.BlockSpec((tm, TOP_K * LANES), lambda i: (i, 0)),
                pl.BlockSpec((tm, d), lambda i: (i, 0)),
                pl.BlockSpec((None, r, d), lambda i: ((i * tm) // rows_per_b, 0, gt_chunk))]
    args = [pos, ys, wtok, res, mods]
    if final:
        in_specs.append(pl.BlockSpec((1, d), lambda i: (0, 0)))
        args.append(g_final.reshape(1, d))
    return pl.pallas_call(
        functools.partial(_combine_kernel, tm=tm, final=final),
        grid=(t // tm,),
        in_specs=in_specs,
        out_specs=pl.BlockSpec((tm, d), lambda i: (i, 0)),
        out_shape=jax.ShapeDtypeStruct((t, d), F32),
        scratch_shapes=[pltpu.VMEM((TOP_K, tm) + ys.shape[1:], F32), pltpu.SemaphoreType.DMA(())],
        compiler_params=_params("arbitrary"),
        name="moe_combine",
    )(*args)


def _moe_routed(h3, e_rows, w_rows, w_gate, w_up, w_down, layer, res, mods, gt_chunk, rows_per_b,
                g_final=None, tg=256, tmc=512):
    t = h3.shape[0]
    ne = N_EXPERTS
    a = TOP_K * t
    rp = a + ne * tg
    rp = -(-rp // SMEM_I32_BLOCK) * SMEM_I32_BLOCK
    n_tiles = rp // tg
    ea = e_rows.reshape(a)
    ta = jnp.tile(jnp.arange(t, dtype=I32), TOP_K)
    onehot = (ea[:, None] == jnp.arange(ne, dtype=I32)[None, :]).astype(I32)
    csum = jnp.cumsum(onehot, axis=0)
    rank = jnp.sum((csum - onehot) * onehot, axis=1)
    counts = csum[-1]
    ptiles = (counts + tg - 1) // tg
    tile_end = jnp.cumsum(ptiles)
    tile_start = tile_end - ptiles
    n_used = tile_end[-1:].astype(I32)
    dest = jnp.sum(onehot * tile_start[None, :], axis=1) * tg + rank
    sorted_tok = jnp.zeros((rp,), I32).at[dest].set(ta, unique_indices=True)
    tile_expert = jnp.sum(jnp.arange(n_tiles, dtype=I32)[:, None] >= tile_end[None, :], axis=1)
    tile_expert = jnp.minimum(tile_expert, ne - 1).astype(I32)
    ys = _experts(tile_expert, n_used, sorted_tok, h3, w_gate, w_up, w_down, layer, tg)
    pos = dest.reshape(TOP_K, t // tmc, tmc).transpose(1, 0, 2).reshape(a)
    wtok = jnp.concatenate([jnp.broadcast_to(w_rows[k][:, None], (t, LANES)) for k in range(TOP_K)], axis=1)
    return _combine(pos, ys, wtok, res, mods, gt_chunk, rows_per_b, tmc, g_final)


def _kv_kernel(x_ref, g_ref, sc_ref, sh_ref, w_ref, gc_ref, cos_ref, sin_ref, ckv_ref, kpe_ref, wb_ref):
    @pl.when(pl.program_id(0) == 0)
    def _():
        wb_ref[...] = w_ref[...].astype(BF16)

    h = _normmod(x_ref[...], g_ref[...], sc_ref[...], sh_ref[...]).astype(BF16)
    lat = lax.dot_general(h, wb_ref[...], DN_LAST, preferred_element_type=F32)
    c = lat[:, :KV_LORA]
    ms = jnp.mean(c * c, axis=-1, keepdims=True)
    ckv_ref[...] = (c * lax.rsqrt(ms + EPS)) * gc_ref[...]
    kpe_ref[...] = _rope(lat[:, KV_LORA:], cos_ref[...], sin_ref[...])


def _kv(x, g_kv, mods, rows_per_b, tm, w_dkv_t, g_ckv, cos2, sin2):
    t, d = x.shape
    n = w_dkv_t.shape[0]
    nt = cos2.shape[0] // tm
    return pl.pallas_call(
        _kv_kernel,
        grid=(t // tm, 1),
        in_specs=[
            pl.BlockSpec((tm, d), lambda i, j: (i, 0)),
            pl.BlockSpec((1, d), lambda i, j: (0, 0)),
            _mod_spec(mods, 1, tm, rows_per_b, d),
            _mod_spec(mods, 0, tm, rows_per_b, d),
            pl.BlockSpec((n, d), lambda i, j: (0, 0)),
            pl.BlockSpec((1, KV_LORA), lambda i, j: (0, 0)),
            pl.BlockSpec((tm, QK_ROPE), lambda i, j: (i % nt, 0)),
            pl.BlockSpec((tm, QK_ROPE), lambda i, j: (i % nt, 0)),
        ],
        out_specs=[pl.BlockSpec((tm, KV_LORA), lambda i, j: (i, 0)),
                   pl.BlockSpec((tm, QK_ROPE), lambda i, j: (i, 0))],
        out_shape=[jax.ShapeDtypeStruct((t, KV_LORA), F32), jax.ShapeDtypeStruct((t, QK_ROPE), F32)],
        scratch_shapes=[pltpu.VMEM((n, d), BF16)],
        compiler_params=_params("arbitrary", "arbitrary"),
        name="latent_kv",
    )(x, g_kv.reshape(1, d), mods, mods, w_dkv_t, g_ckv.reshape(1, KV_LORA), cos2, sin2)


def _qproj_kernel(x_ref, g_ref, sc_ref, sh_ref, w_ref, cos_ref, sin_ref, q_ref, hb_ref, *, hb):
    @pl.when(pl.program_id(1) == 0)
    def _():
        hb_ref[...] = _normmod(x_ref[...], g_ref[...], sc_ref[...], sh_ref[...]).astype(BF16)

    acc = jnp.dot(hb_ref[...], w_ref[...].astype(BF16), preferred_element_type=F32)
    cos2, sin2 = cos_ref[...], sin_ref[...]
    for h in range(hb):
        c0 = h * QK_HEAD
        q_ref[h, :, :QK_NOPE] = acc[:, c0:c0 + QK_NOPE].astype(BF16)
        q_ref[h, :, QK_NOPE:] = _rope(acc[:, c0 + QK_NOPE:c0 + QK_HEAD], cos2, sin2).astype(BF16)


def _qproj(x, g, mods, sc_chunk, sh_chunk, rows_per_b, tm, w_q2, cos2, sin2, hb=4):
    t, d = x.shape
    nh = w_q2.shape[1] // QK_HEAD
    nt = rows_per_b // tm
    return pl.pallas_call(
        functools.partial(_qproj_kernel, hb=hb),
        grid=(t // tm, nh // hb),
        in_specs=[
            pl.BlockSpec((tm, d), lambda i, j: (i, 0)),
            pl.BlockSpec((1, d), lambda i, j: (0, 0)),
            _mod_spec(mods, sc_chunk, tm, rows_per_b, d),
            _mod_spec(mods, sh_chunk, tm, rows_per_b, d),
            pl.BlockSpec((d, hb * QK_HEAD), lambda i, j: (0, j)),
            pl.BlockSpec((tm, QK_ROPE), lambda i, j: (i % nt, 0)),
            pl.BlockSpec((tm, QK_ROPE), lambda i, j: (i % nt, 0)),
        ],
        out_specs=pl.BlockSpec((None, hb, tm, QK_HEAD), lambda i, j: (i // nt, j, i % nt, 0)),
        out_shape=jax.ShapeDtypeStruct((t // rows_per_b, nh, rows_per_b, QK_HEAD), BF16),
        scratch_shapes=[pltpu.VMEM((tm, d), BF16)],
        compiler_params=_params("arbitrary", "arbitrary"),
        name="q_proj",
    )(x, g.reshape(1, d), mods, mods, w_q2, cos2, sin2)


def _kvup_kernel(ckv_ref, kpe_ref, wk_ref, wv_ref, k_ref, v_ref, *, hb):
    c = ckv_ref[...].astype(BF16)
    kn = jnp.dot(c, wk_ref[...].astype(BF16), preferred_element_type=F32)
    vv = jnp.dot(c, wv_ref[...].astype(BF16), preferred_element_type=F32)
    kpe = kpe_ref[...].astype(BF16)
    for h in range(hb):
        k_ref[h, :, :QK_NOPE] = kn[:, h * QK_NOPE:(h + 1) * QK_NOPE].astype(BF16)
        k_ref[h, :, QK_NOPE:] = kpe
        v_ref[h] = vv[:, h * V_HEAD:(h + 1) * V_HEAD].astype(BF16)


def _kvup(ckv, kpe, w_uk2, w_uv2, bsz, seq, tm, hb=4):
    t, r = ckv.shape
    nt = seq // tm
    return pl.pallas_call(
        functools.partial(_kvup_kernel, hb=hb),
        grid=(t // tm, N_HEADS // hb),
        in_specs=[
            pl.BlockSpec((tm, r), lambda i, j: (i, 0)),
            pl.BlockSpec((tm, QK_ROPE), lambda i, j: (i, 0)),
            pl.BlockSpec((r, hb * QK_NOPE), lambda i, j: (0, j)),
            pl.BlockSpec((r, hb * V_HEAD), lambda i, j: (0, j)),
        ],
        out_specs=[pl.BlockSpec((None, hb, tm, QK_HEAD), lambda i, j: (i // nt, j, i % nt, 0)),
                   pl.BlockSpec((None, hb, tm, V_HEAD), lambda i, j: (i // nt, j, i % nt, 0))],
        out_shape=[jax.ShapeDtypeStruct((bsz, N_HEADS, seq, QK_HEAD), BF16),
                   jax.ShapeDtypeStruct((bsz, N_HEADS, seq, V_HEAD), BF16)],
        compiler_params=_params("arbitrary", "arbitrary"),
        name="kv_up",
    )(ckv, kpe, w_uk2, w_uv2)


def _flash_kernel(qt_ref, kt_ref, q_ref, k_ref, v_ref, o_ref, m_ref, l_ref, acc_ref, *, tq, tk, hb):
    p_idx = pl.program_id(2)
    qi = qt_ref[p_idx]
    kj = kt_ref[p_idx]

    @pl.when(kj == 0)
    def _():
        m_ref[...] = jnp.full_like(m_ref, -jnp.inf)
        l_ref[...] = jnp.zeros_like(l_ref)
        acc_ref[...] = jnp.zeros_like(acc_ref)

    def tile(h, masked):
        s = lax.dot_general(q_ref[h], k_ref[h], DN_LAST, preferred_element_type=F32) * (ATTN_SCALE * LOG2_E)
        if masked:
            qpos = lax.broadcasted_iota(I32, (tq, tk), 0)
            kpos = lax.broadcasted_iota(I32, (tq, tk), 1)
            s = jnp.where(kpos <= qpos, s, -jnp.inf)
        m_prev = m_ref[h]
        m_next = jnp.maximum(m_prev, jnp.max(s, axis=1, keepdims=True))
        alpha = jnp.exp2(m_prev - m_next)
        p = jnp.exp2(s - jnp.tile(m_next, (1, tk // LANES)))
        l_ref[h] = alpha * l_ref[h] + jnp.sum(p, axis=1, keepdims=True)
        acc_ref[h] = alpha * acc_ref[h] + jnp.dot(p.astype(BF16), v_ref[h], preferred_element_type=F32)
        m_ref[h] = m_next

    @pl.when(kj < qi)
    def _():
        for h in range(hb):
            tile(h, False)

    @pl.when(kj == qi)
    def _():
        for h in range(hb):
            tile(h, True)
            o_ref[:, h * V_HEAD:(h + 1) * V_HEAD] = (acc_ref[h] / l_ref[h]).astype(BF16)


def _flash(q, k, v, tile_len=512, hb=4):
    bsz, nh, seq, _ = q.shape
    tq = tk = tile_len
    nq = seq // tq
    pairs = [(i, j) for i in range(nq) for j in range(i + 1)]
    qt = jnp.asarray([p[0] for p in pairs], I32)
    kt = jnp.asarray([p[1] for p in pairs], I32)
    grid_spec = pltpu.PrefetchScalarGridSpec(
        num_scalar_prefetch=2,
        grid=(bsz, nh // hb, len(pairs)),
        in_specs=[
            pl.BlockSpec((None, hb, tq, QK_HEAD), lambda b, h, p, qt, kt: (b, h, qt[p], 0)),
            pl.BlockSpec((None, hb, tk, QK_HEAD), lambda b, h, p, qt, kt: (b, h, kt[p], 0)),
            pl.BlockSpec((None, hb, tk, V_HEAD), lambda b, h, p, qt, kt: (b, h, kt[p], 0)),
        ],
        out_specs=pl.BlockSpec((tq, hb * V_HEAD), lambda b, h, p, qt, kt: (b * nq + qt[p], h)),
        scratch_shapes=[pltpu.VMEM((hb, tq, LANES), F32), pltpu.VMEM((hb, tq, LANES), F32),
                        pltpu.VMEM((hb, tq, V_HEAD), F32)],
    )
    return pl.pallas_call(
        functools.partial(_flash_kernel, tq=tq, tk=tk, hb=hb),
        grid_spec=grid_spec,
        out_shape=jax.ShapeDtypeStruct((bsz * seq, nh * V_HEAD), BF16),
        compiler_params=_params("arbitrary", "arbitrary", "arbitrary"),
        name="flash_attn",
    )(qt, kt, q, k, v)


def _qlat_kernel(q_ref, wk_ref, o_ref):
    q = q_ref[...]
    qlat = lax.dot_general(q[:, :QK_NOPE], wk_ref[...].astype(BF16), DN_LAST, preferred_element_type=F32)
    o_ref[:, :KV_LORA] = qlat.astype(BF16)
    o_ref[:, KV_LORA:] = q[:, QK_NOPE:]


def _qlat(q, w_uk2):
    nh, b, _ = q.shape
    r = w_uk2.shape[0]
    return pl.pallas_call(
        _qlat_kernel,
        grid=(nh,),
        in_specs=[pl.BlockSpec((None, b, QK_HEAD), lambda h: (h, 0, 0)),
                  pl.BlockSpec((r, QK_NOPE), lambda h: (0, h))],
        out_specs=pl.BlockSpec((None, b, r + QK_ROPE), lambda h: (h, 0, 0)),
        out_shape=jax.ShapeDtypeStruct((nh, b, r + QK_ROPE), BF16),
        compiler_params=_params("arbitrary"),
        name="q_latent",
    )(q, w_uk2)


def _paged_kernel(pt_ref, q_ref, ckv_hbm, kpe_hbm, cnew_ref, pnew_ref, o_ref, qpad_ref, kcb_ref, pp_ref,
                  m_ref, l_ref, acc_ref, pgc_ref, pgk_ref, semc, semk, *, npg, nsteps, nrows, rb):
    bi = pl.program_id(0)
    j = pl.program_id(1)
    nh = N_HEADS
    wide = KV_LORA // LANES
    first = j == 0

    g_step = bi * nsteps + j
    n_total = (nrows // rb) * nsteps

    def start_pages(gs):
        slot = gs % PAGE_SLOTS
        row_blk = gs // nsteps
        step = gs % nsteps
        for r in range(rb):
            for i in range(npg):
                pg = pt_ref[row_blk * rb + r, step * npg + i]
                pltpu.make_async_copy(ckv_hbm.at[pg], pgc_ref.at[slot, r * npg + i], semc.at[slot]).start()
                pltpu.make_async_copy(kpe_hbm.at[pg], pgk_ref.at[slot, r * npg + i], semk.at[slot]).start()

    @pl.when(g_step == 0)
    def _():
        for ahead in range(PAGE_SLOTS - 1):
            start_pages(g_step + ahead)

    @pl.when(first & (bi == 0))
    def _():
        kcb_ref[...] = jnp.zeros_like(kcb_ref)
        pp_ref[...] = jnp.zeros_like(pp_ref)
        m_ref[...] = jnp.zeros_like(m_ref)
        l_ref[...] = jnp.zeros_like(l_ref)
        acc_ref[...] = jnp.zeros_like(acc_ref)

    @pl.when(first)
    def _():
        qpad_ref[...] = jnp.zeros_like(qpad_ref)
        for r in range(rb):
            qpad_ref[r, :nh, :] = q_ref[r]

    def step(cur, prv):
        @pl.when(g_step + (PAGE_SLOTS - 1) < n_total)
        def _():
            start_pages(g_step + (PAGE_SLOTS - 1))

        pslot = g_step % PAGE_SLOTS
        for k in range(rb * npg):
            pltpu.make_async_copy(ckv_hbm.at[0], pgc_ref.at[pslot, k], semc.at[pslot]).wait()
            pltpu.make_async_copy(kpe_hbm.at[0], pgk_ref.at[pslot, k], semk.at[pslot]).wait()
        for r in range(rb):
            pv_prev = jnp.dot(pp_ref[r], kcb_ref[r, prv], preferred_element_type=F32)
            acc_in = jnp.where(first, 0.0, acc_ref[r] + pv_prev)
            m_prev = jnp.where(first, -jnp.inf, m_ref[r])
            l_prev = jnp.where(first, 0.0, l_ref[r])
            for i in range(npg):
                kcb_ref[r, cur, i * PAGE_SIZE:(i + 1) * PAGE_SIZE, :] = pgc_ref[pslot, r * npg + i].astype(BF16)
            s_c = lax.dot_general(kcb_ref[r, cur], qpad_ref[r, :, :KV_LORA], DN_LAST,
                                  preferred_element_type=F32)
            kp = jnp.concatenate([pgk_ref[pslot, r * npg + i].astype(BF16) for i in range(npg)],
                                 axis=1)
            s_p = jnp.dot(qpad_ref[r, :, KV_LORA:], kp, preferred_element_type=F32)
            s = (s_c.T[:nh, :] + s_p[:nh, :]) * ATTN_SCALE
            m_next = jnp.maximum(m_prev, jnp.max(s, axis=1, keepdims=True))
            alpha = jnp.exp(m_prev - m_next)
            p = jnp.exp(s - jnp.tile(m_next, (1, s.shape[1] // LANES)))
            l_ref[r] = alpha * l_prev + jnp.sum(p, axis=1, keepdims=True)
            acc_ref[r] = jnp.tile(alpha, (1, wide)) * acc_in
            m_ref[r] = m_next
            pp_ref[r] = p.astype(BF16)

    @pl.when(j % 2 == 0)
    def _():
        step(0, 1)

    @pl.when(j % 2 == 1)
    def _():
        step(1, 0)

    @pl.when(j == nsteps - 1)
    def _():
        last_slot = (nsteps - 1) % 2
        for r in range(rb):
            acc_past = acc_ref[r] + jnp.dot(pp_ref[r], kcb_ref[r, last_slot], preferred_element_type=F32)
            q = q_ref[r].astype(F32)
            cnew = cnew_ref[r]
            s_new = (jnp.sum(q[:, :KV_LORA] * cnew, axis=1, keepdims=True)
                     + jnp.sum(q[:, KV_LORA:] * pnew_ref[r], axis=1, keepdims=True)) * ATTN_SCALE
            m_p = m_ref[r]
            m_n = jnp.maximum(m_p, s_new)
            a = jnp.exp(m_p - m_n)
            p_new = jnp.exp(s_new - m_n)
            l_fin = a * l_ref[r] + p_new
            acc = jnp.tile(a, (1, wide)) * acc_past + jnp.tile(p_new, (1, wide)) * cnew
            o_ref[r] = (acc / jnp.tile(l_fin, (1, wide))).astype(BF16)


def _paged_attn(qcat, cache_ckv, cache_kpe_t, page_table, ckv_new, kpe_new, npg=8, rb=2):
    b, nh, w = qcat.shape
    r = KV_LORA
    n_pages = page_table.shape[1]
    keys = npg * PAGE_SIZE
    nsteps = n_pages // npg
    assert (b // rb) * nsteps >= PAGE_SLOTS - 1
    grid_spec = pltpu.PrefetchScalarGridSpec(
        num_scalar_prefetch=1,
        grid=(b // rb, nsteps),
        in_specs=[pl.BlockSpec((rb, nh, w), lambda bi, j, pt: (bi, 0, 0)),
                  pl.BlockSpec(memory_space=pl.ANY),
                  pl.BlockSpec(memory_space=pl.ANY),
                  pl.BlockSpec((rb, 1, r), lambda bi, j, pt: (bi, 0, 0)),
                  pl.BlockSpec((rb, 1, QK_ROPE), lambda bi, j, pt: (bi, 0, 0))],
        out_specs=pl.BlockSpec((rb, nh, r), lambda bi, j, pt: (bi, 0, 0)),
        scratch_shapes=[pltpu.VMEM((rb, LANES, w), BF16), pltpu.VMEM((rb, 2, keys, r), BF16),
                        pltpu.VMEM((rb, nh, keys), BF16), pltpu.VMEM((rb, nh, LANES), F32),
                        pltpu.VMEM((rb, nh, LANES), F32), pltpu.VMEM((rb, nh, r), F32),
                        pltpu.VMEM((PAGE_SLOTS, rb * npg, PAGE_SIZE, r), F32),
                        pltpu.VMEM((PAGE_SLOTS, rb * npg, QK_ROPE, PAGE_SIZE), F32),
                        pltpu.SemaphoreType.DMA((PAGE_SLOTS,)), pltpu.SemaphoreType.DMA((PAGE_SLOTS,))],
    )
    return pl.pallas_call(
        functools.partial(_paged_kernel, npg=npg, nsteps=nsteps, nrows=b, rb=rb),
        grid_spec=grid_spec,
        out_shape=jax.ShapeDtypeStruct((b, nh, r), BF16),
        compiler_params=_params("arbitrary", "arbitrary"),
        name="paged_attn",
    )(page_table, qcat, cache_ckv, cache_kpe_t, ckv_new.reshape(b, 1, r), kpe_new.reshape(b, 1, QK_ROPE))


def _ovup_kernel(o_ref, wv_ref, a_ref):
    a_ref[...] = jnp.dot(o_ref[...], wv_ref[...].astype(BF16), preferred_element_type=F32).astype(BF16)


def _ovup(olat, w_uv2):
    nh, b, r = olat.shape
    return pl.pallas_call(
        _ovup_kernel,
        grid=(nh,),
        in_specs=[pl.BlockSpec((None, b, r), lambda h: (h, 0, 0)), pl.BlockSpec((r, V_HEAD), lambda h: (0, h))],
        out_specs=pl.BlockSpec((b, V_HEAD), lambda h: (0, h)),
        out_shape=jax.ShapeDtypeStruct((b, nh * V_HEAD), BF16),
        compiler_params=_params("arbitrary"),
        name="o_latent_up",
    )(olat, w_uv2)


def _rope_tables(pos):
    inv = ROPE_THETA ** (-jnp.arange(0, QK_ROPE, 2, dtype=F32) / QK_ROPE)
    ang = pos.astype(F32)[:, None] * inv[None, :]
    cos, sin = jnp.cos(ang), jnp.sin(ang)
    return jnp.concatenate([cos, cos], axis=1), jnp.concatenate([-sin, sin], axis=1)


def _trunk(x, mods, mods_kv, rows_per_b, tm, cos2, sin2, P, s5_glu_fn, attn_fn, routed):
    tms = min(tm, 512)

    def moe_block(x, layer, g_final=None):
        h2, gates_t, e_rows, w_rows = _route(x, P["g_norm"][layer, 1], mods[layer], 4, 3, rows_per_b, tms,
                                             P["w_router_t"], P["b_router"], split=routed)
        if routed:
            return _moe_routed(h2, e_rows, w_rows, P["w_gate"], P["w_up"], P["w_down"], layer, x,
                               mods[layer], 5, rows_per_b, g_final)
        x = _moe(h2, gates_t.T, P["w_gate"], P["w_up"], P["w_down"], layer, x, mods[layer], 5,
                 rows_per_b, tms)
        return x if g_final is None else _final_norm(x, g_final, tms)

    x, st_re, st_im = s5_glu_fn(x)
    x = moe_block(x, 0)
    ckv, kpe = _kv(x, P["g_kv"], mods_kv, rows_per_b, tms, P["w_dkv_t"], P["g_ckv"], cos2, sin2)
    q = _qproj(x, P["g_norm"][1, 0], mods[1], 1, 0, rows_per_b, tm, P["w_q2"], cos2, sin2)
    attn = attn_fn(q, ckv, kpe)
    x = _proj_res(attn, P["w_o"][0], x, mods[1], 2, rows_per_b, tm, 512, glu=False)
    y = moe_block(x, 1, P["g_final"])
    return y, st_re, st_im, ckv, kpe


def kernel(x_prompt, x_sample, c_prompt, c_sample, state_ssm_re, state_ssm_im, cache_ckv, cache_kpe, page_table, g_norm, w_ada, b_ada, ssm_lam_re, ssm_lam_im, ssm_log_dt, ssm_b_re, ssm_b_im, ssm_c_re, ssm_c_im, ssm_d, w_glu, g_kv, w_ada_kv, b_ada_kv, w_dkv, g_ckv, w_uk, w_uv, w_q, w_o, w_router, b_router, w_gate, w_up, w_down, g_final):
    bsz, seq, d = x_prompt.shape
    db = x_sample.shape[0]
    g = d // SSM_GROUP
    n = SSM_STATE
    assert g == LANES, "the group-major relayout puts one S5 group per lane"
    past = page_table.shape[1] * PAGE_SIZE
    P = dict(g_norm=g_norm, g_kv=g_kv, g_ckv=g_ckv, w_o=w_o, b_router=b_router,
             w_gate=w_gate, w_up=w_up, w_down=w_down, g_final=g_final)
    P["w_q2"] = w_q[0]
    P["w_dkv_t"] = w_dkv.T
    P["w_router_t"] = w_router.T
    w_uk2 = w_uk.reshape(KV_LORA, N_HEADS * QK_NOPE)
    w_uv2 = w_uv.reshape(KV_LORA, N_HEADS * V_HEAD)

    m_rows = db + bsz
    m_pad = -(-m_rows // 16) * 16
    c_all = jnp.concatenate([c_sample, c_prompt, jnp.zeros((m_pad - m_rows, d), F32)], axis=0)
    mod = _ada(c_all, w_ada, b_ada)
    mod_kv = _ada(c_all, w_ada_kv[None], b_ada_kv[None])
    mods_s = [mod[l, :db][None] for l in range(mod.shape[0])]
    mods_p = [mod[l, db:m_rows][:, None, :] for l in range(mod.shape[0])]
    modkv_s = mod_kv[0, :db][None]
    modkv_p = mod_kv[0, db:m_rows][:, None, :]

    lam_re = ssm_lam_re[0].reshape(g, 1, n)
    lam_im = ssm_lam_im[0].reshape(g, 1, n)
    log_dt = jnp.broadcast_to(ssm_log_dt[0].reshape(g, 1, 1), (g, 1, n))
    bt_re = ssm_b_re[0].transpose(0, 2, 1)
    bt_im = ssm_b_im[0].transpose(0, 2, 1)
    c_re, c_im = ssm_c_re[0], ssm_c_im[0]
    d_g = ssm_d[0].reshape(g, 1, SSM_GROUP)
    d_rep = jnp.tile(d_g, (1, 1, S5_CHUNK))
    dup = lambda a: jnp.concatenate([a, a], axis=-1)
    cat = lambda a, b: jnp.concatenate([a, b], axis=-1)
    nchunk = seq // S5_CHUNK
    tm_p = 1024

    src = jnp.arange(d, dtype=I32)
    dst = (src % SSM_GROUP) * g + src // SSM_GROUP
    perm = (dst[:, None] == jnp.arange(d, dtype=I32)[None, :]).astype(BF16)
    w_glu_p = w_glu[0].reshape(g, SSM_GROUP, -1).transpose(1, 0, 2).reshape(d, -1)

    def s5_glu_prompt(x):
        hx = _s5_in(x, g_norm[0, 0], mods_p[0], 1, 0, seq, 256, perm)
        hx = hx.reshape(g, bsz * nchunk, S5_CHUNK * SSM_GROUP)
        zg, st = _s5_prompt(hx, dup(lam_re), dup(lam_im), dup(log_dt), cat(bt_re, bt_im), cat(bt_im, bt_re),
                            cat(c_re, c_im), cat(c_im, c_re), d_rep, bsz)
        zg = zg.reshape(g, bsz * seq * SSM_GROUP)
        x = _proj_res(zg, w_glu_p, x, mods_p[0], 2, seq, tm_p, 256, glu=True, grouped=True)
        st = st.transpose(1, 0, 2)[None]
        return x, st[..., :n], st[..., n:]

    def s5_glu_sample(x):
        h = _normmod_call(x, g_norm[0, 0], mods_s[0], 1, 0, db, db, F32)
        u = h.reshape(db, g, SSM_GROUP).transpose(1, 0, 2)
        h0_re = state_ssm_re[0].transpose(1, 0, 2)
        h0_im = state_ssm_im[0].transpose(1, 0, 2)
        zg, nr, ni = _s5_sample(u, h0_re, h0_im, lam_re, lam_im, log_dt, bt_re, bt_im, c_re, c_im, d_g)
        z = zg.transpose(1, 0, 2).reshape(db, d)
        x = _proj_res(z, w_glu[0], x, mods_s[0], 2, db, db, 256, glu=True)
        return x, nr.transpose(1, 0, 2)[None], ni.transpose(1, 0, 2)[None]

    def attn_prompt(q, ckv, kpe):
        k, v = _kvup(ckv, kpe, w_uk2, w_uv2, bsz, seq, tm_p)
        return _flash(q, k, v)

    def attn_sample(q, ckv, kpe):
        qcat = _qlat(q[0], w_uk2).transpose(1, 0, 2)
        olat = _paged_attn(qcat, cache_ckv, cache_kpe.transpose(0, 2, 1), page_table, ckv, kpe)
        return _ovup(olat.transpose(1, 0, 2), w_uv2)

    cos_p, sin_p = _rope_tables(jnp.arange(seq))
    cos_s, sin_s = _rope_tables(jnp.full((db,), past))

    y_p, sre_p, sim_p, ckv_p, kpe_p = _trunk(
        x_prompt.reshape(bsz * seq, d), mods_p, modkv_p, seq, tm_p, cos_p, sin_p, P,
        s5_glu_prompt, attn_prompt, routed=True)
    y_s, sre_s, sim_s, ckv_s, kpe_s = _trunk(
        x_sample.reshape(db, d), mods_s, modkv_s, db, db, cos_s, sin_s, P,
        s5_glu_sample, attn_sample, routed=False)
    return (y_p.reshape(bsz, seq, d), y_s.reshape(db, 1, d), sre_p, sim_p,
            ckv_p.reshape(bsz, seq, KV_LORA), kpe_p.reshape(bsz, seq, QK_ROPE),
            sre_s, sim_s, ckv_s.reshape(db, 1, KV_LORA), kpe_s.reshape(db, 1, QK_ROPE))
```

```python
import functools
import math

import jax
import jax.numpy as jnp
from jax import lax
from jax.experimental import pallas as pl
from jax.experimental.pallas import tpu as pltpu

F32 = jnp.float32
BF16 = jnp.bfloat16
I32 = jnp.int32

SSM_GROUP = 16
SSM_STATE = 64
N_HEADS = 16
QK_NOPE = 128
QK_ROPE = 64
V_HEAD = 128
KV_LORA = 512
ROPE_THETA = 10000.0
PAGE_SIZE = 128
N_EXPERTS = 16
N_EXPERT_GROUPS = 4
EXPERTS_PER_GROUP = N_EXPERTS // N_EXPERT_GROUPS
TOP_K = 2
EPS = 1e-6
ATTN_SCALE = (QK_NOPE + QK_ROPE) ** -0.5
QK_HEAD = QK_NOPE + QK_ROPE
LOG2_E = math.log2(math.e)

V7X_VMEM_BYTES = 64 * 1024 * 1024
VMEM_LIMIT_BYTES = V7X_VMEM_BYTES - 8 * 1024 * 1024
LANES = 128
SMEM_I32_BLOCK = 1024
DMA_GROUP = 8
PAGE_SLOTS = 3
S5_CHUNK = 16
DN_LAST = (((1,), (1,)), ((), ()))


def _params(*sem):
    return pltpu.CompilerParams(dimension_semantics=sem, vmem_limit_bytes=VMEM_LIMIT_BYTES)


def _normmod(x, g, sc, sh):
    ms = jnp.mean(x * x, axis=-1, keepdims=True)
    return (x * lax.rsqrt(ms + EPS)) * g * (1.0 + sc) + sh


def _rope(p, cos2, sin2):
    half = QK_ROPE // 2
    swapped = jnp.concatenate([p[:, half:], p[:, :half]], axis=1)
    return p * cos2 + swapped * sin2


def _mod_spec(mods, chunk, tm, rows_per_b, width, j_blocks=None):
    _, r, _ = mods.shape
    if j_blocks is None:
        return pl.BlockSpec((None, r, width), lambda i, j: ((i * tm) // rows_per_b, 0, chunk))
    return pl.BlockSpec((None, r, width), lambda i, j: ((i * tm) // rows_per_b, 0, chunk * j_blocks + j))


def _ada_kernel(c_ref, w_ref, b_ref, o_ref, cb_ref):
    @pl.when((pl.program_id(0) == 0) & (pl.program_id(1) == 0))
    def _():
        c = c_ref[...]
        cb_ref[...] = (c * jax.nn.sigmoid(c)).astype(BF16)

    acc = jnp.dot(cb_ref[...], w_ref[...].astype(BF16), preferred_element_type=F32)
    o_ref[...] = acc + b_ref[...]


def _ada(c_all, w, b, tn=1024):
    m, d = c_all.shape
    lw, _, n = w.shape
    return pl.pallas_call(
        _ada_kernel,
        grid=(lw, n // tn),
        in_specs=[
            pl.BlockSpec((m, d), lambda l, j: (0, 0)),
            pl.BlockSpec((None, d, tn), lambda l, j: (l, 0, j)),
            pl.BlockSpec((None, 1, tn), lambda l, j: (l, 0, j)),
        ],
        out_specs=pl.BlockSpec((None, m, tn), lambda l, j: (l, 0, j)),
        out_shape=jax.ShapeDtypeStruct((lw, m, n), F32),
        scratch_shapes=[pltpu.VMEM((m, d), BF16)],
        compiler_params=_params("arbitrary", "arbitrary"),
        name="ada_mod",
    )(c_all, w, b.reshape(lw, 1, n))


def _normmod_kernel(x_ref, g_ref, sc_ref, sh_ref, o_ref):
    o_ref[...] = _normmod(x_ref[...], g_ref[...], sc_ref[...], sh_ref[...]).astype(o_ref.dtype)


def _normmod_call(x, g, mods, sc_chunk, sh_chunk, rows_per_b, tm, out_dtype):
    t, d = x.shape
    return pl.pallas_call(
        _normmod_kernel,
        grid=(t // tm, 1),
        in_specs=[
            pl.BlockSpec((tm, d), lambda i, j: (i, 0)),
            pl.BlockSpec((1, d), lambda i, j: (0, 0)),
            _mod_spec(mods, sc_chunk, tm, rows_per_b, d),
            _mod_spec(mods, sh_chunk, tm, rows_per_b, d),
        ],
        out_specs=pl.BlockSpec((tm, d), lambda i, j: (i, 0)),
        out_shape=jax.ShapeDtypeStruct((t, d), out_dtype),
        compiler_params=_params("arbitrary", "arbitrary"),
        name="norm_mod",
    )(x, g.reshape(1, d), mods, mods)


def _final_norm_kernel(x_ref, g_ref, o_ref):
    x = x_ref[...]
    ms = jnp.mean(x * x, axis=-1, keepdims=True)
    o_ref[...] = (x * lax.rsqrt(ms + EPS)) * g_ref[...]


def _final_norm(x, g, tm):
    t, d = x.shape
    return pl.pallas_call(
        _final_norm_kernel,
        grid=(t // tm,),
        in_specs=[pl.BlockSpec((tm, d), lambda i: (i, 0)), pl.BlockSpec((1, d), lambda i: (0, 0))],
        out_specs=pl.BlockSpec((tm, d), lambda i: (i, 0)),
        out_shape=jax.ShapeDtypeStruct((t, d), F32),
        compiler_params=_params("arbitrary"),
        name="final_norm",
    )(x, g.reshape(1, d))


def _proj_res_kernel(x_ref, *refs, glu, grouped, tm):
    refs = list(refs)
    xb_ref = refs.pop() if grouped else None
    if glu:
        wa_ref, wb_ref, res_ref, gt_ref, o_ref = refs
    else:
        wa_ref, res_ref, gt_ref, o_ref = refs
    if grouped:
        @pl.when(pl.program_id(1) == 0)
        def _():
            cg = SSM_GROUP
            for k in range(tm // LANES):
                zt = x_ref[:, k * LANES * cg:(k + 1) * LANES * cg].T
                xb_ref[k * LANES:(k + 1) * LANES, :] = zt.reshape(LANES, cg, LANES).reshape(LANES, cg * LANES)

        x = xb_ref[...]
    else:
        x = x_ref[...]
    y = jnp.dot(x, wa_ref[...].astype(BF16), preferred_element_type=F32)
    if glu:
        yb = jnp.dot(x, wb_ref[...].astype(BF16), preferred_element_type=F32)
        y = y * jax.nn.sigmoid(yb)
    o_ref[...] = res_ref[...] + gt_ref[...] * y


def _proj_res(x, w, res, mods, gt_chunk, rows_per_b, tm, tn, glu, grouped=False):
    t, n = res.shape
    k = w.shape[0]
    nj = n // tn
    if grouped:
        x_spec = pl.BlockSpec((x.shape[0], tm * SSM_GROUP), lambda i, j: (0, i))
    else:
        x_spec = pl.BlockSpec((tm, k), lambda i, j: (i, 0))
    in_specs = [x_spec, pl.BlockSpec((k, tn), lambda i, j: (0, j))]
    args = [x, w]
    if glu:
        in_specs.append(pl.BlockSpec((k, tn), lambda i, j: (0, nj + j)))
        args.append(w)
    in_specs += [pl.BlockSpec((tm, tn), lambda i, j: (i, j)), _mod_spec(mods, gt_chunk, tm, rows_per_b, tn, nj)]
    args += [res, mods]
    return pl.pallas_call(
        functools.partial(_proj_res_kernel, glu=glu, grouped=grouped, tm=tm),
        grid=(t // tm, nj),
        in_specs=in_specs,
        out_specs=pl.BlockSpec((tm, tn), lambda i, j: (i, j)),
        out_shape=jax.ShapeDtypeStruct((t, n), F32),
        scratch_shapes=[pltpu.VMEM((tm, k), BF16)] if grouped else [],
        compiler_params=_params("arbitrary", "arbitrary"),
        name="glu_proj" if glu else "out_proj",
    )(*args)


def _s5in_kernel(x_ref, g_ref, sc_ref, sh_ref, p_ref, o_ref):
    h = _normmod(x_ref[...], g_ref[...], sc_ref[...], sh_ref[...]).astype(BF16)
    hp = jnp.dot(h, p_ref[...], preferred_element_type=F32).astype(BF16)
    tm = hp.shape[0]
    o_ref[...] = hp.reshape(tm, SSM_GROUP, LANES).reshape(tm * SSM_GROUP, LANES).T


def _s5_in(x, g, mods, sc_chunk, sh_chunk, rows_per_b, tm, perm):
    t, d = x.shape
    ng = d // SSM_GROUP
    return pl.pallas_call(
        _s5in_kernel,
        grid=(t // tm, 1),
        in_specs=[
            pl.BlockSpec((tm, d), lambda i, j: (i, 0)),
            pl.BlockSpec((1, d), lambda i, j: (0, 0)),
            _mod_spec(mods, sc_chunk, tm, rows_per_b, d),
            _mod_spec(mods, sh_chunk, tm, rows_per_b, d),
            pl.BlockSpec((d, d), lambda i, j: (0, 0)),
        ],
        out_specs=pl.BlockSpec((ng, tm * SSM_GROUP), lambda i, j: (0, i)),
        out_shape=jax.ShapeDtypeStruct((ng, t * SSM_GROUP), BF16),
        compiler_params=_params("arbitrary", "arbitrary"),
        name="s5_in",
    )(x, g.reshape(1, d), mods, mods, perm)


def _cpow(lr_dt, li_dt, k):
    mag = jnp.exp(k * lr_dt)
    ang = k * li_dt
    return mag * jnp.cos(ang), mag * jnp.sin(ang)


def _rep_rows(e, n):
    q, w = e.shape
    return jnp.concatenate([jnp.broadcast_to(e[t:t + 1], (n, w)) for t in range(q)], axis=0)


def _tile_rows(c, n):
    return jnp.concatenate([c] * n, axis=0)


def _zoh(lr, li, ldt):
    dt = jnp.exp(ldt)
    lr_dt = lr * dt
    li_dt = li * dt
    mag = jnp.exp(lr_dt)
    ab_re = mag * jnp.cos(li_dt)
    ab_im = mag * jnp.sin(li_dt)
    den = lr * lr + li * li
    f_re = ((ab_re - 1.0) * lr + ab_im * li) / den
    f_im = (ab_im * lr - (ab_re - 1.0) * li) / den
    return lr_dt, li_dt, ab_re, ab_im, f_re, f_im


def _gelu_tanh(y):
    return 0.5 * y * (1.0 + jnp.tanh(math.sqrt(2.0 / math.pi) * (y + 0.044715 * (y * y * y))))


def _s5_prompt_kernel(x_ref, lr_ref, li_ref, ldt_ref, bt_ref, bts_ref, c_ref, cs_ref, d_ref,
                      z_ref, st_ref, s_sc, ss_sc, hp_sc, yl_sc, *, gb, nb, nchunk):
    q, cg, n = S5_CHUNK, SSM_GROUP, SSM_STATE
    half = q // 2
    ti = lax.broadcasted_iota(I32, (q, 1), 0).astype(F32)
    rows = lax.broadcasted_iota(I32, (q * cg, q * cg), 0)
    cols = lax.broadcasted_iota(I32, (q * cg, q * cg), 1)
    shift = cg.bit_length() - 1
    causal = (rows >> shift) <= (cols >> shift)
    diag = rows == cols
    lane = lax.broadcasted_iota(I32, (1, 2 * n), 1)
    sgn = jnp.where(lane < n, -1.0, 1.0)

    def cmul(x2, xs, er, ei):
        return x2 * er + (xs * sgn) * ei

    a_terms = []
    vcs = []
    for g in range(gb):
        lr_dt, li_dt, _, _, f_re, f_im = _zoh(lr_ref[g], li_ref[g], ldt_ref[g])
        bt2, bts = bt_ref[g], bts_ref[g]
        bb2 = cmul(bt2, bts, f_re, f_im)
        bbs = bts * f_re - (bt2 * sgn) * f_im
        bb2t, bbst = _tile_rows(bb2, q), _tile_rows(bbs, q)
        c2t, cst = _tile_rows(c_ref[g], q), _tile_rows(cs_ref[g], q)

        def table(k, lr_dt=lr_dt, li_dt=li_dt):
            er, ei = _cpow(lr_dt, li_dt, k)
            return _rep_rows(er, cg), _rep_rows(ei, cg)

        pc = cmul(c2t, cst, *table(ti - half)) * (-sgn)
        pb = cmul(bb2t, bbst, *table(half - ti))
        tt = lax.dot_general(pb.astype(BF16), pc.astype(BF16), DN_LAST, preferred_element_type=F32)
        tt = jnp.where(causal, tt, 0.0) + jnp.where(diag, d_ref[g], 0.0)
        w1 = cmul(bb2t, bbst, *table((q - 1) - ti))
        vcs.append((cmul(c2t, cst, *table(ti + 1.0)) * (-sgn)).astype(BF16))
        ar, ai = _cpow(lr_dt, li_dt, float(q))
        a_terms.append((ar, ai * sgn))
        x = x_ref[g]
        yl_sc[g] = jnp.dot(x, tt.astype(BF16), preferred_element_type=F32)
        s = jnp.dot(x, w1.astype(BF16), preferred_element_type=F32)
        s_sc[g] = s
        ss_sc[g] = pltpu.roll(s, n, axis=1)

    def carry(ch, hs):
        new = []
        for g in range(gb):
            h, hsw = hs[2 * g], hs[2 * g + 1]
            hp_sc[g, pl.ds(ch, nb, stride=nchunk), :] = h
            a1, a2 = a_terms[g]
            new.append(a1 * h + a2 * hsw + s_sc[g, pl.ds(ch, nb, stride=nchunk), :])
            new.append(a1 * hsw - a2 * h + ss_sc[g, pl.ds(ch, nb, stride=nchunk), :])
        return tuple(new)

    zero = jnp.zeros((nb, 2 * n), F32)
    hs = lax.fori_loop(0, nchunk, carry, (zero,) * (2 * gb), unroll=8)
    for g in range(gb):
        st_ref[g] = hs[2 * g]
        y = yl_sc[g] + lax.dot_general(hp_sc[g].astype(BF16), vcs[g], DN_LAST, preferred_element_type=F32)
        z_ref[g] = _gelu_tanh(y).astype(BF16)


def _s5_prompt(hx, lam_re2, lam_im2, log_dt2, bt2, bts, c2, cs, d_rep, nb, gb=2):
    g, r, w = hx.shape
    n2 = 2 * SSM_STATE
    nchunk = r // nb
    vec = lambda: pl.BlockSpec((gb, 1, n2), lambda i: (i, 0, 0))
    mat = lambda: pl.BlockSpec((gb, SSM_GROUP, n2), lambda i: (i, 0, 0))
    return pl.pallas_call(
        functools.partial(_s5_prompt_kernel, gb=gb, nb=nb, nchunk=nchunk),
        grid=(g // gb,),
        in_specs=[pl.BlockSpec((gb, r, w), lambda i: (i, 0, 0)), vec(), vec(), vec(),
                  mat(), mat(), mat(), mat(), pl.BlockSpec((gb, 1, w), lambda i: (i, 0, 0))],
        out_specs=[pl.BlockSpec((gb, r, w), lambda i: (i, 0, 0)),
                   pl.BlockSpec((gb, nb, n2), lambda i: (i, 0, 0))],
        out_shape=[jax.ShapeDtypeStruct((g, r, w), BF16),
                   jax.ShapeDtypeStruct((g, nb, n2), F32)],
        scratch_shapes=[pltpu.VMEM((gb, r, n2), F32), pltpu.VMEM((gb, r, n2), F32),
                        pltpu.VMEM((gb, r, n2), F32), pltpu.VMEM((gb, r, w), F32)],
        compiler_params=_params("arbitrary"),
        name="s5_prompt",
    )(hx, lam_re2, lam_im2, log_dt2, bt2, bts, c2, cs, d_rep)


def _s5_sample_kernel(u_ref, h0r_ref, h0i_ref, lr_ref, li_ref, ldt_ref, btr_ref, bti_ref,
                      cr_ref, ci_ref, d_ref, z_ref, nr_ref, ni_ref, *, gb):
    hp = lax.Precision.HIGHEST
    for g in range(gb):
        _, _, ab_re, ab_im, f_re, f_im = _zoh(lr_ref[g], li_ref[g], ldt_ref[g])
        bb_re = f_re * btr_ref[g] - f_im * bti_ref[g]
        bb_im = f_re * bti_ref[g] + f_im * btr_ref[g]
        u = u_ref[g]
        bu_re = jnp.dot(u, bb_re, preferred_element_type=F32, precision=hp)
        bu_im = jnp.dot(u, bb_im, preferred_element_type=F32, precision=hp)
        h0r = h0r_ref[g]
        h0i = h0i_ref[g]
        hr = bu_re + ab_re * h0r - ab_im * h0i
        hi = bu_im + ab_re * h0i + ab_im * h0r
        nr_ref[g] = hr
        ni_ref[g] = hi
        y = (lax.dot_general(hr, cr_ref[g], DN_LAST, preferred_element_type=F32, precision=hp)
             - lax.dot_general(hi, ci_ref[g], DN_LAST, preferred_element_type=F32, precision=hp))
        y = y + d_ref[g] * u
        z_ref[g] = _gelu_tanh(y).astype(BF16)


def _s5_sample(u, h0_re, h0_im, lam_re, lam_im, log_dt, bt_re, bt_im, c_re, c_im, d_g, gb=8):
    g, b, cg = u.shape
    n = SSM_STATE
    vec = lambda: pl.BlockSpec((gb, 1, n), lambda i: (i, 0, 0))
    mat = lambda: pl.BlockSpec((gb, cg, n), lambda i: (i, 0, 0))
    st = lambda: pl.BlockSpec((gb, b, n), lambda i: (i, 0, 0))
    return pl.pallas_call(
        functools.partial(_s5_sample_kernel, gb=gb),
        grid=(g // gb,),
        in_specs=[pl.BlockSpec((gb, b, cg), lambda i: (i, 0, 0)), st(), st(), vec(), vec(), vec(),
                  mat(), mat(), mat(), mat(), pl.BlockSpec((gb, 1, cg), lambda i: (i, 0, 0))],
        out_specs=[pl.BlockSpec((gb, b, cg), lambda i: (i, 0, 0)), st(), st()],
        out_shape=[jax.ShapeDtypeStruct((g, b, cg), BF16),
                   jax.ShapeDtypeStruct((g, b, n), F32),
                   jax.ShapeDtypeStruct((g, b, n), F32)],
        compiler_params=_params("arbitrary"),
        name="s5_sample",
    )(u, h0_re, h0_im, lam_re, lam_im, log_dt, bt_re, bt_im, c_re, c_im, d_g)


def _route_kernel(x_ref, g_ref, sc_ref, sh_ref, wr_ref, br_ref, h_ref, gates_ref, e_ref, w_ref, *, split):
    h = _normmod(x_ref[...], g_ref[...], sc_ref[...], sh_ref[...])
    hb = h.astype(BF16)
    h_ref[...] = hb.reshape(h_ref.shape) if split else hb
    w = wr_ref[...]
    w_hi = w.astype(BF16)
    w_lo = (w - w_hi.astype(F32)).astype(BF16)
    h_lo = (h - hb.astype(F32)).astype(BF16)
    logits = (lax.dot_general(w_hi, hb, DN_LAST, preferred_element_type=F32)
              + lax.dot_general(w_lo, hb, DN_LAST, preferred_element_type=F32)
              + lax.dot_general(w_hi, h_lo, DN_LAST, preferred_element_type=F32))
    s = jax.nn.sigmoid(logits)
    sel = s + br_ref[...]
    epg = EXPERTS_PER_GROUP
    srow = [s[e:e + 1, :] for e in range(N_EXPERTS)]
    row = [sel[e:e + 1, :] for e in range(N_EXPERTS)]

    def top2_sum(a, b, c, d):
        hi1, lo1 = jnp.maximum(a, b), jnp.minimum(a, b)
        hi2, lo2 = jnp.maximum(c, d), jnp.minimum(c, d)
        return jnp.maximum(hi1, hi2) + jnp.maximum(jnp.minimum(hi1, hi2), jnp.maximum(lo1, lo2))

    gs = [top2_sum(*row[epg * g:epg * (g + 1)]) for g in range(N_EXPERT_GROUPS)]
    best = gs[0]
    gidx = jnp.zeros_like(best, dtype=I32)
    for g in range(1, N_EXPERT_GROUPS):
        better = gs[g] > best
        gidx = jnp.where(better, g, gidx)
        best = jnp.where(better, gs[g], best)

    def pick(rows_, j):
        v = rows_[j]
        for g in range(1, N_EXPERT_GROUPS):
            v = jnp.where(gidx == g, rows_[epg * g + j], v)
        return v

    v = [pick(row, j) for j in range(epg)]
    u = [pick(srow, j) for j in range(epg)]

    def argmax_first(vals):
        bv, bi = vals[0], jnp.zeros_like(gidx)
        for j in range(1, epg):
            better = vals[j] > bv
            bi = jnp.where(better, j, bi)
            bv = jnp.where(better, vals[j], bv)
        return bi

    i1 = argmax_first(v)
    i2 = argmax_first([jnp.where(i1 == j, -jnp.inf, v[j]) for j in range(epg)])

    def take(vals, idx):
        out = vals[0]
        for j in range(1, epg):
            out = jnp.where(idx == j, vals[j], out)
        return out

    w1, w2 = take(u, i1), take(u, i2)
    tot = w1 + w2
    w1, w2 = w1 / tot, w2 / tot
    e1 = gidx * epg + i1
    e2 = gidx * epg + i2
    gates_ref[...] = jnp.concatenate(
        [jnp.where(e1 == e, w1, 0.0) + jnp.where(e2 == e, w2, 0.0) for e in range(N_EXPERTS)], axis=0)
    e_ref[...] = jnp.concatenate([e1, e2], axis=0)
    w_ref[...] = jnp.concatenate([w1, w2], axis=0)


def _route(x, g, mods, sc_chunk, sh_chunk, rows_per_b, tm, w_router_t, b_router, split):
    t, d = x.shape
    e = N_EXPERTS
    if split:
        h_spec = pl.BlockSpec((tm, d // LANES, LANES), lambda i, j: (i, 0, 0))
        h_shape = jax.ShapeDtypeStruct((t, d // LANES, LANES), BF16)
    else:
        h_spec = pl.BlockSpec((tm, d), lambda i, j: (i, 0))
        h_shape = jax.ShapeDtypeStruct((t, d), BF16)
    return pl.pallas_call(
        functools.partial(_route_kernel, split=split),
        grid=(t // tm, 1),
        in_specs=[
            pl.BlockSpec((tm, d), lambda i, j: (i, 0)),
            pl.BlockSpec((1, d), lambda i, j: (0, 0)),
            _mod_spec(mods, sc_chunk, tm, rows_per_b, d),
            _mod_spec(mods, sh_chunk, tm, rows_per_b, d),
            pl.BlockSpec((e, d), lambda i, j: (0, 0)),
            pl.BlockSpec((e, 1), lambda i, j: (0, 0)),
        ],
        out_specs=[h_spec, pl.BlockSpec((e, tm), lambda i, j: (0, i)),
                   pl.BlockSpec((TOP_K, tm), lambda i, j: (0, i)), pl.BlockSpec((TOP_K, tm), lambda i, j: (0, i))],
        out_shape=[h_shape, jax.ShapeDtypeStruct((e, t), F32),
                   jax.ShapeDtypeStruct((TOP_K, t), I32), jax.ShapeDtypeStruct((TOP_K, t), F32)],
        compiler_params=_params("arbitrary", "arbitrary"),
        name="moe_route",
    )(x, g.reshape(1, d), mods, mods, w_router_t, b_router.reshape(e, 1))


def _moe_kernel(h_ref, gates_ref, wg_ref, wu_ref, wd_ref, res_ref, gt_ref, o_ref, acc_ref):
    e = pl.program_id(1)
    first = (e == 0) & (pl.program_id(2) == 0)
    last = (e == pl.num_programs(1) - 1) & (pl.program_id(2) == pl.num_programs(2) - 1)

    @pl.when(first)
    def _():
        acc_ref[...] = jnp.zeros_like(acc_ref)

    h = h_ref[...]
    a = jnp.dot(h, wg_ref[...].astype(BF16), preferred_element_type=F32)
    b = jnp.dot(h, wu_ref[...].astype(BF16), preferred_element_type=F32)
    gates = gates_ref[...]
    lane = lax.broadcasted_iota(I32, gates.shape, 1)
    gcol = jnp.sum(jnp.where(lane == e, gates, 0.0), axis=1, keepdims=True)
    hid = (a * jax.nn.sigmoid(a)) * b * gcol
    acc_ref[...] += jnp.dot(hid.astype(BF16), wd_ref[...].astype(BF16), preferred_element_type=F32)

    @pl.when(last)
    def _():
        o_ref[...] = res_ref[...] + gt_ref[...] * acc_ref[...]


def _moe(h, gates, w_gate, w_up, w_down, layer, res, mods, gt_chunk, rows_per_b, tm, fsplit=2):
    t, d = h.shape
    e, _, f = w_gate.shape[1:]
    tf = f // fsplit
    r = mods.shape[1]
    return pl.pallas_call(
        _moe_kernel,
        grid=(t // tm, e, fsplit),
        in_specs=[
            pl.BlockSpec((tm, d), lambda i, j, k: (i, 0)),
            pl.BlockSpec((tm, e), lambda i, j, k: (i, 0)),
            pl.BlockSpec((None, None, d, tf), lambda i, j, k: (layer, j, 0, k)),
            pl.BlockSpec((None, None, d, tf), lambda i, j, k: (layer, j, 0, k)),
            pl.BlockSpec((None, None, tf, d), lambda i, j, k: (layer, j, k, 0)),
            pl.BlockSpec((tm, d), lambda i, j, k: (i, 0)),
            pl.BlockSpec((None, r, d), lambda i, j, k: ((i * tm) // rows_per_b, 0, gt_chunk)),
        ],
        out_specs=pl.BlockSpec((tm, d), lambda i, j, k: (i, 0)),
        out_shape=jax.ShapeDtypeStruct((t, d), F32),
        scratch_shapes=[pltpu.VMEM((tm, d), F32)],
        compiler_params=_params("arbitrary", "arbitrary", "arbitrary"),
        name="moe_experts",
    )(h, gates, w_gate, w_up, w_down, res, mods)


def _experts_kernel(te_ref, nu_ref, idxc_ref, idxn_ref, h_ref, wg_ref, wu_ref, wd_ref, o_ref,
                    wgb, wub, wdb, xbuf, sem, *, tg):
    i = pl.program_id(0)
    nu = nu_ref[0]
    slot = i % 2
    per_blk = SMEM_I32_BLOCK // tg

    def start_gather(idx_ref, tile, dst_slot):
        off = (tile % per_blk) * tg

        def issue(grp, carry):
            for u in range(DMA_GROUP):
                r = grp * DMA_GROUP + u
                pltpu.make_async_copy(h_ref.at[pl.ds(idx_ref[off + r], 1)], xbuf.at[dst_slot, pl.ds(r, 1)],
                                      sem.at[dst_slot]).start()
            return carry

        lax.fori_loop(0, tg // DMA_GROUP, issue, 0)

    @pl.when(i == 0)
    def _():
        start_gather(idxc_ref, i, slot)

    @pl.when(i + 1 < nu)
    def _():
        start_gather(idxn_ref, i + 1, 1 - slot)

    prev = te_ref[jnp.maximum(i - 1, 0)]

    @pl.when((i == 0) | (te_ref[i] != prev))
    def _():
        wgb[...] = wg_ref[...].astype(BF16)
        wub[...] = wu_ref[...].astype(BF16)
        wdb[...] = wd_ref[...].astype(BF16)

    @pl.when(i < nu)
    def _():
        def drain(r, carry):
            pltpu.make_async_copy(h_ref.at[pl.ds(0, 1)], xbuf.at[slot, pl.ds(r, 1)], sem.at[slot]).wait()
            return carry

        lax.fori_loop(0, tg, drain, 0, unroll=8)
        x = xbuf[slot].reshape(tg, wgb.shape[0])
        a = jnp.dot(x, wgb[...], preferred_element_type=F32)
        b = jnp.dot(x, wub[...], preferred_element_type=F32)
        hid = ((a * jax.nn.sigmoid(a)) * b).astype(BF16)
        y = jnp.dot(hid, wdb[...], preferred_element_type=F32)
        o_ref[...] = y.reshape(o_ref.shape)

    @pl.when(i >= nu)
    def _():
        o_ref[...] = jnp.zeros_like(o_ref)


def _experts(tile_expert, n_used, sorted_tok, h3, w_gate, w_up, w_down, layer, tg):
    rp = sorted_tok.shape[0]
    _, _, d, f = w_gate.shape
    blk = SMEM_I32_BLOCK
    per_blk = blk // tg
    last_blk = rp // blk - 1
    row_shape = h3.shape[1:]
    grid_spec = pltpu.PrefetchScalarGridSpec(
        num_scalar_prefetch=2,
        grid=(rp // tg,),
        in_specs=[
            pl.BlockSpec((blk,), lambda i, te, nu: (i // per_blk,), memory_space=pltpu.SMEM),
            pl.BlockSpec((blk,), lambda i, te, nu: (jnp.minimum((i + 1) // per_blk, last_blk),),
                         memory_space=pltpu.SMEM),
            pl.BlockSpec(memory_space=pl.ANY),
            pl.BlockSpec((None, None, d, f), lambda i, te, nu: (layer, te[i], 0, 0)),
            pl.BlockSpec((None, None, d, f), lambda i, te, nu: (layer, te[i], 0, 0)),
            pl.BlockSpec((None, None, f, d), lambda i, te, nu: (layer, te[i], 0, 0)),
        ],
        out_specs=pl.BlockSpec((tg,) + row_shape, lambda i, te, nu: (i, 0, 0)),
        scratch_shapes=[pltpu.VMEM((d, f), BF16), pltpu.VMEM((d, f), BF16), pltpu.VMEM((f, d), BF16),
                        pltpu.VMEM((2, tg) + row_shape, h3.dtype), pltpu.SemaphoreType.DMA((2,))],
    )
    return pl.pallas_call(
        functools.partial(_experts_kernel, tg=tg),
        grid_spec=grid_spec,
        out_shape=jax.ShapeDtypeStruct((rp,) + row_shape, F32),
        compiler_params=_params("arbitrary"),
        name="moe_routed_experts",
    )(tile_expert, n_used, sorted_tok, sorted_tok, h3, w_gate, w_up, w_down)


def _combine_kernel(pos_ref, ys_ref, w_ref, res_ref, gt_ref, *refs, tm, final):
    if final:
        gf_ref, o_ref, buf, sem = refs
    else:
        o_ref, buf, sem = refs

    def issue(grp, carry):
        for u in range(DMA_GROUP):
            r = grp * DMA_GROUP + u
            for k in range(TOP_K):
                pltpu.make_async_copy(ys_ref.at[pl.ds(pos_ref[k * tm + r], 1)], buf.at[k, pl.ds(r, 1)],
                                      sem).start(priority=(u + k) % 2)
        return carry

    lax.fori_loop(0, tm // DMA_GROUP, issue, 0)

    def drain(r, carry):
        for k in range(TOP_K):
            pltpu.make_async_copy(ys_ref.at[pl.ds(0, 1)], buf.at[k, pl.ds(r, 1)], sem).wait()
        return carry

    lax.fori_loop(0, tm, drain, 0, unroll=8)
    d = o_ref.shape[1]
    w = w_ref[...]
    m = (buf[0].reshape(tm, d) * jnp.tile(w[:, :LANES], (1, d // LANES))
         + buf[1].reshape(tm, d) * jnp.tile(w[:, LANES:], (1, d // LANES)))
    x = res_ref[...] + gt_ref[...] * m
    if final:
        ms = jnp.mean(x * x, axis=-1, keepdims=True)
        x = (x * lax.rsqrt(ms + EPS)) * gf_ref[...]
    o_ref[...] = x


def _combine(pos, ys, wtok, res, mods, gt_chunk, rows_per_b, tm, g_final=None):
    t, d = res.shape
    assert TOP_K * tm == SMEM_I32_BLOCK
    r = mods.shape[1]
    final = g_final is not None
    in_specs = [pl.BlockSpec((TOP_K * tm,), lambda i: (i,), memory_space=pltpu.SMEM),
                pl.BlockSpec(memory_space=pl.ANY),
                pl.BlockSpec((tm, TOP_K * LANES), lambda i: (i, 0)),
                pl.BlockSpec((tm, d), lambda i: (i, 0)),
                pl.BlockSpec((None, r, d), lambda i: ((i * tm) // rows_per_b, 0, gt_chunk))]
    args = [pos, ys, wtok, res, mods]
    if final:
        in_specs.append(pl.BlockSpec((1, d), lambda i: (0, 0)))
        args.append(g_final.reshape(1, d))
    return pl.pallas_call(
        functools.partial(_combine_kernel, tm=tm, final=final),
        grid=(t // tm,),
        in_specs=in_specs,
        out_specs=pl.BlockSpec((tm, d), lambda i: (i, 0)),
        out_shape=jax.ShapeDtypeStruct((t, d), F32),
        scratch_shapes=[pltpu.VMEM((TOP_K, tm) + ys.shape[1:], F32), pltpu.SemaphoreType.DMA(())],
        compiler_params=_params("arbitrary"),
        name="moe_combine",
    )(*args)


def _moe_routed(h3, e_rows, w_rows, w_gate, w_up, w_down, layer, res, mods, gt_chunk, rows_per_b,
                g_final=None, tg=256, tmc=512):
    t = h3.shape[0]
    ne = N_EXPERTS
    a = TOP_K * t
    rp = a + ne * tg
    rp = -(-rp // SMEM_I32_BLOCK) * SMEM_I32_BLOCK
    n_tiles = rp // tg
    ea = e_rows.reshape(a)
    ta = jnp.tile(jnp.arange(t, dtype=I32), TOP_K)
    onehot = (ea[:, None] == jnp.arange(ne, dtype=I32)[None, :]).astype(I32)
    csum = jnp.cumsum(onehot, axis=0)
    rank = jnp.sum((csum - onehot) * onehot, axis=1)
    counts = csum[-1]
    ptiles = (counts + tg - 1) // tg
    tile_end = jnp.cumsum(ptiles)
    tile_start = tile_end - ptiles
    n_used = tile_end[-1:].astype(I32)
    dest = jnp.sum(onehot * tile_start[None, :], axis=1) * tg + rank
    sorted_tok = jnp.zeros((rp,), I32).at[dest].set(ta, unique_indices=True)
    tile_expert = jnp.sum(jnp.arange(n_tiles, dtype=I32)[:, None] >= tile_end[None, :], axis=1)
    tile_expert = jnp.minimum(tile_expert, ne - 1).astype(I32)
    ys = _experts(tile_expert, n_used, sorted_tok, h3, w_gate, w_up, w_down, layer, tg)
    pos = dest.reshape(TOP_K, t // tmc, tmc).transpose(1, 0, 2).reshape(a)
    wtok = jnp.concatenate([jnp.broadcast_to(w_rows[k][:, None], (t, LANES)) for k in range(TOP_K)], axis=1)
    return _combine(pos, ys, wtok, res, mods, gt_chunk, rows_per_b, tmc, g_final)


def _kv_kernel(x_ref, g_ref, sc_ref, sh_ref, w_ref, gc_ref, cos_ref, sin_ref, ckv_ref, kpe_ref, wb_ref):
    @pl.when(pl.program_id(0) == 0)
    def _():
        wb_ref[...] = w_ref[...].astype(BF16)

    h = _normmod(x_ref[...], g_ref[...], sc_ref[...], sh_ref[...]).astype(BF16)
    lat = lax.dot_general(h, wb_ref[...], DN_LAST, preferred_element_type=F32)
    c = lat[:, :KV_LORA]
    ms = jnp.mean(c * c, axis=-1, keepdims=True)
    ckv_ref[...] = (c * lax.rsqrt(ms + EPS)) * gc_ref[...]
    kpe_ref[...] = _rope(lat[:, KV_LORA:], cos_ref[...], sin_ref[...])


def _kv(x, g_kv, mods, rows_per_b, tm, w_dkv_t, g_ckv, cos2, sin2):
    t, d = x.shape
    n = w_dkv_t.shape[0]
    nt = cos2.shape[0] // tm
    return pl.pallas_call(
        _kv_kernel,
        grid=(t // tm, 1),
        in_specs=[
            pl.BlockSpec((tm, d), lambda i, j: (i, 0)),
            pl.BlockSpec((1, d), lambda i, j: (0, 0)),
            _mod_spec(mods, 1, tm, rows_per_b, d),
            _mod_spec(mods, 0, tm, rows_per_b, d),
            pl.BlockSpec((n, d), lambda i, j: (0, 0)),
            pl.BlockSpec((1, KV_LORA), lambda i, j: (0, 0)),
            pl.BlockSpec((tm, QK_ROPE), lambda i, j: (i % nt, 0)),
            pl.BlockSpec((tm, QK_ROPE), lambda i, j: (i % nt, 0)),
        ],
        out_specs=[pl.BlockSpec((tm, KV_LORA), lambda i, j: (i, 0)),
                   pl.BlockSpec((tm, QK_ROPE), lambda i, j: (i, 0))],
        out_shape=[jax.ShapeDtypeStruct((t, KV_LORA), F32), jax.ShapeDtypeStruct((t, QK_ROPE), F32)],
        scratch_shapes=[pltpu.VMEM((n, d), BF16)],
        compiler_params=_params("arbitrary", "arbitrary"),
        name="latent_kv",
    )(x, g_kv.reshape(1, d), mods, mods, w_dkv_t, g_ckv.reshape(1, KV_LORA), cos2, sin2)


def _qproj_kernel(x_ref, g_ref, sc_ref, sh_ref, w_ref, cos_ref, sin_ref, q_ref, hb_ref, *, hb):
    @pl.when(pl.program_id(1) == 0)
    def _():
        hb_ref[...] = _normmod(x_ref[...], g_ref[...], sc_ref[...], sh_ref[...]).astype(BF16)

    acc = jnp.dot(hb_ref[...], w_ref[...].astype(BF16), preferred_element_type=F32)
    cos2, sin2 = cos_ref[...], sin_ref[...]
    for h in range(hb):
        c0 = h * QK_HEAD
        q_ref[h, :, :QK_NOPE] = acc[:, c0:c0 + QK_NOPE].astype(BF16)
        q_ref[h, :, QK_NOPE:] = _rope(acc[:, c0 + QK_NOPE:c0 + QK_HEAD], cos2, sin2).astype(BF16)


def _qproj(x, g, mods, sc_chunk, sh_chunk, rows_per_b, tm, w_q2, cos2, sin2, hb=4):
    t, d = x.shape
    nh = w_q2.shape[1] // QK_HEAD
    nt = rows_per_b // tm
    return pl.pallas_call(
        functools.partial(_qproj_kernel, hb=hb),
        grid=(t // tm, nh // hb),
        in_specs=[
            pl.BlockSpec((tm, d), lambda i, j: (i, 0)),
            pl.BlockSpec((1, d), lambda i, j: (0, 0)),
            _mod_spec(mods, sc_chunk, tm, rows_per_b, d),
            _mod_spec(mods, sh_chunk, tm, rows_per_b, d),
            pl.BlockSpec((d, hb * QK_HEAD), lambda i, j: (0, j)),
            pl.BlockSpec((tm, QK_ROPE), lambda i, j: (i % nt, 0)),
            pl.BlockSpec((tm, QK_ROPE), lambda i, j: (i % nt, 0)),
        ],
        out_specs=pl.BlockSpec((None, hb, tm, QK_HEAD), lambda i, j: (i // nt, j, i % nt, 0)),
        out_shape=jax.ShapeDtypeStruct((t // rows_per_b, nh, rows_per_b, QK_HEAD), BF16),
        scratch_shapes=[pltpu.VMEM((tm, d), BF16)],
        compiler_params=_params("arbitrary", "arbitrary"),
        name="q_proj",
    )(x, g.reshape(1, d), mods, mods, w_q2, cos2, sin2)


def _kvup_kernel(ckv_ref, kpe_ref, wk_ref, wv_ref, k_ref, v_ref, *, hb):
    c = ckv_ref[...].astype(BF16)
    kn = jnp.dot(c, wk_ref[...].astype(BF16), preferred_element_type=F32)
    vv = jnp.dot(c, wv_ref[...].astype(BF16), preferred_element_type=F32)
    kpe = kpe_ref[...].astype(BF16)
    for h in range(hb):
        k_ref[h, :, :QK_NOPE] = kn[:, h * QK_NOPE:(h + 1) * QK_NOPE].astype(BF16)
        k_ref[h, :, QK_NOPE:] = kpe
        v_ref[h] = vv[:, h * V_HEAD:(h + 1) * V_HEAD].astype(BF16)


def _kvup(ckv, kpe, w_uk2, w_uv2, bsz, seq, tm, hb=8):
    t, r = ckv.shape
    nt = seq // tm
    return pl.pallas_call(
        functools.partial(_kvup_kernel, hb=hb),
        grid=(t // tm, N_HEADS // hb),
        in_specs=[
            pl.BlockSpec((tm, r), lambda i, j: (i, 0)),
            pl.BlockSpec((tm, QK_ROPE), lambda i, j: (i, 0)),
            pl.BlockSpec((r, hb * QK_NOPE), lambda i, j: (0, j)),
            pl.BlockSpec((r, hb * V_HEAD), lambda i, j: (0, j)),
        ],
        out_specs=[pl.BlockSpec((None, hb, tm, QK_HEAD), lambda i, j: (i // nt, j, i % nt, 0)),
                   pl.BlockSpec((None, hb, tm, V_HEAD), lambda i, j: (i // nt, j, i % nt, 0))],
        out_shape=[jax.ShapeDtypeStruct((bsz, N_HEADS, seq, QK_HEAD), BF16),
                   jax.ShapeDtypeStruct((bsz, N_HEADS, seq, V_HEAD), BF16)],
        compiler_params=_params("arbitrary", "arbitrary"),
        name="kv_up",
    )(ckv, kpe, w_uk2, w_uv2)


def _flash_kernel(qt_ref, kt_ref, q_ref, k_ref, v_ref, o_ref, m_ref, l_ref, acc_ref, *, tq, tk, hb):
    p_idx = pl.program_id(2)
    qi = qt_ref[p_idx]
    kj = kt_ref[p_idx]

    @pl.when(kj == 0)
    def _():
        m_ref[...] = jnp.full_like(m_ref, -jnp.inf)
        l_ref[...] = jnp.zeros_like(l_ref)
        acc_ref[...] = jnp.zeros_like(acc_ref)

    def tile(h, masked):
        s = lax.dot_general(q_ref[h], k_ref[h], DN_LAST, preferred_element_type=F32) * (ATTN_SCALE * LOG2_E)
        if masked:
            qpos = lax.broadcasted_iota(I32, (tq, tk), 0)
            kpos = lax.broadcasted_iota(I32, (tq, tk), 1)
            s = jnp.where(kpos <= qpos, s, -jnp.inf)
        m_prev = m_ref[h]
        m_next = jnp.maximum(m_prev, jnp.max(s, axis=1, keepdims=True))
        alpha = jnp.exp2(m_prev - m_next)
        p = jnp.exp2(s - jnp.tile(m_next, (1, tk // LANES)))
        l_ref[h] = alpha * l_ref[h] + jnp.sum(p, axis=1, keepdims=True)
        acc_ref[h] = alpha * acc_ref[h] + jnp.dot(p.astype(BF16), v_ref[h], preferred_element_type=F32)
        m_ref[h] = m_next

    @pl.when(kj < qi)
    def _():
        for h in range(hb):
            tile(h, False)

    @pl.when(kj == qi)
    def _():
        for h in range(hb):
            tile(h, True)
            o_ref[:, h * V_HEAD:(h + 1) * V_HEAD] = (acc_ref[h] / l_ref[h]).astype(BF16)


def _flash(q, k, v, tile_len=512, hb=8):
    bsz, nh, seq, _ = q.shape
    tq = tk = tile_len
    nq = seq // tq
    pairs = [(i, j) for i in range(nq) for j in range(i + 1)]
    qt = jnp.asarray([p[0] for p in pairs], I32)
    kt = jnp.asarray([p[1] for p in pairs], I32)
    grid_spec = pltpu.PrefetchScalarGridSpec(
        num_scalar_prefetch=2,
        grid=(bsz, nh // hb, len(pairs)),
        in_specs=[
            pl.BlockSpec((None, hb, tq, QK_HEAD), lambda b, h, p, qt, kt: (b, h, qt[p], 0)),
            pl.BlockSpec((None, hb, tk, QK_HEAD), lambda b, h, p, qt, kt: (b, h, kt[p], 0)),
            pl.BlockSpec((None, hb, tk, V_HEAD), lambda b, h, p, qt, kt: (b, h, kt[p], 0)),
        ],
        out_specs=pl.BlockSpec((tq, hb * V_HEAD), lambda b, h, p, qt, kt: (b * nq + qt[p], h)),
        scratch_shapes=[pltpu.VMEM((hb, tq, LANES), F32), pltpu.VMEM((hb, tq, LANES), F32),
                        pltpu.VMEM((hb, tq, V_HEAD), F32)],
    )
    return pl.pallas_call(
        functools.partial(_flash_kernel, tq=tq, tk=tk, hb=hb),
        grid_spec=grid_spec,
        out_shape=jax.ShapeDtypeStruct((bsz * seq, nh * V_HEAD), BF16),
        compiler_params=_params("arbitrary", "arbitrary", "arbitrary"),
        name="flash_attn",
    )(qt, kt, q, k, v)


def _qlat_kernel(q_ref, wk_ref, o_ref):
    q = q_ref[...]
    qlat = lax.dot_general(q[:, :QK_NOPE], wk_ref[...].astype(BF16), DN_LAST, preferred_element_type=F32)
    o_ref[:, :KV_LORA] = qlat.astype(BF16)
    o_ref[:, KV_LORA:] = q[:, QK_NOPE:]


def _qlat(q, w_uk2):
    nh, b, _ = q.shape
    r = w_uk2.shape[0]
    return pl.pallas_call(
        _qlat_kernel,
        grid=(nh,),
        in_specs=[pl.BlockSpec((None, b, QK_HEAD), lambda h: (h, 0, 0)),
                  pl.BlockSpec((r, QK_NOPE), lambda h: (0, h))],
        out_specs=pl.BlockSpec((None, b, r + QK_ROPE), lambda h: (h, 0, 0)),
        out_shape=jax.ShapeDtypeStruct((nh, b, r + QK_ROPE), BF16),
        compiler_params=_params("arbitrary"),
        name="q_latent",
    )(q, w_uk2)


def _paged_kernel(pt_ref, q_ref, ckv_hbm, kpe_hbm, cnew_ref, pnew_ref, o_ref, qpad_ref, kcb_ref, pp_ref,
                  m_ref, l_ref, acc_ref, pgc_ref, pgk_ref, semc, semk, *, npg, nsteps, nrows, rb):
    bi = pl.program_id(0)
    j = pl.program_id(1)
    nh = N_HEADS
    wide = KV_LORA // LANES
    first = j == 0

    g_step = bi * nsteps + j
    n_total = (nrows // rb) * nsteps

    def start_pages(gs):
        slot = gs % PAGE_SLOTS
        row_blk = gs // nsteps
        step = gs % nsteps
        for r in range(rb):
            for i in range(npg):
                pg = pt_ref[row_blk * rb + r, step * npg + i]
                pltpu.make_async_copy(ckv_hbm.at[pg], pgc_ref.at[slot, r * npg + i], semc.at[slot]).start()
                pltpu.make_async_copy(kpe_hbm.at[pg], pgk_ref.at[slot, r * npg + i], semk.at[slot]).start()

    @pl.when(g_step == 0)
    def _():
        for ahead in range(PAGE_SLOTS - 1):
            start_pages(g_step + ahead)

    @pl.when(first & (bi == 0))
    def _():
        kcb_ref[...] = jnp.zeros_like(kcb_ref)
        pp_ref[...] = jnp.zeros_like(pp_ref)
        m_ref[...] = jnp.zeros_like(m_ref)
        l_ref[...] = jnp.zeros_like(l_ref)
        acc_ref[...] = jnp.zeros_like(acc_ref)

    @pl.when(first)
    def _():
        qpad_ref[...] = jnp.zeros_like(qpad_ref)
        for r in range(rb):
            qpad_ref[r, :nh, :] = q_ref[r]

    def step(cur, prv):
        @pl.when(g_step + (PAGE_SLOTS - 1) < n_total)
        def _():
            start_pages(g_step + (PAGE_SLOTS - 1))

        pslot = g_step % PAGE_SLOTS
        for k in range(rb * npg):
            pltpu.make_async_copy(ckv_hbm.at[0], pgc_ref.at[pslot, k], semc.at[pslot]).wait()
            pltpu.make_async_copy(kpe_hbm.at[0], pgk_ref.at[pslot, k], semk.at[pslot]).wait()
        for r in range(rb):
            pv_prev = jnp.dot(pp_ref[r], kcb_ref[r, prv], preferred_element_type=F32)
            acc_in = jnp.where(first, 0.0, acc_ref[r] + pv_prev)
            m_prev = jnp.where(first, -jnp.inf, m_ref[r])
            l_prev = jnp.where(first, 0.0, l_ref[r])
            for i in range(npg):
                kcb_ref[r, cur, i * PAGE_SIZE:(i + 1) * PAGE_SIZE, :] = pgc_ref[pslot, r * npg + i].astype(BF16)
            s_c = lax.dot_general(kcb_ref[r, cur], qpad_ref[r, :, :KV_LORA], DN_LAST,
                                  preferred_element_type=F32)
            kp = jnp.concatenate([pgk_ref[pslot, r * npg + i].astype(BF16) for i in range(npg)],
                                 axis=1)
            s_p = jnp.dot(qpad_ref[r, :, KV_LORA:], kp, preferred_element_type=F32)
            s = (s_c.T[:nh, :] + s_p[:nh, :]) * ATTN_SCALE
            m_next = jnp.maximum(m_prev, jnp.max(s, axis=1, keepdims=True))
            alpha = jnp.exp(m_prev - m_next)
            p = jnp.exp(s - jnp.tile(m_next, (1, s.shape[1] // LANES)))
            l_ref[r] = alpha * l_prev + jnp.sum(p, axis=1, keepdims=True)
            acc_ref[r] = jnp.tile(alpha, (1, wide)) * acc_in
            m_ref[r] = m_next
            pp_ref[r] = p.astype(BF16)

    @pl.when(j % 2 == 0)
    def _():
        step(0, 1)

    @pl.when(j % 2 == 1)
    def _():
        step(1, 0)

    @pl.when(j == nsteps - 1)
    def _():
        last_slot = (nsteps - 1) % 2
        for r in range(rb):
            acc_past = acc_ref[r] + jnp.dot(pp_ref[r], kcb_ref[r, last_slot], preferred_element_type=F32)
            q = q_ref[r].astype(F32)
            cnew = cnew_ref[r]
            s_new = (jnp.sum(q[:, :KV_LORA] * cnew, axis=1, keepdims=True)
                     + jnp.sum(q[:, KV_LORA:] * pnew_ref[r], axis=1, keepdims=True)) * ATTN_SCALE
            m_p = m_ref[r]
            m_n = jnp.maximum(m_p, s_new)
            a = jnp.exp(m_p - m_n)
            p_new = jnp.exp(s_new - m_n)
            l_fin = a * l_ref[r] + p_new
            acc = jnp.tile(a, (1, wide)) * acc_past + jnp.tile(p_new, (1, wide)) * cnew
            o_ref[r] = (acc / jnp.tile(l_fin, (1, wide))).astype(BF16)


def _paged_attn(qcat, cache_ckv, cache_kpe_t, page_table, ckv_new, kpe_new, npg=8, rb=2):
    b, nh, w = qcat.shape
    r = KV_LORA
    n_pages = page_table.shape[1]
    keys = npg * PAGE_SIZE
    nsteps = n_pages // npg
    assert (b // rb) * nsteps >= PAGE_SLOTS - 1
    grid_spec = pltpu.PrefetchScalarGridSpec(
        num_scalar_prefetch=1,
        grid=(b // rb, nsteps),
        in_specs=[pl.BlockSpec((rb, nh, w), lambda bi, j, pt: (bi, 0, 0)),
                  pl.BlockSpec(memory_space=pl.ANY),
                  pl.BlockSpec(memory_space=pl.ANY),
                  pl.BlockSpec((rb, 1, r), lambda bi, j, pt: (bi, 0, 0)),
                  pl.BlockSpec((rb, 1, QK_ROPE), lambda bi, j, pt: (bi, 0, 0))],
        out_specs=pl.BlockSpec((rb, nh, r), lambda bi, j, pt: (bi, 0, 0)),
        scratch_shapes=[pltpu.VMEM((rb, LANES, w), BF16), pltpu.VMEM((rb, 2, keys, r), BF16),
                        pltpu.VMEM((rb, nh, keys), BF16), pltpu.VMEM((rb, nh, LANES), F32),
                        pltpu.VMEM((rb, nh, LANES), F32), pltpu.VMEM((rb, nh, r), F32),
                        pltpu.VMEM((PAGE_SLOTS, rb * npg, PAGE_SIZE, r), F32),
                        pltpu.VMEM((PAGE_SLOTS, rb * npg, QK_ROPE, PAGE_SIZE), F32),
                        pltpu.SemaphoreType.DMA((PAGE_SLOTS,)), pltpu.SemaphoreType.DMA((PAGE_SLOTS,))],
    )
    return pl.pallas_call(
        functools.partial(_paged_kernel, npg=npg, nsteps=nsteps, nrows=b, rb=rb),
        grid_spec=grid_spec,
        out_shape=jax.ShapeDtypeStruct((b, nh, r), BF16),
        compiler_params=_params("arbitrary", "arbitrary"),
        name="paged_attn",
    )(page_table, qcat, cache_ckv, cache_kpe_t, ckv_new.reshape(b, 1, r), kpe_new.reshape(b, 1, QK_ROPE))


def _ovup_kernel(o_ref, wv_ref, a_ref):
    a_ref[...] = jnp.dot(o_ref[...], wv_ref[...].astype(BF16), preferred_element_type=F32).astype(BF16)


def _ovup(olat, w_uv2):
    nh, b, r = olat.shape
    return pl.pallas_call(
        _ovup_kernel,
        grid=(nh,),
        in_specs=[pl.BlockSpec((None, b, r), lambda h: (h, 0, 0)), pl.BlockSpec((r, V_HEAD), lambda h: (0, h))],
        out_specs=pl.BlockSpec((b, V_HEAD), lambda h: (0, h)),
        out_shape=jax.ShapeDtypeStruct((b, nh * V_HEAD), BF16),
        compiler_params=_params("arbitrary"),
        name="o_latent_up",
    )(olat, w_uv2)


def _rope_tables(pos):
    inv = ROPE_THETA ** (-jnp.arange(0, QK_ROPE, 2, dtype=F32) / QK_ROPE)
    ang = pos.astype(F32)[:, None] * inv[None, :]
    cos, sin = jnp.cos(ang), jnp.sin(ang)
    return jnp.concatenate([cos, cos], axis=1), jnp.concatenate([-sin, sin], axis=1)


def _trunk(x, mods, mods_kv, rows_per_b, tm, cos2, sin2, P, s5_glu_fn, attn_fn, routed):
    tms = min(tm, 512)

    def moe_block(x, layer, g_final=None):
        h2, gates_t, e_rows, w_rows = _route(x, P["g_norm"][layer, 1], mods[layer], 4, 3, rows_per_b, tms,
                                             P["w_router_t"], P["b_router"], split=routed)
        if routed:
            return _moe_routed(h2, e_rows, w_rows, P["w_gate"], P["w_up"], P["w_down"], layer, x,
                               mods[layer], 5, rows_per_b, g_final)
        x = _moe(h2, gates_t.T, P["w_gate"], P["w_up"], P["w_down"], layer, x, mods[layer], 5,
                 rows_per_b, tms)
        return x if g_final is None else _final_norm(x, g_final, tms)

    x, st_re, st_im = s5_glu_fn(x)
    x = moe_block(x, 0)
    ckv, kpe = _kv(x, P["g_kv"], mods_kv, rows_per_b, tms, P["w_dkv_t"], P["g_ckv"], cos2, sin2)
    q = _qproj(x, P["g_norm"][1, 0], mods[1], 1, 0, rows_per_b, tm, P["w_q2"], cos2, sin2)
    attn = attn_fn(q, ckv, kpe)
    x = _proj_res(attn, P["w_o"][0], x, mods[1], 2, rows_per_b, tm, 512, glu=False)
    y = moe_block(x, 1, P["g_final"])
    return y, st_re, st_im, ckv, kpe


def kernel(x_prompt, x_sample, c_prompt, c_sample, state_ssm_re, state_ssm_im, cache_ckv, cache_kpe, page_table, g_norm, w_ada, b_ada, ssm_lam_re, ssm_lam_im, ssm_log_dt, ssm_b_re, ssm_b_im, ssm_c_re, ssm_c_im, ssm_d, w_glu, g_kv, w_ada_kv, b_ada_kv, w_dkv, g_ckv, w_uk, w_uv, w_q, w_o, w_router, b_router, w_gate, w_up, w_down, g_final):
    bsz, seq, d = x_prompt.shape
    db = x_sample.shape[0]
    g = d // SSM_GROUP
    n = SSM_STATE
    assert g == LANES, "the group-major relayout puts one S5 group per lane"
    past = page_table.shape[1] * PAGE_SIZE
    P = dict(g_norm=g_norm, g_kv=g_kv, g_ckv=g_ckv, w_o=w_o, b_router=b_router,
             w_gate=w_gate, w_up=w_up, w_down=w_down, g_final=g_final)
    P["w_q2"] = w_q[0]
    P["w_dkv_t"] = w_dkv.T
    P["w_router_t"] = w_router.T
    w_uk2 = w_uk.reshape(KV_LORA, N_HEADS * QK_NOPE)
    w_uv2 = w_uv.reshape(KV_LORA, N_HEADS * V_HEAD)

    m_rows = db + bsz
    m_pad = -(-m_rows // 16) * 16
    c_all = jnp.concatenate([c_sample, c_prompt, jnp.zeros((m_pad - m_rows, d), F32)], axis=0)
    mod = _ada(c_all, w_ada, b_ada)
    mod_kv = _ada(c_all, w_ada_kv[None], b_ada_kv[None])
    mods_s = [mod[l, :db][None] for l in range(mod.shape[0])]
    mods_p = [mod[l, db:m_rows][:, None, :] for l in range(mod.shape[0])]
    modkv_s = mod_kv[0, :db][None]
    modkv_p = mod_kv[0, db:m_rows][:, None, :]

    lam_re = ssm_lam_re[0].reshape(g, 1, n)
    lam_im = ssm_lam_im[0].reshape(g, 1, n)
    log_dt = jnp.broadcast_to(ssm_log_dt[0].reshape(g, 1, 1), (g, 1, n))
    bt_re = ssm_b_re[0].transpose(0, 2, 1)
    bt_im = ssm_b_im[0].transpose(0, 2, 1)
    c_re, c_im = ssm_c_re[0], ssm_c_im[0]
    d_g = ssm_d[0].reshape(g, 1, SSM_GROUP)
    d_rep = jnp.tile(d_g, (1, 1, S5_CHUNK))
    dup = lambda a: jnp.concatenate([a, a], axis=-1)
    cat = lambda a, b: jnp.concatenate([a, b], axis=-1)
    nchunk = seq // S5_CHUNK
    tm_p = 1024

    src = jnp.arange(d, dtype=I32)
    dst = (src % SSM_GROUP) * g + src // SSM_GROUP
    perm = (dst[:, None] == jnp.arange(d, dtype=I32)[None, :]).astype(BF16)
    w_glu_p = w_glu[0].reshape(g, SSM_GROUP, -1).transpose(1, 0, 2).reshape(d, -1)

    def s5_glu_prompt(x):
        hx = _s5_in(x, g_norm[0, 0], mods_p[0], 1, 0, seq, 256, perm)
        hx = hx.reshape(g, bsz * nchunk, S5_CHUNK * SSM_GROUP)
        zg, st = _s5_prompt(hx, dup(lam_re), dup(lam_im), dup(log_dt), cat(bt_re, bt_im), cat(bt_im, bt_re),
                            cat(c_re, c_im), cat(c_im, c_re), d_rep, bsz)
        zg = zg.reshape(g, bsz * seq * SSM_GROUP)
        x = _proj_res(zg, w_glu_p, x, mods_p[0], 2, seq, tm_p, 256, glu=True, grouped=True)
        st = st.transpose(1, 0, 2)[None]
        return x, st[..., :n], st[..., n:]

    def s5_glu_sample(x):
        h = _normmod_call(x, g_norm[0, 0], mods_s[0], 1, 0, db, db, F32)
        u = h.reshape(db, g, SSM_GROUP).transpose(1, 0, 2)
        h0_re = state_ssm_re[0].transpose(1, 0, 2)
        h0_im = state_ssm_im[0].transpose(1, 0, 2)
        zg, nr, ni = _s5_sample(u, h0_re, h0_im, lam_re, lam_im, log_dt, bt_re, bt_im, c_re, c_im, d_g)
        z = zg.transpose(1, 0, 2).reshape(db, d)
        x = _proj_res(z, w_glu[0], x, mods_s[0], 2, db, db, 256, glu=True)
        return x, nr.transpose(1, 0, 2)[None], ni.transpose(1, 0, 2)[None]

    def attn_prompt(q, ckv, kpe):
        k, v = _kvup(ckv, kpe, w_uk2, w_uv2, bsz, seq, tm_p)
        return _flash(q, k, v)

    def attn_sample(q, ckv, kpe):
        qcat = _qlat(q[0], w_uk2).transpose(1, 0, 2)
        olat = _paged_attn(qcat, cache_ckv, cache_kpe.transpose(0, 2, 1), page_table, ckv, kpe)
        return _ovup(olat.transpose(1, 0, 2), w_uv2)

    cos_p, sin_p = _rope_tables(jnp.arange(seq))
    cos_s, sin_s = _rope_tables(jnp.full((db,), past))

    y_p, sre_p, sim_p, ckv_p, kpe_p = _trunk(
        x_prompt.reshape(bsz * seq, d), mods_p, modkv_p, seq, tm_p, cos_p, sin_p, P,
        s5_glu_prompt, attn_prompt, routed=True)
    y_s, sre_s, sim_s, ckv_s, kpe_s = _trunk(
        x_sample.reshape(db, d), mods_s, modkv_s, db, db, cos_s, sin_s, P,
        s5_glu_sample, attn_sample, routed=False)
    return (y_p.reshape(bsz, seq, d), y_s.reshape(db, 1, d), sre_p, sim_p,
            ckv_p.reshape(bsz, seq, KV_LORA), kpe_p.reshape(bsz, seq, QK_ROPE),
            sre_s, sim_s, ckv_s.reshape(db, 1, KV_LORA), kpe_s.reshape(db, 1, QK_ROPE))
```

```python
import functools
import math

import jax
import jax.numpy as jnp
from jax import lax
from jax.experimental import pallas as pl
from jax.experimental.pallas import tpu as pltpu

F32 = jnp.float32
BF16 = jnp.bfloat16
I32 = jnp.int32

SSM_GROUP = 16
SSM_STATE = 64
N_HEADS = 16
QK_NOPE = 128
QK_ROPE = 64
V_HEAD = 128
KV_LORA = 512
ROPE_THETA = 10000.0
PAGE_SIZE = 128
N_EXPERTS = 16
N_EXPERT_GROUPS = 4
EXPERTS_PER_GROUP = N_EXPERTS // N_EXPERT_GROUPS
TOP_K = 2
EPS = 1e-6
ATTN_SCALE = (QK_NOPE + QK_ROPE) ** -0.5
QK_HEAD = QK_NOPE + QK_ROPE
LOG2_E = math.log2(math.e)

V7X_VMEM_BYTES = 64 * 1024 * 1024
VMEM_LIMIT_BYTES = V7X_VMEM_BYTES - 8 * 1024 * 1024
LANES = 128
SMEM_I32_BLOCK = 1024
DMA_GROUP = 8
PAGE_SLOTS = 3
S5_CHUNK = 16
DN_LAST = (((1,), (1,)), ((), ()))


def _params(*sem):
    return pltpu.CompilerParams(dimension_semantics=sem, vmem_limit_bytes=VMEM_LIMIT_BYTES)


def _normmod(x, g, sc, sh):
    ms = jnp.mean(x * x, axis=-1, keepdims=True)
    return (x * lax.rsqrt(ms + EPS)) * g * (1.0 + sc) + sh


def _rope(p, cos2, sin2):
    half = QK_ROPE // 2
    swapped = jnp.concatenate([p[:, half:], p[:, :half]], axis=1)
    return p * cos2 + swapped * sin2


def _mod_spec(mods, chunk, tm, rows_per_b, width, j_blocks=None):
    _, r, _ = mods.shape
    if j_blocks is None:
        return pl.BlockSpec((None, r, width), lambda i, j: ((i * tm) // rows_per_b, 0, chunk))
    return pl.BlockSpec((None, r, width), lambda i, j: ((i * tm) // rows_per_b, 0, chunk * j_blocks + j))


def _ada_kernel(c_ref, w_ref, b_ref, o_ref, cb_ref):
    @pl.when((pl.program_id(0) == 0) & (pl.program_id(1) == 0))
    def _():
        c = c_ref[...]
        cb_ref[...] = (c * jax.nn.sigmoid(c)).astype(BF16)

    acc = jnp.dot(cb_ref[...], w_ref[...].astype(BF16), preferred_element_type=F32)
    o_ref[...] = acc + b_ref[...]


def _ada(c_all, w, b, tn=1024):
    m, d = c_all.shape
    lw, _, n = w.shape
    return pl.pallas_call(
        _ada_kernel,
        grid=(lw, n // tn),
        in_specs=[
            pl.BlockSpec((m, d), lambda l, j: (0, 0)),
            pl.BlockSpec((None, d, tn), lambda l, j: (l, 0, j)),
            pl.BlockSpec((None, 1, tn), lambda l, j: (l, 0, j)),
        ],
        out_specs=pl.BlockSpec((None, m, tn), lambda l, j: (l, 0, j)),
        out_shape=jax.ShapeDtypeStruct((lw, m, n), F32),
        scratch_shapes=[pltpu.VMEM((m, d), BF16)],
        compiler_params=_params("arbitrary", "arbitrary"),
        name="ada_mod",
    )(c_all, w, b.reshape(lw, 1, n))


def _normmod_kernel(x_ref, g_ref, sc_ref, sh_ref, o_ref):
    o_ref[...] = _normmod(x_ref[...], g_ref[...], sc_ref[...], sh_ref[...]).astype(o_ref.dtype)


def _normmod_call(x, g, mods, sc_chunk, sh_chunk, rows_per_b, tm, out_dtype):
    t, d = x.shape
    return pl.pallas_call(
        _normmod_kernel,
        grid=(t // tm, 1),
        in_specs=[
            pl.BlockSpec((tm, d), lambda i, j: (i, 0)),
            pl.BlockSpec((1, d), lambda i, j: (0, 0)),
            _mod_spec(mods, sc_chunk, tm, rows_per_b, d),
            _mod_spec(mods, sh_chunk, tm, rows_per_b, d),
        ],
        out_specs=pl.BlockSpec((tm, d), lambda i, j: (i, 0)),
        out_shape=jax.ShapeDtypeStruct((t, d), out_dtype),
        compiler_params=_params("arbitrary", "arbitrary"),
        name="norm_mod",
    )(x, g.reshape(1, d), mods, mods)


def _final_norm_kernel(x_ref, g_ref, o_ref):
    x = x_ref[...]
    ms = jnp.mean(x * x, axis=-1, keepdims=True)
    o_ref[...] = (x * lax.rsqrt(ms + EPS)) * g_ref[...]


def _final_norm(x, g, tm):
    t, d = x.shape
    return pl.pallas_call(
        _final_norm_kernel,
        grid=(t // tm,),
        in_specs=[pl.BlockSpec((tm, d), lambda i: (i, 0)), pl.BlockSpec((1, d), lambda i: (0, 0))],
        out_specs=pl.BlockSpec((tm, d), lambda i: (i, 0)),
        out_shape=jax.ShapeDtypeStruct((t, d), F32),
        compiler_params=_params("arbitrary"),
        name="final_norm",
    )(x, g.reshape(1, d))


def _proj_res_kernel(x_ref, *refs, glu, grouped, tm):
    refs = list(refs)
    xb_ref = refs.pop() if grouped else None
    if glu:
        wa_ref, wb_ref, res_ref, gt_ref, o_ref = refs
    else:
        wa_ref, res_ref, gt_ref, o_ref = refs
    if grouped:
        @pl.when(pl.program_id(1) == 0)
        def _():
            cg = SSM_GROUP
            for k in range(tm // LANES):
                zt = x_ref[:, k * LANES * cg:(k + 1) * LANES * cg].T
                xb_ref[k * LANES:(k + 1) * LANES, :] = zt.reshape(LANES, cg, LANES).reshape(LANES, cg * LANES)

        x = xb_ref[...]
    else:
        x = x_ref[...]
    y = jnp.dot(x, wa_ref[...].astype(BF16), preferred_element_type=F32)
    if glu:
        yb = jnp.dot(x, wb_ref[...].astype(BF16), preferred_element_type=F32)
        y = y * jax.nn.sigmoid(yb)
    o_ref[...] = res_ref[...] + gt_ref[...] * y


def _proj_res(x, w, res, mods, gt_chunk, rows_per_b, tm, tn, glu, grouped=False):
    t, n = res.shape
    k = w.shape[0]
    nj = n // tn
    if grouped:
        x_spec = pl.BlockSpec((x.shape[0], tm * SSM_GROUP), lambda i, j: (0, i))
    else:
        x_spec = pl.BlockSpec((tm, k), lambda i, j: (i, 0))
    in_specs = [x_spec, pl.BlockSpec((k, tn), lambda i, j: (0, j))]
    args = [x, w]
    if glu:
        in_specs.append(pl.BlockSpec((k, tn), lambda i, j: (0, nj + j)))
        args.append(w)
    in_specs += [pl.BlockSpec((tm, tn), lambda i, j: (i, j)), _mod_spec(mods, gt_chunk, tm, rows_per_b, tn, nj)]
    args += [res, mods]
    return pl.pallas_call(
        functools.partial(_proj_res_kernel, glu=glu, grouped=grouped, tm=tm),
        grid=(t // tm, nj),
        in_specs=in_specs,
        out_specs=pl.BlockSpec((tm, tn), lambda i, j: (i, j)),
        out_shape=jax.ShapeDtypeStruct((t, n), F32),
        scratch_shapes=[pltpu.VMEM((tm, k), BF16)] if grouped else [],
        compiler_params=_params("arbitrary", "arbitrary"),
        name="glu_proj" if glu else "out_proj",
    )(*args)


def _s5in_kernel(x_ref, g_ref, sc_ref, sh_ref, p_ref, o_ref):
    h = _normmod(x_ref[...], g_ref[...], sc_ref[...], sh_ref[...]).astype(BF16)
    hp = jnp.dot(h, p_ref[...], preferred_element_type=F32).astype(BF16)
    tm = hp.shape[0]
    o_ref[...] = hp.reshape(tm, SSM_GROUP, LANES).reshape(tm * SSM_GROUP, LANES).T


def _s5_in(x, g, mods, sc_chunk, sh_chunk, rows_per_b, tm, perm):
    t, d = x.shape
    ng = d // SSM_GROUP
    return pl.pallas_call(
        _s5in_kernel,
        grid=(t // tm, 1),
        in_specs=[
            pl.BlockSpec((tm, d), lambda i, j: (i, 0)),
            pl.BlockSpec((1, d), lambda i, j: (0, 0)),
            _mod_spec(mods, sc_chunk, tm, rows_per_b, d),
            _mod_spec(mods, sh_chunk, tm, rows_per_b, d),
            pl.BlockSpec((d, d), lambda i, j: (0, 0)),
        ],
        out_specs=pl.BlockSpec((ng, tm * SSM_GROUP), lambda i, j: (0, i)),
        out_shape=jax.ShapeDtypeStruct((ng, t * SSM_GROUP), BF16),
        compiler_params=_params("arbitrary", "arbitrary"),
        name="s5_in",
    )(x, g.reshape(1, d), mods, mods, perm)


def _cpow(lr_dt, li_dt, k):
    mag = jnp.exp(k * lr_dt)
    ang = k * li_dt
    return mag * jnp.cos(ang), mag * jnp.sin(ang)


def _rep_rows(e, n):
    q, w = e.shape
    return jnp.concatenate([jnp.broadcast_to(e[t:t + 1], (n, w)) for t in range(q)], axis=0)


def _tile_rows(c, n):
    return jnp.concatenate([c] * n, axis=0)


def _zoh(lr, li, ldt):
    dt = jnp.exp(ldt)
    lr_dt = lr * dt
    li_dt = li * dt
    mag = jnp.exp(lr_dt)
    ab_re = mag * jnp.cos(li_dt)
    ab_im = mag * jnp.sin(li_dt)
    den = lr * lr + li * li
    f_re = ((ab_re - 1.0) * lr + ab_im * li) / den
    f_im = (ab_im * lr - (ab_re - 1.0) * li) / den
    return lr_dt, li_dt, ab_re, ab_im, f_re, f_im


def _gelu_tanh(y):
    return 0.5 * y * (1.0 + jnp.tanh(math.sqrt(2.0 / math.pi) * (y + 0.044715 * (y * y * y))))


def _s5_prompt_kernel(x_ref, lr_ref, li_ref, ldt_ref, bt_ref, bts_ref, c_ref, cs_ref, d_ref,
                      z_ref, st_ref, s_sc, ss_sc, hp_sc, yl_sc, *, gb, nb, nchunk):
    q, cg, n = S5_CHUNK, SSM_GROUP, SSM_STATE
    half = q // 2
    ti = lax.broadcasted_iota(I32, (q, 1), 0).astype(F32)
    rows = lax.broadcasted_iota(I32, (q * cg, q * cg), 0)
    cols = lax.broadcasted_iota(I32, (q * cg, q * cg), 1)
    shift = cg.bit_length() - 1
    causal = (rows >> shift) <= (cols >> shift)
    diag = rows == cols
    lane = lax.broadcasted_iota(I32, (1, 2 * n), 1)
    sgn = jnp.where(lane < n, -1.0, 1.0)

    def cmul(x2, xs, er, ei):
        return x2 * er + (xs * sgn) * ei

    a_terms = []
    vcs = []
    for g in range(gb):
        lr_dt, li_dt, _, _, f_re, f_im = _zoh(lr_ref[g], li_ref[g], ldt_ref[g])
        bt2, bts = bt_ref[g], bts_ref[g]
        bb2 = cmul(bt2, bts, f_re, f_im)
        bbs = bts * f_re - (bt2 * sgn) * f_im
        bb2t, bbst = _tile_rows(bb2, q), _tile_rows(bbs, q)
        c2t, cst = _tile_rows(c_ref[g], q), _tile_rows(cs_ref[g], q)

        def table(k, lr_dt=lr_dt, li_dt=li_dt):
            er, ei = _cpow(lr_dt, li_dt, k)
            return _rep_rows(er, cg), _rep_rows(ei, cg)

        pc = cmul(c2t, cst, *table(ti - half)) * (-sgn)
        pb = cmul(bb2t, bbst, *table(half - ti))
        tt = lax.dot_general(pb.astype(BF16), pc.astype(BF16), DN_LAST, preferred_element_type=F32)
        tt = jnp.where(causal, tt, 0.0) + jnp.where(diag, d_ref[g], 0.0)
        w1 = cmul(bb2t, bbst, *table((q - 1) - ti))
        vcs.append((cmul(c2t, cst, *table(ti + 1.0)) * (-sgn)).astype(BF16))
        ar, ai = _cpow(lr_dt, li_dt, float(q))
        a_terms.append((ar, ai * sgn))
        x = x_ref[g]
        yl_sc[g] = jnp.dot(x, tt.astype(BF16), preferred_element_type=F32)
        s = jnp.dot(x, w1.astype(BF16), preferred_element_type=F32)
        s_sc[g] = s
        ss_sc[g] = pltpu.roll(s, n, axis=1)

    def carry(ch, hs):
        new = []
        for g in range(gb):
            h, hsw = hs[2 * g], hs[2 * g + 1]
            hp_sc[g, pl.ds(ch, nb, stride=nchunk), :] = h
            a1, a2 = a_terms[g]
            new.append(a1 * h + a2 * hsw + s_sc[g, pl.ds(ch, nb, stride=nchunk), :])
            new.append(a1 * hsw - a2 * h + ss_sc[g, pl.ds(ch, nb, stride=nchunk), :])
        return tuple(new)

    zero = jnp.zeros((nb, 2 * n), F32)
    hs = lax.fori_loop(0, nchunk, carry, (zero,) * (2 * gb), unroll=8)
    for g in range(gb):
        st_ref[g] = hs[2 * g]
        y = yl_sc[g] + lax.dot_general(hp_sc[g].astype(BF16), vcs[g], DN_LAST, preferred_element_type=F32)
        z_ref[g] = _gelu_tanh(y).astype(BF16)


def _s5_prompt(hx, lam_re2, lam_im2, log_dt2, bt2, bts, c2, cs, d_rep, nb, gb=4):
    g, r, w = hx.shape
    n2 = 2 * SSM_STATE
    nchunk = r // nb
    vec = lambda: pl.BlockSpec((gb, 1, n2), lambda i: (i, 0, 0))
    mat = lambda: pl.BlockSpec((gb, SSM_GROUP, n2), lambda i: (i, 0, 0))
    return pl.pallas_call(
        functools.partial(_s5_prompt_kernel, gb=gb, nb=nb, nchunk=nchunk),
        grid=(g // gb,),
        in_specs=[pl.BlockSpec((gb, r, w), lambda i: (i, 0, 0)), vec(), vec(), vec(),
                  mat(), mat(), mat(), mat(), pl.BlockSpec((gb, 1, w), lambda i: (i, 0, 0))],
        out_specs=[pl.BlockSpec((gb, r, w), lambda i: (i, 0, 0)),
                   pl.BlockSpec((gb, nb, n2), lambda i: (i, 0, 0))],
        out_shape=[jax.ShapeDtypeStruct((g, r, w), BF16),
                   jax.ShapeDtypeStruct((g, nb, n2), F32)],
        scratch_shapes=[pltpu.VMEM((gb, r, n2), F32), pltpu.VMEM((gb, r, n2), F32),
                        pltpu.VMEM((gb, r, n2), F32), pltpu.VMEM((gb, r, w), F32)],
        compiler_params=_params("arbitrary"),
        name="s5_prompt",
    )(hx, lam_re2, lam_im2, log_dt2, bt2, bts, c2, cs, d_rep)


def _s5_sample_kernel(u_ref, h0r_ref, h0i_ref, lr_ref, li_ref, ldt_ref, btr_ref, bti_ref,
                      cr_ref, ci_ref, d_ref, z_ref, nr_ref, ni_ref, *, gb):
    hp = lax.Precision.HIGHEST
    for g in range(gb):
        _, _, ab_re, ab_im, f_re, f_im = _zoh(lr_ref[g], li_ref[g], ldt_ref[g])
        bb_re = f_re * btr_ref[g] - f_im * bti_ref[g]
        bb_im = f_re * bti_ref[g] + f_im * btr_ref[g]
        u = u_ref[g]
        bu_re = jnp.dot(u, bb_re, preferred_element_type=F32, precision=hp)
        bu_im = jnp.dot(u, bb_im, preferred_element_type=F32, precision=hp)
        h0r = h0r_ref[g]
        h0i = h0i_ref[g]
        hr = bu_re + ab_re * h0r - ab_im * h0i
        hi = bu_im + ab_re * h0i + ab_im * h0r
        nr_ref[g] = hr
        ni_ref[g] = hi
        y = (lax.dot_general(hr, cr_ref[g], DN_LAST, preferred_element_type=F32, precision=hp)
             - lax.dot_general(hi, ci_ref[g], DN_LAST, preferred_element_type=F32, precision=hp))
        y = y + d_ref[g] * u
        z_ref[g] = _gelu_tanh(y).astype(BF16)


def _s5_sample(u, h0_re, h0_im, lam_re, lam_im, log_dt, bt_re, bt_im, c_re, c_im, d_g, gb=8):
    g, b, cg = u.shape
    n = SSM_STATE
    vec = lambda: pl.BlockSpec((gb, 1, n), lambda i: (i, 0, 0))
    mat = lambda: pl.BlockSpec((gb, cg, n), lambda i: (i, 0, 0))
    st = lambda: pl.BlockSpec((gb, b, n), lambda i: (i, 0, 0))
    return pl.pallas_call(
        functools.partial(_s5_sample_kernel, gb=gb),
        grid=(g // gb,),
        in_specs=[pl.BlockSpec((gb, b, cg), lambda i: (i, 0, 0)), st(), st(), vec(), vec(), vec(),
                  mat(), mat(), mat(), mat(), pl.BlockSpec((gb, 1, cg), lambda i: (i, 0, 0))],
        out_specs=[pl.BlockSpec((gb, b, cg), lambda i: (i, 0, 0)), st(), st()],
        out_shape=[jax.ShapeDtypeStruct((g, b, cg), BF16),
                   jax.ShapeDtypeStruct((g, b, n), F32),
                   jax.ShapeDtypeStruct((g, b, n), F32)],
        compiler_params=_params("arbitrary"),
        name="s5_sample",
    )(u, h0_re, h0_im, lam_re, lam_im, log_dt, bt_re, bt_im, c_re, c_im, d_g)


def _route_kernel(x_ref, g_ref, sc_ref, sh_ref, wr_ref, br_ref, h_ref, gates_ref, e_ref, w_ref, *, split):
    h = _normmod(x_ref[...], g_ref[...], sc_ref[...], sh_ref[...])
    hb = h.astype(BF16)
    h_ref[...] = hb.reshape(h_ref.shape) if split else hb
    w = wr_ref[...]
    w_hi = w.astype(BF16)
    w_lo = (w - w_hi.astype(F32)).astype(BF16)
    h_lo = (h - hb.astype(F32)).astype(BF16)
    logits = (lax.dot_general(w_hi, hb, DN_LAST, preferred_element_type=F32)
              + lax.dot_general(w_lo, hb, DN_LAST, preferred_element_type=F32)
              + lax.dot_general(w_hi, h_lo, DN_LAST, preferred_element_type=F32))
    s = jax.nn.sigmoid(logits)
    sel = s + br_ref[...]
    epg = EXPERTS_PER_GROUP
    srow = [s[e:e + 1, :] for e in range(N_EXPERTS)]
    row = [sel[e:e + 1, :] for e in range(N_EXPERTS)]

    def top2_sum(a, b, c, d):
        hi1, lo1 = jnp.maximum(a, b), jnp.minimum(a, b)
        hi2, lo2 = jnp.maximum(c, d), jnp.minimum(c, d)
        return jnp.maximum(hi1, hi2) + jnp.maximum(jnp.minimum(hi1, hi2), jnp.maximum(lo1, lo2))

    gs = [top2_sum(*row[epg * g:epg * (g + 1)]) for g in range(N_EXPERT_GROUPS)]
    best = gs[0]
    gidx = jnp.zeros_like(best, dtype=I32)
    for g in range(1, N_EXPERT_GROUPS):
        better = gs[g] > best
        gidx = jnp.where(better, g, gidx)
        best = jnp.where(better, gs[g], best)

    def pick(rows_, j):
        v = rows_[j]
        for g in range(1, N_EXPERT_GROUPS):
            v = jnp.where(gidx == g, rows_[epg * g + j], v)
        return v

    v = [pick(row, j) for j in range(epg)]
    u = [pick(srow, j) for j in range(epg)]

    def argmax_first(vals):
        bv, bi = vals[0], jnp.zeros_like(gidx)
        for j in range(1, epg):
            better = vals[j] > bv
            bi = jnp.where(better, j, bi)
            bv = jnp.where(better, vals[j], bv)
        return bi

    i1 = argmax_first(v)
    i2 = argmax_first([jnp.where(i1 == j, -jnp.inf, v[j]) for j in range(epg)])

    def take(vals, idx):
        out = vals[0]
        for j in range(1, epg):
            out = jnp.where(idx == j, vals[j], out)
        return out

    w1, w2 = take(u, i1), take(u, i2)
    tot = w1 + w2
    w1, w2 = w1 / tot, w2 / tot
    e1 = gidx * epg + i1
    e2 = gidx * epg + i2
    gates_ref[...] = jnp.concatenate(
        [jnp.where(e1 == e, w1, 0.0) + jnp.where(e2 == e, w2, 0.0) for e in range(N_EXPERTS)], axis=0)
    e_ref[...] = jnp.concatenate([e1, e2], axis=0)
    w_ref[...] = jnp.concatenate([w1, w2], axis=0)


def _route(x, g, mods, sc_chunk, sh_chunk, rows_per_b, tm, w_router_t, b_router, split):
    t, d = x.shape
    e = N_EXPERTS
    if split:
        h_spec = pl.BlockSpec((tm, d // LANES, LANES), lambda i, j: (i, 0, 0))
        h_shape = jax.ShapeDtypeStruct((t, d // LANES, LANES), BF16)
    else:
        h_spec = pl.BlockSpec((tm, d), lambda i, j: (i, 0))
        h_shape = jax.ShapeDtypeStruct((t, d), BF16)
    return pl.pallas_call(
        functools.partial(_route_kernel, split=split),
        grid=(t // tm, 1),
        in_specs=[
            pl.BlockSpec((tm, d), lambda i, j: (i, 0)),
            pl.BlockSpec((1, d), lambda i, j: (0, 0)),
            _mod_spec(mods, sc_chunk, tm, rows_per_b, d),
            _mod_spec(mods, sh_chunk, tm, rows_per_b, d),
            pl.BlockSpec((e, d), lambda i, j: (0, 0)),
            pl.BlockSpec((e, 1), lambda i, j: (0, 0)),
        ],
        out_specs=[h_spec, pl.BlockSpec((e, tm), lambda i, j: (0, i)),
                   pl.BlockSpec((TOP_K, tm), lambda i, j: (0, i)), pl.BlockSpec((TOP_K, tm), lambda i, j: (0, i))],
        out_shape=[h_shape, jax.ShapeDtypeStruct((e, t), F32),
                   jax.ShapeDtypeStruct((TOP_K, t), I32), jax.ShapeDtypeStruct((TOP_K, t), F32)],
        compiler_params=_params("arbitrary", "arbitrary"),
        name="moe_route",
    )(x, g.reshape(1, d), mods, mods, w_router_t, b_router.reshape(e, 1))


def _moe_kernel(h_ref, gates_ref, wg_ref, wu_ref, wd_ref, res_ref, gt_ref, o_ref, acc_ref):
    e = pl.program_id(1)
    first = (e == 0) & (pl.program_id(2) == 0)
    last = (e == pl.num_programs(1) - 1) & (pl.program_id(2) == pl.num_programs(2) - 1)

    @pl.when(first)
    def _():
        acc_ref[...] = jnp.zeros_like(acc_ref)

    h = h_ref[...]
    a = jnp.dot(h, wg_ref[...].astype(BF16), preferred_element_type=F32)
    b = jnp.dot(h, wu_ref[...].astype(BF16), preferred_element_type=F32)
    gates = gates_ref[...]
    lane = lax.broadcasted_iota(I32, gates.shape, 1)
    gcol = jnp.sum(jnp.where(lane == e, gates, 0.0), axis=1, keepdims=True)
    hid = (a * jax.nn.sigmoid(a)) * b * gcol
    acc_ref[...] += jnp.dot(hid.astype(BF16), wd_ref[...].astype(BF16), preferred_element_type=F32)

    @pl.when(last)
    def _():
        o_ref[...] = res_ref[...] + gt_ref[...] * acc_ref[...]


def _moe(h, gates, w_gate, w_up, w_down, layer, res, mods, gt_chunk, rows_per_b, tm, fsplit=2):
    t, d = h.shape
    e, _, f = w_gate.shape[1:]
    tf = f // fsplit
    r = mods.shape[1]
    return pl.pallas_call(
        _moe_kernel,
        grid=(t // tm, e, fsplit),
        in_specs=[
            pl.BlockSpec((tm, d), lambda i, j, k: (i, 0)),
            pl.BlockSpec((tm, e), lambda i, j, k: (i, 0)),
            pl.BlockSpec((None, None, d, tf), lambda i, j, k: (layer, j, 0, k)),
            pl.BlockSpec((None, None, d, tf), lambda i, j, k: (layer, j, 0, k)),
            pl.BlockSpec((None, None, tf, d), lambda i, j, k: (layer, j, k, 0)),
            pl.BlockSpec((tm, d), lambda i, j, k: (i, 0)),
            pl.BlockSpec((None, r, d), lambda i, j, k: ((i * tm) // rows_per_b, 0, gt_chunk)),
        ],
        out_specs=pl.BlockSpec((tm, d), lambda i, j, k: (i, 0)),
        out_shape=jax.ShapeDtypeStruct((t, d), F32),
        scratch_shapes=[pltpu.VMEM((tm, d), F32)],
        compiler_params=_params("arbitrary", "arbitrary", "arbitrary"),
        name="moe_experts",
    )(h, gates, w_gate, w_up, w_down, res, mods)


def _experts_kernel(te_ref, nu_ref, idxc_ref, idxn_ref, h_ref, wg_ref, wu_ref, wd_ref, o_ref,
                    wgb, wub, wdb, xbuf, sem, *, tg):
    i = pl.program_id(0)
    nu = nu_ref[0]
    slot = i % 2
    per_blk = SMEM_I32_BLOCK // tg

    def start_gather(idx_ref, tile, dst_slot):
        off = (tile % per_blk) * tg

        def issue(grp, carry):
            for u in range(DMA_GROUP):
                r = grp * DMA_GROUP + u
                pltpu.make_async_copy(h_ref.at[pl.ds(idx_ref[off + r], 1)], xbuf.at[dst_slot, pl.ds(r, 1)],
                                      sem.at[dst_slot]).start()
            return carry

        lax.fori_loop(0, tg // DMA_GROUP, issue, 0)

    @pl.when(i == 0)
    def _():
        start_gather(idxc_ref, i, slot)

    @pl.when(i + 1 < nu)
    def _():
        start_gather(idxn_ref, i + 1, 1 - slot)

    prev = te_ref[jnp.maximum(i - 1, 0)]

    @pl.when((i == 0) | (te_ref[i] != prev))
    def _():
        wgb[...] = wg_ref[...].astype(BF16)
        wub[...] = wu_ref[...].astype(BF16)
        wdb[...] = wd_ref[...].astype(BF16)

    @pl.when(i < nu)
    def _():
        def drain(r, carry):
            pltpu.make_async_copy(h_ref.at[pl.ds(0, 1)], xbuf.at[slot, pl.ds(r, 1)], sem.at[slot]).wait()
            return carry

        lax.fori_loop(0, tg, drain, 0, unroll=8)
        x = xbuf[slot].reshape(tg, wgb.shape[0])
        a = jnp.dot(x, wgb[...], preferred_element_type=F32)
        b = jnp.dot(x, wub[...], preferred_element_type=F32)
        hid = ((a * jax.nn.sigmoid(a)) * b).astype(BF16)
        y = jnp.dot(hid, wdb[...], preferred_element_type=F32)
        o_ref[...] = y.reshape(o_ref.shape)

    @pl.when(i >= nu)
    def _():
        o_ref[...] = jnp.zeros_like(o_ref)


def _experts(tile_expert, n_used, sorted_tok, h3, w_gate, w_up, w_down, layer, tg):
    rp = sorted_tok.shape[0]
    _, _, d, f = w_gate.shape
    blk = SMEM_I32_BLOCK
    per_blk = blk // tg
    last_blk = rp // blk - 1
    row_shape = h3.shape[1:]
    grid_spec = pltpu.PrefetchScalarGridSpec(
        num_scalar_prefetch=2,
        grid=(rp // tg,),
        in_specs=[
            pl.BlockSpec((blk,), lambda i, te, nu: (i // per_blk,), memory_space=pltpu.SMEM),
            pl.BlockSpec((blk,), lambda i, te, nu: (jnp.minimum((i + 1) // per_blk, last_blk),),
                         memory_space=pltpu.SMEM),
            pl.BlockSpec(memory_space=pl.ANY),
            pl.BlockSpec((None, None, d, f), lambda i, te, nu: (layer, te[i], 0, 0)),
            pl.BlockSpec((None, None, d, f), lambda i, te, nu: (layer, te[i], 0, 0)),
            pl.BlockSpec((None, None, f, d), lambda i, te, nu: (layer, te[i], 0, 0)),
        ],
        out_specs=pl.BlockSpec((tg,) + row_shape, lambda i, te, nu: (i, 0, 0)),
        scratch_shapes=[pltpu.VMEM((d, f), BF16), pltpu.VMEM((d, f), BF16), pltpu.VMEM((f, d), BF16),
                        pltpu.VMEM((2, tg) + row_shape, h3.dtype), pltpu.SemaphoreType.DMA((2,))],
    )
    return pl.pallas_call(
        functools.partial(_experts_kernel, tg=tg),
        grid_spec=grid_spec,
        out_shape=jax.ShapeDtypeStruct((rp,) + row_shape, F32),
        compiler_params=_params("arbitrary"),
        name="moe_routed_experts",
    )(tile_expert, n_used, sorted_tok, sorted_tok, h3, w_gate, w_up, w_down)


def _combine_kernel(pos_ref, ys_ref, w_ref, res_ref, gt_ref, *refs, tm, final):
    if final:
        gf_ref, o_ref, buf, sem = refs
    else:
        o_ref, buf, sem = refs

    def issue(grp, carry):
        for u in range(DMA_GROUP):
            r = grp * DMA_GROUP + u
            for k in range(TOP_K):
                pltpu.make_async_copy(ys_ref.at[pl.ds(pos_ref[k * tm + r], 1)], buf.at[k, pl.ds(r, 1)],
                                      sem).start(priority=(u + k) % 2)
        return carry

    lax.fori_loop(0, tm // DMA_GROUP, issue, 0)

    def drain(r, carry):
        for k in range(TOP_K):
            pltpu.make_async_copy(ys_ref.at[pl.ds(0, 1)], buf.at[k, pl.ds(r, 1)], sem).wait()
        return carry

    lax.fori_loop(0, tm, drain, 0, unroll=8)
    d = o_ref.shape[1]
    w = w_ref[...]
    m = (buf[0].reshape(tm, d) * jnp.tile(w[:, :LANES], (1, d // LANES))
         + buf[1].reshape(tm, d) * jnp.tile(w[:, LANES:], (1, d // LANES)))
    x = res_ref[...] + gt_ref[...] * m
    if final:
        ms = jnp.mean(x * x, axis=-1, keepdims=True)
        x = (x * lax.rsqrt(ms + EPS)) * gf_ref[...]
    o_ref[...] = x


def _combine(pos, ys, wtok, res, mods, gt_chunk, rows_per_b, tm, g_final=None):
    t, d = res.shape
    assert TOP_K * tm == SMEM_I32_BLOCK
    r = mods.shape[1]
    final = g_final is not None
    in_specs = [pl.BlockSpec((TOP_K * tm,), lambda i: (i,), memory_space=pltpu.SMEM),
                pl.BlockSpec(memory_space=pl.ANY),
                pl.BlockSpec((tm, TOP_K * LANES), lambda i: (i, 0)),
                pl.BlockSpec((tm, d), lambda i: (i, 0)),
                pl.BlockSpec((None, r, d), lambda i: ((i * tm) // rows_per_b, 0, gt_chunk))]
    args = [pos, ys, wtok, res, mods]
    if final:
        in_specs.append(pl.BlockSpec((1, d), lambda i: (0, 0)))
        args.append(g_final.reshape(1, d))
    return pl.pallas_call(
        functools.partial(_combine_kernel, tm=tm, final=final),
        grid=(t // tm,),
        in_specs=in_specs,
        out_specs=pl.BlockSpec((tm, d), lambda i: (i, 0)),
        out_shape=jax.ShapeDtypeStruct((t, d), F32),
        scratch_shapes=[pltpu.VMEM((TOP_K, tm) + ys.shape[1:], F32), pltpu.SemaphoreType.DMA(())],
        compiler_params=_params("arbitrary"),
        name="moe_combine",
    )(*args)


def _moe_routed(h3, e_rows, w_rows, w_gate, w_up, w_down, layer, res, mods, gt_chunk, rows_per_b,
                g_final=None, tg=256, tmc=512):
    t = h3.shape[0]
    ne = N_EXPERTS
    a = TOP_K * t
    rp = a + ne * tg
    rp = -(-rp // SMEM_I32_BLOCK) * SMEM_I32_BLOCK
    n_tiles = rp // tg
    ea = e_rows.reshape(a)
    ta = jnp.tile(jnp.arange(t, dtype=I32), TOP_K)
    onehot = (ea[:, None] == jnp.arange(ne, dtype=I32)[None, :]).astype(I32)
    csum = jnp.cumsum(onehot, axis=0)
    rank = jnp.sum((csum - onehot) * onehot, axis=1)
    counts = csum[-1]
    ptiles = (counts + tg - 1) // tg
    tile_end = jnp.cumsum(ptiles)
    tile_start = tile_end - ptiles
    n_used = tile_end[-1:].astype(I32)
    dest = jnp.sum(onehot * tile_start[None, :], axis=1) * tg + rank
    sorted_tok = jnp.zeros((rp,), I32).at[dest].set(ta, unique_indices=True)
    tile_expert = jnp.sum(jnp.arange(n_tiles, dtype=I32)[:, None] >= tile_end[None, :], axis=1)
    tile_expert = jnp.minimum(tile_expert, ne - 1).astype(I32)
    ys = _experts(tile_expert, n_used, sorted_tok, h3, w_gate, w_up, w_down, layer, tg)
    pos = dest.reshape(TOP_K, t // tmc, tmc).transpose(1, 0, 2).reshape(a)
    wtok = jnp.concatenate([jnp.broadcast_to(w_rows[k][:, None], (t, LANES)) for k in range(TOP_K)], axis=1)
    return _combine(pos, ys, wtok, res, mods, gt_chunk, rows_per_b, tmc, g_final)


def _kv_kernel(x_ref, g_ref, sc_ref, sh_ref, w_ref, gc_ref, cos_ref, sin_ref, ckv_ref, kpe_ref, wb_ref):
    @pl.when(pl.program_id(0) == 0)
    def _():
        wb_ref[...] = w_ref[...].astype(BF16)

    h = _normmod(x_ref[...], g_ref[...], sc_ref[...], sh_ref[...]).astype(BF16)
    lat = lax.dot_general(h, wb_ref[...], DN_LAST, preferred_element_type=F32)
    c = lat[:, :KV_LORA]
    ms = jnp.mean(c * c, axis=-1, keepdims=True)
    ckv_ref[...] = (c * lax.rsqrt(ms + EPS)) * gc_ref[...]
    kpe_ref[...] = _rope(lat[:, KV_LORA:], cos_ref[...], sin_ref[...])


def _kv(x, g_kv, mods, rows_per_b, tm, w_dkv_t, g_ckv, cos2, sin2):
    t, d = x.shape
    n = w_dkv_t.shape[0]
    nt = cos2.shape[0] // tm
    return pl.pallas_call(
        _kv_kernel,
        grid=(t // tm, 1),
        in_specs=[
            pl.BlockSpec((tm, d), lambda i, j: (i, 0)),
            pl.BlockSpec((1, d), lambda i, j: (0, 0)),
            _mod_spec(mods, 1, tm, rows_per_b, d),
            _mod_spec(mods, 0, tm, rows_per_b, d),
            pl.BlockSpec((n, d), lambda i, j: (0, 0)),
            pl.BlockSpec((1, KV_LORA), lambda i, j: (0, 0)),
            pl.BlockSpec((tm, QK_ROPE), lambda i, j: (i % nt, 0)),
            pl.BlockSpec((tm, QK_ROPE), lambda i, j: (i % nt, 0)),
        ],
        out_specs=[pl.BlockSpec((tm, KV_LORA), lambda i, j: (i, 0)),
                   pl.BlockSpec((tm, QK_ROPE), lambda i, j: (i, 0))],
        out_shape=[jax.ShapeDtypeStruct((t, KV_LORA), F32), jax.ShapeDtypeStruct((t, QK_ROPE), F32)],
        scratch_shapes=[pltpu.VMEM((n, d), BF16)],
        compiler_params=_params("arbitrary", "arbitrary"),
        name="latent_kv",
    )(x, g_kv.reshape(1, d), mods, mods, w_dkv_t, g_ckv.reshape(1, KV_LORA), cos2, sin2)


def _qproj_kernel(x_ref, g_ref, sc_ref, sh_ref, w_ref, cos_ref, sin_ref, q_ref, hb_ref, *, hb):
    @pl.when(pl.program_id(1) == 0)
    def _():
        hb_ref[...] = _normmod(x_ref[...], g_ref[...], sc_ref[...], sh_ref[...]).astype(BF16)

    acc = jnp.dot(hb_ref[...], w_ref[...].astype(BF16), preferred_element_type=F32)
    cos2, sin2 = cos_ref[...], sin_ref[...]
    for h in range(hb):
        c0 = h * QK_HEAD
        q_ref[h, :, :QK_NOPE] = acc[:, c0:c0 + QK_NOPE].astype(BF16)
        q_ref[h, :, QK_NOPE:] = _rope(acc[:, c0 + QK_NOPE:c0 + QK_HEAD], cos2, sin2).astype(BF16)


def _qproj(x, g, mods, sc_chunk, sh_chunk, rows_per_b, tm, w_q2, cos2, sin2, hb=4):
    t, d = x.shape
    nh = w_q2.shape[1] // QK_HEAD
    nt = rows_per_b // tm
    return pl.pallas_call(
        functools.partial(_qproj_kernel, hb=hb),
        grid=(t // tm, nh // hb),
        in_specs=[
            pl.BlockSpec((tm, d), lambda i, j: (i, 0)),
            pl.BlockSpec((1, d), lambda i, j: (0, 0)),
            _mod_spec(mods, sc_chunk, tm, rows_per_b, d),
            _mod_spec(mods, sh_chunk, tm, rows_per_b, d),
            pl.BlockSpec((d, hb * QK_HEAD), lambda i, j: (0, j)),
            pl.BlockSpec((tm, QK_ROPE), lambda i, j: (i % nt, 0)),
            pl.BlockSpec((tm, QK_ROPE), lambda i, j: (i % nt, 0)),
        ],
        out_specs=pl.BlockSpec((None, hb, tm, QK_HEAD), lambda i, j: (i // nt, j, i % nt, 0)),
        out_shape=jax.ShapeDtypeStruct((t // rows_per_b, nh, rows_per_b, QK_HEAD), BF16),
        scratch_shapes=[pltpu.VMEM((tm, d), BF16)],
        compiler_params=_params("arbitrary", "arbitrary"),
        name="q_proj",
    )(x, g.reshape(1, d), mods, mods, w_q2, cos2, sin2)


def _kvup_kernel(ckv_ref, kpe_ref, wk_ref, wv_ref, k_ref, v_ref, *, hb):
    c = ckv_ref[...].astype(BF16)
    kn = jnp.dot(c, wk_ref[...].astype(BF16), preferred_element_type=F32)
    vv = jnp.dot(c, wv_ref[...].astype(BF16), preferred_element_type=F32)
    kpe = kpe_ref[...].astype(BF16)
    for h in range(hb):
        k_ref[h, :, :QK_NOPE] = kn[:, h * QK_NOPE:(h + 1) * QK_NOPE].astype(BF16)
        k_ref[h, :, QK_NOPE:] = kpe
        v_ref[h] = vv[:, h * V_HEAD:(h + 1) * V_HEAD].astype(BF16)


def _kvup(ckv, kpe, w_uk2, w_uv2, bsz, seq, tm, hb=8):
    t, r = ckv.shape
    nt = seq // tm
    return pl.pallas_call(
        functools.partial(_kvup_kernel, hb=hb),
        grid=(t // tm, N_HEADS // hb),
        in_specs=[
            pl.BlockSpec((tm, r), lambda i, j: (i, 0)),
            pl.BlockSpec((tm, QK_ROPE), lambda i, j: (i, 0)),
            pl.BlockSpec((r, hb * QK_NOPE), lambda i, j: (0, j)),
            pl.BlockSpec((r, hb * V_HEAD), lambda i, j: (0, j)),
        ],
        out_specs=[pl.BlockSpec((None, hb, tm, QK_HEAD), lambda i, j: (i // nt, j, i % nt, 0)),
                   pl.BlockSpec((None, hb, tm, V_HEAD), lambda i, j: (i // nt, j, i % nt, 0))],
        out_shape=[jax.ShapeDtypeStruct((bsz, N_HEADS, seq, QK_HEAD), BF16),
                   jax.ShapeDtypeStruct((bsz, N_HEADS, seq, V_HEAD), BF16)],
        compiler_params=_params("arbitrary", "arbitrary"),
        name="kv_up",
    )(ckv, kpe, w_uk2, w_uv2)


def _flash_kernel(qt_ref, kt_ref, q_ref, k_ref, v_ref, o_ref, m_ref, l_ref, acc_ref, *, tq, tk, hb):
    p_idx = pl.program_id(2)
    qi = qt_ref[p_idx]
    kj = kt_ref[p_idx]

    @pl.when(kj == 0)
    def _():
        m_ref[...] = jnp.full_like(m_ref, -jnp.inf)
        l_ref[...] = jnp.zeros_like(l_ref)
        acc_ref[...] = jnp.zeros_like(acc_ref)

    def tile(h, masked):
        s = lax.dot_general(q_ref[h], k_ref[h], DN_LAST, preferred_element_type=F32) * (ATTN_SCALE * LOG2_E)
        if masked:
            qpos = lax.broadcasted_iota(I32, (tq, tk), 0)
            kpos = lax.broadcasted_iota(I32, (tq, tk), 1)
            s = jnp.where(kpos <= qpos, s, -jnp.inf)
        m_prev = m_ref[h]
        m_next = jnp.maximum(m_prev, jnp.max(s, axis=1, keepdims=True))
        alpha = jnp.exp2(m_prev - m_next)
        p = jnp.exp2(s - jnp.tile(m_next, (1, tk // LANES)))
        l_ref[h] = alpha * l_ref[h] + jnp.sum(p, axis=1, keepdims=True)
        acc_ref[h] = alpha * acc_ref[h] + jnp.dot(p.astype(BF16), v_ref[h], preferred_element_type=F32)
        m_ref[h] = m_next

    @pl.when(kj < qi)
    def _():
        for h in range(hb):
            tile(h, False)

    @pl.when(kj == qi)
    def _():
        for h in range(hb):
            tile(h, True)
            o_ref[:, h * V_HEAD:(h + 1) * V_HEAD] = (acc_ref[h] / l_ref[h]).astype(BF16)


def _flash(q, k, v, tile_len=512, hb=8):
    bsz, nh, seq, _ = q.shape
    tq = tk = tile_len
    nq = seq // tq
    pairs = [(i, j) for i in range(nq) for j in range(i + 1)]
    qt = jnp.asarray([p[0] for p in pairs], I32)
    kt = jnp.asarray([p[1] for p in pairs], I32)
    grid_spec = pltpu.PrefetchScalarGridSpec(
        num_scalar_prefetch=2,
        grid=(bsz, nh // hb, len(pairs)),
        in_specs=[
            pl.BlockSpec((None, hb, tq, QK_HEAD), lambda b, h, p, qt, kt: (b, h, qt[p], 0)),
            pl.BlockSpec((None, hb, tk, QK_HEAD), lambda b, h, p, qt, kt: (b, h, kt[p], 0)),
            pl.BlockSpec((None, hb, tk, V_HEAD), lambda b, h, p, qt, kt: (b, h, kt[p], 0)),
        ],
        out_specs=pl.BlockSpec((tq, hb * V_HEAD), lambda b, h, p, qt, kt: (b * nq + qt[p], h)),
        scratch_shapes=[pltpu.VMEM((hb, tq, LANES), F32), pltpu.VMEM((hb, tq, LANES), F32),
                        pltpu.VMEM((hb, tq, V_HEAD), F32)],
    )
    return pl.pallas_call(
        functools.partial(_flash_kernel, tq=tq, tk=tk, hb=hb),
        grid_spec=grid_spec,
        out_shape=jax.ShapeDtypeStruct((bsz * seq, nh * V_HEAD), BF16),
        compiler_params=_params("arbitrary", "arbitrary", "arbitrary"),
        name="flash_attn",
    )(qt, kt, q, k, v)


def _qlat_kernel(q_ref, wk_ref, o_ref):
    q = q_ref[...]
    qlat = lax.dot_general(q[:, :QK_NOPE], wk_ref[...].astype(BF16), DN_LAST, preferred_element_type=F32)
    o_ref[:, :KV_LORA] = qlat.astype(BF16)
    o_ref[:, KV_LORA:] = q[:, QK_NOPE:]


def _qlat(q, w_uk2):
    nh, b, _ = q.shape
    r = w_uk2.shape[0]
    return pl.pallas_call(
        _qlat_kernel,
        grid=(nh,),
        in_specs=[pl.BlockSpec((None, b, QK_HEAD), lambda h: (h, 0, 0)),
                  pl.BlockSpec((r, QK_NOPE), lambda h: (0, h))],
        out_specs=pl.BlockSpec((None, b, r + QK_ROPE), lambda h: (h, 0, 0)),
        out_shape=jax.ShapeDtypeStruct((nh, b, r + QK_ROPE), BF16),
        compiler_params=_params("arbitrary"),
        name="q_latent",
    )(q, w_uk2)


def _paged_kernel(pt_ref, q_ref, ckv_hbm, kpe_hbm, cnew_ref, pnew_ref, o_ref, qpad_ref, kcb_ref, pp_ref,
                  m_ref, l_ref, acc_ref, pgc_ref, pgk_ref, semc, semk, *, npg, nsteps, nrows, rb):
    bi = pl.program_id(0)
    j = pl.program_id(1)
    nh = N_HEADS
    wide = KV_LORA // LANES
    first = j == 0

    g_step = bi * nsteps + j
    n_total = (nrows // rb) * nsteps

    def start_pages(gs):
        slot = gs % PAGE_SLOTS
        row_blk = gs // nsteps
        step = gs % nsteps
        for r in range(rb):
            for i in range(npg):
                pg = pt_ref[row_blk * rb + r, step * npg + i]
                pltpu.make_async_copy(ckv_hbm.at[pg], pgc_ref.at[slot, r * npg + i], semc.at[slot]).start()
                pltpu.make_async_copy(kpe_hbm.at[pg], pgk_ref.at[slot, r * npg + i], semk.at[slot]).start()

    @pl.when(g_step == 0)
    def _():
        for ahead in range(PAGE_SLOTS - 1):
            start_pages(g_step + ahead)

    @pl.when(first & (bi == 0))
    def _():
        kcb_ref[...] = jnp.zeros_like(kcb_ref)
        pp_ref[...] = jnp.zeros_like(pp_ref)
        m_ref[...] = jnp.zeros_like(m_ref)
        l_ref[...] = jnp.zeros_like(l_ref)
        acc_ref[...] = jnp.zeros_like(acc_ref)

    @pl.when(first)
    def _():
        qpad_ref[...] = jnp.zeros_like(qpad_ref)
        for r in range(rb):
            qpad_ref[r, :nh, :] = q_ref[r]

    def step(cur, prv):
        @pl.when(g_step + (PAGE_SLOTS - 1) < n_total)
        def _():
            start_pages(g_step + (PAGE_SLOTS - 1))

        pslot = g_step % PAGE_SLOTS
        for k in range(rb * npg):
            pltpu.make_async_copy(ckv_hbm.at[0], pgc_ref.at[pslot, k], semc.at[pslot]).wait()
            pltpu.make_async_copy(kpe_hbm.at[0], pgk_ref.at[pslot, k], semk.at[pslot]).wait()
        for r in range(rb):
            pv_prev = jnp.dot(pp_ref[r], kcb_ref[r, prv], preferred_element_type=F32)
            acc_in = jnp.where(first, 0.0, acc_ref[r] + pv_prev)
            m_prev = jnp.where(first, -jnp.inf, m_ref[r])
            l_prev = jnp.where(first, 0.0, l_ref[r])
            for i in range(npg):
                kcb_ref[r, cur, i * PAGE_SIZE:(i + 1) * PAGE_SIZE, :] = pgc_ref[pslot, r * npg + i].astype(BF16)
            s_c = lax.dot_general(kcb_ref[r, cur], qpad_ref[r, :, :KV_LORA], DN_LAST,
                                  preferred_element_type=F32)
            kp = jnp.concatenate([pgk_ref[pslot, r * npg + i].astype(BF16) for i in range(npg)],
                                 axis=1)
            s_p = jnp.dot(qpad_ref[r, :, KV_LORA:], kp, preferred_element_type=F32)
            s = (s_c.T[:nh, :] + s_p[:nh, :]) * ATTN_SCALE
            m_next = jnp.maximum(m_prev, jnp.max(s, axis=1, keepdims=True))
            alpha = jnp.exp(m_prev - m_next)
            p = jnp.exp(s - jnp.tile(m_next, (1, s.shape[1] // LANES)))
            l_ref[r] = alpha * l_prev + jnp.sum(p, axis=1, keepdims=True)
            acc_ref[r] = jnp.tile(alpha, (1, wide)) * acc_in
            m_ref[r] = m_next
            pp_ref[r] = p.astype(BF16)

    @pl.when(j % 2 == 0)
    def _():
        step(0, 1)

    @pl.when(j % 2 == 1)
    def _():
        step(1, 0)

    @pl.when(j == nsteps - 1)
    def _():
        last_slot = (nsteps - 1) % 2
        for r in range(rb):
            acc_past = acc_ref[r] + jnp.dot(pp_ref[r], kcb_ref[r, last_slot], preferred_element_type=F32)
            q = q_ref[r].astype(F32)
            cnew = cnew_ref[r]
            s_new = (jnp.sum(q[:, :KV_LORA] * cnew, axis=1, keepdims=True)
                     + jnp.sum(q[:, KV_LORA:] * pnew_ref[r], axis=1, keepdims=True)) * ATTN_SCALE
            m_p = m_ref[r]
            m_n = jnp.maximum(m_p, s_new)
            a = jnp.exp(m_p - m_n)
            p_new = jnp.exp(s_new - m_n)
            l_fin = a * l_ref[r] + p_new
            acc = jnp.tile(a, (1, wide)) * acc_past + jnp.tile(p_new, (1, wide)) * cnew
            o_ref[r] = (acc / jnp.tile(l_fin, (1, wide))).astype(BF16)


def _paged_attn(qcat, cache_ckv, cache_kpe_t, page_table, ckv_new, kpe_new, npg=8, rb=2):
    b, nh, w = qcat.shape
    r = KV_LORA
    n_pages = page_table.shape[1]
    keys = npg * PAGE_SIZE
    nsteps = n_pages // npg
    assert (b // rb) * nsteps >= PAGE_SLOTS - 1
    grid_spec = pltpu.PrefetchScalarGridSpec(
        num_scalar_prefetch=1,
        grid=(b // rb, nsteps),
        in_specs=[pl.BlockSpec((rb, nh, w), lambda bi, j, pt: (bi, 0, 0)),
                  pl.BlockSpec(memory_space=pl.ANY),
                  pl.BlockSpec(memory_space=pl.ANY),
                  pl.BlockSpec((rb, 1, r), lambda bi, j, pt: (bi, 0, 0)),
                  pl.BlockSpec((rb, 1, QK_ROPE), lambda bi, j, pt: (bi, 0, 0))],
        out_specs=pl.BlockSpec((rb, nh, r), lambda bi, j, pt: (bi, 0, 0)),
        scratch_shapes=[pltpu.VMEM((rb, LANES, w), BF16), pltpu.VMEM((rb, 2, keys, r), BF16),
                        pltpu.VMEM((rb, nh, keys), BF16), pltpu.VMEM((rb, nh, LANES), F32),
                        pltpu.VMEM((rb, nh, LANES), F32), pltpu.VMEM((rb, nh, r), F32),
                        pltpu.VMEM((PAGE_SLOTS, rb * npg, PAGE_SIZE, r), F32),
                        pltpu.VMEM((PAGE_SLOTS, rb * npg, QK_ROPE, PAGE_SIZE), F32),
                        pltpu.SemaphoreType.DMA((PAGE_SLOTS,)), pltpu.SemaphoreType.DMA((PAGE_SLOTS,))],
    )
    return pl.pallas_call(
        functools.partial(_paged_kernel, npg=npg, nsteps=nsteps, nrows=b, rb=rb),
        grid_spec=grid_spec,
        out_shape=jax.ShapeDtypeStruct((b, nh, r), BF16),
        compiler_params=_params("arbitrary", "arbitrary"),
        name="paged_attn",
    )(page_table, qcat, cache_ckv, cache_kpe_t, ckv_new.reshape(b, 1, r), kpe_new.reshape(b, 1, QK_ROPE))


def _ovup_kernel(o_ref, wv_ref, a_ref):
    a_ref[...] = jnp.dot(o_ref[...], wv_ref[...].astype(BF16), preferred_element_type=F32).astype(BF16)


def _ovup(olat, w_uv2):
    nh, b, r = olat.shape
    return pl.pallas_call(
        _ovup_kernel,
        grid=(nh,),
        in_specs=[pl.BlockSpec((None, b, r), lambda h: (h, 0, 0)), pl.BlockSpec((r, V_HEAD), lambda h: (0, h))],
        out_specs=pl.BlockSpec((b, V_HEAD), lambda h: (0, h)),
        out_shape=jax.ShapeDtypeStruct((b, nh * V_HEAD), BF16),
        compiler_params=_params("arbitrary"),
        name="o_latent_up",
    )(olat, w_uv2)


def _rope_tables(pos):
    inv = ROPE_THETA ** (-jnp.arange(0, QK_ROPE, 2, dtype=F32) / QK_ROPE)
    ang = pos.astype(F32)[:, None] * inv[None, :]
    cos, sin = jnp.cos(ang), jnp.sin(ang)
    return jnp.concatenate([cos, cos], axis=1), jnp.concatenate([-sin, sin], axis=1)


def _trunk(x, mods, mods_kv, rows_per_b, tm, cos2, sin2, P, s5_glu_fn, attn_fn, routed):
    tms = min(tm, 512)

    def moe_block(x, layer, g_final=None):
        h2, gates_t, e_rows, w_rows = _route(x, P["g_norm"][layer, 1], mods[layer], 4, 3, rows_per_b, tms,
                                             P["w_router_t"], P["b_router"], split=routed)
        if routed:
            return _moe_routed(h2, e_rows, w_rows, P["w_gate"], P["w_up"], P["w_down"], layer, x,
                               mods[layer], 5, rows_per_b, g_final)
        x = _moe(h2, gates_t.T, P["w_gate"], P["w_up"], P["w_down"], layer, x, mods[layer], 5,
                 rows_per_b, tms)
        return x if g_final is None else _final_norm(x, g_final, tms)

    x, st_re, st_im = s5_glu_fn(x)
    x = moe_block(x, 0)
    ckv, kpe = _kv(x, P["g_kv"], mods_kv, rows_per_b, tms, P["w_dkv_t"], P["g_ckv"], cos2, sin2)
    q = _qproj(x, P["g_norm"][1, 0], mods[1], 1, 0, rows_per_b, tm, P["w_q2"], cos2, sin2)
    attn = attn_fn(q, ckv, kpe)
    x = _proj_res(attn, P["w_o"][0], x, mods[1], 2, rows_per_b, tm, 512, glu=False)
    y = moe_block(x, 1, P["g_final"])
    return y, st_re, st_im, ckv, kpe


def kernel(x_prompt, x_sample, c_prompt, c_sample, state_ssm_re, state_ssm_im, cache_ckv, cache_kpe, page_table, g_norm, w_ada, b_ada, ssm_lam_re, ssm_lam_im, ssm_log_dt, ssm_b_re, ssm_b_im, ssm_c_re, ssm_c_im, ssm_d, w_glu, g_kv, w_ada_kv, b_ada_kv, w_dkv, g_ckv, w_uk, w_uv, w_q, w_o, w_router, b_router, w_gate, w_up, w_down, g_final):
    bsz, seq, d = x_prompt.shape
    db = x_sample.shape[0]
    g = d // SSM_GROUP
    n = SSM_STATE
    assert g == LANES, "the group-major relayout puts one S5 group per lane"
    past = page_table.shape[1] * PAGE_SIZE
    P = dict(g_norm=g_norm, g_kv=g_kv, g_ckv=g_ckv, w_o=w_o, b_router=b_router,
             w_gate=w_gate, w_up=w_up, w_down=w_down, g_final=g_final)
    P["w_q2"] = w_q[0]
    P["w_dkv_t"] = w_dkv.T
    P["w_router_t"] = w_router.T
    w_uk2 = w_uk.reshape(KV_LORA, N_HEADS * QK_NOPE)
    w_uv2 = w_uv.reshape(KV_LORA, N_HEADS * V_HEAD)

    m_rows = db + bsz
    m_pad = -(-m_rows // 16) * 16
    c_all = jnp.concatenate([c_sample, c_prompt, jnp.zeros((m_pad - m_rows, d), F32)], axis=0)
    mod = _ada(c_all, w_ada, b_ada)
    mod_kv = _ada(c_all, w_ada_kv[None], b_ada_kv[None])
    mods_s = [mod[l, :db][None] for l in range(mod.shape[0])]
    mods_p = [mod[l, db:m_rows][:, None, :] for l in range(mod.shape[0])]
    modkv_s = mod_kv[0, :db][None]
    modkv_p = mod_kv[0, db:m_rows][:, None, :]

    lam_re = ssm_lam_re[0].reshape(g, 1, n)
    lam_im = ssm_lam_im[0].reshape(g, 1, n)
    log_dt = jnp.broadcast_to(ssm_log_dt[0].reshape(g, 1, 1), (g, 1, n))
    bt_re = ssm_b_re[0].transpose(0, 2, 1)
    bt_im = ssm_b_im[0].transpose(0, 2, 1)
    c_re, c_im = ssm_c_re[0], ssm_c_im[0]
    d_g = ssm_d[0].reshape(g, 1, SSM_GROUP)
    d_rep = jnp.tile(d_g, (1, 1, S5_CHUNK))
    dup = lambda a: jnp.concatenate([a, a], axis=-1)
    cat = lambda a, b: jnp.concatenate([a, b], axis=-1)
    nchunk = seq // S5_CHUNK
    tm_p = 1024

    src = jnp.arange(d, dtype=I32)
    dst = (src % SSM_GROUP) * g + src // SSM_GROUP
    perm = (dst[:, None] == jnp.arange(d, dtype=I32)[None, :]).astype(BF16)
    w_glu_p = w_glu[0].reshape(g, SSM_GROUP, -1).transpose(1, 0, 2).reshape(d, -1)

    def s5_glu_prompt(x):
        hx = _s5_in(x, g_norm[0, 0], mods_p[0], 1, 0, seq, 256, perm)
        hx = hx.reshape(g, bsz * nchunk, S5_CHUNK * SSM_GROUP)
        zg, st = _s5_prompt(hx, dup(lam_re), dup(lam_im), dup(log_dt), cat(bt_re, bt_im), cat(bt_im, bt_re),
                            cat(c_re, c_im), cat(c_im, c_re), d_rep, bsz)
        zg = zg.reshape(g, bsz * seq * SSM_GROUP)
        x = _proj_res(zg, w_glu_p, x, mods_p[0], 2, seq, tm_p, 512, glu=True, grouped=True)
        st = st.transpose(1, 0, 2)[None]
        return x, st[..., :n], st[..., n:]

    def s5_glu_sample(x):
        h = _normmod_call(x, g_norm[0, 0], mods_s[0], 1, 0, db, db, F32)
        u = h.reshape(db, g, SSM_GROUP).transpose(1, 0, 2)
        h0_re = state_ssm_re[0].transpose(1, 0, 2)
        h0_im = state_ssm_im[0].transpose(1, 0, 2)
        zg, nr, ni = _s5_sample(u, h0_re, h0_im, lam_re, lam_im, log_dt, bt_re, bt_im, c_re, c_im, d_g)
        z = zg.transpose(1, 0, 2).reshape(db, d)
        x = _proj_res(z, w_glu[0], x, mods_s[0], 2, db, db, 256, glu=True)
        return x, nr.transpose(1, 0, 2)[None], ni.transpose(1, 0, 2)[None]

    def attn_prompt(q, ckv, kpe):
        k, v = _kvup(ckv, kpe, w_uk2, w_uv2, bsz, seq, tm_p)
        return _flash(q, k, v)

    def attn_sample(q, ckv, kpe):
        qcat = _qlat(q[0], w_uk2).transpose(1, 0, 2)
        olat = _paged_attn(qcat, cache_ckv, cache_kpe.transpose(0, 2, 1), page_table, ckv, kpe)
        return _ovup(olat.transpose(1, 0, 2), w_uv2)

    cos_p, sin_p = _rope_tables(jnp.arange(seq))
    cos_s, sin_s = _rope_tables(jnp.full((db,), past))

    y_p, sre_p, sim_p, ckv_p, kpe_p = _trunk(
        x_prompt.reshape(bsz * seq, d), mods_p, modkv_p, seq, tm_p, cos_p, sin_p, P,
        s5_glu_prompt, attn_prompt, routed=True)
    y_s, sre_s, sim_s, ckv_s, kpe_s = _trunk(
        x_sample.reshape(db, d), mods_s, modkv_s, db, db, cos_s, sin_s, P,
        s5_glu_sample, attn_sample, routed=False)
    return (y_p.reshape(bsz, seq, d), y_s.reshape(db, 1, d), sre_p, sim_p,
            ckv_p.reshape(bsz, seq, KV_LORA), kpe_p.reshape(bsz, seq, QK_ROPE),
            sre_s, sim_s, ckv_s.reshape(db, 1, KV_LORA), kpe_s.reshape(db, 1, QK_ROPE))
```
